```python
import jax
import jax.numpy as jnp
from jax import lax
import numpy as np

D_MODEL = 4096
BATCH = 16
SEQ = 2048
DEPTH = 4

W_A = D_MODEL // 4
W_B = D_MODEL // 2
W_C = D_MODEL // 4
D_MIX = W_A + W_B + W_C
HEAD_DIM = 64
N_Q_HEADS = W_B // HEAD_DIM
KV_GROUP = 8
N_KV_HEADS = N_Q_HEADS // KV_GROUP
KV_W = N_KV_HEADS * HEAD_DIM
WINDOW = 128
BLOCK = 128
CONV_A = 3
CONV_C = 4
N_RG_HEADS = 8
RG_BLOCK = W_C // N_RG_HEADS
RG_C = 8.0
IN_W = 4 * W_A + (W_B + 2 * KV_W + W_B) + 2 * W_C
DEEPNORM_ALPHA = (2.0 * DEPTH) ** 0.25
DEEPNORM_BETA = (8.0 * DEPTH) ** -0.25
LN_EPS = 1e-5
RMS_EPS = 1e-6
NEG_INF = -1e30

kernel_name = "hybrid_shortconv_swa_rglru_deepnorm"


def layer_norm(x, g, b):
    xf = x.astype(jnp.float32)
    mu = xf.mean(-1, keepdims=True)
    var = jnp.mean(jnp.square(xf - mu), -1, keepdims=True)
    y = (xf - mu) * lax.rsqrt(var + LN_EPS) * g.astype(jnp.float32) + b.astype(jnp.float32)
    return y.astype(x.dtype)


def rms_norm(x, g):
    xf = x.astype(jnp.float32)
    y = xf * lax.rsqrt(jnp.mean(xf * xf, -1, keepdims=True) + RMS_EPS) * g.astype(jnp.float32)
    return y.astype(x.dtype)


def causal_depthwise_conv(u, w):
    K = w.shape[0]
    S = u.shape[1]
    up = jnp.pad(u, ((0, 0), (K - 1, 0), (0, 0)))
    y = up[:, 0:S] * w[0]
    for k in range(1, K):
        y = y + up[:, k:k + S] * w[k]
    return y


def sliding_window_attention(q, k, v, sinks):
    B, S = q.shape[0], q.shape[1]
    nb = S // BLOCK
    qb = q.reshape(B, nb, BLOCK, N_KV_HEADS, KV_GROUP, HEAD_DIM)

    def with_prev(t):
        tb = t.reshape(B, nb, BLOCK, N_KV_HEADS, HEAD_DIM)
        prev = jnp.pad(tb, ((0, 0), (1, 0), (0, 0), (0, 0), (0, 0)))[:, :-1]
        return jnp.concatenate([prev, tb], axis=2)

    kk = with_prev(k)
    vv = with_prev(v)
    s = jnp.einsum("bnqhgd,bnkhd->bnhgqk", qb, kk).astype(jnp.float32) * (HEAD_DIM ** -0.5)
    qi = jnp.arange(BLOCK)[:, None]
    kj = jnp.arange(2 * BLOCK)[None, :]
    dist = qi + BLOCK - kj
    band = (dist >= 0) & (dist < WINDOW)
    blk = jnp.arange(nb)[:, None, None]
    valid = band[None] & ((blk > 0) | (kj[None] >= BLOCK))
    s = jnp.where(valid[None, :, None, None], s, NEG_INF)
    sink = sinks.astype(jnp.float32).reshape(N_KV_HEADS, KV_GROUP)[None, None, :, :, None, None]
    m = jnp.maximum(s.max(-1, keepdims=True), sink)
    p = jnp.exp(s - m)
    p = p / (p.sum(-1, keepdims=True) + jnp.exp(sink - m))
    o = jnp.einsum("bnhgqk,bnkhd->bnqhgd", p.astype(v.dtype), vv)
    return o.reshape(B, S, N_Q_HEADS * HEAD_DIM)


def rg_lru(xc, w_r, b_r, w_i, b_i, lam):
    B, S, _ = xc.shape
    xh = xc.reshape(B, S, N_RG_HEADS, RG_BLOCK)
    r = jax.nn.sigmoid(jnp.einsum("bshi,hij->bshj", xh, w_r).reshape(B, S, W_C) + b_r)
    i = jax.nn.sigmoid(jnp.einsum("bshi,hij->bshj", xh, w_i).reshape(B, S, W_C) + b_i)
    log_a = RG_C * r.astype(jnp.float32) * jax.nn.log_sigmoid(lam.astype(jnp.float32))
    a = jnp.exp(log_a)
    u = jnp.sqrt(-jnp.expm1(2.0 * log_a)) * (i * xc).astype(jnp.float32)

    def combine(c1, c2):
        a1, b1 = c1
        a2, b2 = c2
        return a1 * a2, a2 * b1 + b2

    _, hs = lax.associative_scan(combine, (a, u), axis=1)
    return hs.astype(xc.dtype)


def hybrid_layer(x, w_in, conv_a_w, sinks, conv_c_w, conv_c_b, gate_r_w, gate_r_b,
                 gate_i_w, gate_i_b, rg_lambda, norm_a, norm_b, norm_c, w_out, ln_g, ln_b):
    B, S, _ = x.shape
    h = jnp.einsum("bsd,de->bse", x, w_in)
    sizes = [W_A, W_A, W_A, W_A, W_B, KV_W, KV_W, W_B, W_C, W_C]
    offs = np.cumsum(sizes)[:-1].tolist()
    a_b, a_c, a_x, a_g, q, k, v, b_g, c_x, c_g = jnp.split(h, offs, axis=-1)
    y_a = a_b * causal_depthwise_conv(a_c * a_x, conv_a_w)
    y_b = sliding_window_attention(q.reshape(B, S, N_Q_HEADS, HEAD_DIM),
                                   k.reshape(B, S, N_KV_HEADS, HEAD_DIM),
                                   v.reshape(B, S, N_KV_HEADS, HEAD_DIM), sinks)
    xc = causal_depthwise_conv(c_x, conv_c_w) + conv_c_b
    y_c = rg_lru(xc, gate_r_w, gate_r_b, gate_i_w, gate_i_b, rg_lambda)
    mix = jnp.concatenate([rms_norm(y_a, norm_a) * jax.nn.silu(a_g),
                           rms_norm(y_b, norm_b) * jax.nn.silu(b_g),
                           rms_norm(y_c, norm_c) * jax.nn.silu(c_g)], axis=-1)
    out = jnp.einsum("bse,ed->bsd", mix, w_out)
    return layer_norm(DEEPNORM_ALPHA * x + out, ln_g, ln_b)


def _fwd_setup_inputs(seed: int = 0) -> dict:
    key = jax.random.key(seed)
    ks = jax.random.split(key, 20)
    f32 = jnp.float32
    x = jax.random.normal(ks[0], (BATCH, SEQ, D_MODEL), f32)
    w_in = jax.random.normal(ks[1], (DEPTH, D_MODEL, IN_W), f32) * D_MODEL ** -0.5
    conv_a_w = jax.random.normal(ks[2], (DEPTH, CONV_A, W_A), f32) * CONV_A ** -0.5
    sinks = jax.random.normal(ks[3], (DEPTH, N_Q_HEADS), f32) * 0.5
    conv_c_w = jax.random.normal(ks[4], (DEPTH, CONV_C, W_C), f32) * CONV_C ** -0.5
    conv_c_b = jax.random.normal(ks[5], (DEPTH, W_C), f32) * 0.01
    gate_r_w = jax.random.normal(ks[6], (DEPTH, N_RG_HEADS, RG_BLOCK, RG_BLOCK), f32) * RG_BLOCK ** -0.5
    gate_r_b = jax.random.normal(ks[7], (DEPTH, W_C), f32) * 0.01
    gate_i_w = jax.random.normal(ks[8], (DEPTH, N_RG_HEADS, RG_BLOCK, RG_BLOCK), f32) * RG_BLOCK ** -0.5
    gate_i_b = jax.random.normal(ks[9], (DEPTH, W_C), f32) * 0.01
    a_pow_c = jax.random.uniform(ks[10], (DEPTH, W_C), f32, minval=0.9, maxval=0.999)
    a0 = a_pow_c ** (1.0 / RG_C)
    rg_lambda = jnp.log(a0) - jnp.log1p(-a0)
    norm_a = 1.0 + 0.01 * jax.random.normal(ks[11], (DEPTH, W_A), f32)
    norm_b = 1.0 + 0.01 * jax.random.normal(ks[12], (DEPTH, W_B), f32)
    norm_c = 1.0 + 0.01 * jax.random.normal(ks[13], (DEPTH, W_C), f32)
    w_out = jax.random.normal(ks[14], (DEPTH, D_MIX, D_MODEL), f32) * (D_MIX ** -0.5) * DEEPNORM_BETA
    ln_g = 1.0 + 0.01 * jax.random.normal(ks[15], (DEPTH, D_MODEL), f32)
    ln_b = 0.01 * jax.random.normal(ks[16], (DEPTH, D_MODEL), f32)
    return {"x": x, "w_in": w_in, "conv_a_w": conv_a_w, "sinks": sinks,
            "conv_c_w": conv_c_w, "conv_c_b": conv_c_b,
            "gate_r_w": gate_r_w, "gate_r_b": gate_r_b,
            "gate_i_w": gate_i_w, "gate_i_b": gate_i_b, "rg_lambda": rg_lambda,
            "norm_a": norm_a, "norm_b": norm_b, "norm_c": norm_c,
            "w_out": w_out, "ln_g": ln_g, "ln_b": ln_b}


def _fwd_reference(x, w_in, conv_a_w, sinks, conv_c_w, conv_c_b, gate_r_w, gate_r_b,
              gate_i_w, gate_i_b, rg_lambda, norm_a, norm_b, norm_c, w_out, ln_g, ln_b):
    for l in range(DEPTH):
        x = hybrid_layer(x, w_in[l], conv_a_w[l], sinks[l], conv_c_w[l], conv_c_b[l],
                         gate_r_w[l], gate_r_b[l], gate_i_w[l], gate_i_b[l], rg_lambda[l],
                         norm_a[l], norm_b[l], norm_c[l], w_out[l], ln_g[l], ln_b[l])
    return x


import jax as _jax
import jax.numpy as _jnp

TWIN_FORMAT = 'train_step'
FWD_PARAMS = ['x', 'w_in', 'conv_a_w', 'sinks', 'conv_c_w', 'conv_c_b', 'gate_r_w', 'gate_r_b', 'gate_i_w', 'gate_i_b', 'rg_lambda', 'norm_a', 'norm_b', 'norm_c', 'w_out', 'ln_g', 'ln_b']
TWIN_WEIGHTS = ['w_in', 'conv_a_w', 'sinks', 'conv_c_w', 'conv_c_b', 'gate_r_w', 'gate_r_b', 'gate_i_w', 'gate_i_b', 'rg_lambda', 'norm_a', 'norm_b', 'norm_c', 'w_out', 'ln_g', 'ln_b']
TWIN_DIFF_INPUT = 'x'
TWIN_INPUTS = ['x', 'w_in', 'conv_a_w', 'sinks', 'conv_c_w', 'conv_c_b', 'gate_r_w', 'gate_r_b', 'gate_i_w', 'gate_i_b', 'rg_lambda', 'norm_a', 'norm_b', 'norm_c', 'w_out', 'ln_g', 'ln_b', 'loss_target', 'm_w_in', 'm_conv_a_w', 'm_sinks', 'm_conv_c_w', 'm_conv_c_b', 'm_gate_r_w', 'm_gate_r_b', 'm_gate_i_w', 'm_gate_i_b', 'm_rg_lambda', 'm_norm_a', 'm_norm_b', 'm_norm_c', 'm_w_out', 'm_ln_g', 'm_ln_b', 'v_w_in', 'v_conv_a_w', 'v_sinks', 'v_conv_c_w', 'v_conv_c_b', 'v_gate_r_w', 'v_gate_r_b', 'v_gate_i_w', 'v_gate_i_b', 'v_rg_lambda', 'v_norm_a', 'v_norm_b', 'v_norm_c', 'v_w_out', 'v_ln_g', 'v_ln_b']
TWIN_OUTPUTS = ['loss', 'grad_x', 'grad_w_in', 'grad_conv_a_w', 'grad_sinks', 'grad_conv_c_w', 'grad_conv_c_b', 'grad_gate_r_w', 'grad_gate_r_b', 'grad_gate_i_w', 'grad_gate_i_b', 'grad_rg_lambda', 'grad_norm_a', 'grad_norm_b', 'grad_norm_c', 'grad_w_out', 'grad_ln_g', 'grad_ln_b', 'delta_w_in', 'delta_conv_a_w', 'delta_sinks', 'delta_conv_c_w', 'delta_conv_c_b', 'delta_gate_r_w', 'delta_gate_r_b', 'delta_gate_i_w', 'delta_gate_i_b', 'delta_rg_lambda', 'delta_norm_a', 'delta_norm_b', 'delta_norm_c', 'delta_w_out', 'delta_ln_g', 'delta_ln_b', 'new_m_w_in', 'new_m_conv_a_w', 'new_m_sinks', 'new_m_conv_c_w', 'new_m_conv_c_b', 'new_m_gate_r_w', 'new_m_gate_r_b', 'new_m_gate_i_w', 'new_m_gate_i_b', 'new_m_rg_lambda', 'new_m_norm_a', 'new_m_norm_b', 'new_m_norm_c', 'new_m_w_out', 'new_m_ln_g', 'new_m_ln_b', 'new_v_w_in', 'new_v_conv_a_w', 'new_v_sinks', 'new_v_conv_c_w', 'new_v_conv_c_b', 'new_v_gate_r_w', 'new_v_gate_r_b', 'new_v_gate_i_w', 'new_v_gate_i_b', 'new_v_rg_lambda', 'new_v_norm_a', 'new_v_norm_b', 'new_v_norm_c', 'new_v_w_out', 'new_v_ln_g', 'new_v_ln_b']
TWIN_LEAF_KINDS = {'loss': 'loss', 'grad_x': 'grad_x', 'grad_w_in': 'grad_w', 'grad_conv_a_w': 'grad_w', 'grad_sinks': 'grad_w', 'grad_conv_c_w': 'grad_w', 'grad_conv_c_b': 'grad_w', 'grad_gate_r_w': 'grad_w', 'grad_gate_r_b': 'grad_w', 'grad_gate_i_w': 'grad_w', 'grad_gate_i_b': 'grad_w', 'grad_rg_lambda': 'grad_w', 'grad_norm_a': 'grad_w', 'grad_norm_b': 'grad_w', 'grad_norm_c': 'grad_w', 'grad_w_out': 'grad_w', 'grad_ln_g': 'grad_w', 'grad_ln_b': 'grad_w', 'delta_w_in': 'delta_w', 'delta_conv_a_w': 'delta_w', 'delta_sinks': 'delta_w', 'delta_conv_c_w': 'delta_w', 'delta_conv_c_b': 'delta_w', 'delta_gate_r_w': 'delta_w', 'delta_gate_r_b': 'delta_w', 'delta_gate_i_w': 'delta_w', 'delta_gate_i_b': 'delta_w', 'delta_rg_lambda': 'delta_w', 'delta_norm_a': 'delta_w', 'delta_norm_b': 'delta_w', 'delta_norm_c': 'delta_w', 'delta_w_out': 'delta_w', 'delta_ln_g': 'delta_w', 'delta_ln_b': 'delta_w', 'new_m_w_in': 'new_m', 'new_m_conv_a_w': 'new_m', 'new_m_sinks': 'new_m', 'new_m_conv_c_w': 'new_m', 'new_m_conv_c_b': 'new_m', 'new_m_gate_r_w': 'new_m', 'new_m_gate_r_b': 'new_m', 'new_m_gate_i_w': 'new_m', 'new_m_gate_i_b': 'new_m', 'new_m_rg_lambda': 'new_m', 'new_m_norm_a': 'new_m', 'new_m_norm_b': 'new_m', 'new_m_norm_c': 'new_m', 'new_m_w_out': 'new_m', 'new_m_ln_g': 'new_m', 'new_m_ln_b': 'new_m', 'new_v_w_in': 'new_v', 'new_v_conv_a_w': 'new_v', 'new_v_sinks': 'new_v', 'new_v_conv_c_w': 'new_v', 'new_v_conv_c_b': 'new_v', 'new_v_gate_r_w': 'new_v', 'new_v_gate_r_b': 'new_v', 'new_v_gate_i_w': 'new_v', 'new_v_gate_i_b': 'new_v', 'new_v_rg_lambda': 'new_v', 'new_v_norm_a': 'new_v', 'new_v_norm_b': 'new_v', 'new_v_norm_c': 'new_v', 'new_v_w_out': 'new_v', 'new_v_ln_g': 'new_v', 'new_v_ln_b': 'new_v'}


def _forward(args):
    return _fwd_reference(*[args[k] for k in FWD_PARAMS])


def _output_shape():
    def fwd():
        inp = _fwd_setup_inputs(0)
        return _fwd_reference(*[inp[k] for k in FWD_PARAMS])
    out = _jax.eval_shape(fwd)
    return out.shape, out.dtype

N_MICROBATCH = 1
ADAM_LR = 0.001
ADAM_B1 = 0.9
ADAM_B2 = 0.999
ADAM_EPS = 1e-08
ADAM_WD = 0.01
ADAM_STEP = 10
PER_EXAMPLE_BATCH_AXIS = {'x': 0, 'loss_target': 0}
SHARED_INPUTS = []
_WEIGHT_DTYPES = {'w_in': _jnp.float32, 'conv_a_w': _jnp.float32, 'sinks': _jnp.float32, 'conv_c_w': _jnp.float32, 'conv_c_b': _jnp.float32, 'gate_r_w': _jnp.float32, 'gate_r_b': _jnp.float32, 'gate_i_w': _jnp.float32, 'gate_i_b': _jnp.float32, 'rg_lambda': _jnp.float32, 'norm_a': _jnp.float32, 'norm_b': _jnp.float32, 'norm_c': _jnp.float32, 'w_out': _jnp.float32, 'ln_g': _jnp.float32, 'ln_b': _jnp.float32}
MOMENT_SCALE = {'w_in': 7.826491e-03, 'conv_a_w': 6.759713e-03, 'sinks': 2.323470e-03, 'conv_c_w': 7.317472e-03, 'conv_c_b': 8.151798e-02, 'gate_r_w': 2.216088e-03, 'gate_r_b': 1.876255e-03, 'gate_i_w': 3.990539e-03, 'gate_i_b': 2.614695e-03, 'rg_lambda': 3.576494e-03, 'norm_a': 6.825100e-03, 'norm_b': 7.165447e-03, 'norm_c': 7.001975e-03, 'w_out': 1.654474e-02, 'ln_g': 4.002102e+00, 'ln_b': 1.412565e-01}


def _to_microbatches(a, axis):
    t = _jnp.moveaxis(a, axis, 0)
    t = t.reshape((N_MICROBATCH, t.shape[0] // N_MICROBATCH) + t.shape[1:])
    return _jnp.moveaxis(t, 1, axis + 1)


def setup_inputs(seed: int = 0) -> dict:
    inp = _fwd_setup_inputs(seed)
    key = _jax.random.fold_in(_jax.random.key(seed), 7919)
    shape, _ = _output_shape()
    out = dict(inp)
    out["loss_target"] = _jax.random.normal(_jax.random.fold_in(key, 0), shape, _jnp.float32)
    for i, name in enumerate(TWIN_WEIGHTS):
        w = inp[name].astype(_jnp.float32)
        if MOMENT_SCALE is None:
            s = _jnp.sqrt(_jnp.mean(_jnp.square(w)) + 1e-30)
        else:
            s = MOMENT_SCALE[name]
        km, kv = _jax.random.split(_jax.random.fold_in(key, i + 1))
        out[name] = w
        out["m_" + name] = s * _jax.random.normal(km, w.shape, _jnp.float32)
        out["v_" + name] = (s * s) * _jax.random.uniform(kv, w.shape, _jnp.float32, 0.5, 1.5)
    if N_MICROBATCH > 1:
        for name, axis in PER_EXAMPLE_BATCH_AXIS.items():
            out[name] = _to_microbatches(out[name], axis)
    return {'x': out['x'], 'w_in': out['w_in'], 'conv_a_w': out['conv_a_w'], 'sinks': out['sinks'], 'conv_c_w': out['conv_c_w'], 'conv_c_b': out['conv_c_b'], 'gate_r_w': out['gate_r_w'], 'gate_r_b': out['gate_r_b'], 'gate_i_w': out['gate_i_w'], 'gate_i_b': out['gate_i_b'], 'rg_lambda': out['rg_lambda'], 'norm_a': out['norm_a'], 'norm_b': out['norm_b'], 'norm_c': out['norm_c'], 'w_out': out['w_out'], 'ln_g': out['ln_g'], 'ln_b': out['ln_b'], 'loss_target': out['loss_target'], 'm_w_in': out['m_w_in'], 'm_conv_a_w': out['m_conv_a_w'], 'm_sinks': out['m_sinks'], 'm_conv_c_w': out['m_conv_c_w'], 'm_conv_c_b': out['m_conv_c_b'], 'm_gate_r_w': out['m_gate_r_w'], 'm_gate_r_b': out['m_gate_r_b'], 'm_gate_i_w': out['m_gate_i_w'], 'm_gate_i_b': out['m_gate_i_b'], 'm_rg_lambda': out['m_rg_lambda'], 'm_norm_a': out['m_norm_a'], 'm_norm_b': out['m_norm_b'], 'm_norm_c': out['m_norm_c'], 'm_w_out': out['m_w_out'], 'm_ln_g': out['m_ln_g'], 'm_ln_b': out['m_ln_b'], 'v_w_in': out['v_w_in'], 'v_conv_a_w': out['v_conv_a_w'], 'v_sinks': out['v_sinks'], 'v_conv_c_w': out['v_conv_c_w'], 'v_conv_c_b': out['v_conv_c_b'], 'v_gate_r_w': out['v_gate_r_w'], 'v_gate_r_b': out['v_gate_r_b'], 'v_gate_i_w': out['v_gate_i_w'], 'v_gate_i_b': out['v_gate_i_b'], 'v_rg_lambda': out['v_rg_lambda'], 'v_norm_a': out['v_norm_a'], 'v_norm_b': out['v_norm_b'], 'v_norm_c': out['v_norm_c'], 'v_w_out': out['v_w_out'], 'v_ln_g': out['v_ln_g'], 'v_ln_b': out['v_ln_b']}


def _loss(weights, diff, rest, loss_target):
    with _jax.named_scope("forward"):
        args = {**rest, TWIN_DIFF_INPUT: diff, **{k: w.astype(_WEIGHT_DTYPES[k]) for k, w in weights.items()}}
        y = _forward(args)
    with _jax.named_scope("loss_head"):
        err = _jnp.square(y.astype(_jnp.float32) - loss_target)
        return 0.5 * _jnp.sum(_jnp.mean(err, axis=-1)) if err.ndim else 0.5 * err


def _adamw(w, g, m, v):
    m = ADAM_B1 * m + (1.0 - ADAM_B1) * g
    v = ADAM_B2 * v + (1.0 - ADAM_B2) * _jnp.square(g)
    m_hat = m / (1.0 - ADAM_B1 ** ADAM_STEP)
    v_hat = v / (1.0 - ADAM_B2 ** ADAM_STEP)
    delta = -ADAM_LR * (m_hat / (_jnp.sqrt(v_hat) + ADAM_EPS) + ADAM_WD * w)
    return delta, m, v


def reference(x, w_in, conv_a_w, sinks, conv_c_w, conv_c_b, gate_r_w, gate_r_b, gate_i_w, gate_i_b, rg_lambda, norm_a, norm_b, norm_c, w_out, ln_g, ln_b, loss_target, m_w_in, m_conv_a_w, m_sinks, m_conv_c_w, m_conv_c_b, m_gate_r_w, m_gate_r_b, m_gate_i_w, m_gate_i_b, m_rg_lambda, m_norm_a, m_norm_b, m_norm_c, m_w_out, m_ln_g, m_ln_b, v_w_in, v_conv_a_w, v_sinks, v_conv_c_w, v_conv_c_b, v_gate_r_w, v_gate_r_b, v_gate_i_w, v_gate_i_b, v_rg_lambda, v_norm_a, v_norm_b, v_norm_c, v_w_out, v_ln_g, v_ln_b):
    given = dict(x=x, w_in=w_in, conv_a_w=conv_a_w, sinks=sinks, conv_c_w=conv_c_w, conv_c_b=conv_c_b, gate_r_w=gate_r_w, gate_r_b=gate_r_b, gate_i_w=gate_i_w, gate_i_b=gate_i_b, rg_lambda=rg_lambda, norm_a=norm_a, norm_b=norm_b, norm_c=norm_c, w_out=w_out, ln_g=ln_g, ln_b=ln_b, loss_target=loss_target, m_w_in=m_w_in, m_conv_a_w=m_conv_a_w, m_sinks=m_sinks, m_conv_c_w=m_conv_c_w, m_conv_c_b=m_conv_c_b, m_gate_r_w=m_gate_r_w, m_gate_r_b=m_gate_r_b, m_gate_i_w=m_gate_i_w, m_gate_i_b=m_gate_i_b, m_rg_lambda=m_rg_lambda, m_norm_a=m_norm_a, m_norm_b=m_norm_b, m_norm_c=m_norm_c, m_w_out=m_w_out, m_ln_g=m_ln_g, m_ln_b=m_ln_b, v_w_in=v_w_in, v_conv_a_w=v_conv_a_w, v_sinks=v_sinks, v_conv_c_w=v_conv_c_w, v_conv_c_b=v_conv_c_b, v_gate_r_w=v_gate_r_w, v_gate_r_b=v_gate_r_b, v_gate_i_w=v_gate_i_w, v_gate_i_b=v_gate_i_b, v_rg_lambda=v_rg_lambda, v_norm_a=v_norm_a, v_norm_b=v_norm_b, v_norm_c=v_norm_c, v_w_out=v_w_out, v_ln_g=v_ln_g, v_ln_b=v_ln_b)
    weights = {n: given[n] for n in TWIN_WEIGHTS}
    shared = {n: given[n] for n in SHARED_INPUTS}
    per_example = {n: given[n] for n in ['x']}
    grad_fn = _jax.value_and_grad(_loss, argnums=(0, 1))

    def one_microbatch(ex, loss_target):
        ex = dict(ex)
        diff = ex.pop(TWIN_DIFF_INPUT)
        return grad_fn(weights, diff, {**shared, **ex}, loss_target)

    if N_MICROBATCH == 1:
        loss, (grad_w, grad_x) = one_microbatch(per_example, given["loss_target"])
    else:
        def body(carry, xs):
            loss_sum, grad_sum = carry
            l_k, (gw_k, gx_k) = one_microbatch(xs[0], xs[1])
            with _jax.named_scope("update"):
                return (loss_sum + l_k, _jax.tree.map(_jnp.add, grad_sum, gw_k)), gx_k

        init = (_jnp.zeros((), _jnp.float32), _jax.tree.map(_jnp.zeros_like, weights))
        (loss, grad_w), grad_x = _jax.lax.scan(body, init, (per_example, given["loss_target"]))
    with _jax.named_scope("update"):
        delta_w, new_m, new_v = {}, {}, {}
        for n in TWIN_WEIGHTS:
            delta_w[n], new_m[n], new_v[n] = _adamw(weights[n], grad_w[n], given["m_" + n], given["v_" + n])
    return (loss, grad_x, *[grad_w[n] for n in TWIN_WEIGHTS], *[delta_w[n] for n in TWIN_WEIGHTS],
            *[new_m[n] for n in TWIN_WEIGHTS], *[new_v[n] for n in TWIN_WEIGHTS])
```

```python
import functools

import jax
import jax.numpy as jnp
from jax import lax
from jax.experimental import pallas as pl
from jax.experimental.pallas import tpu as pltpu

F32 = jnp.float32
BF16 = jnp.bfloat16
MESH = pl.DeviceIdType.MESH
ANY = pl.BlockSpec(memory_space=pl.ANY)

N_DEV = 8
LANES = 128
SUBLANES = 8
HEAD_DIM = 64
KV_GROUP = 8
ATT_BLOCK = 128
CONV_A = 3
CONV_C = 4
RG_C = 8.0
LN_EPS = 1e-5
RMS_EPS = 1e-6
NEG_INF = -1e30
ADAM_LR = 0.001
ADAM_B1 = 0.9
ADAM_B2 = 0.999
ADAM_EPS = 1e-08
ADAM_WD = 0.01
ADAM_STEP = 10
VMEM_LIMIT = 56 * 1024 * 1024

NN = ((1,), (0,))
NT = ((1,), (1,))
TN = ((0,), (0,))


def _pcall(body, **kw):
    return pl.pallas_call(body, **kw)


def _roll(x, shift, axis):
    return pltpu.roll(x, shift, axis)


def _params(sem=None, vmem=VMEM_LIMIT):
    return pltpu.CompilerParams(dimension_semantics=sem, vmem_limit_bytes=vmem)


def _dot(a, b, dims):
    return lax.dot_general(a.astype(BF16), b.astype(BF16), (dims, ((), ())), preferred_element_type=F32)


@jax.custom_vjp
def _mm(a, b):
    return _dot(a, b, NN)


def _mm_fwd(a, b):
    return _dot(a, b, NN), (a.astype(BF16), b.astype(BF16))


def _mm_bwd(res, g):
    a, b = res
    return _dot(g, b, NT), _dot(a, g, TN)


_mm.defvjp(_mm_fwd, _mm_bwd)


@jax.custom_vjp
def _mm_nt(a, b):
    return _dot(a, b, NT)


def _mm_nt_fwd(a, b):
    return _dot(a, b, NT), (a.astype(BF16), b.astype(BF16))


def _mm_nt_bwd(res, g):
    a, b = res
    return _dot(g, b, NN), _dot(g, a, TN)


_mm_nt.defvjp(_mm_nt_fwd, _mm_nt_bwd)


def _rows(shape):
    return lax.broadcasted_iota(jnp.int32, shape, 0)


@functools.partial(jax.custom_vjp, nondiff_argnums=(2,))
def _shift_halo(u, prev, k):
    r, c = u.shape
    fill = jnp.concatenate([_roll(prev, k, 0), jnp.zeros((r - SUBLANES, c), u.dtype)], axis=0)
    return jnp.where(_rows((r, c)) < k, fill, _roll(u, k, 0))


def _shift_halo_fwd(u, prev, k):
    return _shift_halo(u, prev, k), None


def _shift_halo_bwd(k, _, g):
    r, c = g.shape
    du = jnp.where(_rows((r, c)) < r - k, _roll(g, r - k, 0), 0.0)
    dprev = jnp.where(_rows((SUBLANES, c)) >= SUBLANES - k, _roll(g[0:SUBLANES], SUBLANES - k, 0), 0.0)
    return du, dprev


_shift_halo.defvjp(_shift_halo_fwd, _shift_halo_bwd)


@functools.partial(jax.custom_vjp, nondiff_argnums=(1, 2))
def _shift_fill(u, k, fill):
    return jnp.where(_rows(u.shape) < k, fill, _roll(u, k, 0))


def _shift_fill_fwd(u, k, fill):
    return _shift_fill(u, k, fill), None


def _shift_fill_bwd(k, fill, _, g):
    r = g.shape[0]
    return (jnp.where(_rows(g.shape) < r - k, _roll(g, r - k, 0), 0.0),)


_shift_fill.defvjp(_shift_fill_fwd, _shift_fill_bwd)


@jax.custom_vjp
def _swap_halves(x):
    return _roll(x, HEAD_DIM, 1)


_swap_halves.defvjp(lambda x: (_roll(x, HEAD_DIM, 1), None), lambda _, g: (_roll(g, HEAD_DIM, 1),))


@functools.partial(jax.custom_vjp, nondiff_argnums=(1,))
def _split_rows(x, n):
    r = x.shape[0] // n
    return tuple(x[i * r:(i + 1) * r] for i in range(n))


def _split_rows_fwd(x, n):
    return _split_rows(x, n), None


def _split_rows_bwd(n, _, gs):
    return (jnp.concatenate(list(gs), axis=0),)


_split_rows.defvjp(_split_rows_fwd, _split_rows_bwd)


def _sigmoid(x):
    return 1.0 / (1.0 + jnp.exp(-x))


def _silu(x):
    return x * _sigmoid(x)


def _log_sigmoid(x):
    return -(jnp.maximum(-x, 0.0) + jnp.log1p(jnp.exp(-jnp.abs(x))))


def _neg_expm1(x):
    series = x * (1 + x * (1 / 2) * (1 + x * (1 / 3) * (1 + x * (1 / 4) * (1 + x * (1 / 5) * (1 + x * (1 / 6) * (1 + x * (1 / 7)))))))
    return -jnp.where(jnp.abs(x) < 0.25, series, jnp.exp(x) - 1.0)


def _scan_block(a, u, s_prev):
    acc_a, acc_b = a, u
    d = 1
    while d < a.shape[0]:
        acc_b = acc_a * _shift_fill(acc_b, d, 0.0) + acc_b
        acc_a = acc_a * _shift_fill(acc_a, d, 1.0)
        d *= 2
    return acc_a * s_prev + acc_b


def _last_row(h):
    return jnp.sum(jnp.where(_rows(h.shape) == h.shape[0] - 1, h, 0.0), axis=0, keepdims=True)


def _branch_a(ab, ac, ax, ag, acp, axp, w0, w1, w2, na):
    u = ac * ax
    up = acp * axp
    ya = ab * (w2 * u + w1 * _shift_halo(u, up, 1) + w0 * _shift_halo(u, up, 2))
    ms = jnp.sum(ya * ya, axis=1, keepdims=True) * (1.0 / ya.shape[1])
    return ya * lax.rsqrt(ms + RMS_EPS) * na * _silu(ag)


def _branch_c(cx, cg, cxp, sp, wc, bc, wr, br, wi, bi, lam, nc):
    hs, lasts = [], []
    for j in range(len(cx)):
        xc = (wc[3][j] * cx[j] + wc[2][j] * _shift_halo(cx[j], cxp[j], 1) + wc[1][j] * _shift_halo(cx[j], cxp[j], 2)
              + wc[0][j] * _shift_halo(cx[j], cxp[j], 3) + bc[j])
        r = _sigmoid(_mm(xc, wr[j]) + br[j])
        i = _sigmoid(_mm(xc, wi[j]) + bi[j])
        log_a = RG_C * r * _log_sigmoid(lam[j])
        a = jnp.exp(log_a)
        u = jnp.sqrt(_neg_expm1(2.0 * log_a)) * (i * xc)
        h = _scan_block(a, u, sp[j])
        hs.append(h)
        lasts.append(_last_row(h))
    width = LANES * len(cx)
    ms = sum(jnp.sum(h * h, axis=1, keepdims=True) for h in hs) * (1.0 / width)
    inv = lax.rsqrt(ms + RMS_EPS)
    return [hs[j] * inv * nc[j] * _silu(cg[j]) for j in range(len(cx))], lasts


def _branch_b(q, k, v, kp, vp, bg, snk, nb, first_thr):
    rows = ATT_BLOCK
    n_kv = 2 * len(k)
    gr = KV_GROUP * rows
    qi = _rows((gr, 2 * rows)) & (rows - 1)
    kj = lax.broadcasted_iota(jnp.int32, (gr, 2 * rows), 1)
    dist = qi + rows - kj
    valid = (dist >= 0) & (dist < rows) & (kj >= first_thr)
    lane0 = kj == 0
    upper = lax.broadcasted_iota(jnp.int32, (2 * rows, LANES), 1) >= HEAD_DIM
    heads = [None] * (n_kv * KV_GROUP)
    for g in range(n_kv):
        half = g % 2
        keep = upper if half else jnp.logical_not(upper)
        kc = jnp.where(keep, jnp.concatenate([kp[g // 2], k[g // 2]], axis=0), 0.0)
        vc = jnp.where(keep, jnp.concatenate([vp[g // 2], v[g // 2]], axis=0), 0.0)
        hs = range(g * KV_GROUP, (g + 1) * KV_GROUP)
        qg = jnp.concatenate([q[h // 2] if h % 2 == half else _swap_halves(q[h // 2]) for h in hs], axis=0)
        s = _mm_nt(qg, kc) * (HEAD_DIM ** -0.5)
        s = jnp.where(valid, s, NEG_INF)
        sink = jnp.concatenate([jnp.broadcast_to(snk[h], (rows, 2 * rows)) for h in hs], axis=0)
        m = lax.stop_gradient(jnp.maximum(jnp.max(s, axis=1, keepdims=True), sink))
        p = jnp.exp(s - m)
        e_sink = jnp.sum(jnp.where(lane0, jnp.exp(sink - m), 0.0), axis=1, keepdims=True)
        p = p / (jnp.sum(p, axis=1, keepdims=True) + e_sink)
        o = _split_rows(_mm(p, vc), KV_GROUP)
        for i, h in enumerate(hs):
            heads[h] = o[i] if h % 2 == half else _swap_halves(o[i])
    yb = [heads[2 * j] + heads[2 * j + 1] for j in range(len(q))]
    width = LANES * len(q)
    ms = sum(jnp.sum(y * y, axis=1, keepdims=True) for y in yb) * (1.0 / width)
    inv = lax.rsqrt(ms + RMS_EPS)
    return [yb[j] * inv * nb[j] * _silu(bg[j]) for j in range(len(q))]


class _Dims:
    def __init__(self, d_model, n_rg_heads):
        self.d = d_model
        self.wa = d_model // 4
        self.wb = d_model // 2
        self.wc = d_model // 4
        self.kvw = self.wb // KV_GROUP
        self.nq = self.wb // HEAD_DIM
        self.in_w = 4 * self.wa + 2 * self.wb + 2 * self.kvw + 2 * self.wc
        self.o_q = 4 * self.wa
        self.o_k = self.o_q + self.wb
        self.o_v = self.o_k + self.kvw
        self.o_bg = self.o_v + self.kvw
        self.o_cx = self.o_bg + self.wb
        self.o_cg = self.o_cx + self.wc
        self.nh = n_rg_heads
        assert self.wc // n_rg_heads == LANES and self.kvw % LANES == 0
        assert self.o_k % self.kvw == 0 and self.o_cx % (self.wc // 2) == 0


def _chunks(ref, rows, off, width):
    return [ref[rows, off + LANES * j: off + LANES * (j + 1)] for j in range(width // LANES)]


def _read_params(dm, caw, ccw, ccb, grw, grb, giw, gib, lam, na, nb, nc, snk):
    row = slice(0, 1)
    return dict(
        wa=[caw[k:k + 1, :] for k in range(CONV_A)], na=na[...],
        wc=[_chunks(ccw, slice(k, k + 1), 0, dm.wc) for k in range(CONV_C)], bc=_chunks(ccb, row, 0, dm.wc),
        wr=[grw[j] for j in range(dm.nh)], br=_chunks(grb, row, 0, dm.wc),
        wi=[giw[j] for j in range(dm.nh)], bi=_chunks(gib, row, 0, dm.wc),
        lam=_chunks(lam, row, 0, dm.wc), nc=_chunks(nc, row, 0, dm.wc),
        nb=_chunks(nb, row, 0, dm.wb), snk=[snk[h:h + 1, :] for h in range(dm.nq)])


def _param_specs(dm):
    shapes = [(CONV_A, dm.wa), (CONV_C, dm.wc), (1, dm.wc), (dm.nh, LANES, LANES), (1, dm.wc), (dm.nh, LANES, LANES),
              (1, dm.wc), (1, dm.wc), (1, dm.wa), (1, dm.wb), (1, dm.wc), (dm.nq, 2 * ATT_BLOCK)]
    specs = [pl.BlockSpec(s, (lambda b, n, _r=len(s): (0,) * _r)) for s in shapes]
    return shapes, specs


def _mixer_fwd(dm, h, prm, b_loc, nblk):
    t = h.shape[0]
    r = ATT_BLOCK
    tail = slice(r - SUBLANES, r)

    def body(h_ref, caw, ccw, ccb, grw, grb, giw, gib, lam, na, nb, nc, snk, mix_ref, sst_ref, kp, vp, acp, axp, cxp, sp):
        n = pl.program_id(1)

        @pl.when(n == 0)
        def _():
            for ref in (kp, vp, acp, axp, cxp, sp):
                ref[...] = jnp.zeros(ref.shape, ref.dtype)

        p = _read_params(dm, caw, ccw, ccb, grw, grb, giw, gib, lam, na, nb, nc, snk)
        full = slice(None)
        mix_a = _branch_a(h_ref[:, 0:dm.wa], h_ref[:, dm.wa:2 * dm.wa], h_ref[:, 2 * dm.wa:3 * dm.wa],
                          h_ref[:, 3 * dm.wa:4 * dm.wa], acp[...], axp[...], p["wa"][0], p["wa"][1], p["wa"][2], p["na"])
        mix_ref[:, 0:dm.wa] = mix_a.astype(BF16)
        first_thr = jnp.where(n == 0, ATT_BLOCK, 0)
        mix_b = _branch_b(_chunks(h_ref, full, dm.o_q, dm.wb), _chunks(h_ref, full, dm.o_k, dm.kvw),
                          _chunks(h_ref, full, dm.o_v, dm.kvw), _chunks(kp, full, 0, dm.kvw), _chunks(vp, full, 0, dm.kvw),
                          _chunks(h_ref, full, dm.o_bg, dm.wb), p["snk"], p["nb"], first_thr)
        for j, mb in enumerate(mix_b):
            mix_ref[:, dm.wa + LANES * j: dm.wa + LANES * (j + 1)] = mb.astype(BF16)
        sst_ref[0] = sp[...]
        mix_c, lasts = _branch_c(_chunks(h_ref, full, dm.o_cx, dm.wc), _chunks(h_ref, full, dm.o_cg, dm.wc),
                                 _chunks(cxp, full, 0, dm.wc), _chunks(sp, slice(0, 1), 0, dm.wc), p["wc"], p["bc"],
                                 p["wr"], p["br"], p["wi"], p["bi"], p["lam"], p["nc"])
        o_c = dm.wa + dm.wb
        for j, mc in enumerate(mix_c):
            mix_ref[:, o_c + LANES * j: o_c + LANES * (j + 1)] = mc.astype(BF16)
            sp[:, LANES * j: LANES * (j + 1)] = jnp.broadcast_to(lasts[j], (SUBLANES, LANES))
        kp[...] = h_ref[:, dm.o_k:dm.o_k + dm.kvw]
        vp[...] = h_ref[:, dm.o_v:dm.o_v + dm.kvw]
        acp[...] = h_ref[tail, dm.wa:2 * dm.wa]
        axp[...] = h_ref[tail, 2 * dm.wa:3 * dm.wa]
        cxp[...] = h_ref[tail, dm.o_cx:dm.o_cx + dm.wc]

    _, pspecs = _param_specs(dm)
    return _pcall(
        body, name="mixer_fwd", grid=(b_loc, nblk),
        in_specs=[pl.BlockSpec((r, dm.in_w), lambda b, n: (b * nblk + n, 0))] + pspecs,
        out_specs=[pl.BlockSpec((r, dm.d), lambda b, n: (b * nblk + n, 0)),
                   pl.BlockSpec((1, SUBLANES, dm.wc), lambda b, n: (b * nblk + n, 0, 0))],
        out_shape=[jax.ShapeDtypeStruct((t, dm.d), BF16), jax.ShapeDtypeStruct((b_loc * nblk, SUBLANES, dm.wc), F32)],
        scratch_shapes=[pltpu.VMEM((r, dm.kvw), F32), pltpu.VMEM((r, dm.kvw), F32), pltpu.VMEM((SUBLANES, dm.wa), F32),
                        pltpu.VMEM((SUBLANES, dm.wa), F32), pltpu.VMEM((SUBLANES, dm.wc), F32),
                        pltpu.VMEM((SUBLANES, dm.wc), F32)],
        compiler_params=_params(("arbitrary", "arbitrary")),
    )(h, *prm)


def _mixer_bwd(dm, h, sst, dmix, prm, b_loc, nblk):
    t = h.shape[0]
    r = ATT_BLOCK
    rb8 = r // SUBLANES
    n_small = 12

    def body(h_ref, kp_ref, vp_ref, acp_ref, axp_ref, cxp0_ref, cxp1_ref, sst_ref, dmix_ref,
             caw, ccw, ccb, grw, grb, giw, gib, lam, na, nb, nc, snk,
             dh_ref, d_caw, d_ccw, d_ccb, d_grw, d_grb, d_giw, d_gib, d_lam, d_na, d_nb, d_nc, d_snk,
             dkp, dvp, dacp, daxp, dcxp, dsp):
        step = pl.program_id(1)
        n = nblk - 1 - step

        @pl.when(step == 0)
        def _():
            for ref in (dkp, dvp, dacp, daxp, dcxp, dsp):
                ref[...] = jnp.zeros(ref.shape, ref.dtype)

        @pl.when((step == 0) & (pl.program_id(0) == 0))
        def _():
            for ref in (d_caw, d_ccw, d_ccb, d_grw, d_grb, d_giw, d_gib, d_lam, d_na, d_nb, d_nc, d_snk):
                ref[...] = jnp.zeros(ref.shape, ref.dtype)

        p = _read_params(dm, caw, ccw, ccb, grw, grb, giw, gib, lam, na, nb, nc, snk)
        has_prev = jnp.where(n > 0, 1.0, 0.0)
        full = slice(None)
        pad = jnp.zeros((r - SUBLANES, LANES), F32)

        def with_tail(own, carry):
            z = jnp.zeros((r - SUBLANES, own.shape[1]), F32)
            return own + jnp.concatenate([z, carry], axis=0)

        a_in = (h_ref[:, 0:dm.wa], h_ref[:, dm.wa:2 * dm.wa], h_ref[:, 2 * dm.wa:3 * dm.wa], h_ref[:, 3 * dm.wa:4 * dm.wa],
                acp_ref[...] * has_prev, axp_ref[...] * has_prev, p["wa"][0], p["wa"][1], p["wa"][2], p["na"])
        _, vjp_a = jax.vjp(_branch_a, *a_in)
        g_ab, g_ac, g_ax, g_ag, g_acp, g_axp, g_w0, g_w1, g_w2, g_na = vjp_a(dmix_ref[:, 0:dm.wa])
        dh_ref[:, 0:dm.wa] = g_ab.astype(BF16)
        dh_ref[:, dm.wa:2 * dm.wa] = with_tail(g_ac, dacp[...]).astype(BF16)
        dh_ref[:, 2 * dm.wa:3 * dm.wa] = with_tail(g_ax, daxp[...]).astype(BF16)
        dh_ref[:, 3 * dm.wa:4 * dm.wa] = g_ag.astype(BF16)
        dacp[...] = g_acp
        daxp[...] = g_axp
        for k, gw in enumerate((g_w0, g_w1, g_w2)):
            d_caw[k:k + 1, :] += gw
        d_na[...] += g_na

        first_thr = jnp.where(n == 0, ATT_BLOCK, 0)
        kp_in = [c * has_prev for c in _chunks(kp_ref, full, 0, dm.kvw)]
        vp_in = [c * has_prev for c in _chunks(vp_ref, full, 0, dm.kvw)]
        b_in = (_chunks(h_ref, full, dm.o_q, dm.wb), _chunks(h_ref, full, dm.o_k, dm.kvw), _chunks(h_ref, full, dm.o_v, dm.kvw),
                kp_in, vp_in, _chunks(h_ref, full, dm.o_bg, dm.wb), p["snk"], p["nb"])
        _, vjp_b = jax.vjp(lambda *a: _branch_b(*a, first_thr), *b_in)
        g_q, g_k, g_v, g_kp, g_vp, g_bg, g_snk, g_nb = vjp_b(_chunks(dmix_ref, full, dm.wa, dm.wb))
        for j in range(len(g_q)):
            dh_ref[:, dm.o_q + LANES * j: dm.o_q + LANES * (j + 1)] = g_q[j].astype(BF16)
            dh_ref[:, dm.o_bg + LANES * j: dm.o_bg + LANES * (j + 1)] = g_bg[j].astype(BF16)
            d_nb[:, LANES * j: LANES * (j + 1)] += g_nb[j]
        for j in range(len(g_k)):
            cols = slice(LANES * j, LANES * (j + 1))
            dh_ref[:, dm.o_k + LANES * j: dm.o_k + LANES * (j + 1)] = (g_k[j] + dkp[:, cols]).astype(BF16)
            dh_ref[:, dm.o_v + LANES * j: dm.o_v + LANES * (j + 1)] = (g_v[j] + dvp[:, cols]).astype(BF16)
            dkp[:, cols] = g_kp[j]
            dvp[:, cols] = g_vp[j]
        for hd in range(dm.nq):
            d_snk[hd:hd + 1, :] += g_snk[hd]

        half_c = dm.wc // 2
        cxp_in = ([c * has_prev for c in _chunks(cxp0_ref, full, 0, half_c)]
                  + [c * has_prev for c in _chunks(cxp1_ref, full, 0, half_c)])
        c_in = (_chunks(h_ref, full, dm.o_cx, dm.wc), _chunks(h_ref, full, dm.o_cg, dm.wc), cxp_in,
                [sst_ref[0, 0:1, LANES * j: LANES * (j + 1)] for j in range(dm.nh)], p["wc"], p["bc"], p["wr"], p["br"], p["wi"], p["bi"],
                p["lam"], p["nc"])
        _, vjp_c = jax.vjp(_branch_c, *c_in)
        ct_last = [dsp[0:1, LANES * j: LANES * (j + 1)] for j in range(dm.nh)]
        g_cx, g_cg, g_cxp, g_sp, g_wc, g_bc, g_wr, g_br, g_wi, g_bi, g_lam, g_nc = vjp_c(
            (_chunks(dmix_ref, full, dm.wa + dm.wb, dm.wc), ct_last))
        for j in range(dm.nh):
            cols = slice(LANES * j, LANES * (j + 1))
            tot = g_cx[j] + jnp.concatenate([pad, dcxp[:, cols]], axis=0)
            dh_ref[:, dm.o_cx + LANES * j: dm.o_cx + LANES * (j + 1)] = tot.astype(BF16)
            dh_ref[:, dm.o_cg + LANES * j: dm.o_cg + LANES * (j + 1)] = g_cg[j].astype(BF16)
            dcxp[:, cols] = g_cxp[j]
            dsp[:, cols] = jnp.broadcast_to(g_sp[j], (SUBLANES, LANES))
            for k in range(CONV_C):
                d_ccw[k:k + 1, cols] += g_wc[k][j]
            d_ccb[:, cols] += g_bc[j]
            d_grw[j] += g_wr[j]
            d_grb[:, cols] += g_br[j]
            d_giw[j] += g_wi[j]
            d_gib[:, cols] += g_bi[j]
            d_lam[:, cols] += g_lam[j]
            d_nc[:, cols] += g_nc[j]

    def blk(b, s):
        return b * nblk + (nblk - 1 - s)

    def prev_rows8(b, s):
        return jnp.maximum(blk(b, s) * rb8 - 1, 0)

    pshapes, pspecs = _param_specs(dm)
    half_c = dm.wc // 2
    in_specs = [
        pl.BlockSpec((r, dm.in_w), lambda b, s: (blk(b, s), 0)),
        pl.BlockSpec((r, dm.kvw), lambda b, s: (jnp.maximum(blk(b, s) - 1, 0), dm.o_k // dm.kvw)),
        pl.BlockSpec((r, dm.kvw), lambda b, s: (jnp.maximum(blk(b, s) - 1, 0), dm.o_v // dm.kvw)),
        pl.BlockSpec((SUBLANES, dm.wa), lambda b, s: (prev_rows8(b, s), 1)),
        pl.BlockSpec((SUBLANES, dm.wa), lambda b, s: (prev_rows8(b, s), 2)),
        pl.BlockSpec((SUBLANES, half_c), lambda b, s: (prev_rows8(b, s), dm.o_cx // half_c)),
        pl.BlockSpec((SUBLANES, half_c), lambda b, s: (prev_rows8(b, s), dm.o_cx // half_c + 1)),
        pl.BlockSpec((1, SUBLANES, dm.wc), lambda b, s: (blk(b, s), 0, 0)),
        pl.BlockSpec((r, dm.d), lambda b, s: (blk(b, s), 0)),
    ] + pspecs
    outs = _pcall(
        body, name="mixer_bwd", grid=(b_loc, nblk), in_specs=in_specs,
        out_specs=[pl.BlockSpec((r, dm.in_w), lambda b, s: (blk(b, s), 0))] + pspecs,
        out_shape=[jax.ShapeDtypeStruct((t, dm.in_w), BF16)] + [jax.ShapeDtypeStruct(s, F32) for s in pshapes],
        scratch_shapes=[pltpu.VMEM((r, dm.kvw), F32), pltpu.VMEM((r, dm.kvw), F32), pltpu.VMEM((SUBLANES, dm.wa), F32),
                        pltpu.VMEM((SUBLANES, dm.wa), F32), pltpu.VMEM((SUBLANES, dm.wc), F32),
                        pltpu.VMEM((SUBLANES, dm.wc), F32)],
        compiler_params=_params(("arbitrary", "arbitrary")),
    )(h, h, h, h, h, h, h, sst, dmix, *prm)
    assert len(outs) == 1 + n_small
    return outs[0], outs[1:]


def _matmul(a, b, *, dims, tm, tn, tk, out_dtype, name, addend=None, alpha=None):
    if dims == TN:
        (k_dim, m), n_dim = a.shape, b.shape[1]
        a_spec = pl.BlockSpec((tk, tm), lambda i, j, k: (k, i))
    else:
        (m, k_dim), n_dim = a.shape, (b.shape[0] if dims == NT else b.shape[1])
        a_spec = pl.BlockSpec((tm, tk), lambda i, j, k: (i, k))
    b_spec = pl.BlockSpec((tn, tk), lambda i, j, k: (j, k)) if dims == NT else pl.BlockSpec((tk, tn), lambda i, j, k: (k, j))
    assert m % tm == 0 and n_dim % tn == 0 and k_dim % tk == 0, (a.shape, b.shape, tm, tn, tk)
    nk = k_dim // tk
    o_spec = pl.BlockSpec((tm, tn), lambda i, j, k: (i, j))

    def body(*refs):
        a_ref, b_ref = refs[0], refs[1]
        add_ref = refs[2] if addend is not None else None
        o_ref = refs[3 if addend is not None else 2]
        acc_ref = refs[-1]
        k = pl.program_id(2)
        part = lax.dot_general(a_ref[...], b_ref[...], (dims, ((), ())), preferred_element_type=F32)

        def finish(acc):
            if add_ref is not None:
                acc = acc + alpha * add_ref[...]
            o_ref[...] = acc.astype(out_dtype)

        if nk == 1:
            finish(part)
        else:
            @pl.when(k == 0)
            def _():
                acc_ref[...] = part

            @pl.when((k > 0) & (k < nk - 1))
            def _():
                acc_ref[...] += part

            @pl.when(k == nk - 1)
            def _():
                finish(acc_ref[...] + part)

    ins = [a, b] + ([addend] if addend is not None else [])
    in_specs = [a_spec, b_spec] + ([o_spec] if addend is not None else [])
    return _pcall(
        body, name=name, grid=(m // tm, n_dim // tn, nk), in_specs=in_specs, out_specs=o_spec,
        out_shape=jax.ShapeDtypeStruct((m, n_dim), out_dtype),
        scratch_shapes=[pltpu.VMEM((tm, tn) if nk > 1 else (SUBLANES, LANES), F32)],
        compiler_params=_params(("parallel", "parallel", "arbitrary")),
    )(*ins)


def _tile(n, want, quantum=LANES):
    if n <= want:
        return n
    for cand in range(want - want % quantum, 0, -quantum):
        if n % cand == 0:
            return cand
    return n


def _row_tile(t, d):
    return _tile(t, max(2 * SUBLANES, (1 << 19) // d), 2 * SUBLANES)


def _ln_fwd(z, g, b):
    t, d = z.shape
    tr = _row_tile(t, d)

    def body(z_ref, g_ref, b_ref, y_ref, yb_ref):
        zz = z_ref[...]
        mu = jnp.mean(zz, axis=1, keepdims=True)
        zc = zz - mu
        var = jnp.mean(zc * zc, axis=1, keepdims=True)
        y = zc * lax.rsqrt(var + LN_EPS) * g_ref[...] + b_ref[...]
        y_ref[...] = y
        yb_ref[...] = y.astype(BF16)

    row = pl.BlockSpec((tr, d), lambda i: (i, 0))
    vec = pl.BlockSpec((1, d), lambda i: (0, 0))
    return _pcall(body, name="ln_fwd", grid=(t // tr,), in_specs=[row, vec, vec], out_specs=[row, row],
                  out_shape=[jax.ShapeDtypeStruct((t, d), F32), jax.ShapeDtypeStruct((t, d), BF16)],
                  compiler_params=_params(("parallel",)))(z, g, b)


def _ln_bwd(dy, z, g):
    t, d = z.shape
    tr = _row_tile(t, d)

    def body(dy_ref, z_ref, g_ref, dz_ref, dzb_ref, dg_ref, db_ref):
        @pl.when(pl.program_id(0) == 0)
        def _():
            dg_ref[...] = jnp.zeros(dg_ref.shape, F32)
            db_ref[...] = jnp.zeros(db_ref.shape, F32)

        zz = z_ref[...]
        dyy = dy_ref[...]
        mu = jnp.mean(zz, axis=1, keepdims=True)
        zc = zz - mu
        rstd = lax.rsqrt(jnp.mean(zc * zc, axis=1, keepdims=True) + LN_EPS)
        xhat = zc * rstd
        dyg = dyy * g_ref[...]
        dz = rstd * (dyg - jnp.mean(dyg, axis=1, keepdims=True) - xhat * jnp.mean(dyg * xhat, axis=1, keepdims=True))
        dz_ref[...] = dz
        dzb_ref[...] = dz.astype(BF16)
        dg_ref[...] += jnp.sum(dyy * xhat, axis=0, keepdims=True)
        db_ref[...] += jnp.sum(dyy, axis=0, keepdims=True)

    row = pl.BlockSpec((tr, d), lambda i: (i, 0))
    vec = pl.BlockSpec((1, d), lambda i: (0, 0))
    return _pcall(body, name="ln_bwd", grid=(t // tr,), in_specs=[row, row, vec], out_specs=[row, row, vec, vec],
                  out_shape=[jax.ShapeDtypeStruct((t, d), F32), jax.ShapeDtypeStruct((t, d), BF16),
                             jax.ShapeDtypeStruct((1, d), F32), jax.ShapeDtypeStruct((1, d), F32)],
                  compiler_params=_params(("arbitrary",)))(dy, z, g)


def _loss_head(y, target):
    t, d = y.shape
    tr = _row_tile(t, d)

    def body(y_ref, t_ref, dy_ref, loss_ref):
        @pl.when(pl.program_id(0) == 0)
        def _():
            loss_ref[...] = jnp.zeros(loss_ref.shape, F32)

        err = y_ref[...] - t_ref[...]
        dy_ref[...] = err * (1.0 / d)
        per_token = jnp.sum(err * err, axis=1, keepdims=True) * (1.0 / d)
        loss_ref[...] += 0.5 * jnp.sum(per_token, axis=0, keepdims=True)

    row = pl.BlockSpec((tr, d), lambda i: (i, 0))
    one = pl.BlockSpec((1, 1), lambda i: (0, 0))
    return _pcall(body, name="loss_head", grid=(t // tr,), in_specs=[row, row], out_specs=[row, one],
                  out_shape=[jax.ShapeDtypeStruct((t, d), F32), jax.ShapeDtypeStruct((1, 1), F32)],
                  compiler_params=_params(("arbitrary",)))(y, target)


def _sum_parts(parts, counts, out_dtype, name):
    r, c = parts[0].shape[1:]
    tr = _row_tile(r, c)

    def body(*refs):
        o_ref = refs[-1]
        acc = None
        for ref, cnt in zip(refs[:-1], counts):
            for i in range(cnt):
                term = ref[i].astype(F32)
                acc = term if acc is None else acc + term
        o_ref[...] = acc.astype(out_dtype)

    return _pcall(body, name=name, grid=(r // tr,),
                  in_specs=[pl.BlockSpec((cnt, tr, c), lambda i: (0, i, 0)) for cnt in counts],
                  out_specs=pl.BlockSpec((tr, c), lambda i: (i, 0)), out_shape=jax.ShapeDtypeStruct((r, c), out_dtype),
                  compiler_params=_params(("parallel",)))(*parts)


def _pair_add(a, b, name):
    p, r, c = a.shape
    tr = _row_tile(r, c)

    def body(a_ref, b_ref, o_ref):
        o_ref[...] = (a_ref[...].astype(F32) + b_ref[...].astype(F32)).astype(BF16)

    spec = pl.BlockSpec((1, tr, c), lambda q, i: (q, i, 0))
    return _pcall(body, name=name, grid=(p, r // tr), in_specs=[spec, spec], out_specs=spec,
                  out_shape=jax.ShapeDtypeStruct((p, r, c), BF16), compiler_params=_params(("parallel", "parallel")))(a, b)


def _adamw(w, g_parts, m, v, name):
    r, c = w.shape
    n_parts = g_parts.shape[0]
    tr = _row_tile(r, c) if r % SUBLANES == 0 else r

    def body(w_ref, g_ref, m_ref, v_ref, go_ref, d_ref, mo_ref, vo_ref):
        g = g_ref[0].astype(F32)
        for i in range(1, n_parts):
            g = g + g_ref[i].astype(F32)
        m_new = ADAM_B1 * m_ref[...] + (1.0 - ADAM_B1) * g
        v_new = ADAM_B2 * v_ref[...] + (1.0 - ADAM_B2) * (g * g)
        m_hat = m_new / (1.0 - ADAM_B1 ** ADAM_STEP)
        v_hat = v_new / (1.0 - ADAM_B2 ** ADAM_STEP)
        go_ref[...] = g
        d_ref[...] = -ADAM_LR * (m_hat / (jnp.sqrt(v_hat) + ADAM_EPS) + ADAM_WD * w_ref[...])
        mo_ref[...] = m_new
        vo_ref[...] = v_new

    spec = pl.BlockSpec((tr, c), lambda i: (i, 0))
    shape = jax.ShapeDtypeStruct((r, c), F32)
    return _pcall(body, name=name, grid=(r // tr,),
                  in_specs=[spec, pl.BlockSpec((n_parts, tr, c), lambda i: (0, i, 0)), spec, spec],
                  out_specs=[spec] * 4, out_shape=[shape] * 4, compiler_params=_params(("parallel",)))(w, g_parts, m, v)


def _me():
    return lax.axis_index("x"), lax.axis_index("y"), lax.axis_index("c")


def _dev(px, py, pc):
    return 4 * px + 2 * py + pc


def _remote(src, dst, send_sems, recv_sems, k, to):
    return pltpu.make_async_remote_copy(src_ref=src, dst_ref=dst, send_sem=send_sems.at[k], recv_sem=recv_sems.at[k],
                                        device_id=to, device_id_type=MESH)


def _all_gather(arrs, name):
    n = len(arrs)

    def body(*refs):
        ins, outs = refs[:n], refs[n:2 * n]
        send_sems, recv_sems, local_sems = refs[2 * n:]
        x, y, c = _me()
        me, sibling = (x, y, c), (x, y, 1 - c)
        chips = [(1 - x, y), (x, 1 - y), (1 - x, 1 - y)]
        pending = []
        for a in range(n):
            mine = pltpu.make_async_copy(ins[a], outs[a].at[_dev(*me)], local_sems.at[a])
            mine.start()
            pending.append(mine)
        sends = []
        for a in range(n):
            dst = outs[a].at[_dev(*me)]
            sends.append(_remote(ins[a], dst, send_sems, recv_sems, 7 * a, sibling))
            sends += [_remote(ins[a], dst, send_sems, recv_sems, 7 * a + 1 + j, (*chip, c)) for j, chip in enumerate(chips)]
        for cp in sends:
            cp.start()
        for a in range(n):
            for j, chip in enumerate(chips):
                blk = outs[a].at[_dev(*chip, c)]
                _remote(blk, blk, send_sems, recv_sems, 7 * a + 1 + j, me).wait_recv()
                fwd = _remote(blk, blk, send_sems, recv_sems, 7 * a + 4 + j, sibling)
                fwd.start()
                sends.append(fwd)
        for a in range(n):
            blk = outs[a].at[_dev(*sibling)]
            _remote(blk, blk, send_sems, recv_sems, 7 * a, me).wait_recv()
            for j, chip in enumerate(chips):
                blk = outs[a].at[_dev(*chip, 1 - c)]
                _remote(blk, blk, send_sems, recv_sems, 7 * a + 4 + j, me).wait_recv()
        for cp in sends:
            cp.wait_send()
        for cp in pending:
            cp.wait()

    return _pcall(
        body, name=name, in_specs=[ANY] * n, out_specs=[ANY] * n,
        out_shape=[jax.ShapeDtypeStruct((N_DEV,) + a.shape, a.dtype) for a in arrs],
        scratch_shapes=[pltpu.SemaphoreType.DMA((7 * n,)), pltpu.SemaphoreType.DMA((7 * n,)), pltpu.SemaphoreType.DMA((n,))],
    )(*arrs)


def _relations(x, y):
    return [(x, y), (1 - x, y), (x, 1 - y), (1 - x, 1 - y)]


def _rs_within_chip(parts, name):
    n = len(parts)

    def body(*refs):
        ins, kept, got = refs[:n], refs[n:2 * n], refs[2 * n:3 * n]
        send_sems, recv_sems, local_sems = refs[3 * n:]
        x, y, c = _me()
        sibling = (x, y, 1 - c)
        copies, locals_ = [], []
        for a in range(n):
            for r, chip in enumerate(_relations(x, y)):
                cp = _remote(ins[a].at[_dev(*chip, 1 - c)], got[a].at[r], send_sems, recv_sems, 4 * a + r, sibling)
                cp.start()
                copies.append(cp)
                lc = pltpu.make_async_copy(ins[a].at[_dev(*chip, c)], kept[a].at[r], local_sems.at[4 * a + r])
                lc.start()
                locals_.append(lc)
        for cp in copies:
            cp.wait()
        for lc in locals_:
            lc.wait()

    shapes = [jax.ShapeDtypeStruct((4,) + p.shape[1:], p.dtype) for p in parts]
    outs = _pcall(
        body, name=name, in_specs=[ANY] * n, out_specs=[ANY] * (2 * n), out_shape=shapes + shapes,
        scratch_shapes=[pltpu.SemaphoreType.DMA((4 * n,)), pltpu.SemaphoreType.DMA((4 * n,)), pltpu.SemaphoreType.DMA((4 * n,))],
    )(*parts)
    return outs[:n], outs[n:]


def _rs_across_chips(sums, name):
    n = len(sums)

    def body(*refs):
        ins, outs = refs[:n], refs[n:2 * n]
        send_sems, recv_sems = refs[2 * n:]
        x, y, c = _me()
        copies = []
        for a in range(n):
            for r, chip in enumerate(_relations(x, y)):
                if r == 0:
                    continue
                cp = _remote(ins[a].at[r], outs[a].at[r - 1], send_sems, recv_sems, 3 * a + r - 1, (*chip, c))
                cp.start()
                copies.append(cp)
        for cp in copies:
            cp.wait()

    return _pcall(
        body, name=name, in_specs=[ANY] * n, out_specs=[ANY] * n,
        out_shape=[jax.ShapeDtypeStruct((3,) + s.shape[1:], s.dtype) for s in sums],
        scratch_shapes=[pltpu.SemaphoreType.DMA((3 * n,)), pltpu.SemaphoreType.DMA((3 * n,))],
    )(*sums)


def _reduce_scatter(parts, tag):
    kept, got = _rs_within_chip(parts, "rs_d2d_" + tag)
    sums = [_pair_add(k, g, "rs_add_%s_%d" % (tag, i)) for i, (k, g) in enumerate(zip(kept, got))]
    far = _rs_across_chips(sums, "rs_ici_" + tag)
    return [_sum_parts([s, f], [1, 3], F32, "rs_sum_%s_%d" % (tag, i)) for i, (s, f) in enumerate(zip(sums, far))]


SMALL = ("conv_a_w", "conv_c_w", "conv_c_b", "gate_r_w", "gate_r_b", "gate_i_w", "gate_i_b", "rg_lambda",
         "norm_a", "norm_b", "norm_c", "sinks", "ln_g", "ln_b")
PACK_COLS = 1024


def _pack(arrs):
    flat = jnp.concatenate([a.reshape(-1) for a in arrs])
    pad = (-flat.shape[0]) % (SUBLANES * PACK_COLS)
    return jnp.pad(flat, (0, pad)).reshape(-1, PACK_COLS)


def _unpack(packed, shapes):
    flat = packed.reshape(-1)
    out, off = [], 0
    for s in shapes:
        size = 1
        for dim in s:
            size *= dim
        out.append(flat[off:off + size].reshape(s))
        off += size
    return out


def kernel(x, w_in, conv_a_w, sinks, conv_c_w, conv_c_b, gate_r_w, gate_r_b, gate_i_w, gate_i_b, rg_lambda, norm_a, norm_b, norm_c, w_out, ln_g, ln_b, loss_target, m_w_in, m_conv_a_w, m_sinks, m_conv_c_w, m_conv_c_b, m_gate_r_w, m_gate_r_b, m_gate_i_w, m_gate_i_b, m_rg_lambda, m_norm_a, m_norm_b, m_norm_c, m_w_out, m_ln_g, m_ln_b, v_w_in, v_conv_a_w, v_sinks, v_conv_c_w, v_conv_c_b, v_gate_r_w, v_gate_r_b, v_gate_i_w, v_gate_i_b, v_rg_lambda, v_norm_a, v_norm_b, v_norm_c, v_w_out, v_ln_g, v_ln_b):
    b_loc, seq, d = x.shape
    depth = w_in.shape[0]
    dm = _Dims(d, gate_r_w.shape[1])
    t = b_loc * seq
    nblk = seq // ATT_BLOCK
    alpha = (2.0 * depth) ** 0.25
    ch = dm.wa // N_DEV
    dev = _dev(*_me())

    wt_shard = jnp.swapaxes(w_in, 1, 2).astype(BF16)
    wo_shard = w_out.astype(BF16)
    conv_shard = jnp.concatenate([conv_a_w.reshape(depth * CONV_A, ch), conv_c_w.reshape(depth * CONV_C, ch)], axis=0)
    conv_all = _all_gather([jnp.pad(conv_shard, ((0, (-conv_shard.shape[0]) % SUBLANES), (0, 0)))], "ag_conv")[0]
    conv_all = jnp.swapaxes(conv_all, 0, 1).reshape(conv_all.shape[1], dm.wa)
    conv_a_full = conv_all[:depth * CONV_A].reshape(depth, CONV_A, dm.wa)
    conv_c_full = conv_all[depth * CONV_A:depth * (CONV_A + CONV_C)].reshape(depth, CONV_C, dm.wc)
    sinks_wide = jnp.broadcast_to(sinks[:, :, None], (depth, dm.nq, 2 * ATT_BLOCK))

    def layer_params(l):
        return (conv_a_full[l], conv_c_full[l], conv_c_b[l][None], gate_r_w[l], gate_r_b[l][None], gate_i_w[l],
                gate_i_b[l][None], rg_lambda[l][None], norm_a[l][None], norm_b[l][None], norm_c[l][None], sinks_wide[l])

    tm = _tile(t, 1024)
    xs = x.reshape(t, d)
    xb = xs.astype(BF16)
    saved = []
    for l in range(depth):
        wt, wo = _all_gather([wt_shard[l], wo_shard[l]], "ag_weights")
        wt = wt.reshape(dm.in_w, d)
        wo = wo.reshape(d, d)
        h = _matmul(xb, wt, dims=NT, tm=tm, tn=_tile(dm.in_w, 512), tk=d, out_dtype=F32, name="mm_in")
        mix, sst = _mixer_fwd(dm, h, layer_params(l), b_loc, nblk)
        z = _matmul(mix, wo, dims=NN, tm=tm, tn=_tile(d, 512), tk=d, out_dtype=F32, name="mm_out", addend=xs, alpha=alpha)
        saved.append((xb, h, sst, mix, z, wt, wo))
        xs, xb = _ln_fwd(z, ln_g[l][None], ln_b[l][None])

    dy, loss_part = _loss_head(xs, loss_target.reshape(t, d))
    loss = lax.psum(loss_part[0, 0], ("x", "y", "c"))

    g_wt, g_wo, small = [None] * depth, [None] * depth, [None] * depth
    for l in reversed(range(depth)):
        xb, h, sst, mix, z, wt, wo = saved[l]
        dz, dzb, d_lng, d_lnb = _ln_bwd(dy, z, ln_g[l][None])
        dmix = _matmul(dzb, wo, dims=NT, tm=tm, tn=_tile(d, 512), tk=d, out_dtype=F32, name="mm_dmix")
        dwo = _matmul(mix, dzb, dims=TN, tm=_tile(d, 1024), tn=_tile(d, 1024), tk=_tile(t, 1024), out_dtype=BF16, name="mm_dwo")
        dh, sm = _mixer_bwd(dm, h, sst, dmix, layer_params(l), b_loc, nblk)
        dy = _matmul(dh, wt, dims=NN, tm=tm, tn=_tile(d, 512), tk=_tile(dm.in_w, 3584), out_dtype=F32, name="mm_dx",
                     addend=dz, alpha=alpha)
        dwt = _matmul(dh, xb, dims=TN, tm=_tile(dm.in_w, 1536), tn=_tile(d, 1024), tk=_tile(t, 1024), out_dtype=BF16, name="mm_dwt")
        g_wt[l], g_wo[l] = _reduce_scatter([dwt.reshape(N_DEV, dm.in_w // N_DEV, d), dwo.reshape(N_DEV, d // N_DEV, d)], "w")
        (d_caw, d_ccw, d_ccb, d_grw, d_grb, d_giw, d_gib, d_lam, d_na, d_nb, d_nc, d_snk) = sm
        small[l] = dict(conv_a_w=d_caw, conv_c_w=d_ccw, conv_c_b=d_ccb[0], gate_r_w=d_grw, gate_r_b=d_grb[0], gate_i_w=d_giw,
                        gate_i_b=d_gib[0], rg_lambda=d_lam[0], norm_a=d_na[0], norm_b=d_nb[0], norm_c=d_nc[0],
                        sinks=d_snk[:, 0], ln_g=d_lng[0], ln_b=d_lnb[0])
    grad_x = dy.reshape(b_loc, seq, d)

    gin = jnp.swapaxes(jnp.stack(g_wt), 1, 2)
    cols = dm.in_w // N_DEV
    gw_in, dl_in, nm_in, nv_in = [o.reshape(depth, d, cols) for o in _adamw(
        w_in.reshape(depth * d, cols), gin.reshape(1, depth * d, cols), m_w_in.reshape(depth * d, cols),
        v_w_in.reshape(depth * d, cols), "adamw_in")]
    rows = d // N_DEV
    gw_out, dl_out, nm_out, nv_out = [o.reshape(depth, rows, d) for o in _adamw(
        w_out.reshape(depth * rows, d), jnp.stack(g_wo).reshape(1, depth * rows, d), m_w_out.reshape(depth * rows, d),
        v_w_out.reshape(depth * rows, d), "adamw_out")]

    given = dict(conv_a_w=(conv_a_w, m_conv_a_w, v_conv_a_w), conv_c_w=(conv_c_w, m_conv_c_w, v_conv_c_w),
                 conv_c_b=(conv_c_b, m_conv_c_b, v_conv_c_b), gate_r_w=(gate_r_w, m_gate_r_w, v_gate_r_w),
                 gate_r_b=(gate_r_b, m_gate_r_b, v_gate_r_b), gate_i_w=(gate_i_w, m_gate_i_w, v_gate_i_w),
                 gate_i_b=(gate_i_b, m_gate_i_b, v_gate_i_b), rg_lambda=(rg_lambda, m_rg_lambda, v_rg_lambda),
                 norm_a=(norm_a, m_norm_a, v_norm_a), norm_b=(norm_b, m_norm_b, v_norm_b), norm_c=(norm_c, m_norm_c, v_norm_c),
                 sinks=(sinks, m_sinks, v_sinks), ln_g=(ln_g, m_ln_g, v_ln_g), ln_b=(ln_b, m_ln_b, v_ln_b))
    full_shapes = [jnp.stack([small[l][n] for l in range(depth)]).shape for n in SMALL]
    g_local = _pack([jnp.stack([small[l][n] for l in range(depth)]) for n in SMALL])
    g_all = _all_gather([g_local], "ag_small")[0]

    def mine_of(n, a):
        if n in ("conv_a_w", "conv_c_w"):
            return lax.dynamic_update_slice(jnp.zeros(a.shape[:2] + (dm.wa,), F32), a, (0, 0, dev * ch))
        return a

    packs = [_pack([mine_of(n, given[n][i]) for n in SMALL]) for i in range(3)]
    outs = _adamw(packs[0], g_all, packs[1], packs[2], "adamw_small")
    res = {}
    for kind, packed in zip(("grad", "delta", "new_m", "new_v"), outs):
        for n, a in zip(SMALL, _unpack(packed, full_shapes)):
            if n in ("conv_a_w", "conv_c_w"):
                a = lax.dynamic_slice(a, (0, 0, dev * ch), a.shape[:2] + (ch,))
            res[kind, n] = a
    res.update({("grad", "w_in"): gw_in, ("delta", "w_in"): dl_in, ("new_m", "w_in"): nm_in, ("new_v", "w_in"): nv_in,
                ("grad", "w_out"): gw_out, ("delta", "w_out"): dl_out, ("new_m", "w_out"): nm_out, ("new_v", "w_out"): nv_out})
    order = ("w_in", "conv_a_w", "sinks", "conv_c_w", "conv_c_b", "gate_r_w", "gate_r_b", "gate_i_w", "gate_i_b", "rg_lambda",
             "norm_a", "norm_b", "norm_c", "w_out", "ln_g", "ln_b")
    return (loss, grad_x, *[res[kind, n] for kind in ("grad", "delta", "new_m", "new_v") for n in order])
```

```python
import functools

import jax
import jax.numpy as jnp
from jax import lax
from jax.experimental import pallas as pl
from jax.experimental.pallas import tpu as pltpu

F32 = jnp.float32
BF16 = jnp.bfloat16
MESH = pl.DeviceIdType.MESH
ANY = pl.BlockSpec(memory_space=pl.ANY)

N_DEV = 8
LANES = 128
SUBLANES = 8
HEAD_DIM = 64
KV_GROUP = 8
ATT_BLOCK = 128
CONV_A = 3
CONV_C = 4
RG_C = 8.0
LN_EPS = 1e-5
RMS_EPS = 1e-6
NEG_INF = -1e30
ADAM_LR = 0.001
ADAM_B1 = 0.9
ADAM_B2 = 0.999
ADAM_EPS = 1e-08
ADAM_WD = 0.01
ADAM_STEP = 10
VMEM_LIMIT = 56 * 1024 * 1024

NN = ((1,), (0,))
NT = ((1,), (1,))
TN = ((0,), (0,))


def _pcall(body, **kw):
    return pl.pallas_call(body, **kw)


def _roll(x, shift, axis):
    return pltpu.roll(x, shift, axis)


def _params(sem=None, vmem=VMEM_LIMIT):
    return pltpu.CompilerParams(dimension_semantics=sem, vmem_limit_bytes=vmem)


def _dot(a, b, dims):
    return lax.dot_general(a.astype(BF16), b.astype(BF16), (dims, ((), ())), preferred_element_type=F32)


@jax.custom_vjp
def _mm(a, b):
    return _dot(a, b, NN)


def _mm_fwd(a, b):
    return _dot(a, b, NN), (a.astype(BF16), b.astype(BF16))


def _mm_bwd(res, g):
    a, b = res
    return _dot(g, b, NT), _dot(a, g, TN)


_mm.defvjp(_mm_fwd, _mm_bwd)


@jax.custom_vjp
def _mm_nt(a, b):
    return _dot(a, b, NT)


def _mm_nt_fwd(a, b):
    return _dot(a, b, NT), (a.astype(BF16), b.astype(BF16))


def _mm_nt_bwd(res, g):
    a, b = res
    return _dot(g, b, NN), _dot(g, a, TN)


_mm_nt.defvjp(_mm_nt_fwd, _mm_nt_bwd)


def _rows(shape):
    return lax.broadcasted_iota(jnp.int32, shape, 0)


@functools.partial(jax.custom_vjp, nondiff_argnums=(2,))
def _shift_halo(u, prev, k):
    r, c = u.shape
    fill = jnp.concatenate([_roll(prev, k, 0), jnp.zeros((r - SUBLANES, c), u.dtype)], axis=0)
    return jnp.where(_rows((r, c)) < k, fill, _roll(u, k, 0))


def _shift_halo_fwd(u, prev, k):
    return _shift_halo(u, prev, k), None


def _shift_halo_bwd(k, _, g):
    r, c = g.shape
    du = jnp.where(_rows((r, c)) < r - k, _roll(g, r - k, 0), 0.0)
    dprev = jnp.where(_rows((SUBLANES, c)) >= SUBLANES - k, _roll(g[0:SUBLANES], SUBLANES - k, 0), 0.0)
    return du, dprev


_shift_halo.defvjp(_shift_halo_fwd, _shift_halo_bwd)


@functools.partial(jax.custom_vjp, nondiff_argnums=(1, 2))
def _shift_fill(u, k, fill):
    return jnp.where(_rows(u.shape) < k, fill, _roll(u, k, 0))


def _shift_fill_fwd(u, k, fill):
    return _shift_fill(u, k, fill), None


def _shift_fill_bwd(k, fill, _, g):
    r = g.shape[0]
    return (jnp.where(_rows(g.shape) < r - k, _roll(g, r - k, 0), 0.0),)


_shift_fill.defvjp(_shift_fill_fwd, _shift_fill_bwd)


@jax.custom_vjp
def _swap_halves(x):
    return _roll(x, HEAD_DIM, 1)


_swap_halves.defvjp(lambda x: (_roll(x, HEAD_DIM, 1), None), lambda _, g: (_roll(g, HEAD_DIM, 1),))


@functools.partial(jax.custom_vjp, nondiff_argnums=(1,))
def _split_rows(x, n):
    r = x.shape[0] // n
    return tuple(x[i * r:(i + 1) * r] for i in range(n))


def _split_rows_fwd(x, n):
    return _split_rows(x, n), None


def _split_rows_bwd(n, _, gs):
    return (jnp.concatenate(list(gs), axis=0),)


_split_rows.defvjp(_split_rows_fwd, _split_rows_bwd)


def _sigmoid(x):
    return 1.0 / (1.0 + jnp.exp(-x))


def _silu(x):
    return x * _sigmoid(x)


def _log_sigmoid(x):
    return -(jnp.maximum(-x, 0.0) + jnp.log1p(jnp.exp(-jnp.abs(x))))


def _neg_expm1(x):
    series = x * (1 + x * (1 / 2) * (1 + x * (1 / 3) * (1 + x * (1 / 4) * (1 + x * (1 / 5) * (1 + x * (1 / 6) * (1 + x * (1 / 7)))))))
    return -jnp.where(jnp.abs(x) < 0.25, series, jnp.exp(x) - 1.0)


def _scan_block(a, u, s_prev):
    acc_a, acc_b = a, u
    d = 1
    while d < a.shape[0]:
        acc_b = acc_a * _shift_fill(acc_b, d, 0.0) + acc_b
        acc_a = acc_a * _shift_fill(acc_a, d, 1.0)
        d *= 2
    return acc_a * s_prev + acc_b


def _last_row(h):
    return jnp.sum(jnp.where(_rows(h.shape) == h.shape[0] - 1, h, 0.0), axis=0, keepdims=True)


def _branch_a(ab, ac, ax, ag, acp, axp, w0, w1, w2, na):
    u = ac * ax
    up = acp * axp
    ya = ab * (w2 * u + w1 * _shift_halo(u, up, 1) + w0 * _shift_halo(u, up, 2))
    ms = jnp.sum(ya * ya, axis=1, keepdims=True) * (1.0 / ya.shape[1])
    return ya * lax.rsqrt(ms + RMS_EPS) * na * _silu(ag)


def _branch_c(cx, cg, cxp, sp, wc, bc, wr, br, wi, bi, lam, nc):
    hs, lasts = [], []
    for j in range(len(cx)):
        xc = (wc[3][j] * cx[j] + wc[2][j] * _shift_halo(cx[j], cxp[j], 1) + wc[1][j] * _shift_halo(cx[j], cxp[j], 2)
              + wc[0][j] * _shift_halo(cx[j], cxp[j], 3) + bc[j])
        r = _sigmoid(_mm(xc, wr[j]) + br[j])
        i = _sigmoid(_mm(xc, wi[j]) + bi[j])
        log_a = RG_C * r * _log_sigmoid(lam[j])
        a = jnp.exp(log_a)
        u = jnp.sqrt(_neg_expm1(2.0 * log_a)) * (i * xc)
        h = _scan_block(a, u, sp[j])
        hs.append(h)
        lasts.append(_last_row(h))
    width = LANES * len(cx)
    ms = sum(jnp.sum(h * h, axis=1, keepdims=True) for h in hs) * (1.0 / width)
    inv = lax.rsqrt(ms + RMS_EPS)
    return [hs[j] * inv * nc[j] * _silu(cg[j]) for j in range(len(cx))], lasts


def _branch_b(q, k, v, kp, vp, bg, snk, nb, first_thr):
    rows = ATT_BLOCK
    n_kv = 2 * len(k)
    gr = KV_GROUP * rows
    qi = _rows((gr, 2 * rows)) & (rows - 1)
    kj = lax.broadcasted_iota(jnp.int32, (gr, 2 * rows), 1)
    dist = qi + rows - kj
    valid = (dist >= 0) & (dist < rows) & (kj >= first_thr)
    lane0 = kj == 0
    upper = lax.broadcasted_iota(jnp.int32, (2 * rows, LANES), 1) >= HEAD_DIM
    heads = [None] * (n_kv * KV_GROUP)
    for g in range(n_kv):
        half = g % 2
        keep = upper if half else jnp.logical_not(upper)
        kc = jnp.where(keep, jnp.concatenate([kp[g // 2], k[g // 2]], axis=0), 0.0)
        vc = jnp.where(keep, jnp.concatenate([vp[g // 2], v[g // 2]], axis=0), 0.0)
        hs = range(g * KV_GROUP, (g + 1) * KV_GROUP)
        qg = jnp.concatenate([q[h // 2] if h % 2 == half else _swap_halves(q[h // 2]) for h in hs], axis=0)
        s = _mm_nt(qg, kc) * (HEAD_DIM ** -0.5)
        s = jnp.where(valid, s, NEG_INF)
        sink = jnp.concatenate([jnp.broadcast_to(snk[h], (rows, 2 * rows)) for h in hs], axis=0)
        m = lax.stop_gradient(jnp.maximum(jnp.max(s, axis=1, keepdims=True), sink))
        p = jnp.exp(s - m)
        e_sink = jnp.sum(jnp.where(lane0, jnp.exp(sink - m), 0.0), axis=1, keepdims=True)
        p = p / (jnp.sum(p, axis=1, keepdims=True) + e_sink)
        o = _split_rows(_mm(p, vc), KV_GROUP)
        for i, h in enumerate(hs):
            heads[h] = o[i] if h % 2 == half else _swap_halves(o[i])
    yb = [heads[2 * j] + heads[2 * j + 1] for j in range(len(q))]
    width = LANES * len(q)
    ms = sum(jnp.sum(y * y, axis=1, keepdims=True) for y in yb) * (1.0 / width)
    inv = lax.rsqrt(ms + RMS_EPS)
    return [yb[j] * inv * nb[j] * _silu(bg[j]) for j in range(len(q))]


class _Dims:
    def __init__(self, d_model, n_rg_heads):
        self.d = d_model
        self.wa = d_model // 4
        self.wb = d_model // 2
        self.wc = d_model // 4
        self.kvw = self.wb // KV_GROUP
        self.nq = self.wb // HEAD_DIM
        self.in_w = 4 * self.wa + 2 * self.wb + 2 * self.kvw + 2 * self.wc
        self.o_q = 4 * self.wa
        self.o_k = self.o_q + self.wb
        self.o_v = self.o_k + self.kvw
        self.o_bg = self.o_v + self.kvw
        self.o_cx = self.o_bg + self.wb
        self.o_cg = self.o_cx + self.wc
        self.nh = n_rg_heads
        assert self.wc // n_rg_heads == LANES and self.kvw % LANES == 0
        assert self.o_k % self.kvw == 0 and self.o_cx % (self.wc // 2) == 0


def _chunks(ref, rows, off, width):
    return [ref[rows, off + LANES * j: off + LANES * (j + 1)] for j in range(width // LANES)]


def _read_params(dm, caw, ccw, ccb, grw, grb, giw, gib, lam, na, nb, nc, snk):
    row = slice(0, 1)
    return dict(
        wa=[caw[k:k + 1, :] for k in range(CONV_A)], na=na[...],
        wc=[_chunks(ccw, slice(k, k + 1), 0, dm.wc) for k in range(CONV_C)], bc=_chunks(ccb, row, 0, dm.wc),
        wr=[grw[j] for j in range(dm.nh)], br=_chunks(grb, row, 0, dm.wc),
        wi=[giw[j] for j in range(dm.nh)], bi=_chunks(gib, row, 0, dm.wc),
        lam=_chunks(lam, row, 0, dm.wc), nc=_chunks(nc, row, 0, dm.wc),
        nb=_chunks(nb, row, 0, dm.wb), snk=[snk[h:h + 1, :] for h in range(dm.nq)])


def _param_specs(dm):
    shapes = [(CONV_A, dm.wa), (CONV_C, dm.wc), (1, dm.wc), (dm.nh, LANES, LANES), (1, dm.wc), (dm.nh, LANES, LANES),
              (1, dm.wc), (1, dm.wc), (1, dm.wa), (1, dm.wb), (1, dm.wc), (dm.nq, 2 * ATT_BLOCK)]
    specs = [pl.BlockSpec(s, (lambda b, n, _r=len(s): (0,) * _r)) for s in shapes]
    return shapes, specs


def _mixer_fwd(dm, h, prm, b_loc, nblk):
    t = h.shape[0]
    r = ATT_BLOCK
    tail = slice(r - SUBLANES, r)

    def body(h_ref, caw, ccw, ccb, grw, grb, giw, gib, lam, na, nb, nc, snk, mix_ref, sst_ref, kp, vp, acp, axp, cxp, sp):
        n = pl.program_id(1)

        @pl.when(n == 0)
        def _():
            for ref in (kp, vp, acp, axp, cxp, sp):
                ref[...] = jnp.zeros(ref.shape, ref.dtype)

        p = _read_params(dm, caw, ccw, ccb, grw, grb, giw, gib, lam, na, nb, nc, snk)
        full = slice(None)
        mix_a = _branch_a(h_ref[:, 0:dm.wa], h_ref[:, dm.wa:2 * dm.wa], h_ref[:, 2 * dm.wa:3 * dm.wa],
                          h_ref[:, 3 * dm.wa:4 * dm.wa], acp[...], axp[...], p["wa"][0], p["wa"][1], p["wa"][2], p["na"])
        mix_ref[:, 0:dm.wa] = mix_a.astype(BF16)
        first_thr = jnp.where(n == 0, ATT_BLOCK, 0)
        mix_b = _branch_b(_chunks(h_ref, full, dm.o_q, dm.wb), _chunks(h_ref, full, dm.o_k, dm.kvw),
                          _chunks(h_ref, full, dm.o_v, dm.kvw), _chunks(kp, full, 0, dm.kvw), _chunks(vp, full, 0, dm.kvw),
                          _chunks(h_ref, full, dm.o_bg, dm.wb), p["snk"], p["nb"], first_thr)
        for j, mb in enumerate(mix_b):
            mix_ref[:, dm.wa + LANES * j: dm.wa + LANES * (j + 1)] = mb.astype(BF16)
        sst_ref[0] = sp[...]
        mix_c, lasts = _branch_c(_chunks(h_ref, full, dm.o_cx, dm.wc), _chunks(h_ref, full, dm.o_cg, dm.wc),
                                 _chunks(cxp, full, 0, dm.wc), _chunks(sp, slice(0, 1), 0, dm.wc), p["wc"], p["bc"],
                                 p["wr"], p["br"], p["wi"], p["bi"], p["lam"], p["nc"])
        o_c = dm.wa + dm.wb
        for j, mc in enumerate(mix_c):
            mix_ref[:, o_c + LANES * j: o_c + LANES * (j + 1)] = mc.astype(BF16)
            sp[:, LANES * j: LANES * (j + 1)] = jnp.broadcast_to(lasts[j], (SUBLANES, LANES))
        kp[...] = h_ref[:, dm.o_k:dm.o_k + dm.kvw]
        vp[...] = h_ref[:, dm.o_v:dm.o_v + dm.kvw]
        acp[...] = h_ref[tail, dm.wa:2 * dm.wa]
        axp[...] = h_ref[tail, 2 * dm.wa:3 * dm.wa]
        cxp[...] = h_ref[tail, dm.o_cx:dm.o_cx + dm.wc]

    _, pspecs = _param_specs(dm)
    return _pcall(
        body, name="mixer_fwd", grid=(b_loc, nblk),
        in_specs=[pl.BlockSpec((r, dm.in_w), lambda b, n: (b * nblk + n, 0))] + pspecs,
        out_specs=[pl.BlockSpec((r, dm.d), lambda b, n: (b * nblk + n, 0)),
                   pl.BlockSpec((1, SUBLANES, dm.wc), lambda b, n: (b * nblk + n, 0, 0))],
        out_shape=[jax.ShapeDtypeStruct((t, dm.d), BF16), jax.ShapeDtypeStruct((b_loc * nblk, SUBLANES, dm.wc), F32)],
        scratch_shapes=[pltpu.VMEM((r, dm.kvw), F32), pltpu.VMEM((r, dm.kvw), F32), pltpu.VMEM((SUBLANES, dm.wa), F32),
                        pltpu.VMEM((SUBLANES, dm.wa), F32), pltpu.VMEM((SUBLANES, dm.wc), F32),
                        pltpu.VMEM((SUBLANES, dm.wc), F32)],
        compiler_params=_params(("arbitrary", "arbitrary")),
    )(h, *prm)


def _mixer_bwd(dm, h, sst, dmix, prm, b_loc, nblk):
    t = h.shape[0]
    r = ATT_BLOCK
    rb8 = r // SUBLANES
    n_small = 12

    def body(h_ref, kp_ref, vp_ref, acp_ref, axp_ref, cxp0_ref, cxp1_ref, sst_ref, dmix_ref,
             caw, ccw, ccb, grw, grb, giw, gib, lam, na, nb, nc, snk,
             dh_ref, d_caw, d_ccw, d_ccb, d_grw, d_grb, d_giw, d_gib, d_lam, d_na, d_nb, d_nc, d_snk,
             dkp, dvp, dacp, daxp, dcxp, dsp):
        step = pl.program_id(1)
        n = nblk - 1 - step

        @pl.when(step == 0)
        def _():
            for ref in (dkp, dvp, dacp, daxp, dcxp, dsp):
                ref[...] = jnp.zeros(ref.shape, ref.dtype)

        @pl.when((step == 0) & (pl.program_id(0) == 0))
        def _():
            for ref in (d_caw, d_ccw, d_ccb, d_grw, d_grb, d_giw, d_gib, d_lam, d_na, d_nb, d_nc, d_snk):
                ref[...] = jnp.zeros(ref.shape, ref.dtype)

        p = _read_params(dm, caw, ccw, ccb, grw, grb, giw, gib, lam, na, nb, nc, snk)
        has_prev = jnp.where(n > 0, 1.0, 0.0)
        full = slice(None)
        pad = jnp.zeros((r - SUBLANES, LANES), F32)

        def with_tail(own, carry):
            z = jnp.zeros((r - SUBLANES, own.shape[1]), F32)
            return own + jnp.concatenate([z, carry], axis=0)

        a_in = (h_ref[:, 0:dm.wa], h_ref[:, dm.wa:2 * dm.wa], h_ref[:, 2 * dm.wa:3 * dm.wa], h_ref[:, 3 * dm.wa:4 * dm.wa],
                acp_ref[...] * has_prev, axp_ref[...] * has_prev, p["wa"][0], p["wa"][1], p["wa"][2], p["na"])
        _, vjp_a = jax.vjp(_branch_a, *a_in)
        g_ab, g_ac, g_ax, g_ag, g_acp, g_axp, g_w0, g_w1, g_w2, g_na = vjp_a(dmix_ref[:, 0:dm.wa])
        dh_ref[:, 0:dm.wa] = g_ab.astype(BF16)
        dh_ref[:, dm.wa:2 * dm.wa] = with_tail(g_ac, dacp[...]).astype(BF16)
        dh_ref[:, 2 * dm.wa:3 * dm.wa] = with_tail(g_ax, daxp[...]).astype(BF16)
        dh_ref[:, 3 * dm.wa:4 * dm.wa] = g_ag.astype(BF16)
        dacp[...] = g_acp
        daxp[...] = g_axp
        for k, gw in enumerate((g_w0, g_w1, g_w2)):
            d_caw[k:k + 1, :] += gw
        d_na[...] += g_na

        first_thr = jnp.where(n == 0, ATT_BLOCK, 0)
        kp_in = [c * has_prev for c in _chunks(kp_ref, full, 0, dm.kvw)]
        vp_in = [c * has_prev for c in _chunks(vp_ref, full, 0, dm.kvw)]
        b_in = (_chunks(h_ref, full, dm.o_q, dm.wb), _chunks(h_ref, full, dm.o_k, dm.kvw), _chunks(h_ref, full, dm.o_v, dm.kvw),
                kp_in, vp_in, _chunks(h_ref, full, dm.o_bg, dm.wb), p["snk"], p["nb"])
        _, vjp_b = jax.vjp(lambda *a: _branch_b(*a, first_thr), *b_in)
        g_q, g_k, g_v, g_kp, g_vp, g_bg, g_snk, g_nb = vjp_b(_chunks(dmix_ref, full, dm.wa, dm.wb))
        for j in range(len(g_q)):
            dh_ref[:, dm.o_q + LANES * j: dm.o_q + LANES * (j + 1)] = g_q[j].astype(BF16)
            dh_ref[:, dm.o_bg + LANES * j: dm.o_bg + LANES * (j + 1)] = g_bg[j].astype(BF16)
            d_nb[:, LANES * j: LANES * (j + 1)] += g_nb[j]
        for j in range(len(g_k)):
            cols = slice(LANES * j, LANES * (j + 1))
            dh_ref[:, dm.o_k + LANES * j: dm.o_k + LANES * (j + 1)] = (g_k[j] + dkp[:, cols]).astype(BF16)
            dh_ref[:, dm.o_v + LANES * j: dm.o_v + LANES * (j + 1)] = (g_v[j] + dvp[:, cols]).astype(BF16)
            dkp[:, cols] = g_kp[j]
            dvp[:, cols] = g_vp[j]
        for hd in range(dm.nq):
            d_snk[hd:hd + 1, :] += g_snk[hd]

        half_c = dm.wc // 2
        cxp_in = ([c * has_prev for c in _chunks(cxp0_ref, full, 0, half_c)]
                  + [c * has_prev for c in _chunks(cxp1_ref, full, 0, half_c)])
        c_in = (_chunks(h_ref, full, dm.o_cx, dm.wc), _chunks(h_ref, full, dm.o_cg, dm.wc), cxp_in,
                [sst_ref[0, 0:1, LANES * j: LANES * (j + 1)] for j in range(dm.nh)], p["wc"], p["bc"], p["wr"], p["br"], p["wi"], p["bi"],
                p["lam"], p["nc"])
        _, vjp_c = jax.vjp(_branch_c, *c_in)
        ct_last = [dsp[0:1, LANES * j: LANES * (j + 1)] for j in range(dm.nh)]
        g_cx, g_cg, g_cxp, g_sp, g_wc, g_bc, g_wr, g_br, g_wi, g_bi, g_lam, g_nc = vjp_c(
            (_chunks(dmix_ref, full, dm.wa + dm.wb, dm.wc), ct_last))
        for j in range(dm.nh):
            cols = slice(LANES * j, LANES * (j + 1))
            tot = g_cx[j] + jnp.concatenate([pad, dcxp[:, cols]], axis=0)
            dh_ref[:, dm.o_cx + LANES * j: dm.o_cx + LANES * (j + 1)] = tot.astype(BF16)
            dh_ref[:, dm.o_cg + LANES * j: dm.o_cg + LANES * (j + 1)] = g_cg[j].astype(BF16)
            dcxp[:, cols] = g_cxp[j]
            dsp[:, cols] = jnp.broadcast_to(g_sp[j], (SUBLANES, LANES))
            for k in range(CONV_C):
                d_ccw[k:k + 1, cols] += g_wc[k][j]
            d_ccb[:, cols] += g_bc[j]
            d_grw[j] += g_wr[j]
            d_grb[:, cols] += g_br[j]
            d_giw[j] += g_wi[j]
            d_gib[:, cols] += g_bi[j]
            d_lam[:, cols] += g_lam[j]
            d_nc[:, cols] += g_nc[j]

    def blk(b, s):
        return b * nblk + (nblk - 1 - s)

    def prev_rows8(b, s):
        return jnp.maximum(blk(b, s) * rb8 - 1, 0)

    pshapes, pspecs = _param_specs(dm)
    half_c = dm.wc // 2
    in_specs = [
        pl.BlockSpec((r, dm.in_w), lambda b, s: (blk(b, s), 0)),
        pl.BlockSpec((r, dm.kvw), lambda b, s: (jnp.maximum(blk(b, s) - 1, 0), dm.o_k // dm.kvw)),
        pl.BlockSpec((r, dm.kvw), lambda b, s: (jnp.maximum(blk(b, s) - 1, 0), dm.o_v // dm.kvw)),
        pl.BlockSpec((SUBLANES, dm.wa), lambda b, s: (prev_rows8(b, s), 1)),
        pl.BlockSpec((SUBLANES, dm.wa), lambda b, s: (prev_rows8(b, s), 2)),
        pl.BlockSpec((SUBLANES, half_c), lambda b, s: (prev_rows8(b, s), dm.o_cx // half_c)),
        pl.BlockSpec((SUBLANES, half_c), lambda b, s: (prev_rows8(b, s), dm.o_cx // half_c + 1)),
        pl.BlockSpec((1, SUBLANES, dm.wc), lambda b, s: (blk(b, s), 0, 0)),
        pl.BlockSpec((r, dm.d), lambda b, s: (blk(b, s), 0)),
    ] + pspecs
    outs = _pcall(
        body, name="mixer_bwd", grid=(b_loc, nblk), in_specs=in_specs,
        out_specs=[pl.BlockSpec((r, dm.in_w), lambda b, s: (blk(b, s), 0))] + pspecs,
        out_shape=[jax.ShapeDtypeStruct((t, dm.in_w), BF16)] + [jax.ShapeDtypeStruct(s, F32) for s in pshapes],
        scratch_shapes=[pltpu.VMEM((r, dm.kvw), F32), pltpu.VMEM((r, dm.kvw), F32), pltpu.VMEM((SUBLANES, dm.wa), F32),
                        pltpu.VMEM((SUBLANES, dm.wa), F32), pltpu.VMEM((SUBLANES, dm.wc), F32),
                        pltpu.VMEM((SUBLANES, dm.wc), F32)],
        compiler_params=_params(("arbitrary", "arbitrary")),
    )(h, h, h, h, h, h, h, sst, dmix, *prm)
    assert len(outs) == 1 + n_small
    return outs[0], outs[1:]


def _matmul(a, b, *, dims, tm, tn, tk, out_dtype, name, addend=None, alpha=None):
    if dims == TN:
        (k_dim, m), n_dim = a.shape, b.shape[1]
        a_spec = pl.BlockSpec((tk, tm), lambda i, j, k: (k, i))
    else:
        (m, k_dim), n_dim = a.shape, (b.shape[0] if dims == NT else b.shape[1])
        a_spec = pl.BlockSpec((tm, tk), lambda i, j, k: (i, k))
    b_spec = pl.BlockSpec((tn, tk), lambda i, j, k: (j, k)) if dims == NT else pl.BlockSpec((tk, tn), lambda i, j, k: (k, j))
    assert m % tm == 0 and n_dim % tn == 0 and k_dim % tk == 0, (a.shape, b.shape, tm, tn, tk)
    nk = k_dim // tk
    o_spec = pl.BlockSpec((tm, tn), lambda i, j, k: (i, j))

    def body(*refs):
        a_ref, b_ref = refs[0], refs[1]
        add_ref = refs[2] if addend is not None else None
        o_ref = refs[3 if addend is not None else 2]
        acc_ref = refs[-1]
        k = pl.program_id(2)
        part = lax.dot_general(a_ref[...], b_ref[...], (dims, ((), ())), preferred_element_type=F32)

        def finish(acc):
            if add_ref is not None:
                acc = acc + alpha * add_ref[...]
            o_ref[...] = acc.astype(out_dtype)

        if nk == 1:
            finish(part)
        else:
            @pl.when(k == 0)
            def _():
                acc_ref[...] = part

            @pl.when((k > 0) & (k < nk - 1))
            def _():
                acc_ref[...] += part

            @pl.when(k == nk - 1)
            def _():
                finish(acc_ref[...] + part)

    ins = [a, b] + ([addend] if addend is not None else [])
    in_specs = [a_spec, b_spec] + ([o_spec] if addend is not None else [])
    return _pcall(
        body, name=name, grid=(m // tm, n_dim // tn, nk), in_specs=in_specs, out_specs=o_spec,
        out_shape=jax.ShapeDtypeStruct((m, n_dim), out_dtype),
        scratch_shapes=[pltpu.VMEM((tm, tn) if nk > 1 else (SUBLANES, LANES), F32)],
        compiler_params=_params(("parallel", "parallel", "arbitrary")),
    )(*ins)


def _tile(n, want, quantum=LANES):
    if n <= want:
        return n
    for cand in range(want - want % quantum, 0, -quantum):
        if n % cand == 0:
            return cand
    return n


def _row_tile(t, d):
    return _tile(t, max(2 * SUBLANES, (1 << 19) // d), 2 * SUBLANES)


def _ln_fwd(z, g, b):
    t, d = z.shape
    tr = _row_tile(t, d)

    def body(z_ref, g_ref, b_ref, y_ref, yb_ref):
        zz = z_ref[...]
        mu = jnp.mean(zz, axis=1, keepdims=True)
        zc = zz - mu
        var = jnp.mean(zc * zc, axis=1, keepdims=True)
        y = zc * lax.rsqrt(var + LN_EPS) * g_ref[...] + b_ref[...]
        y_ref[...] = y
        yb_ref[...] = y.astype(BF16)

    row = pl.BlockSpec((tr, d), lambda i: (i, 0))
    vec = pl.BlockSpec((1, d), lambda i: (0, 0))
    return _pcall(body, name="ln_fwd", grid=(t // tr,), in_specs=[row, vec, vec], out_specs=[row, row],
                  out_shape=[jax.ShapeDtypeStruct((t, d), F32), jax.ShapeDtypeStruct((t, d), BF16)],
                  compiler_params=_params(("parallel",)))(z, g, b)


def _ln_bwd(dy, z, g):
    t, d = z.shape
    tr = _row_tile(t, d)

    def body(dy_ref, z_ref, g_ref, dz_ref, dzb_ref, dg_ref, db_ref):
        @pl.when(pl.program_id(0) == 0)
        def _():
            dg_ref[...] = jnp.zeros(dg_ref.shape, F32)
            db_ref[...] = jnp.zeros(db_ref.shape, F32)

        zz = z_ref[...]
        dyy = dy_ref[...]
        mu = jnp.mean(zz, axis=1, keepdims=True)
        zc = zz - mu
        rstd = lax.rsqrt(jnp.mean(zc * zc, axis=1, keepdims=True) + LN_EPS)
        xhat = zc * rstd
        dyg = dyy * g_ref[...]
        dz = rstd * (dyg - jnp.mean(dyg, axis=1, keepdims=True) - xhat * jnp.mean(dyg * xhat, axis=1, keepdims=True))
        dz_ref[...] = dz
        dzb_ref[...] = dz.astype(BF16)
        dg_ref[...] += jnp.sum(dyy * xhat, axis=0, keepdims=True)
        db_ref[...] += jnp.sum(dyy, axis=0, keepdims=True)

    row = pl.BlockSpec((tr, d), lambda i: (i, 0))
    vec = pl.BlockSpec((1, d), lambda i: (0, 0))
    return _pcall(body, name="ln_bwd", grid=(t // tr,), in_specs=[row, row, vec], out_specs=[row, row, vec, vec],
                  out_shape=[jax.ShapeDtypeStruct((t, d), F32), jax.ShapeDtypeStruct((t, d), BF16),
                             jax.ShapeDtypeStruct((1, d), F32), jax.ShapeDtypeStruct((1, d), F32)],
                  compiler_params=_params(("arbitrary",)))(dy, z, g)


def _loss_head(y, target):
    t, d = y.shape
    tr = _row_tile(t, d)

    def body(y_ref, t_ref, dy_ref, loss_ref):
        @pl.when(pl.program_id(0) == 0)
        def _():
            loss_ref[...] = jnp.zeros(loss_ref.shape, F32)

        err = y_ref[...] - t_ref[...]
        dy_ref[...] = err * (1.0 / d)
        per_token = jnp.sum(err * err, axis=1, keepdims=True) * (1.0 / d)
        loss_ref[...] += 0.5 * jnp.sum(per_token, axis=0, keepdims=True)

    row = pl.BlockSpec((tr, d), lambda i: (i, 0))
    one = pl.BlockSpec((1, 1), lambda i: (0, 0))
    return _pcall(body, name="loss_head", grid=(t // tr,), in_specs=[row, row], out_specs=[row, one],
                  out_shape=[jax.ShapeDtypeStruct((t, d), F32), jax.ShapeDtypeStruct((1, 1), F32)],
                  compiler_params=_params(("arbitrary",)))(y, target)


def _chip_sum(sums, far, name):
    r, c = sums.shape[1:]
    tr = _row_tile(r, c)

    def body(s_ref, f_ref, o_ref):
        acc = s_ref[0].astype(F32)
        for i in range(3):
            acc = acc + f_ref[i].astype(F32)
        o_ref[...] = acc

    return _pcall(body, name=name, grid=(r // tr,),
                  in_specs=[pl.BlockSpec((1, tr, c), lambda i: (2 * lax.axis_index("x") + lax.axis_index("y"), i, 0)),
                            pl.BlockSpec((3, tr, c), lambda i: (0, i, 0))],
                  out_specs=pl.BlockSpec((tr, c), lambda i: (i, 0)), out_shape=jax.ShapeDtypeStruct((r, c), F32),
                  compiler_params=_params(("parallel",)))(sums, far)


def _pair_add(a, b, name):
    p, r, c = b.shape
    tr = _row_tile(r, c)

    def body(a_ref, b_ref, o_ref):
        o_ref[...] = (a_ref[...].astype(F32) + b_ref[...].astype(F32)).astype(BF16)

    spec = pl.BlockSpec((1, tr, c), lambda q, i: (q, i, 0))
    return _pcall(body, name=name, grid=(p, r // tr),
                  in_specs=[pl.BlockSpec((1, tr, c), lambda q, i: (2 * q + lax.axis_index("c"), i, 0)), spec], out_specs=spec,
                  out_shape=jax.ShapeDtypeStruct((p, r, c), BF16), compiler_params=_params(("parallel", "parallel")))(a, b)


def _adamw(w, g_parts, m, v, name):
    r, c = w.shape
    n_parts = g_parts.shape[0]
    tr = _row_tile(r, c) if r % SUBLANES == 0 else r

    def body(w_ref, g_ref, m_ref, v_ref, go_ref, d_ref, mo_ref, vo_ref):
        g = g_ref[0].astype(F32)
        for i in range(1, n_parts):
            g = g + g_ref[i].astype(F32)
        m_new = ADAM_B1 * m_ref[...] + (1.0 - ADAM_B1) * g
        v_new = ADAM_B2 * v_ref[...] + (1.0 - ADAM_B2) * (g * g)
        m_hat = m_new / (1.0 - ADAM_B1 ** ADAM_STEP)
        v_hat = v_new / (1.0 - ADAM_B2 ** ADAM_STEP)
        go_ref[...] = g
        d_ref[...] = -ADAM_LR * (m_hat / (jnp.sqrt(v_hat) + ADAM_EPS) + ADAM_WD * w_ref[...])
        mo_ref[...] = m_new
        vo_ref[...] = v_new

    spec = pl.BlockSpec((tr, c), lambda i: (i, 0))
    shape = jax.ShapeDtypeStruct((r, c), F32)
    return _pcall(body, name=name, grid=(r // tr,),
                  in_specs=[spec, pl.BlockSpec((n_parts, tr, c), lambda i: (0, i, 0)), spec, spec],
                  out_specs=[spec] * 4, out_shape=[shape] * 4, compiler_params=_params(("parallel",)))(w, g_parts, m, v)


def _me():
    return lax.axis_index("x"), lax.axis_index("y"), lax.axis_index("c")


def _dev(px, py, pc):
    return 4 * px + 2 * py + pc


def _remote(src, dst, send_sems, recv_sems, k, to):
    return pltpu.make_async_remote_copy(src_ref=src, dst_ref=dst, send_sem=send_sems.at[k], recv_sem=recv_sems.at[k],
                                        device_id=to, device_id_type=MESH)


def _all_gather(arrs, name):
    n = len(arrs)

    def body(*refs):
        ins, outs = refs[:n], refs[n:2 * n]
        send_sems, recv_sems, local_sems = refs[2 * n:]
        x, y, c = _me()
        me, sibling = (x, y, c), (x, y, 1 - c)
        chips = [(1 - x, y), (x, 1 - y), (1 - x, 1 - y)]
        pending = []
        for a in range(n):
            mine = pltpu.make_async_copy(ins[a], outs[a].at[_dev(*me)], local_sems.at[a])
            mine.start()
            pending.append(mine)
        sends = []
        for a in range(n):
            dst = outs[a].at[_dev(*me)]
            sends.append(_remote(ins[a], dst, send_sems, recv_sems, 7 * a, sibling))
            sends += [_remote(ins[a], dst, send_sems, recv_sems, 7 * a + 1 + j, (*chip, c)) for j, chip in enumerate(chips)]
        for cp in sends:
            cp.start()
        for a in range(n):
            for j, chip in enumerate(chips):
                blk = outs[a].at[_dev(*chip, c)]
                _remote(blk, blk, send_sems, recv_sems, 7 * a + 1 + j, me).wait_recv()
                fwd = _remote(blk, blk, send_sems, recv_sems, 7 * a + 4 + j, sibling)
                fwd.start()
                sends.append(fwd)
        for a in range(n):
            blk = outs[a].at[_dev(*sibling)]
            _remote(blk, blk, send_sems, recv_sems, 7 * a, me).wait_recv()
            for j, chip in enumerate(chips):
                blk = outs[a].at[_dev(*chip, 1 - c)]
                _remote(blk, blk, send_sems, recv_sems, 7 * a + 4 + j, me).wait_recv()
        for cp in sends:
            cp.wait_send()
        for cp in pending:
            cp.wait()

    return _pcall(
        body, name=name, in_specs=[ANY] * n, out_specs=[ANY] * n,
        out_shape=[jax.ShapeDtypeStruct((N_DEV,) + a.shape, a.dtype) for a in arrs],
        scratch_shapes=[pltpu.SemaphoreType.DMA((7 * n,)), pltpu.SemaphoreType.DMA((7 * n,)), pltpu.SemaphoreType.DMA((n,))],
    )(*arrs)


def _relations(x, y):
    return [(x, y), (1 - x, y), (x, 1 - y), (1 - x, 1 - y)]


def _rs_within_chip(parts, name):
    n = len(parts)

    def body(*refs):
        ins, got = refs[:n], refs[n:2 * n]
        send_sems, recv_sems = refs[2 * n:]
        x, y, c = _me()
        sibling = (x, y, 1 - c)
        copies = []
        for a in range(n):
            for k in range(4):
                cp = _remote(ins[a].at[2 * k + 1 - c], got[a].at[k], send_sems, recv_sems, 4 * a + k, sibling)
                cp.start()
                copies.append(cp)
        for cp in copies:
            cp.wait()

    return _pcall(
        body, name=name, in_specs=[ANY] * n, out_specs=[ANY] * n,
        out_shape=[jax.ShapeDtypeStruct((4,) + p.shape[1:], p.dtype) for p in parts],
        scratch_shapes=[pltpu.SemaphoreType.DMA((4 * n,)), pltpu.SemaphoreType.DMA((4 * n,))],
    )(*parts)


def _rs_across_chips(sums, name):
    n = len(sums)

    def body(*refs):
        ins, outs = refs[:n], refs[n:2 * n]
        send_sems, recv_sems = refs[2 * n:]
        x, y, c = _me()
        copies = []
        for a in range(n):
            for j, (cx, cy) in enumerate(_relations(x, y)[1:]):
                cp = _remote(ins[a].at[2 * cx + cy], outs[a].at[j], send_sems, recv_sems, 3 * a + j, (cx, cy, c))
                cp.start()
                copies.append(cp)
        for cp in copies:
            cp.wait()

    return _pcall(
        body, name=name, in_specs=[ANY] * n, out_specs=[ANY] * n,
        out_shape=[jax.ShapeDtypeStruct((3,) + s.shape[1:], s.dtype) for s in sums],
        scratch_shapes=[pltpu.SemaphoreType.DMA((3 * n,)), pltpu.SemaphoreType.DMA((3 * n,))],
    )(*sums)


def _reduce_scatter(parts, tag):
    got = _rs_within_chip(parts, "rs_d2d_" + tag)
    sums = [_pair_add(p, g, "rs_add_%s_%d" % (tag, i)) for i, (p, g) in enumerate(zip(parts, got))]
    far = _rs_across_chips(sums, "rs_ici_" + tag)
    return [_chip_sum(s, f, "rs_sum_%s_%d" % (tag, i)) for i, (s, f) in enumerate(zip(sums, far))]


SMALL = ("conv_a_w", "conv_c_w", "conv_c_b", "gate_r_w", "gate_r_b", "gate_i_w", "gate_i_b", "rg_lambda",
         "norm_a", "norm_b", "norm_c", "sinks", "ln_g", "ln_b")
PACK_COLS = 1024


def _pack(arrs):
    flat = jnp.concatenate([a.reshape(-1) for a in arrs])
    pad = (-flat.shape[0]) % (SUBLANES * PACK_COLS)
    return jnp.pad(flat, (0, pad)).reshape(-1, PACK_COLS)


def _unpack(packed, shapes):
    flat = packed.reshape(-1)
    out, off = [], 0
    for s in shapes:
        size = 1
        for dim in s:
            size *= dim
        out.append(flat[off:off + size].reshape(s))
        off += size
    return out


def kernel(x, w_in, conv_a_w, sinks, conv_c_w, conv_c_b, gate_r_w, gate_r_b, gate_i_w, gate_i_b, rg_lambda, norm_a, norm_b, norm_c, w_out, ln_g, ln_b, loss_target, m_w_in, m_conv_a_w, m_sinks, m_conv_c_w, m_conv_c_b, m_gate_r_w, m_gate_r_b, m_gate_i_w, m_gate_i_b, m_rg_lambda, m_norm_a, m_norm_b, m_norm_c, m_w_out, m_ln_g, m_ln_b, v_w_in, v_conv_a_w, v_sinks, v_conv_c_w, v_conv_c_b, v_gate_r_w, v_gate_r_b, v_gate_i_w, v_gate_i_b, v_rg_lambda, v_norm_a, v_norm_b, v_norm_c, v_w_out, v_ln_g, v_ln_b):
    b_loc, seq, d = x.shape
    depth = w_in.shape[0]
    dm = _Dims(d, gate_r_w.shape[1])
    t = b_loc * seq
    nblk = seq // ATT_BLOCK
    alpha = (2.0 * depth) ** 0.25
    ch = dm.wa // N_DEV
    dev = _dev(*_me())

    wt_shard = jnp.swapaxes(w_in, 1, 2).astype(BF16)
    wo_shard = w_out.astype(BF16)
    conv_shard = jnp.concatenate([conv_a_w.reshape(depth * CONV_A, ch), conv_c_w.reshape(depth * CONV_C, ch)], axis=0)
    conv_all = _all_gather([jnp.pad(conv_shard, ((0, (-conv_shard.shape[0]) % SUBLANES), (0, 0)))], "ag_conv")[0]
    conv_all = jnp.swapaxes(conv_all, 0, 1).reshape(conv_all.shape[1], dm.wa)
    conv_a_full = conv_all[:depth * CONV_A].reshape(depth, CONV_A, dm.wa)
    conv_c_full = conv_all[depth * CONV_A:depth * (CONV_A + CONV_C)].reshape(depth, CONV_C, dm.wc)
    sinks_wide = jnp.broadcast_to(sinks[:, :, None], (depth, dm.nq, 2 * ATT_BLOCK))

    def layer_params(l):
        return (conv_a_full[l], conv_c_full[l], conv_c_b[l][None], gate_r_w[l], gate_r_b[l][None], gate_i_w[l],
                gate_i_b[l][None], rg_lambda[l][None], norm_a[l][None], norm_b[l][None], norm_c[l][None], sinks_wide[l])

    tm = _tile(t, 1024)
    xs = x.reshape(t, d)
    xb = xs.astype(BF16)
    saved = []
    for l in range(depth):
        wt, wo = _all_gather([wt_shard[l], wo_shard[l]], "ag_weights")
        wt = wt.reshape(dm.in_w, d)
        wo = wo.reshape(d, d)
        h = _matmul(xb, wt, dims=NT, tm=tm, tn=_tile(dm.in_w, 512), tk=d, out_dtype=F32, name="mm_in")
        mix, sst = _mixer_fwd(dm, h, layer_params(l), b_loc, nblk)
        z = _matmul(mix, wo, dims=NN, tm=tm, tn=_tile(d, 512), tk=d, out_dtype=F32, name="mm_out", addend=xs, alpha=alpha)
        saved.append((xb, h, sst, mix, z, wt, wo))
        xs, xb = _ln_fwd(z, ln_g[l][None], ln_b[l][None])

    dy, loss_part = _loss_head(xs, loss_target.reshape(t, d))
    loss = lax.psum(loss_part[0, 0], ("x", "y", "c"))

    g_wt, g_wo, small = [None] * depth, [None] * depth, [None] * depth
    for l in reversed(range(depth)):
        xb, h, sst, mix, z, wt, wo = saved[l]
        dz, dzb, d_lng, d_lnb = _ln_bwd(dy, z, ln_g[l][None])
        dmix = _matmul(dzb, wo, dims=NT, tm=tm, tn=_tile(d, 512), tk=d, out_dtype=F32, name="mm_dmix")
        dwo = _matmul(mix, dzb, dims=TN, tm=_tile(d, 1024), tn=_tile(d, 1024), tk=_tile(t, 1024), out_dtype=BF16, name="mm_dwo")
        dh, sm = _mixer_bwd(dm, h, sst, dmix, layer_params(l), b_loc, nblk)
        dy = _matmul(dh, wt, dims=NN, tm=tm, tn=_tile(d, 512), tk=_tile(dm.in_w, 3584), out_dtype=F32, name="mm_dx",
                     addend=dz, alpha=alpha)
        dwt = _matmul(dh, xb, dims=TN, tm=_tile(dm.in_w, 1536), tn=_tile(d, 1024), tk=_tile(t, 1024), out_dtype=BF16, name="mm_dwt")
        g_wt[l], g_wo[l] = _reduce_scatter([dwt.reshape(N_DEV, dm.in_w // N_DEV, d), dwo.reshape(N_DEV, d // N_DEV, d)], "w")
        (d_caw, d_ccw, d_ccb, d_grw, d_grb, d_giw, d_gib, d_lam, d_na, d_nb, d_nc, d_snk) = sm
        small[l] = dict(conv_a_w=d_caw, conv_c_w=d_ccw, conv_c_b=d_ccb[0], gate_r_w=d_grw, gate_r_b=d_grb[0], gate_i_w=d_giw,
                        gate_i_b=d_gib[0], rg_lambda=d_lam[0], norm_a=d_na[0], norm_b=d_nb[0], norm_c=d_nc[0],
                        sinks=d_snk[:, 0], ln_g=d_lng[0], ln_b=d_lnb[0])
    grad_x = dy.reshape(b_loc, seq, d)

    gin = jnp.swapaxes(jnp.stack(g_wt), 1, 2)
    cols = dm.in_w // N_DEV
    gw_in, dl_in, nm_in, nv_in = [o.reshape(depth, d, cols) for o in _adamw(
        w_in.reshape(depth * d, cols), gin.reshape(1, depth * d, cols), m_w_in.reshape(depth * d, cols),
        v_w_in.reshape(depth * d, cols), "adamw_in")]
    rows = d // N_DEV
    gw_out, dl_out, nm_out, nv_out = [o.reshape(depth, rows, d) for o in _adamw(
        w_out.reshape(depth * rows, d), jnp.stack(g_wo).reshape(1, depth * rows, d), m_w_out.reshape(depth * rows, d),
        v_w_out.reshape(depth * rows, d), "adamw_out")]

    given = dict(conv_a_w=(conv_a_w, m_conv_a_w, v_conv_a_w), conv_c_w=(conv_c_w, m_conv_c_w, v_conv_c_w),
                 conv_c_b=(conv_c_b, m_conv_c_b, v_conv_c_b), gate_r_w=(gate_r_w, m_gate_r_w, v_gate_r_w),
                 gate_r_b=(gate_r_b, m_gate_r_b, v_gate_r_b), gate_i_w=(gate_i_w, m_gate_i_w, v_gate_i_w),
                 gate_i_b=(gate_i_b, m_gate_i_b, v_gate_i_b), rg_lambda=(rg_lambda, m_rg_lambda, v_rg_lambda),
                 norm_a=(norm_a, m_norm_a, v_norm_a), norm_b=(norm_b, m_norm_b, v_norm_b), norm_c=(norm_c, m_norm_c, v_norm_c),
                 sinks=(sinks, m_sinks, v_sinks), ln_g=(ln_g, m_ln_g, v_ln_g), ln_b=(ln_b, m_ln_b, v_ln_b))
    full_shapes = [jnp.stack([small[l][n] for l in range(depth)]).shape for n in SMALL]
    g_local = _pack([jnp.stack([small[l][n] for l in range(depth)]) for n in SMALL])
    g_all = _all_gather([g_local], "ag_small")[0]

    def mine_of(n, a):
        if n in ("conv_a_w", "conv_c_w"):
            return lax.dynamic_update_slice(jnp.zeros(a.shape[:2] + (dm.wa,), F32), a, (0, 0, dev * ch))
        return a

    packs = [_pack([mine_of(n, given[n][i]) for n in SMALL]) for i in range(3)]
    outs = _adamw(packs[0], g_all, packs[1], packs[2], "adamw_small")
    res = {}
    for kind, packed in zip(("grad", "delta", "new_m", "new_v"), outs):
        for n, a in zip(SMALL, _unpack(packed, full_shapes)):
            if n in ("conv_a_w", "conv_c_w"):
                a = lax.dynamic_slice(a, (0, 0, dev * ch), a.shape[:2] + (ch,))
            res[kind, n] = a
    res.update({("grad", "w_in"): gw_in, ("delta", "w_in"): dl_in, ("new_m", "w_in"): nm_in, ("new_v", "w_in"): nv_in,
                ("grad", "w_out"): gw_out, ("delta", "w_out"): dl_out, ("new_m", "w_out"): nm_out, ("new_v", "w_out"): nv_out})
    order = ("w_in", "conv_a_w", "sinks", "conv_c_w", "conv_c_b", "gate_r_w", "gate_r_b", "gate_i_w", "gate_i_b", "rg_lambda",
             "norm_a", "norm_b", "norm_c", "w_out", "ln_g", "ln_b")
    return (loss, grad_x, *[res[kind, n] for kind in ("grad", "delta", "new_m", "new_v") for n in order])
```

```python
import functools

import jax
import jax.numpy as jnp
from jax import lax
from jax.experimental import pallas as pl
from jax.experimental.pallas import tpu as pltpu

F32 = jnp.float32
BF16 = jnp.bfloat16
MESH = pl.DeviceIdType.MESH
ANY = pl.BlockSpec(memory_space=pl.ANY)

N_DEV = 8
LANES = 128
SUBLANES = 8
HEAD_DIM = 64
KV_GROUP = 8
ATT_BLOCK = 128
CONV_A = 3
CONV_C = 4
RG_C = 8.0
LN_EPS = 1e-5
RMS_EPS = 1e-6
NEG_INF = -1e30
ADAM_LR = 0.001
ADAM_B1 = 0.9
ADAM_B2 = 0.999
ADAM_EPS = 1e-08
ADAM_WD = 0.01
ADAM_STEP = 10
VMEM_LIMIT = 56 * 1024 * 1024

NN = ((1,), (0,))
NT = ((1,), (1,))
TN = ((0,), (0,))


def _pcall(body, **kw):
    return pl.pallas_call(body, **kw)


def _roll(x, shift, axis):
    return pltpu.roll(x, shift, axis)


def _params(sem=None, vmem=VMEM_LIMIT):
    return pltpu.CompilerParams(dimension_semantics=sem, vmem_limit_bytes=vmem)


def _dot(a, b, dims):
    return lax.dot_general(a.astype(BF16), b.astype(BF16), (dims, ((), ())), preferred_element_type=F32)


@jax.custom_vjp
def _mm(a, b):
    return _dot(a, b, NN)


def _mm_fwd(a, b):
    return _dot(a, b, NN), (a.astype(BF16), b.astype(BF16))


def _mm_bwd(res, g):
    a, b = res
    return _dot(g, b, NT), _dot(a, g, TN)


_mm.defvjp(_mm_fwd, _mm_bwd)


@jax.custom_vjp
def _mm_nt(a, b):
    return _dot(a, b, NT)


def _mm_nt_fwd(a, b):
    return _dot(a, b, NT), (a.astype(BF16), b.astype(BF16))


def _mm_nt_bwd(res, g):
    a, b = res
    return _dot(g, b, NN), _dot(g, a, TN)


_mm_nt.defvjp(_mm_nt_fwd, _mm_nt_bwd)


def _rows(shape):
    return lax.broadcasted_iota(jnp.int32, shape, 0)


@functools.partial(jax.custom_vjp, nondiff_argnums=(2,))
def _shift_halo(u, prev, k):
    r, c = u.shape
    fill = jnp.concatenate([_roll(prev, k, 0), jnp.zeros((r - SUBLANES, c), u.dtype)], axis=0)
    return jnp.where(_rows((r, c)) < k, fill, _roll(u, k, 0))


def _shift_halo_fwd(u, prev, k):
    return _shift_halo(u, prev, k), None


def _shift_halo_bwd(k, _, g):
    r, c = g.shape
    du = jnp.where(_rows((r, c)) < r - k, _roll(g, r - k, 0), 0.0)
    dprev = jnp.where(_rows((SUBLANES, c)) >= SUBLANES - k, _roll(g[0:SUBLANES], SUBLANES - k, 0), 0.0)
    return du, dprev


_shift_halo.defvjp(_shift_halo_fwd, _shift_halo_bwd)


@functools.partial(jax.custom_vjp, nondiff_argnums=(1, 2))
def _shift_fill(u, k, fill):
    return jnp.where(_rows(u.shape) < k, fill, _roll(u, k, 0))


def _shift_fill_fwd(u, k, fill):
    return _shift_fill(u, k, fill), None


def _shift_fill_bwd(k, fill, _, g):
    r = g.shape[0]
    return (jnp.where(_rows(g.shape) < r - k, _roll(g, r - k, 0), 0.0),)


_shift_fill.defvjp(_shift_fill_fwd, _shift_fill_bwd)


@jax.custom_vjp
def _swap_halves(x):
    return _roll(x, HEAD_DIM, 1)


_swap_halves.defvjp(lambda x: (_roll(x, HEAD_DIM, 1), None), lambda _, g: (_roll(g, HEAD_DIM, 1),))


@functools.partial(jax.custom_vjp, nondiff_argnums=(1,))
def _split_rows(x, n):
    r = x.shape[0] // n
    return tuple(x[i * r:(i + 1) * r] for i in range(n))


def _split_rows_fwd(x, n):
    return _split_rows(x, n), None


def _split_rows_bwd(n, _, gs):
    return (jnp.concatenate(list(gs), axis=0),)


_split_rows.defvjp(_split_rows_fwd, _split_rows_bwd)


def _sigmoid(x):
    return 1.0 / (1.0 + jnp.exp(-x))


def _silu(x):
    return x * _sigmoid(x)


def _log_sigmoid(x):
    return -(jnp.maximum(-x, 0.0) + jnp.log1p(jnp.exp(-jnp.abs(x))))


def _neg_expm1(x):
    series = x * (1 + x * (1 / 2) * (1 + x * (1 / 3) * (1 + x * (1 / 4) * (1 + x * (1 / 5) * (1 + x * (1 / 6) * (1 + x * (1 / 7)))))))
    return -jnp.where(jnp.abs(x) < 0.25, series, jnp.exp(x) - 1.0)


def _scan_block(a, u, s_prev):
    acc_a, acc_b = a, u
    d = 1
    while d < a.shape[0]:
        acc_b = acc_a * _shift_fill(acc_b, d, 0.0) + acc_b
        acc_a = acc_a * _shift_fill(acc_a, d, 1.0)
        d *= 2
    return acc_a * s_prev + acc_b


def _last_row(h):
    return jnp.sum(jnp.where(_rows(h.shape) == h.shape[0] - 1, h, 0.0), axis=0, keepdims=True)


def _branch_a(ab, ac, ax, ag, acp, axp, w0, w1, w2, na):
    u = ac * ax
    up = acp * axp
    ya = ab * (w2 * u + w1 * _shift_halo(u, up, 1) + w0 * _shift_halo(u, up, 2))
    ms = jnp.sum(ya * ya, axis=1, keepdims=True) * (1.0 / ya.shape[1])
    return ya * lax.rsqrt(ms + RMS_EPS) * na * _silu(ag)


def _branch_c(cx, cg, cxp, sp, wc, bc, wr, br, wi, bi, lam, nc):
    hs, lasts = [], []
    for j in range(len(cx)):
        xc = (wc[3][j] * cx[j] + wc[2][j] * _shift_halo(cx[j], cxp[j], 1) + wc[1][j] * _shift_halo(cx[j], cxp[j], 2)
              + wc[0][j] * _shift_halo(cx[j], cxp[j], 3) + bc[j])
        r = _sigmoid(_mm(xc, wr[j]) + br[j])
        i = _sigmoid(_mm(xc, wi[j]) + bi[j])
        log_a = RG_C * r * _log_sigmoid(lam[j])
        a = jnp.exp(log_a)
        u = jnp.sqrt(_neg_expm1(2.0 * log_a)) * (i * xc)
        h = _scan_block(a, u, sp[j])
        hs.append(h)
        lasts.append(_last_row(h))
    width = LANES * len(cx)
    ms = sum(jnp.sum(h * h, axis=1, keepdims=True) for h in hs) * (1.0 / width)
    inv = lax.rsqrt(ms + RMS_EPS)
    return [hs[j] * inv * nc[j] * _silu(cg[j]) for j in range(len(cx))], lasts


def _branch_b(q, k, v, kp, vp, bg, snk, nb, first_thr):
    rows = ATT_BLOCK
    n_kv = 2 * len(k)
    gr = KV_GROUP * rows
    qi = _rows((gr, 2 * rows)) & (rows - 1)
    kj = lax.broadcasted_iota(jnp.int32, (gr, 2 * rows), 1)
    dist = qi + rows - kj
    valid = (dist >= 0) & (dist < rows) & (kj >= first_thr)
    lane0 = kj == 0
    upper = lax.broadcasted_iota(jnp.int32, (2 * rows, LANES), 1) >= HEAD_DIM
    heads = [None] * (n_kv * KV_GROUP)
    for g in range(n_kv):
        half = g % 2
        keep = upper if half else jnp.logical_not(upper)
        kc = jnp.where(keep, jnp.concatenate([kp[g // 2], k[g // 2]], axis=0), 0.0)
        vc = jnp.where(keep, jnp.concatenate([vp[g // 2], v[g // 2]], axis=0), 0.0)
        hs = range(g * KV_GROUP, (g + 1) * KV_GROUP)
        qg = jnp.concatenate([q[h // 2] if h % 2 == half else _swap_halves(q[h // 2]) for h in hs], axis=0)
        s = _mm_nt(qg, kc) * (HEAD_DIM ** -0.5)
        s = jnp.where(valid, s, NEG_INF)
        sink = jnp.concatenate([jnp.broadcast_to(snk[h], (rows, 2 * rows)) for h in hs], axis=0)
        m = lax.stop_gradient(jnp.maximum(jnp.max(s, axis=1, keepdims=True), sink))
        p = jnp.exp(s - m)
        e_sink = jnp.sum(jnp.where(lane0, jnp.exp(sink - m), 0.0), axis=1, keepdims=True)
        p = p / (jnp.sum(p, axis=1, keepdims=True) + e_sink)
        o = _split_rows(_mm(p, vc), KV_GROUP)
        for i, h in enumerate(hs):
            heads[h] = o[i] if h % 2 == half else _swap_halves(o[i])
    yb = [heads[2 * j] + heads[2 * j + 1] for j in range(len(q))]
    width = LANES * len(q)
    ms = sum(jnp.sum(y * y, axis=1, keepdims=True) for y in yb) * (1.0 / width)
    inv = lax.rsqrt(ms + RMS_EPS)
    return [yb[j] * inv * nb[j] * _silu(bg[j]) for j in range(len(q))]


class _Dims:
    def __init__(self, d_model, n_rg_heads):
        self.d = d_model
        self.wa = d_model // 4
        self.wb = d_model // 2
        self.wc = d_model // 4
        self.kvw = self.wb // KV_GROUP
        self.nq = self.wb // HEAD_DIM
        self.in_w = 4 * self.wa + 2 * self.wb + 2 * self.kvw + 2 * self.wc
        self.o_q = 4 * self.wa
        self.o_k = self.o_q + self.wb
        self.o_v = self.o_k + self.kvw
        self.o_bg = self.o_v + self.kvw
        self.o_cx = self.o_bg + self.wb
        self.o_cg = self.o_cx + self.wc
        self.nh = n_rg_heads
        assert self.wc // n_rg_heads == LANES and self.kvw % LANES == 0
        assert self.o_k % self.kvw == 0 and self.o_cx % (self.wc // 2) == 0


def _chunks(ref, rows, off, width):
    return [ref[rows, off + LANES * j: off + LANES * (j + 1)] for j in range(width // LANES)]


def _read_params(dm, caw, ccw, ccb, grw, grb, giw, gib, lam, na, nb, nc, snk):
    row = slice(0, 1)
    return dict(
        wa=[caw[k:k + 1, :] for k in range(CONV_A)], na=na[...],
        wc=[_chunks(ccw, slice(k, k + 1), 0, dm.wc) for k in range(CONV_C)], bc=_chunks(ccb, row, 0, dm.wc),
        wr=[grw[j] for j in range(dm.nh)], br=_chunks(grb, row, 0, dm.wc),
        wi=[giw[j] for j in range(dm.nh)], bi=_chunks(gib, row, 0, dm.wc),
        lam=_chunks(lam, row, 0, dm.wc), nc=_chunks(nc, row, 0, dm.wc),
        nb=_chunks(nb, row, 0, dm.wb), snk=[snk[h:h + 1, :] for h in range(dm.nq)])


def _param_specs(dm):
    shapes = [(CONV_A, dm.wa), (CONV_C, dm.wc), (1, dm.wc), (dm.nh, LANES, LANES), (1, dm.wc), (dm.nh, LANES, LANES),
              (1, dm.wc), (1, dm.wc), (1, dm.wa), (1, dm.wb), (1, dm.wc), (dm.nq, 2 * ATT_BLOCK)]
    specs = [pl.BlockSpec(s, (lambda b, n, _r=len(s): (0,) * _r)) for s in shapes]
    return shapes, specs


def _mixer_fwd(dm, h, prm, b_loc, nblk):
    t = h.shape[0]
    r = ATT_BLOCK
    tail = slice(r - SUBLANES, r)

    def body(h_ref, caw, ccw, ccb, grw, grb, giw, gib, lam, na, nb, nc, snk, mix_ref, sst_ref, kp, vp, acp, axp, cxp, sp):
        n = pl.program_id(1)

        @pl.when(n == 0)
        def _():
            for ref in (kp, vp, acp, axp, cxp, sp):
                ref[...] = jnp.zeros(ref.shape, ref.dtype)

        p = _read_params(dm, caw, ccw, ccb, grw, grb, giw, gib, lam, na, nb, nc, snk)
        full = slice(None)
        mix_a = _branch_a(h_ref[:, 0:dm.wa], h_ref[:, dm.wa:2 * dm.wa], h_ref[:, 2 * dm.wa:3 * dm.wa],
                          h_ref[:, 3 * dm.wa:4 * dm.wa], acp[...], axp[...], p["wa"][0], p["wa"][1], p["wa"][2], p["na"])
        mix_ref[:, 0:dm.wa] = mix_a.astype(BF16)
        first_thr = jnp.where(n == 0, ATT_BLOCK, 0)
        mix_b = _branch_b(_chunks(h_ref, full, dm.o_q, dm.wb), _chunks(h_ref, full, dm.o_k, dm.kvw),
                          _chunks(h_ref, full, dm.o_v, dm.kvw), _chunks(kp, full, 0, dm.kvw), _chunks(vp, full, 0, dm.kvw),
                          _chunks(h_ref, full, dm.o_bg, dm.wb), p["snk"], p["nb"], first_thr)
        for j, mb in enumerate(mix_b):
            mix_ref[:, dm.wa + LANES * j: dm.wa + LANES * (j + 1)] = mb.astype(BF16)
        sst_ref[0] = sp[...]
        mix_c, lasts = _branch_c(_chunks(h_ref, full, dm.o_cx, dm.wc), _chunks(h_ref, full, dm.o_cg, dm.wc),
                                 _chunks(cxp, full, 0, dm.wc), _chunks(sp, slice(0, 1), 0, dm.wc), p["wc"], p["bc"],
                                 p["wr"], p["br"], p["wi"], p["bi"], p["lam"], p["nc"])
        o_c = dm.wa + dm.wb
        for j, mc in enumerate(mix_c):
            mix_ref[:, o_c + LANES * j: o_c + LANES * (j + 1)] = mc.astype(BF16)
            sp[:, LANES * j: LANES * (j + 1)] = jnp.broadcast_to(lasts[j], (SUBLANES, LANES))
        kp[...] = h_ref[:, dm.o_k:dm.o_k + dm.kvw]
        vp[...] = h_ref[:, dm.o_v:dm.o_v + dm.kvw]
        acp[...] = h_ref[tail, dm.wa:2 * dm.wa]
        axp[...] = h_ref[tail, 2 * dm.wa:3 * dm.wa]
        cxp[...] = h_ref[tail, dm.o_cx:dm.o_cx + dm.wc]

    _, pspecs = _param_specs(dm)
    return _pcall(
        body, name="mixer_fwd", grid=(b_loc, nblk),
        in_specs=[pl.BlockSpec((r, dm.in_w), lambda b, n: (b * nblk + n, 0))] + pspecs,
        out_specs=[pl.BlockSpec((r, dm.d), lambda b, n: (b * nblk + n, 0)),
                   pl.BlockSpec((1, SUBLANES, dm.wc), lambda b, n: (b * nblk + n, 0, 0))],
        out_shape=[jax.ShapeDtypeStruct((t, dm.d), BF16), jax.ShapeDtypeStruct((b_loc * nblk, SUBLANES, dm.wc), F32)],
        scratch_shapes=[pltpu.VMEM((r, dm.kvw), F32), pltpu.VMEM((r, dm.kvw), F32), pltpu.VMEM((SUBLANES, dm.wa), F32),
                        pltpu.VMEM((SUBLANES, dm.wa), F32), pltpu.VMEM((SUBLANES, dm.wc), F32),
                        pltpu.VMEM((SUBLANES, dm.wc), F32)],
        compiler_params=_params(("arbitrary", "arbitrary")),
    )(h, *prm)


def _mixer_bwd(dm, h, sst, dmix, prm, b_loc, nblk):
    t = h.shape[0]
    r = ATT_BLOCK
    rb8 = r // SUBLANES
    n_small = 12

    def body(h_ref, kp_ref, vp_ref, acp_ref, axp_ref, cxp0_ref, cxp1_ref, sst_ref, dmix_ref,
             caw, ccw, ccb, grw, grb, giw, gib, lam, na, nb, nc, snk,
             dh_ref, d_caw, d_ccw, d_ccb, d_grw, d_grb, d_giw, d_gib, d_lam, d_na, d_nb, d_nc, d_snk,
             dkp, dvp, dacp, daxp, dcxp, dsp):
        step = pl.program_id(1)
        n = nblk - 1 - step

        @pl.when(step == 0)
        def _():
            for ref in (dkp, dvp, dacp, daxp, dcxp, dsp):
                ref[...] = jnp.zeros(ref.shape, ref.dtype)

        @pl.when((step == 0) & (pl.program_id(0) == 0))
        def _():
            for ref in (d_caw, d_ccw, d_ccb, d_grw, d_grb, d_giw, d_gib, d_lam, d_na, d_nb, d_nc, d_snk):
                ref[...] = jnp.zeros(ref.shape, ref.dtype)

        p = _read_params(dm, caw, ccw, ccb, grw, grb, giw, gib, lam, na, nb, nc, snk)
        has_prev = jnp.where(n > 0, 1.0, 0.0)
        full = slice(None)
        pad = jnp.zeros((r - SUBLANES, LANES), F32)

        def with_tail(own, carry):
            z = jnp.zeros((r - SUBLANES, own.shape[1]), F32)
            return own + jnp.concatenate([z, carry], axis=0)

        a_in = (h_ref[:, 0:dm.wa], h_ref[:, dm.wa:2 * dm.wa], h_ref[:, 2 * dm.wa:3 * dm.wa], h_ref[:, 3 * dm.wa:4 * dm.wa],
                acp_ref[...] * has_prev, axp_ref[...] * has_prev, p["wa"][0], p["wa"][1], p["wa"][2], p["na"])
        _, vjp_a = jax.vjp(_branch_a, *a_in)
        g_ab, g_ac, g_ax, g_ag, g_acp, g_axp, g_w0, g_w1, g_w2, g_na = vjp_a(dmix_ref[:, 0:dm.wa])
        dh_ref[:, 0:dm.wa] = g_ab.astype(BF16)
        dh_ref[:, dm.wa:2 * dm.wa] = with_tail(g_ac, dacp[...]).astype(BF16)
        dh_ref[:, 2 * dm.wa:3 * dm.wa] = with_tail(g_ax, daxp[...]).astype(BF16)
        dh_ref[:, 3 * dm.wa:4 * dm.wa] = g_ag.astype(BF16)
        dacp[...] = g_acp
        daxp[...] = g_axp
        for k, gw in enumerate((g_w0, g_w1, g_w2)):
            d_caw[k:k + 1, :] += gw
        d_na[...] += g_na

        first_thr = jnp.where(n == 0, ATT_BLOCK, 0)
        kp_in = [c * has_prev for c in _chunks(kp_ref, full, 0, dm.kvw)]
        vp_in = [c * has_prev for c in _chunks(vp_ref, full, 0, dm.kvw)]
        b_in = (_chunks(h_ref, full, dm.o_q, dm.wb), _chunks(h_ref, full, dm.o_k, dm.kvw), _chunks(h_ref, full, dm.o_v, dm.kvw),
                kp_in, vp_in, _chunks(h_ref, full, dm.o_bg, dm.wb), p["snk"], p["nb"])
        _, vjp_b = jax.vjp(lambda *a: _branch_b(*a, first_thr), *b_in)
        g_q, g_k, g_v, g_kp, g_vp, g_bg, g_snk, g_nb = vjp_b(_chunks(dmix_ref, full, dm.wa, dm.wb))
        for j in range(len(g_q)):
            dh_ref[:, dm.o_q + LANES * j: dm.o_q + LANES * (j + 1)] = g_q[j].astype(BF16)
            dh_ref[:, dm.o_bg + LANES * j: dm.o_bg + LANES * (j + 1)] = g_bg[j].astype(BF16)
            d_nb[:, LANES * j: LANES * (j + 1)] += g_nb[j]
        for j in range(len(g_k)):
            cols = slice(LANES * j, LANES * (j + 1))
            dh_ref[:, dm.o_k + LANES * j: dm.o_k + LANES * (j + 1)] = (g_k[j] + dkp[:, cols]).astype(BF16)
            dh_ref[:, dm.o_v + LANES * j: dm.o_v + LANES * (j + 1)] = (g_v[j] + dvp[:, cols]).astype(BF16)
            dkp[:, cols] = g_kp[j]
            dvp[:, cols] = g_vp[j]
        for hd in range(dm.nq):
            d_snk[hd:hd + 1, :] += g_snk[hd]

        half_c = dm.wc // 2
        cxp_in = ([c * has_prev for c in _chunks(cxp0_ref, full, 0, half_c)]
                  + [c * has_prev for c in _chunks(cxp1_ref, full, 0, half_c)])
        c_in = (_chunks(h_ref, full, dm.o_cx, dm.wc), _chunks(h_ref, full, dm.o_cg, dm.wc), cxp_in,
                [sst_ref[0, 0:1, LANES * j: LANES * (j + 1)] for j in range(dm.nh)], p["wc"], p["bc"], p["wr"], p["br"], p["wi"], p["bi"],
                p["lam"], p["nc"])
        _, vjp_c = jax.vjp(_branch_c, *c_in)
        ct_last = [dsp[0:1, LANES * j: LANES * (j + 1)] for j in range(dm.nh)]
        g_cx, g_cg, g_cxp, g_sp, g_wc, g_bc, g_wr, g_br, g_wi, g_bi, g_lam, g_nc = vjp_c(
            (_chunks(dmix_ref, full, dm.wa + dm.wb, dm.wc), ct_last))
        for j in range(dm.nh):
            cols = slice(LANES * j, LANES * (j + 1))
            tot = g_cx[j] + jnp.concatenate([pad, dcxp[:, cols]], axis=0)
            dh_ref[:, dm.o_cx + LANES * j: dm.o_cx + LANES * (j + 1)] = tot.astype(BF16)
            dh_ref[:, dm.o_cg + LANES * j: dm.o_cg + LANES * (j + 1)] = g_cg[j].astype(BF16)
            dcxp[:, cols] = g_cxp[j]
            dsp[:, cols] = jnp.broadcast_to(g_sp[j], (SUBLANES, LANES))
            for k in range(CONV_C):
                d_ccw[k:k + 1, cols] += g_wc[k][j]
            d_ccb[:, cols] += g_bc[j]
            d_grw[j] += g_wr[j]
            d_grb[:, cols] += g_br[j]
            d_giw[j] += g_wi[j]
            d_gib[:, cols] += g_bi[j]
            d_lam[:, cols] += g_lam[j]
            d_nc[:, cols] += g_nc[j]

    def blk(b, s):
        return b * nblk + (nblk - 1 - s)

    def prev_rows8(b, s):
        return jnp.maximum(blk(b, s) * rb8 - 1, 0)

    pshapes, pspecs = _param_specs(dm)
    half_c = dm.wc // 2
    in_specs = [
        pl.BlockSpec((r, dm.in_w), lambda b, s: (blk(b, s), 0)),
        pl.BlockSpec((r, dm.kvw), lambda b, s: (jnp.maximum(blk(b, s) - 1, 0), dm.o_k // dm.kvw)),
        pl.BlockSpec((r, dm.kvw), lambda b, s: (jnp.maximum(blk(b, s) - 1, 0), dm.o_v // dm.kvw)),
        pl.BlockSpec((SUBLANES, dm.wa), lambda b, s: (prev_rows8(b, s), 1)),
        pl.BlockSpec((SUBLANES, dm.wa), lambda b, s: (prev_rows8(b, s), 2)),
        pl.BlockSpec((SUBLANES, half_c), lambda b, s: (prev_rows8(b, s), dm.o_cx // half_c)),
        pl.BlockSpec((SUBLANES, half_c), lambda b, s: (prev_rows8(b, s), dm.o_cx // half_c + 1)),
        pl.BlockSpec((1, SUBLANES, dm.wc), lambda b, s: (blk(b, s), 0, 0)),
        pl.BlockSpec((r, dm.d), lambda b, s: (blk(b, s), 0)),
    ] + pspecs
    outs = _pcall(
        body, name="mixer_bwd", grid=(b_loc, nblk), in_specs=in_specs,
        out_specs=[pl.BlockSpec((r, dm.in_w), lambda b, s: (blk(b, s), 0))] + pspecs,
        out_shape=[jax.ShapeDtypeStruct((t, dm.in_w), BF16)] + [jax.ShapeDtypeStruct(s, F32) for s in pshapes],
        scratch_shapes=[pltpu.VMEM((r, dm.kvw), F32), pltpu.VMEM((r, dm.kvw), F32), pltpu.VMEM((SUBLANES, dm.wa), F32),
                        pltpu.VMEM((SUBLANES, dm.wa), F32), pltpu.VMEM((SUBLANES, dm.wc), F32),
                        pltpu.VMEM((SUBLANES, dm.wc), F32)],
        compiler_params=_params(("arbitrary", "arbitrary")),
    )(h, h, h, h, h, h, h, sst, dmix, *prm)
    assert len(outs) == 1 + n_small
    return outs[0], outs[1:]


def _token_spec():
    return pl.BlockSpec((SUBLANES, LANES), lambda *_: (0, 0))


def _no_token():
    return jnp.zeros((SUBLANES, LANES), F32)


def _matmul(a, b, *, dims, tm, tn, tk, out_dtype, name, addend=None, alpha=None, token=None):
    if dims == TN:
        (k_dim, m), n_dim = a.shape, b.shape[1]
        a_spec = pl.BlockSpec((tk, tm), lambda i, j, k: (k, i))
    else:
        (m, k_dim), n_dim = a.shape, (b.shape[0] if dims == NT else b.shape[1])
        a_spec = pl.BlockSpec((tm, tk), lambda i, j, k: (i, k))
    b_spec = pl.BlockSpec((tn, tk), lambda i, j, k: (j, k)) if dims == NT else pl.BlockSpec((tk, tn), lambda i, j, k: (k, j))
    assert m % tm == 0 and n_dim % tn == 0 and k_dim % tk == 0, (a.shape, b.shape, tm, tn, tk)
    nk = k_dim // tk
    o_spec = pl.BlockSpec((tm, tn), lambda i, j, k: (i, j))

    def body(*refs):
        a_ref, b_ref = refs[0], refs[1]
        add_ref = refs[2] if addend is not None else None
        o_ref, acc_ref = refs[-2], refs[-1]
        k = pl.program_id(2)
        part = lax.dot_general(a_ref[...], b_ref[...], (dims, ((), ())), preferred_element_type=F32)

        def finish(acc):
            if add_ref is not None:
                acc = acc + alpha * add_ref[...]
            o_ref[...] = acc.astype(out_dtype)

        if nk == 1:
            finish(part)
        else:
            @pl.when(k == 0)
            def _():
                acc_ref[...] = part

            @pl.when((k > 0) & (k < nk - 1))
            def _():
                acc_ref[...] += part

            @pl.when(k == nk - 1)
            def _():
                finish(acc_ref[...] + part)

    ins = [a, b] + ([addend] if addend is not None else []) + ([token] if token is not None else [])
    in_specs = [a_spec, b_spec] + ([o_spec] if addend is not None else []) + ([_token_spec()] if token is not None else [])
    return _pcall(
        body, name=name, grid=(m // tm, n_dim // tn, nk), in_specs=in_specs, out_specs=o_spec,
        out_shape=jax.ShapeDtypeStruct((m, n_dim), out_dtype),
        scratch_shapes=[pltpu.VMEM((tm, tn) if nk > 1 else (SUBLANES, LANES), F32)],
        compiler_params=_params(("parallel", "parallel", "arbitrary")),
    )(*ins)


def _tile(n, want, quantum=LANES):
    if n <= want:
        return n
    for cand in range(want - want % quantum, 0, -quantum):
        if n % cand == 0:
            return cand
    return n


def _row_tile(t, d):
    return _tile(t, max(2 * SUBLANES, (1 << 19) // d), 2 * SUBLANES)


def _ln_fwd(z, g, b, token):
    t, d = z.shape
    tr = _row_tile(t, d)

    def body(z_ref, g_ref, b_ref, _, y_ref, yb_ref):
        zz = z_ref[...]
        mu = jnp.mean(zz, axis=1, keepdims=True)
        zc = zz - mu
        var = jnp.mean(zc * zc, axis=1, keepdims=True)
        y = zc * lax.rsqrt(var + LN_EPS) * g_ref[...] + b_ref[...]
        y_ref[...] = y
        yb_ref[...] = y.astype(BF16)

    row = pl.BlockSpec((tr, d), lambda i: (i, 0))
    vec = pl.BlockSpec((1, d), lambda i: (0, 0))
    return _pcall(body, name="ln_fwd", grid=(t // tr,), in_specs=[row, vec, vec, _token_spec()], out_specs=[row, row],
                  out_shape=[jax.ShapeDtypeStruct((t, d), F32), jax.ShapeDtypeStruct((t, d), BF16)],
                  compiler_params=_params(("parallel",)))(z, g, b, token)


def _ln_bwd(dy, z, g, token):
    t, d = z.shape
    tr = _row_tile(t, d)

    def body(dy_ref, z_ref, g_ref, _, dz_ref, dzb_ref, dg_ref, db_ref):
        @pl.when(pl.program_id(0) == 0)
        def _():
            dg_ref[...] = jnp.zeros(dg_ref.shape, F32)
            db_ref[...] = jnp.zeros(db_ref.shape, F32)

        zz = z_ref[...]
        dyy = dy_ref[...]
        mu = jnp.mean(zz, axis=1, keepdims=True)
        zc = zz - mu
        rstd = lax.rsqrt(jnp.mean(zc * zc, axis=1, keepdims=True) + LN_EPS)
        xhat = zc * rstd
        dyg = dyy * g_ref[...]
        dz = rstd * (dyg - jnp.mean(dyg, axis=1, keepdims=True) - xhat * jnp.mean(dyg * xhat, axis=1, keepdims=True))
        dz_ref[...] = dz
        dzb_ref[...] = dz.astype(BF16)
        dg_ref[...] += jnp.sum(dyy * xhat, axis=0, keepdims=True)
        db_ref[...] += jnp.sum(dyy, axis=0, keepdims=True)

    row = pl.BlockSpec((tr, d), lambda i: (i, 0))
    vec = pl.BlockSpec((1, d), lambda i: (0, 0))
    return _pcall(body, name="ln_bwd", grid=(t // tr,), in_specs=[row, row, vec, _token_spec()], out_specs=[row, row, vec, vec],
                  out_shape=[jax.ShapeDtypeStruct((t, d), F32), jax.ShapeDtypeStruct((t, d), BF16),
                             jax.ShapeDtypeStruct((1, d), F32), jax.ShapeDtypeStruct((1, d), F32)],
                  compiler_params=_params(("arbitrary",)))(dy, z, g, token)


def _loss_head(y, target):
    t, d = y.shape
    tr = _row_tile(t, d)

    def body(y_ref, t_ref, dy_ref, loss_ref):
        @pl.when(pl.program_id(0) == 0)
        def _():
            loss_ref[...] = jnp.zeros(loss_ref.shape, F32)

        err = y_ref[...] - t_ref[...]
        dy_ref[...] = err * (1.0 / d)
        per_token = jnp.sum(err * err, axis=1, keepdims=True) * (1.0 / d)
        loss_ref[...] += 0.5 * jnp.sum(per_token, axis=0, keepdims=True)

    row = pl.BlockSpec((tr, d), lambda i: (i, 0))
    one = pl.BlockSpec((1, 1), lambda i: (0, 0))
    return _pcall(body, name="loss_head", grid=(t // tr,), in_specs=[row, row], out_specs=[row, one],
                  out_shape=[jax.ShapeDtypeStruct((t, d), F32), jax.ShapeDtypeStruct((1, 1), F32)],
                  compiler_params=_params(("arbitrary",)))(y, target)


def _chip_sum(sums, far, name):
    r, c = sums.shape[1:]
    tr = _row_tile(r, c)

    def body(s_ref, f_ref, o_ref):
        acc = s_ref[0].astype(F32)
        for i in range(3):
            acc = acc + f_ref[i].astype(F32)
        o_ref[...] = acc

    return _pcall(body, name=name, grid=(r // tr,),
                  in_specs=[pl.BlockSpec((1, tr, c), lambda i: (2 * lax.axis_index("x") + lax.axis_index("y"), i, 0)),
                            pl.BlockSpec((3, tr, c), lambda i: (0, i, 0))],
                  out_specs=pl.BlockSpec((tr, c), lambda i: (i, 0)), out_shape=jax.ShapeDtypeStruct((r, c), F32),
                  compiler_params=_params(("parallel",)))(sums, far)


def _pair_add(a, b, name):
    p, r, c = b.shape
    tr = _row_tile(r, c)

    def body(a_ref, b_ref, o_ref):
        o_ref[...] = (a_ref[...].astype(F32) + b_ref[...].astype(F32)).astype(BF16)

    spec = pl.BlockSpec((1, tr, c), lambda q, i: (q, i, 0))
    return _pcall(body, name=name, grid=(p, r // tr),
                  in_specs=[pl.BlockSpec((1, tr, c), lambda q, i: (2 * q + lax.axis_index("c"), i, 0)), spec], out_specs=spec,
                  out_shape=jax.ShapeDtypeStruct((p, r, c), BF16), compiler_params=_params(("parallel", "parallel")))(a, b)


def _adamw(w, g_parts, m, v, name):
    r, c = w.shape
    n_parts = g_parts.shape[0]
    tr = _row_tile(r, c) if r % SUBLANES == 0 else r

    def body(w_ref, g_ref, m_ref, v_ref, go_ref, d_ref, mo_ref, vo_ref):
        g = g_ref[0].astype(F32)
        for i in range(1, n_parts):
            g = g + g_ref[i].astype(F32)
        m_new = ADAM_B1 * m_ref[...] + (1.0 - ADAM_B1) * g
        v_new = ADAM_B2 * v_ref[...] + (1.0 - ADAM_B2) * (g * g)
        m_hat = m_new / (1.0 - ADAM_B1 ** ADAM_STEP)
        v_hat = v_new / (1.0 - ADAM_B2 ** ADAM_STEP)
        go_ref[...] = g
        d_ref[...] = -ADAM_LR * (m_hat / (jnp.sqrt(v_hat) + ADAM_EPS) + ADAM_WD * w_ref[...])
        mo_ref[...] = m_new
        vo_ref[...] = v_new

    spec = pl.BlockSpec((tr, c), lambda i: (i, 0))
    shape = jax.ShapeDtypeStruct((r, c), F32)
    return _pcall(body, name=name, grid=(r // tr,),
                  in_specs=[spec, pl.BlockSpec((n_parts, tr, c), lambda i: (0, i, 0)), spec, spec],
                  out_specs=[spec] * 4, out_shape=[shape] * 4, compiler_params=_params(("parallel",)))(w, g_parts, m, v)


def _me():
    return lax.axis_index("x"), lax.axis_index("y"), lax.axis_index("c")


def _dev(px, py, pc):
    return 4 * px + 2 * py + pc


def _remote(src, dst, send_sems, recv_sems, k, to):
    return pltpu.make_async_remote_copy(src_ref=src, dst_ref=dst, send_sem=send_sems.at[k], recv_sem=recv_sems.at[k],
                                        device_id=to, device_id_type=MESH)


def _all_gather(arrs, name):
    n = len(arrs)

    def body(*refs):
        ins, outs = refs[:n], refs[n:2 * n]
        send_sems, recv_sems, local_sems = refs[2 * n:]
        x, y, c = _me()
        me, sibling = (x, y, c), (x, y, 1 - c)
        chips = [(1 - x, y), (x, 1 - y), (1 - x, 1 - y)]
        pending = []
        for a in range(n):
            mine = pltpu.make_async_copy(ins[a], outs[a].at[_dev(*me)], local_sems.at[a])
            mine.start()
            pending.append(mine)
        sends = []
        for a in range(n):
            dst = outs[a].at[_dev(*me)]
            sends.append(_remote(ins[a], dst, send_sems, recv_sems, 7 * a, sibling))
            sends += [_remote(ins[a], dst, send_sems, recv_sems, 7 * a + 1 + j, (*chip, c)) for j, chip in enumerate(chips)]
        for cp in sends:
            cp.start()
        for a in range(n):
            for j, chip in enumerate(chips):
                blk = outs[a].at[_dev(*chip, c)]
                _remote(blk, blk, send_sems, recv_sems, 7 * a + 1 + j, me).wait_recv()
                fwd = _remote(blk, blk, send_sems, recv_sems, 7 * a + 4 + j, sibling)
                fwd.start()
                sends.append(fwd)
        for a in range(n):
            blk = outs[a].at[_dev(*sibling)]
            _remote(blk, blk, send_sems, recv_sems, 7 * a, me).wait_recv()
            for j, chip in enumerate(chips):
                blk = outs[a].at[_dev(*chip, 1 - c)]
                _remote(blk, blk, send_sems, recv_sems, 7 * a + 4 + j, me).wait_recv()
        for cp in sends:
            cp.wait_send()
        for cp in pending:
            cp.wait()

    return _pcall(
        body, name=name, in_specs=[ANY] * n, out_specs=[ANY] * n,
        out_shape=[jax.ShapeDtypeStruct((N_DEV,) + a.shape, a.dtype) for a in arrs],
        scratch_shapes=[pltpu.SemaphoreType.DMA((7 * n,)), pltpu.SemaphoreType.DMA((7 * n,)), pltpu.SemaphoreType.DMA((n,))],
    )(*arrs)


def _relations(x, y):
    return [(x, y), (1 - x, y), (x, 1 - y), (1 - x, 1 - y)]


HBM_SPEC = pl.BlockSpec(memory_space=pltpu.HBM)
SEM_SPEC = pl.BlockSpec(memory_space=pltpu.SEMAPHORE)
DATAFLOW = pltpu.SideEffectType.DATAFLOW_SIDE_EFFECTING


def _exchange_start(name, bufs, plan, n_remote, n_local, dep):
    nb = len(bufs)
    sem_shapes = [pltpu.SemaphoreType.DMA((n_remote,)), pltpu.SemaphoreType.DMA((n_remote,))]
    if n_local:
        sem_shapes.append(pltpu.SemaphoreType.DMA((n_local,)))
    ns = len(sem_shapes)

    def body(*refs):
        ins, sems, token = refs[:nb], refs[nb + 1:nb + 1 + ns], refs[-1]
        starts, _, local = plan(ins, sems[0], sems[1], sems[2] if n_local else None)
        for cp in local + starts:
            cp.start()
        token[...] = jnp.zeros(token.shape, F32)

    outs = _pcall(
        body, name=name, in_specs=[HBM_SPEC] * nb + [ANY],
        out_specs=[SEM_SPEC] * ns + [HBM_SPEC] * nb + [pl.BlockSpec(memory_space=pltpu.VMEM)],
        out_shape=sem_shapes + [pltpu.HBM(b.shape, b.dtype) for b in bufs] + [jax.ShapeDtypeStruct((SUBLANES, LANES), F32)],
        input_output_aliases={i: ns + i for i in range(nb)}, compiler_params=pltpu.CompilerParams(has_side_effects=DATAFLOW),
    )(*[pltpu.with_memory_space_constraint(b, pltpu.HBM) for b in bufs], dep)
    return dict(sems=outs[:ns], thru=outs[ns:ns + nb], plan=plan, n_local=n_local), outs[-1]


def _exchange_wait(name, handle, after):
    thru, sems, plan, n_local = handle["thru"], handle["sems"], handle["plan"], handle["n_local"]
    nb, ns = len(thru), len(sems)

    def body(*refs):
        ins, sem_refs = refs[:nb], refs[nb:nb + ns]
        starts, arrivals, local = plan(ins, sem_refs[0], sem_refs[1], sem_refs[2] if n_local else None)
        for cp in starts:
            cp.wait_send()
        for cp in arrivals:
            cp.wait_recv()
        for cp in local:
            cp.wait()

    return _pcall(
        body, name=name, in_specs=[HBM_SPEC] * nb + [SEM_SPEC] * ns + [ANY], out_specs=[HBM_SPEC] * nb,
        out_shape=[pltpu.HBM(b.shape, b.dtype) for b in thru], input_output_aliases={i: i for i in range(nb)},
        compiler_params=pltpu.CompilerParams(has_side_effects=DATAFLOW),
    )(*thru, *sems, after)


def _landing(shape, dtype):
    return lax.empty(shape, dtype)


def _plan_gather_ici(n):
    def plan(refs, send_sems, recv_sems, local_sems):
        x, y, c = _me()
        me, sibling = (x, y, c), (x, y, 1 - c)
        chips = _relations(x, y)[1:]
        starts, arrivals, local = [], [], []
        for a in range(n):
            shard, land = refs[a], refs[n + a]
            own = land.at[_dev(*me)]
            local.append(pltpu.make_async_copy(shard, own, local_sems.at[a]))
            starts.append(_remote(shard, own, send_sems, recv_sems, 4 * a, sibling))
            blk = land.at[_dev(*sibling)]
            arrivals.append(_remote(blk, blk, send_sems, recv_sems, 4 * a, me))
            for j, chip in enumerate(chips):
                starts.append(_remote(shard, own, send_sems, recv_sems, 4 * a + 1 + j, (*chip, c)))
                blk = land.at[_dev(*chip, c)]
                arrivals.append(_remote(blk, blk, send_sems, recv_sems, 4 * a + 1 + j, me))
        return starts, arrivals, local
    return plan


def _plan_gather_d2d(n):
    def plan(refs, send_sems, recv_sems, local_sems):
        x, y, c = _me()
        me, sibling = (x, y, c), (x, y, 1 - c)
        starts, arrivals = [], []
        for a in range(n):
            for j, chip in enumerate(_relations(x, y)[1:]):
                blk = refs[a].at[_dev(*chip, c)]
                starts.append(_remote(blk, blk, send_sems, recv_sems, 3 * a + j, sibling))
                blk = refs[a].at[_dev(*chip, 1 - c)]
                arrivals.append(_remote(blk, blk, send_sems, recv_sems, 3 * a + j, me))
        return starts, arrivals, []
    return plan


def _plan_scatter_d2d(n):
    def plan(refs, send_sems, recv_sems, local_sems):
        x, y, c = _me()
        me, sibling = (x, y, c), (x, y, 1 - c)
        starts, arrivals = [], []
        for a in range(n):
            for k in range(4):
                starts.append(_remote(refs[a].at[2 * k + 1 - c], refs[n + a].at[k], send_sems, recv_sems, 4 * a + k, sibling))
                blk = refs[n + a].at[k]
                arrivals.append(_remote(blk, blk, send_sems, recv_sems, 4 * a + k, me))
        return starts, arrivals, []
    return plan


def _plan_scatter_ici(n):
    def plan(refs, send_sems, recv_sems, local_sems):
        x, y, c = _me()
        me = (x, y, c)
        starts, arrivals = [], []
        for a in range(n):
            for j, (cx, cy) in enumerate(_relations(x, y)[1:]):
                starts.append(_remote(refs[a].at[2 * cx + cy], refs[n + a].at[j], send_sems, recv_sems, 3 * a + j, (cx, cy, c)))
                blk = refs[n + a].at[j]
                arrivals.append(_remote(blk, blk, send_sems, recv_sems, 3 * a + j, me))
        return starts, arrivals, []
    return plan


SMALL = ("conv_a_w", "conv_c_w", "conv_c_b", "gate_r_w", "gate_r_b", "gate_i_w", "gate_i_b", "rg_lambda",
         "norm_a", "norm_b", "norm_c", "sinks", "ln_g", "ln_b")
PACK_COLS = 1024


def _pack(arrs):
    flat = jnp.concatenate([a.reshape(-1) for a in arrs])
    pad = (-flat.shape[0]) % (SUBLANES * PACK_COLS)
    return jnp.pad(flat, (0, pad)).reshape(-1, PACK_COLS)


def _unpack(packed, shapes):
    flat = packed.reshape(-1)
    out, off = [], 0
    for s in shapes:
        size = 1
        for dim in s:
            size *= dim
        out.append(flat[off:off + size].reshape(s))
        off += size
    return out


def kernel(x, w_in, conv_a_w, sinks, conv_c_w, conv_c_b, gate_r_w, gate_r_b, gate_i_w, gate_i_b, rg_lambda, norm_a, norm_b, norm_c, w_out, ln_g, ln_b, loss_target, m_w_in, m_conv_a_w, m_sinks, m_conv_c_w, m_conv_c_b, m_gate_r_w, m_gate_r_b, m_gate_i_w, m_gate_i_b, m_rg_lambda, m_norm_a, m_norm_b, m_norm_c, m_w_out, m_ln_g, m_ln_b, v_w_in, v_conv_a_w, v_sinks, v_conv_c_w, v_conv_c_b, v_gate_r_w, v_gate_r_b, v_gate_i_w, v_gate_i_b, v_rg_lambda, v_norm_a, v_norm_b, v_norm_c, v_w_out, v_ln_g, v_ln_b):
    b_loc, seq, d = x.shape
    depth = w_in.shape[0]
    dm = _Dims(d, gate_r_w.shape[1])
    t = b_loc * seq
    nblk = seq // ATT_BLOCK
    alpha = (2.0 * depth) ** 0.25
    ch = dm.wa // N_DEV
    dev = _dev(*_me())

    wt_shard = jnp.swapaxes(w_in, 1, 2).astype(BF16)
    wo_shard = w_out.astype(BF16)
    conv_shard = jnp.concatenate([conv_a_w.reshape(depth * CONV_A, ch), conv_c_w.reshape(depth * CONV_C, ch)], axis=0)
    conv_all = _all_gather([jnp.pad(conv_shard, ((0, (-conv_shard.shape[0]) % SUBLANES), (0, 0)))], "ag_conv")[0]
    conv_all = jnp.swapaxes(conv_all, 0, 1).reshape(conv_all.shape[1], dm.wa)
    conv_a_full = conv_all[:depth * CONV_A].reshape(depth, CONV_A, dm.wa)
    conv_c_full = conv_all[depth * CONV_A:depth * (CONV_A + CONV_C)].reshape(depth, CONV_C, dm.wc)
    sinks_wide = jnp.broadcast_to(sinks[:, :, None], (depth, dm.nq, 2 * ATT_BLOCK))

    def layer_params(l):
        return (conv_a_full[l], conv_c_full[l], conv_c_b[l][None], gate_r_w[l], gate_r_b[l][None], gate_i_w[l],
                gate_i_b[l][None], rg_lambda[l][None], norm_a[l][None], norm_b[l][None], norm_c[l][None], sinks_wide[l])

    tm = _tile(t, 1024)
    xs = x.reshape(t, d)
    xb = xs.astype(BF16)
    saved = []
    rows_t, rows_o = dm.in_w // N_DEV, d // N_DEV
    wt, wo = _all_gather([wt_shard[0], wo_shard[0]], "ag_weights")
    for l in range(depth):
        token = _no_token()
        if l + 1 < depth:
            lands = [_landing((N_DEV, rows_t, d), BF16), _landing((N_DEV, rows_o, d), BF16)]
            ici, token = _exchange_start("ag_ici_start_%d" % l, [wt_shard[l + 1], wo_shard[l + 1]] + lands, _plan_gather_ici(2), 8, 2, wt)
        h = _matmul(xb, wt.reshape(dm.in_w, d), dims=NT, tm=tm, tn=_tile(dm.in_w, 512), tk=d, out_dtype=F32, name="mm_in",
                    token=token)
        mix, sst = _mixer_fwd(dm, h, layer_params(l), b_loc, nblk)
        z = _matmul(mix, wo.reshape(d, d), dims=NN, tm=tm, tn=_tile(d, 512), tk=d, out_dtype=F32, name="mm_out", addend=xs,
                    alpha=alpha)
        saved.append((xb, h, sst, mix, z, wt, wo))
        token = _no_token()
        if l + 1 < depth:
            lands = _exchange_wait("ag_ici_wait_%d" % l, ici, z)[2:]
            d2d, token = _exchange_start("ag_d2d_start_%d" % l, lands, _plan_gather_d2d(2), 6, 0, z)
        xs, xb = _ln_fwd(z, ln_g[l][None], ln_b[l][None], token)
        if l + 1 < depth:
            wt, wo = _exchange_wait("ag_d2d_wait_%d" % l, d2d, xb)

    dy, loss_part = _loss_head(xs, loss_target.reshape(t, d))
    loss = lax.psum(loss_part[0, 0], ("x", "y", "c"))

    g_wt, g_wo, small = [None] * depth, [None] * depth, [None] * depth
    ici, token_ici = None, _no_token()

    def finish_scatter(l, ici, after):
        done = _exchange_wait("rs_ici_wait_%d" % l, ici, after)
        g_wt[l], g_wo[l] = [_chip_sum(s, f, "rs_sum_%d" % i) for i, (s, f) in enumerate(zip(done[:2], done[2:]))]

    for l in reversed(range(depth)):
        xb, h, sst, mix, z, wt, wo = saved[l]
        dz, dzb, d_lng, d_lnb = _ln_bwd(dy, z, ln_g[l][None], token_ici)
        dmix = _matmul(dzb, wo.reshape(d, d), dims=NT, tm=tm, tn=_tile(d, 512), tk=d, out_dtype=F32, name="mm_dmix")
        dwo = _matmul(mix, dzb, dims=TN, tm=_tile(d, 1024), tn=_tile(d, 1024), tk=_tile(t, 1024), out_dtype=BF16, name="mm_dwo")
        dh, sm = _mixer_bwd(dm, h, sst, dmix, layer_params(l), b_loc, nblk)
        dwt = _matmul(dh, xb, dims=TN, tm=_tile(dm.in_w, 1536), tn=_tile(d, 1024), tk=_tile(t, 1024), out_dtype=BF16, name="mm_dwt")
        if ici is not None:
            finish_scatter(l + 1, ici, dh)
        parts = [dwt.reshape(N_DEV, rows_t, d), dwo.reshape(N_DEV, rows_o, d)]
        lands = [_landing((4, rows_t, d), BF16), _landing((4, rows_o, d), BF16)]
        d2d, token = _exchange_start("rs_d2d_start_%d" % l, parts + lands, _plan_scatter_d2d(2), 8, 0, dh)
        dy = _matmul(dh, wt.reshape(dm.in_w, d), dims=NN, tm=tm, tn=_tile(d, 512), tk=_tile(dm.in_w, 3584), out_dtype=F32,
                     name="mm_dx", addend=dz, alpha=alpha, token=token)
        done = _exchange_wait("rs_d2d_wait_%d" % l, d2d, dy)
        sums = [_pair_add(p, g, "rs_add_%d" % i) for i, (p, g) in enumerate(zip(done[:2], done[2:]))]
        lands = [_landing((3, rows_t, d), BF16), _landing((3, rows_o, d), BF16)]
        ici, token_ici = _exchange_start("rs_ici_start_%d" % l, sums + lands, _plan_scatter_ici(2), 6, 0, dy)
        (d_caw, d_ccw, d_ccb, d_grw, d_grb, d_giw, d_gib, d_lam, d_na, d_nb, d_nc, d_snk) = sm
        small[l] = dict(conv_a_w=d_caw, conv_c_w=d_ccw, conv_c_b=d_ccb[0], gate_r_w=d_grw, gate_r_b=d_grb[0], gate_i_w=d_giw,
                        gate_i_b=d_gib[0], rg_lambda=d_lam[0], norm_a=d_na[0], norm_b=d_nb[0], norm_c=d_nc[0],
                        sinks=d_snk[:, 0], ln_g=d_lng[0], ln_b=d_lnb[0])
    grad_x = dy.reshape(b_loc, seq, d)
    finish_scatter(0, ici, dy)

    gin = jnp.swapaxes(jnp.stack(g_wt), 1, 2)
    cols = dm.in_w // N_DEV
    gw_in, dl_in, nm_in, nv_in = [o.reshape(depth, d, cols) for o in _adamw(
        w_in.reshape(depth * d, cols), gin.reshape(1, depth * d, cols), m_w_in.reshape(depth * d, cols),
        v_w_in.reshape(depth * d, cols), "adamw_in")]
    rows = d // N_DEV
    gw_out, dl_out, nm_out, nv_out = [o.reshape(depth, rows, d) for o in _adamw(
        w_out.reshape(depth * rows, d), jnp.stack(g_wo).reshape(1, depth * rows, d), m_w_out.reshape(depth * rows, d),
        v_w_out.reshape(depth * rows, d), "adamw_out")]

    given = dict(conv_a_w=(conv_a_w, m_conv_a_w, v_conv_a_w), conv_c_w=(conv_c_w, m_conv_c_w, v_conv_c_w),
                 conv_c_b=(conv_c_b, m_conv_c_b, v_conv_c_b), gate_r_w=(gate_r_w, m_gate_r_w, v_gate_r_w),
                 gate_r_b=(gate_r_b, m_gate_r_b, v_gate_r_b), gate_i_w=(gate_i_w, m_gate_i_w, v_gate_i_w),
                 gate_i_b=(gate_i_b, m_gate_i_b, v_gate_i_b), rg_lambda=(rg_lambda, m_rg_lambda, v_rg_lambda),
                 norm_a=(norm_a, m_norm_a, v_norm_a), norm_b=(norm_b, m_norm_b, v_norm_b), norm_c=(norm_c, m_norm_c, v_norm_c),
                 sinks=(sinks, m_sinks, v_sinks), ln_g=(ln_g, m_ln_g, v_ln_g), ln_b=(ln_b, m_ln_b, v_ln_b))
    full_shapes = [jnp.stack([small[l][n] for l in range(depth)]).shape for n in SMALL]
    g_local = _pack([jnp.stack([small[l][n] for l in range(depth)]) for n in SMALL])
    g_all = _all_gather([g_local], "ag_small")[0]

    def mine_of(n, a):
        if n in ("conv_a_w", "conv_c_w"):
            return lax.dynamic_update_slice(jnp.zeros(a.shape[:2] + (dm.wa,), F32), a, (0, 0, dev * ch))
        return a

    packs = [_pack([mine_of(n, given[n][i]) for n in SMALL]) for i in range(3)]
    outs = _adamw(packs[0], g_all, packs[1], packs[2], "adamw_small")
    res = {}
    for kind, packed in zip(("grad", "delta", "new_m", "new_v"), outs):
        for n, a in zip(SMALL, _unpack(packed, full_shapes)):
            if n in ("conv_a_w", "conv_c_w"):
                a = lax.dynamic_slice(a, (0, 0, dev * ch), a.shape[:2] + (ch,))
            res[kind, n] = a
    res.update({("grad", "w_in"): gw_in, ("delta", "w_in"): dl_in, ("new_m", "w_in"): nm_in, ("new_v", "w_in"): nv_in,
                ("grad", "w_out"): gw_out, ("delta", "w_out"): dl_out, ("new_m", "w_out"): nm_out, ("new_v", "w_out"): nv_out})
    order = ("w_in", "conv_a_w", "sinks", "conv_c_w", "conv_c_b", "gate_r_w", "gate_r_b", "gate_i_w", "gate_i_b", "rg_lambda",
             "norm_a", "norm_b", "norm_c", "w_out", "ln_g", "ln_b")
    return (loss, grad_x, *[res[kind, n] for kind in ("grad", "delta", "new_m", "new_v") for n in order])
```

```python
import functools

import jax
import jax.numpy as jnp
from jax import lax
from jax.experimental import pallas as pl
from jax.experimental.pallas import tpu as pltpu

F32 = jnp.float32
BF16 = jnp.bfloat16
MESH = pl.DeviceIdType.MESH
ANY = pl.BlockSpec(memory_space=pl.ANY)

N_DEV = 8
LANES = 128
SUBLANES = 8
HEAD_DIM = 64
KV_GROUP = 8
ATT_BLOCK = 128
CONV_A = 3
CONV_C = 4
RG_C = 8.0
LN_EPS = 1e-5
RMS_EPS = 1e-6
NEG_INF = -1e30
ADAM_LR = 0.001
ADAM_B1 = 0.9
ADAM_B2 = 0.999
ADAM_EPS = 1e-08
ADAM_WD = 0.01
ADAM_STEP = 10
VMEM_LIMIT = 56 * 1024 * 1024

NN = ((1,), (0,))
NT = ((1,), (1,))
TN = ((0,), (0,))


def _pcall(body, **kw):
    return pl.pallas_call(body, **kw)


def _roll(x, shift, axis):
    return pltpu.roll(x, shift, axis)


def _params(sem=None, vmem=VMEM_LIMIT):
    return pltpu.CompilerParams(dimension_semantics=sem, vmem_limit_bytes=vmem)


def _dot(a, b, dims):
    return lax.dot_general(a.astype(BF16), b.astype(BF16), (dims, ((), ())), preferred_element_type=F32)


@jax.custom_vjp
def _mm(a, b):
    return _dot(a, b, NN)


def _mm_fwd(a, b):
    return _dot(a, b, NN), (a.astype(BF16), b.astype(BF16))


def _mm_bwd(res, g):
    a, b = res
    return _dot(g, b, NT), _dot(a, g, TN)


_mm.defvjp(_mm_fwd, _mm_bwd)


@jax.custom_vjp
def _mm_nt(a, b):
    return _dot(a, b, NT)


def _mm_nt_fwd(a, b):
    return _dot(a, b, NT), (a.astype(BF16), b.astype(BF16))


def _mm_nt_bwd(res, g):
    a, b = res
    return _dot(g, b, NN), _dot(g, a, TN)


_mm_nt.defvjp(_mm_nt_fwd, _mm_nt_bwd)


def _rows(shape):
    return lax.broadcasted_iota(jnp.int32, shape, 0)


@functools.partial(jax.custom_vjp, nondiff_argnums=(2,))
def _shift_halo(u, prev, k):
    r, c = u.shape
    fill = jnp.concatenate([_roll(prev, k, 0), jnp.zeros((r - SUBLANES, c), u.dtype)], axis=0)
    return jnp.where(_rows((r, c)) < k, fill, _roll(u, k, 0))


def _shift_halo_fwd(u, prev, k):
    return _shift_halo(u, prev, k), None


def _shift_halo_bwd(k, _, g):
    r, c = g.shape
    du = jnp.where(_rows((r, c)) < r - k, _roll(g, r - k, 0), 0.0)
    dprev = jnp.where(_rows((SUBLANES, c)) >= SUBLANES - k, _roll(g[0:SUBLANES], SUBLANES - k, 0), 0.0)
    return du, dprev


_shift_halo.defvjp(_shift_halo_fwd, _shift_halo_bwd)


@functools.partial(jax.custom_vjp, nondiff_argnums=(1, 2))
def _shift_fill(u, k, fill):
    return jnp.where(_rows(u.shape) < k, fill, _roll(u, k, 0))


def _shift_fill_fwd(u, k, fill):
    return _shift_fill(u, k, fill), None


def _shift_fill_bwd(k, fill, _, g):
    r = g.shape[0]
    return (jnp.where(_rows(g.shape) < r - k, _roll(g, r - k, 0), 0.0),)


_shift_fill.defvjp(_shift_fill_fwd, _shift_fill_bwd)


@jax.custom_vjp
def _swap_halves(x):
    return _roll(x, HEAD_DIM, 1)


_swap_halves.defvjp(lambda x: (_roll(x, HEAD_DIM, 1), None), lambda _, g: (_roll(g, HEAD_DIM, 1),))


@functools.partial(jax.custom_vjp, nondiff_argnums=(1,))
def _split_rows(x, n):
    r = x.shape[0] // n
    return tuple(x[i * r:(i + 1) * r] for i in range(n))


def _split_rows_fwd(x, n):
    return _split_rows(x, n), None


def _split_rows_bwd(n, _, gs):
    return (jnp.concatenate(list(gs), axis=0),)


_split_rows.defvjp(_split_rows_fwd, _split_rows_bwd)


def _logistic(x):
    return 1.0 / (1.0 + jnp.exp(-x))


@jax.custom_vjp
def _sigmoid(x):
    return _logistic(x)


def _sigmoid_fwd(x):
    s = _logistic(x)
    return s, s


_sigmoid.defvjp(_sigmoid_fwd, lambda s, g: (g * s * (1.0 - s),))


@jax.custom_vjp
def _silu(x):
    return x * _logistic(x)


def _silu_fwd(x):
    s = _logistic(x)
    return x * s, (x, s)


_silu.defvjp(_silu_fwd, lambda res, g: (g * res[1] * (1.0 + res[0] * (1.0 - res[1])),))


def _log_sigmoid(x):
    return -(jnp.maximum(-x, 0.0) + jnp.log1p(jnp.exp(-jnp.abs(x))))


@jax.custom_vjp
def _neg_expm1(x):
    series = x * (1 + x * (1 / 2) * (1 + x * (1 / 3) * (1 + x * (1 / 4) * (1 + x * (1 / 5) * (1 + x * (1 / 6) * (1 + x * (1 / 7)))))))
    return -jnp.where(jnp.abs(x) < 0.25, series, jnp.exp(x) - 1.0)


_neg_expm1.defvjp(lambda x: (_neg_expm1(x), x), lambda x, g: (-g * jnp.exp(x),))


def _scan_block(a, u, s_prev):
    acc_a, acc_b = a, u
    d = 1
    while d < a.shape[0]:
        acc_b = acc_a * _shift_fill(acc_b, d, 0.0) + acc_b
        acc_a = acc_a * _shift_fill(acc_a, d, 1.0)
        d *= 2
    return acc_a * s_prev + acc_b


def _last_row(h):
    return jnp.sum(jnp.where(_rows(h.shape) == h.shape[0] - 1, h, 0.0), axis=0, keepdims=True)


def _branch_a(ab, ac, ax, ag, acp, axp, w0, w1, w2, na):
    u = ac * ax
    up = acp * axp
    ya = ab * (w2 * u + w1 * _shift_halo(u, up, 1) + w0 * _shift_halo(u, up, 2))
    ms = jnp.sum(ya * ya, axis=1, keepdims=True) * (1.0 / ya.shape[1])
    return ya * lax.rsqrt(ms + RMS_EPS) * na * _silu(ag)


def _branch_c(cx, cg, cxp, sp, wc, bc, wr, br, wi, bi, lam, nc):
    hs, lasts = [], []
    for j in range(len(cx)):
        xc = (wc[3][j] * cx[j] + wc[2][j] * _shift_halo(cx[j], cxp[j], 1) + wc[1][j] * _shift_halo(cx[j], cxp[j], 2)
              + wc[0][j] * _shift_halo(cx[j], cxp[j], 3) + bc[j])
        r = _sigmoid(_mm(xc, wr[j]) + br[j])
        i = _sigmoid(_mm(xc, wi[j]) + bi[j])
        log_a = RG_C * r * _log_sigmoid(lam[j])
        a = jnp.exp(log_a)
        u = jnp.sqrt(_neg_expm1(2.0 * log_a)) * (i * xc)
        h = _scan_block(a, u, sp[j])
        hs.append(h)
        lasts.append(_last_row(h))
    width = LANES * len(cx)
    ms = sum(jnp.sum(h * h, axis=1, keepdims=True) for h in hs) * (1.0 / width)
    inv = lax.rsqrt(ms + RMS_EPS)
    return [hs[j] * inv * nc[j] * _silu(cg[j]) for j in range(len(cx))], lasts


def _attention_bias():
    qi = (jnp.arange(KV_GROUP * ATT_BLOCK) % ATT_BLOCK)[:, None]
    kj = jnp.arange(2 * ATT_BLOCK)[None, :]
    dist = qi + ATT_BLOCK - kj
    band = (dist >= 0) & (dist < ATT_BLOCK)
    return jnp.where(jnp.stack([band & (kj >= ATT_BLOCK), band]), 0.0, NEG_INF).astype(F32)


def _branch_b(q, k, v, kp, vp, bg, snk, nb, bias):
    rows = ATT_BLOCK
    n_kv = 2 * len(k)
    lane0 = lax.broadcasted_iota(jnp.int32, (1, 2 * rows), 1) == 0
    upper = lax.broadcasted_iota(jnp.int32, (2 * rows, LANES), 1) >= HEAD_DIM
    heads = [None] * (n_kv * KV_GROUP)
    for g in range(n_kv):
        half = g % 2
        keep = upper if half else jnp.logical_not(upper)
        kc = jnp.where(keep, jnp.concatenate([kp[g // 2], k[g // 2]], axis=0), 0.0)
        vc = jnp.where(keep, jnp.concatenate([vp[g // 2], v[g // 2]], axis=0), 0.0)
        hs = range(g * KV_GROUP, (g + 1) * KV_GROUP)
        qg = jnp.concatenate([q[h // 2] if h % 2 == half else _swap_halves(q[h // 2]) for h in hs], axis=0)
        s = _mm_nt(qg, kc) * (HEAD_DIM ** -0.5) + bias
        sink = jnp.concatenate([jnp.broadcast_to(jnp.sum(jnp.where(lane0, snk[h], 0.0), axis=1, keepdims=True), (rows, 1))
                                for h in hs], axis=0)
        m = lax.stop_gradient(jnp.maximum(jnp.max(s, axis=1, keepdims=True), sink))
        p = jnp.exp(s - m)
        inv = 1.0 / (jnp.sum(p, axis=1, keepdims=True) + jnp.exp(sink - m))
        o = _split_rows(_mm(p * inv, vc), KV_GROUP)
        for i, h in enumerate(hs):
            heads[h] = o[i] if h % 2 == half else _swap_halves(o[i])
    yb = [heads[2 * j] + heads[2 * j + 1] for j in range(len(q))]
    width = LANES * len(q)
    ms = sum(jnp.sum(y * y, axis=1, keepdims=True) for y in yb) * (1.0 / width)
    inv = lax.rsqrt(ms + RMS_EPS)
    return [yb[j] * inv * nb[j] * _silu(bg[j]) for j in range(len(q))]


class _Dims:
    def __init__(self, d_model, n_rg_heads):
        self.d = d_model
        self.wa = d_model // 4
        self.wb = d_model // 2
        self.wc = d_model // 4
        self.kvw = self.wb // KV_GROUP
        self.nq = self.wb // HEAD_DIM
        self.in_w = 4 * self.wa + 2 * self.wb + 2 * self.kvw + 2 * self.wc
        self.o_q = 4 * self.wa
        self.o_k = self.o_q + self.wb
        self.o_v = self.o_k + self.kvw
        self.o_bg = self.o_v + self.kvw
        self.o_cx = self.o_bg + self.wb
        self.o_cg = self.o_cx + self.wc
        self.nh = n_rg_heads
        assert self.wc // n_rg_heads == LANES and self.kvw % LANES == 0
        assert self.o_k % self.kvw == 0 and self.o_cx % (self.wc // 2) == 0


def _chunks(ref, rows, off, width):
    return [ref[rows, off + LANES * j: off + LANES * (j + 1)] for j in range(width // LANES)]


def _read_params(dm, caw, ccw, ccb, grw, grb, giw, gib, lam, na, nb, nc, snk):
    row = slice(0, 1)
    return dict(
        wa=[caw[k:k + 1, :] for k in range(CONV_A)], na=na[...],
        wc=[_chunks(ccw, slice(k, k + 1), 0, dm.wc) for k in range(CONV_C)], bc=_chunks(ccb, row, 0, dm.wc),
        wr=[grw[j] for j in range(dm.nh)], br=_chunks(grb, row, 0, dm.wc),
        wi=[giw[j] for j in range(dm.nh)], bi=_chunks(gib, row, 0, dm.wc),
        lam=_chunks(lam, row, 0, dm.wc), nc=_chunks(nc, row, 0, dm.wc),
        nb=_chunks(nb, row, 0, dm.wb), snk=[snk[h:h + 1, :] for h in range(dm.nq)])


def _param_specs(dm):
    shapes = [(CONV_A, dm.wa), (CONV_C, dm.wc), (1, dm.wc), (dm.nh, LANES, LANES), (1, dm.wc), (dm.nh, LANES, LANES),
              (1, dm.wc), (1, dm.wc), (1, dm.wa), (1, dm.wb), (1, dm.wc), (dm.nq, 2 * ATT_BLOCK)]
    specs = [pl.BlockSpec(s, (lambda b, n, _r=len(s): (0,) * _r)) for s in shapes]
    return shapes, specs


def _bias_spec():
    return pl.BlockSpec((2, KV_GROUP * ATT_BLOCK, 2 * ATT_BLOCK), lambda *_: (0, 0, 0))


def _mixer_fwd(dm, h, bias, prm, b_loc, nblk):
    t = h.shape[0]
    r = ATT_BLOCK
    tail = slice(r - SUBLANES, r)

    def body(h_ref, bias_ref, caw, ccw, ccb, grw, grb, giw, gib, lam, na, nb, nc, snk, mix_ref, sst_ref, kp, vp, acp, axp, cxp, sp):
        n = pl.program_id(1)

        @pl.when(n == 0)
        def _():
            for ref in (kp, vp, acp, axp, cxp, sp):
                ref[...] = jnp.zeros(ref.shape, ref.dtype)

        p = _read_params(dm, caw, ccw, ccb, grw, grb, giw, gib, lam, na, nb, nc, snk)
        full = slice(None)
        mix_a = _branch_a(h_ref[:, 0:dm.wa], h_ref[:, dm.wa:2 * dm.wa], h_ref[:, 2 * dm.wa:3 * dm.wa],
                          h_ref[:, 3 * dm.wa:4 * dm.wa], acp[...], axp[...], p["wa"][0], p["wa"][1], p["wa"][2], p["na"])
        mix_ref[:, 0:dm.wa] = mix_a.astype(BF16)
        bias = bias_ref[jnp.where(n == 0, 0, 1)]
        mix_b = _branch_b(_chunks(h_ref, full, dm.o_q, dm.wb), _chunks(h_ref, full, dm.o_k, dm.kvw),
                          _chunks(h_ref, full, dm.o_v, dm.kvw), _chunks(kp, full, 0, dm.kvw), _chunks(vp, full, 0, dm.kvw),
                          _chunks(h_ref, full, dm.o_bg, dm.wb), p["snk"], p["nb"], bias)
        for j, mb in enumerate(mix_b):
            mix_ref[:, dm.wa + LANES * j: dm.wa + LANES * (j + 1)] = mb.astype(BF16)
        sst_ref[0] = sp[...]
        mix_c, lasts = _branch_c(_chunks(h_ref, full, dm.o_cx, dm.wc), _chunks(h_ref, full, dm.o_cg, dm.wc),
                                 _chunks(cxp, full, 0, dm.wc), _chunks(sp, slice(0, 1), 0, dm.wc), p["wc"], p["bc"],
                                 p["wr"], p["br"], p["wi"], p["bi"], p["lam"], p["nc"])
        o_c = dm.wa + dm.wb
        for j, mc in enumerate(mix_c):
            mix_ref[:, o_c + LANES * j: o_c + LANES * (j + 1)] = mc.astype(BF16)
            sp[:, LANES * j: LANES * (j + 1)] = jnp.broadcast_to(lasts[j], (SUBLANES, LANES))
        kp[...] = h_ref[:, dm.o_k:dm.o_k + dm.kvw]
        vp[...] = h_ref[:, dm.o_v:dm.o_v + dm.kvw]
        acp[...] = h_ref[tail, dm.wa:2 * dm.wa]
        axp[...] = h_ref[tail, 2 * dm.wa:3 * dm.wa]
        cxp[...] = h_ref[tail, dm.o_cx:dm.o_cx + dm.wc]

    _, pspecs = _param_specs(dm)
    return _pcall(
        body, name="mixer_fwd", grid=(b_loc, nblk),
        in_specs=[pl.BlockSpec((r, dm.in_w), lambda b, n: (b * nblk + n, 0)), _bias_spec()] + pspecs,
        out_specs=[pl.BlockSpec((r, dm.d), lambda b, n: (b * nblk + n, 0)),
                   pl.BlockSpec((1, SUBLANES, dm.wc), lambda b, n: (b * nblk + n, 0, 0))],
        out_shape=[jax.ShapeDtypeStruct((t, dm.d), BF16), jax.ShapeDtypeStruct((b_loc * nblk, SUBLANES, dm.wc), F32)],
        scratch_shapes=[pltpu.VMEM((r, dm.kvw), F32), pltpu.VMEM((r, dm.kvw), F32), pltpu.VMEM((SUBLANES, dm.wa), F32),
                        pltpu.VMEM((SUBLANES, dm.wa), F32), pltpu.VMEM((SUBLANES, dm.wc), F32),
                        pltpu.VMEM((SUBLANES, dm.wc), F32)],
        compiler_params=_params(("arbitrary", "arbitrary")),
    )(h, bias, *prm)


def _mixer_bwd(dm, h, sst, dmix, bias, prm, b_loc, nblk):
    t = h.shape[0]
    r = ATT_BLOCK
    rb8 = r // SUBLANES
    n_small = 12

    def body(h_ref, kp_ref, vp_ref, acp_ref, axp_ref, cxp0_ref, cxp1_ref, sst_ref, dmix_ref, bias_ref,
             caw, ccw, ccb, grw, grb, giw, gib, lam, na, nb, nc, snk,
             dh_ref, d_caw, d_ccw, d_ccb, d_grw, d_grb, d_giw, d_gib, d_lam, d_na, d_nb, d_nc, d_snk,
             dkp, dvp, dacp, daxp, dcxp, dsp):
        step = pl.program_id(1)
        n = nblk - 1 - step

        @pl.when(step == 0)
        def _():
            for ref in (dkp, dvp, dacp, daxp, dcxp, dsp):
                ref[...] = jnp.zeros(ref.shape, ref.dtype)

        @pl.when((step == 0) & (pl.program_id(0) == 0))
        def _():
            for ref in (d_caw, d_ccw, d_ccb, d_grw, d_grb, d_giw, d_gib, d_lam, d_na, d_nb, d_nc, d_snk):
                ref[...] = jnp.zeros(ref.shape, ref.dtype)

        p = _read_params(dm, caw, ccw, ccb, grw, grb, giw, gib, lam, na, nb, nc, snk)
        has_prev = jnp.where(n > 0, 1.0, 0.0)
        full = slice(None)
        pad = jnp.zeros((r - SUBLANES, LANES), F32)

        def with_tail(own, carry):
            z = jnp.zeros((r - SUBLANES, own.shape[1]), F32)
            return own + jnp.concatenate([z, carry], axis=0)

        a_in = (h_ref[:, 0:dm.wa], h_ref[:, dm.wa:2 * dm.wa], h_ref[:, 2 * dm.wa:3 * dm.wa], h_ref[:, 3 * dm.wa:4 * dm.wa],
                acp_ref[...] * has_prev, axp_ref[...] * has_prev, p["wa"][0], p["wa"][1], p["wa"][2], p["na"])
        _, vjp_a = jax.vjp(_branch_a, *a_in)
        g_ab, g_ac, g_ax, g_ag, g_acp, g_axp, g_w0, g_w1, g_w2, g_na = vjp_a(dmix_ref[:, 0:dm.wa])
        dh_ref[:, 0:dm.wa] = g_ab.astype(BF16)
        dh_ref[:, dm.wa:2 * dm.wa] = with_tail(g_ac, dacp[...]).astype(BF16)
        dh_ref[:, 2 * dm.wa:3 * dm.wa] = with_tail(g_ax, daxp[...]).astype(BF16)
        dh_ref[:, 3 * dm.wa:4 * dm.wa] = g_ag.astype(BF16)
        dacp[...] = g_acp
        daxp[...] = g_axp
        for k, gw in enumerate((g_w0, g_w1, g_w2)):
            d_caw[k:k + 1, :] += gw
        d_na[...] += g_na

        bias = bias_ref[jnp.where(n == 0, 0, 1)]
        kp_in = [c * has_prev for c in _chunks(kp_ref, full, 0, dm.kvw)]
        vp_in = [c * has_prev for c in _chunks(vp_ref, full, 0, dm.kvw)]
        b_in = (_chunks(h_ref, full, dm.o_q, dm.wb), _chunks(h_ref, full, dm.o_k, dm.kvw), _chunks(h_ref, full, dm.o_v, dm.kvw),
                kp_in, vp_in, _chunks(h_ref, full, dm.o_bg, dm.wb), p["snk"], p["nb"])
        _, vjp_b = jax.vjp(lambda *a: _branch_b(*a, bias), *b_in)
        g_q, g_k, g_v, g_kp, g_vp, g_bg, g_snk, g_nb = vjp_b(_chunks(dmix_ref, full, dm.wa, dm.wb))
        for j in range(len(g_q)):
            dh_ref[:, dm.o_q + LANES * j: dm.o_q + LANES * (j + 1)] = g_q[j].astype(BF16)
            dh_ref[:, dm.o_bg + LANES * j: dm.o_bg + LANES * (j + 1)] = g_bg[j].astype(BF16)
            d_nb[:, LANES * j: LANES * (j + 1)] += g_nb[j]
        for j in range(len(g_k)):
            cols = slice(LANES * j, LANES * (j + 1))
            dh_ref[:, dm.o_k + LANES * j: dm.o_k + LANES * (j + 1)] = (g_k[j] + dkp[:, cols]).astype(BF16)
            dh_ref[:, dm.o_v + LANES * j: dm.o_v + LANES * (j + 1)] = (g_v[j] + dvp[:, cols]).astype(BF16)
            dkp[:, cols] = g_kp[j]
            dvp[:, cols] = g_vp[j]
        for hd in range(dm.nq):
            d_snk[hd:hd + 1, :] += g_snk[hd]

        half_c = dm.wc // 2
        cxp_in = ([c * has_prev for c in _chunks(cxp0_ref, full, 0, half_c)]
                  + [c * has_prev for c in _chunks(cxp1_ref, full, 0, half_c)])
        c_in = (_chunks(h_ref, full, dm.o_cx, dm.wc), _chunks(h_ref, full, dm.o_cg, dm.wc), cxp_in,
                [sst_ref[0, 0:1, LANES * j: LANES * (j + 1)] for j in range(dm.nh)], p["wc"], p["bc"], p["wr"], p["br"], p["wi"], p["bi"],
                p["lam"], p["nc"])
        _, vjp_c = jax.vjp(_branch_c, *c_in)
        ct_last = [dsp[0:1, LANES * j: LANES * (j + 1)] for j in range(dm.nh)]
        g_cx, g_cg, g_cxp, g_sp, g_wc, g_bc, g_wr, g_br, g_wi, g_bi, g_lam, g_nc = vjp_c(
            (_chunks(dmix_ref, full, dm.wa + dm.wb, dm.wc), ct_last))
        for j in range(dm.nh):
            cols = slice(LANES * j, LANES * (j + 1))
            tot = g_cx[j] + jnp.concatenate([pad, dcxp[:, cols]], axis=0)
            dh_ref[:, dm.o_cx + LANES * j: dm.o_cx + LANES * (j + 1)] = tot.astype(BF16)
            dh_ref[:, dm.o_cg + LANES * j: dm.o_cg + LANES * (j + 1)] = g_cg[j].astype(BF16)
            dcxp[:, cols] = g_cxp[j]
            dsp[:, cols] = jnp.broadcast_to(g_sp[j], (SUBLANES, LANES))
            for k in range(CONV_C):
                d_ccw[k:k + 1, cols] += g_wc[k][j]
            d_ccb[:, cols] += g_bc[j]
            d_grw[j] += g_wr[j]
            d_grb[:, cols] += g_br[j]
            d_giw[j] += g_wi[j]
            d_gib[:, cols] += g_bi[j]
            d_lam[:, cols] += g_lam[j]
            d_nc[:, cols] += g_nc[j]

    def blk(b, s):
        return b * nblk + (nblk - 1 - s)

    def prev_rows8(b, s):
        return jnp.maximum(blk(b, s) * rb8 - 1, 0)

    pshapes, pspecs = _param_specs(dm)
    half_c = dm.wc // 2
    in_specs = [
        pl.BlockSpec((r, dm.in_w), lambda b, s: (blk(b, s), 0)),
        pl.BlockSpec((r, dm.kvw), lambda b, s: (jnp.maximum(blk(b, s) - 1, 0), dm.o_k // dm.kvw)),
        pl.BlockSpec((r, dm.kvw), lambda b, s: (jnp.maximum(blk(b, s) - 1, 0), dm.o_v // dm.kvw)),
        pl.BlockSpec((SUBLANES, dm.wa), lambda b, s: (prev_rows8(b, s), 1)),
        pl.BlockSpec((SUBLANES, dm.wa), lambda b, s: (prev_rows8(b, s), 2)),
        pl.BlockSpec((SUBLANES, half_c), lambda b, s: (prev_rows8(b, s), dm.o_cx // half_c)),
        pl.BlockSpec((SUBLANES, half_c), lambda b, s: (prev_rows8(b, s), dm.o_cx // half_c + 1)),
        pl.BlockSpec((1, SUBLANES, dm.wc), lambda b, s: (blk(b, s), 0, 0)),
        pl.BlockSpec((r, dm.d), lambda b, s: (blk(b, s), 0)),
        _bias_spec(),
    ] + pspecs
    outs = _pcall(
        body, name="mixer_bwd", grid=(b_loc, nblk), in_specs=in_specs,
        out_specs=[pl.BlockSpec((r, dm.in_w), lambda b, s: (blk(b, s), 0))] + pspecs,
        out_shape=[jax.ShapeDtypeStruct((t, dm.in_w), BF16)] + [jax.ShapeDtypeStruct(s, F32) for s in pshapes],
        scratch_shapes=[pltpu.VMEM((r, dm.kvw), F32), pltpu.VMEM((r, dm.kvw), F32), pltpu.VMEM((SUBLANES, dm.wa), F32),
                        pltpu.VMEM((SUBLANES, dm.wa), F32), pltpu.VMEM((SUBLANES, dm.wc), F32),
                        pltpu.VMEM((SUBLANES, dm.wc), F32)],
        compiler_params=_params(("arbitrary", "arbitrary")),
    )(h, h, h, h, h, h, h, sst, dmix, bias, *prm)
    assert len(outs) == 1 + n_small
    return outs[0], outs[1:]


def _token_spec():
    return pl.BlockSpec((SUBLANES, LANES), lambda *_: (0, 0))


def _no_token():
    return jnp.zeros((SUBLANES, LANES), F32)


def _matmul(a, b, *, dims, tm, tn, tk, out_dtype, name, addend=None, alpha=None, token=None):
    if dims == TN:
        (k_dim, m), n_dim = a.shape, b.shape[1]
        a_spec = pl.BlockSpec((tk, tm), lambda i, j, k: (k, i))
    else:
        (m, k_dim), n_dim = a.shape, (b.shape[0] if dims == NT else b.shape[1])
        a_spec = pl.BlockSpec((tm, tk), lambda i, j, k: (i, k))
    b_spec = pl.BlockSpec((tn, tk), lambda i, j, k: (j, k)) if dims == NT else pl.BlockSpec((tk, tn), lambda i, j, k: (k, j))
    assert m % tm == 0 and n_dim % tn == 0 and k_dim % tk == 0, (a.shape, b.shape, tm, tn, tk)
    nk = k_dim // tk
    o_spec = pl.BlockSpec((tm, tn), lambda i, j, k: (i, j))

    def body(*refs):
        a_ref, b_ref = refs[0], refs[1]
        add_ref = refs[2] if addend is not None else None
        o_ref, acc_ref = refs[-2], refs[-1]
        k = pl.program_id(2)
        part = lax.dot_general(a_ref[...], b_ref[...], (dims, ((), ())), preferred_element_type=F32)

        def finish(acc):
            if add_ref is not None:
                acc = acc + alpha * add_ref[...]
            o_ref[...] = acc.astype(out_dtype)

        if nk == 1:
            finish(part)
        else:
            @pl.when(k == 0)
            def _():
                acc_ref[...] = part

            @pl.when((k > 0) & (k < nk - 1))
            def _():
                acc_ref[...] += part

            @pl.when(k == nk - 1)
            def _():
                finish(acc_ref[...] + part)

    ins = [a, b] + ([addend] if addend is not None else []) + ([token] if token is not None else [])
    in_specs = [a_spec, b_spec] + ([o_spec] if addend is not None else []) + ([_token_spec()] if token is not None else [])
    return _pcall(
        body, name=name, grid=(m // tm, n_dim // tn, nk), in_specs=in_specs, out_specs=o_spec,
        out_shape=jax.ShapeDtypeStruct((m, n_dim), out_dtype),
        scratch_shapes=[pltpu.VMEM((tm, tn) if nk > 1 else (SUBLANES, LANES), F32)],
        compiler_params=_params(("parallel", "parallel", "arbitrary")),
    )(*ins)


def _tile(n, want, quantum=LANES):
    if n <= want:
        return n
    for cand in range(want - want % quantum, 0, -quantum):
        if n % cand == 0:
            return cand
    return n


def _row_tile(t, d):
    return _tile(t, max(2 * SUBLANES, (1 << 19) // d), 2 * SUBLANES)


def _ln_fwd(z, g, b, token):
    t, d = z.shape
    tr = _row_tile(t, d)

    def body(z_ref, g_ref, b_ref, _, y_ref, yb_ref):
        zz = z_ref[...]
        mu = jnp.mean(zz, axis=1, keepdims=True)
        zc = zz - mu
        var = jnp.mean(zc * zc, axis=1, keepdims=True)
        y = zc * lax.rsqrt(var + LN_EPS) * g_ref[...] + b_ref[...]
        y_ref[...] = y
        yb_ref[...] = y.astype(BF16)

    row = pl.BlockSpec((tr, d), lambda i: (i, 0))
    vec = pl.BlockSpec((1, d), lambda i: (0, 0))
    return _pcall(body, name="ln_fwd", grid=(t // tr,), in_specs=[row, vec, vec, _token_spec()], out_specs=[row, row],
                  out_shape=[jax.ShapeDtypeStruct((t, d), F32), jax.ShapeDtypeStruct((t, d), BF16)],
                  compiler_params=_params(("parallel",)))(z, g, b, token)


def _ln_bwd(dy, z, g, token):
    t, d = z.shape
    tr = _row_tile(t, d)

    def body(dy_ref, z_ref, g_ref, _, dz_ref, dzb_ref, dg_ref, db_ref):
        @pl.when(pl.program_id(0) == 0)
        def _():
            dg_ref[...] = jnp.zeros(dg_ref.shape, F32)
            db_ref[...] = jnp.zeros(db_ref.shape, F32)

        zz = z_ref[...]
        dyy = dy_ref[...]
        mu = jnp.mean(zz, axis=1, keepdims=True)
        zc = zz - mu
        rstd = lax.rsqrt(jnp.mean(zc * zc, axis=1, keepdims=True) + LN_EPS)
        xhat = zc * rstd
        dyg = dyy * g_ref[...]
        dz = rstd * (dyg - jnp.mean(dyg, axis=1, keepdims=True) - xhat * jnp.mean(dyg * xhat, axis=1, keepdims=True))
        dz_ref[...] = dz
        dzb_ref[...] = dz.astype(BF16)
        dg_ref[...] += jnp.sum(dyy * xhat, axis=0, keepdims=True)
        db_ref[...] += jnp.sum(dyy, axis=0, keepdims=True)

    row = pl.BlockSpec((tr, d), lambda i: (i, 0))
    vec = pl.BlockSpec((1, d), lambda i: (0, 0))
    return _pcall(body, name="ln_bwd", grid=(t // tr,), in_specs=[row, row, vec, _token_spec()], out_specs=[row, row, vec, vec],
                  out_shape=[jax.ShapeDtypeStruct((t, d), F32), jax.ShapeDtypeStruct((t, d), BF16),
                             jax.ShapeDtypeStruct((1, d), F32), jax.ShapeDtypeStruct((1, d), F32)],
                  compiler_params=_params(("arbitrary",)))(dy, z, g, token)


def _loss_head(y, target):
    t, d = y.shape
    tr = _row_tile(t, d)

    def body(y_ref, t_ref, dy_ref, loss_ref):
        @pl.when(pl.program_id(0) == 0)
        def _():
            loss_ref[...] = jnp.zeros(loss_ref.shape, F32)

        err = y_ref[...] - t_ref[...]
        dy_ref[...] = err * (1.0 / d)
        per_token = jnp.sum(err * err, axis=1, keepdims=True) * (1.0 / d)
        loss_ref[...] += 0.5 * jnp.sum(per_token, axis=0, keepdims=True)

    row = pl.BlockSpec((tr, d), lambda i: (i, 0))
    one = pl.BlockSpec((1, 1), lambda i: (0, 0))
    return _pcall(body, name="loss_head", grid=(t // tr,), in_specs=[row, row], out_specs=[row, one],
                  out_shape=[jax.ShapeDtypeStruct((t, d), F32), jax.ShapeDtypeStruct((1, 1), F32)],
                  compiler_params=_params(("arbitrary",)))(y, target)


def _chip_sum(sums, far, name):
    r, c = sums.shape[1:]
    tr = _row_tile(r, c)

    def body(s_ref, f_ref, o_ref):
        acc = s_ref[0].astype(F32)
        for i in range(3):
            acc = acc + f_ref[i].astype(F32)
        o_ref[...] = acc

    return _pcall(body, name=name, grid=(r // tr,),
                  in_specs=[pl.BlockSpec((1, tr, c), lambda i: (2 * lax.axis_index("x") + lax.axis_index("y"), i, 0)),
                            pl.BlockSpec((3, tr, c), lambda i: (0, i, 0))],
                  out_specs=pl.BlockSpec((tr, c), lambda i: (i, 0)), out_shape=jax.ShapeDtypeStruct((r, c), F32),
                  compiler_params=_params(("parallel",)))(sums, far)


def _pair_add(a, b, name):
    p, r, c = b.shape
    tr = _row_tile(r, c)

    def body(a_ref, b_ref, o_ref):
        o_ref[...] = (a_ref[...].astype(F32) + b_ref[...].astype(F32)).astype(BF16)

    spec = pl.BlockSpec((1, tr, c), lambda q, i: (q, i, 0))
    return _pcall(body, name=name, grid=(p, r // tr),
                  in_specs=[pl.BlockSpec((1, tr, c), lambda q, i: (2 * q + lax.axis_index("c"), i, 0)), spec], out_specs=spec,
                  out_shape=jax.ShapeDtypeStruct((p, r, c), BF16), compiler_params=_params(("parallel", "parallel")))(a, b)


def _adamw(w, g_parts, m, v, name):
    r, c = w.shape
    n_parts = g_parts.shape[0]
    tr = _row_tile(r, c) if r % SUBLANES == 0 else r

    def body(w_ref, g_ref, m_ref, v_ref, go_ref, d_ref, mo_ref, vo_ref):
        g = g_ref[0].astype(F32)
        for i in range(1, n_parts):
            g = g + g_ref[i].astype(F32)
        m_new = ADAM_B1 * m_ref[...] + (1.0 - ADAM_B1) * g
        v_new = ADAM_B2 * v_ref[...] + (1.0 - ADAM_B2) * (g * g)
        m_hat = m_new / (1.0 - ADAM_B1 ** ADAM_STEP)
        v_hat = v_new / (1.0 - ADAM_B2 ** ADAM_STEP)
        go_ref[...] = g
        d_ref[...] = -ADAM_LR * (m_hat / (jnp.sqrt(v_hat) + ADAM_EPS) + ADAM_WD * w_ref[...])
        mo_ref[...] = m_new
        vo_ref[...] = v_new

    spec = pl.BlockSpec((tr, c), lambda i: (i, 0))
    shape = jax.ShapeDtypeStruct((r, c), F32)
    return _pcall(body, name=name, grid=(r // tr,),
                  in_specs=[spec, pl.BlockSpec((n_parts, tr, c), lambda i: (0, i, 0)), spec, spec],
                  out_specs=[spec] * 4, out_shape=[shape] * 4, compiler_params=_params(("parallel",)))(w, g_parts, m, v)


def _adamw_layers(w, g, m, v, first, prev, name):
    n_layers, r, c = w.shape
    tr = _row_tile(r, c)

    def body(*refs):
        w_ref, g_ref, m_ref, v_ref = refs[:4]
        go_ref, d_ref, mo_ref, vo_ref = refs[-4:]
        gg = g_ref[...]
        m_new = ADAM_B1 * m_ref[...] + (1.0 - ADAM_B1) * gg
        v_new = ADAM_B2 * v_ref[...] + (1.0 - ADAM_B2) * (gg * gg)
        m_hat = m_new / (1.0 - ADAM_B1 ** ADAM_STEP)
        v_hat = v_new / (1.0 - ADAM_B2 ** ADAM_STEP)
        go_ref[...] = gg
        d_ref[...] = -ADAM_LR * (m_hat / (jnp.sqrt(v_hat) + ADAM_EPS) + ADAM_WD * w_ref[...])
        mo_ref[...] = m_new
        vo_ref[...] = v_new

    own = pl.BlockSpec((1, tr, c), lambda l, i: (first + l, i, 0))
    ins = [w, g, m, v] + (list(prev) if prev is not None else [])
    in_specs = [own, pl.BlockSpec((1, tr, c), lambda l, i: (l, i, 0)), own, own] + ([ANY] * 4 if prev is not None else [])
    return _pcall(body, name=name, grid=(g.shape[0], r // tr), in_specs=in_specs, out_specs=[own] * 4,
                  out_shape=[jax.ShapeDtypeStruct((n_layers, r, c), F32)] * 4,
                  input_output_aliases={4 + i: i for i in range(4)} if prev is not None else {},
                  compiler_params=_params(("parallel", "parallel")))(*ins)


def _me():
    return lax.axis_index("x"), lax.axis_index("y"), lax.axis_index("c")


def _dev(px, py, pc):
    return 4 * px + 2 * py + pc


def _remote(src, dst, send_sems, recv_sems, k, to):
    return pltpu.make_async_remote_copy(src_ref=src, dst_ref=dst, send_sem=send_sems.at[k], recv_sem=recv_sems.at[k],
                                        device_id=to, device_id_type=MESH)


def _all_gather(arrs, name):
    n = len(arrs)

    def body(*refs):
        ins, outs = refs[:n], refs[n:2 * n]
        send_sems, recv_sems, local_sems = refs[2 * n:]
        x, y, c = _me()
        me, sibling = (x, y, c), (x, y, 1 - c)
        chips = [(1 - x, y), (x, 1 - y), (1 - x, 1 - y)]
        pending = []
        for a in range(n):
            mine = pltpu.make_async_copy(ins[a], outs[a].at[_dev(*me)], local_sems.at[a])
            mine.start()
            pending.append(mine)
        sends = []
        for a in range(n):
            dst = outs[a].at[_dev(*me)]
            sends.append(_remote(ins[a], dst, send_sems, recv_sems, 7 * a, sibling))
            sends += [_remote(ins[a], dst, send_sems, recv_sems, 7 * a + 1 + j, (*chip, c)) for j, chip in enumerate(chips)]
        for cp in sends:
            cp.start()
        for a in range(n):
            for j, chip in enumerate(chips):
                blk = outs[a].at[_dev(*chip, c)]
                _remote(blk, blk, send_sems, recv_sems, 7 * a + 1 + j, me).wait_recv()
                fwd = _remote(blk, blk, send_sems, recv_sems, 7 * a + 4 + j, sibling)
                fwd.start()
                sends.append(fwd)
        for a in range(n):
            blk = outs[a].at[_dev(*sibling)]
            _remote(blk, blk, send_sems, recv_sems, 7 * a, me).wait_recv()
            for j, chip in enumerate(chips):
                blk = outs[a].at[_dev(*chip, 1 - c)]
                _remote(blk, blk, send_sems, recv_sems, 7 * a + 4 + j, me).wait_recv()
        for cp in sends:
            cp.wait_send()
        for cp in pending:
            cp.wait()

    return _pcall(
        body, name=name, in_specs=[ANY] * n, out_specs=[ANY] * n,
        out_shape=[jax.ShapeDtypeStruct((N_DEV,) + a.shape, a.dtype) for a in arrs],
        scratch_shapes=[pltpu.SemaphoreType.DMA((7 * n,)), pltpu.SemaphoreType.DMA((7 * n,)), pltpu.SemaphoreType.DMA((n,))],
    )(*arrs)


def _relations(x, y):
    return [(x, y), (1 - x, y), (x, 1 - y), (1 - x, 1 - y)]


HBM_SPEC = pl.BlockSpec(memory_space=pltpu.HBM)
SEM_SPEC = pl.BlockSpec(memory_space=pltpu.SEMAPHORE)
DATAFLOW = pltpu.SideEffectType.DATAFLOW_SIDE_EFFECTING


def _exchange_start(name, bufs, plan, n_remote, n_local, dep):
    nb = len(bufs)
    sem_shapes = [pltpu.SemaphoreType.DMA((n_remote,)), pltpu.SemaphoreType.DMA((n_remote,))]
    if n_local:
        sem_shapes.append(pltpu.SemaphoreType.DMA((n_local,)))
    ns = len(sem_shapes)

    def body(*refs):
        ins, sems, token = refs[:nb], refs[nb + 1:nb + 1 + ns], refs[-1]
        starts, _, local = plan(ins, sems[0], sems[1], sems[2] if n_local else None)
        for cp in local + starts:
            cp.start()
        token[...] = jnp.zeros(token.shape, F32)

    outs = _pcall(
        body, name=name, in_specs=[HBM_SPEC] * nb + [ANY],
        out_specs=[SEM_SPEC] * ns + [HBM_SPEC] * nb + [pl.BlockSpec(memory_space=pltpu.VMEM)],
        out_shape=sem_shapes + [pltpu.HBM(b.shape, b.dtype) for b in bufs] + [jax.ShapeDtypeStruct((SUBLANES, LANES), F32)],
        input_output_aliases={i: ns + i for i in range(nb)}, compiler_params=pltpu.CompilerParams(has_side_effects=DATAFLOW),
    )(*[pltpu.with_memory_space_constraint(b, pltpu.HBM) for b in bufs], dep)
    return dict(sems=outs[:ns], thru=outs[ns:ns + nb], plan=plan, n_local=n_local), outs[-1]


def _exchange_wait(name, handle, *after):
    thru, sems, plan, n_local = handle["thru"], handle["sems"], handle["plan"], handle["n_local"]
    nb, ns = len(thru), len(sems)

    def body(*refs):
        ins, sem_refs = refs[:nb], refs[nb:nb + ns]
        starts, arrivals, local = plan(ins, sem_refs[0], sem_refs[1], sem_refs[2] if n_local else None)
        for cp in starts:
            cp.wait_send()
        for cp in arrivals:
            cp.wait_recv()
        for cp in local:
            cp.wait()

    return _pcall(
        body, name=name, in_specs=[HBM_SPEC] * nb + [SEM_SPEC] * ns + [ANY] * len(after), out_specs=[HBM_SPEC] * nb,
        out_shape=[pltpu.HBM(b.shape, b.dtype) for b in thru], input_output_aliases={i: i for i in range(nb)},
        compiler_params=pltpu.CompilerParams(has_side_effects=DATAFLOW),
    )(*thru, *sems, *after)


def _landing(shape, dtype):
    return lax.empty(shape, dtype)


def _plan_gather_ici(n):
    def plan(refs, send_sems, recv_sems, local_sems):
        x, y, c = _me()
        me, sibling = (x, y, c), (x, y, 1 - c)
        chips = _relations(x, y)[1:]
        starts, arrivals, local = [], [], []
        for a in range(n):
            shard, land = refs[a], refs[n + a]
            own = land.at[_dev(*me)]
            local.append(pltpu.make_async_copy(shard, own, local_sems.at[a]))
            starts.append(_remote(shard, own, send_sems, recv_sems, 4 * a, sibling))
            blk = land.at[_dev(*sibling)]
            arrivals.append(_remote(blk, blk, send_sems, recv_sems, 4 * a, me))
            for j, chip in enumerate(chips):
                starts.append(_remote(shard, own, send_sems, recv_sems, 4 * a + 1 + j, (*chip, c)))
                blk = land.at[_dev(*chip, c)]
                arrivals.append(_remote(blk, blk, send_sems, recv_sems, 4 * a + 1 + j, me))
        return starts, arrivals, local
    return plan


def _plan_gather_d2d(n):
    def plan(refs, send_sems, recv_sems, local_sems):
        x, y, c = _me()
        me, sibling = (x, y, c), (x, y, 1 - c)
        starts, arrivals = [], []
        for a in range(n):
            for j, chip in enumerate(_relations(x, y)[1:]):
                blk = refs[a].at[_dev(*chip, c)]
                starts.append(_remote(blk, blk, send_sems, recv_sems, 3 * a + j, sibling))
                blk = refs[a].at[_dev(*chip, 1 - c)]
                arrivals.append(_remote(blk, blk, send_sems, recv_sems, 3 * a + j, me))
        return starts, arrivals, []
    return plan


def _plan_scatter_d2d(n):
    def plan(refs, send_sems, recv_sems, local_sems):
        x, y, c = _me()
        me, sibling = (x, y, c), (x, y, 1 - c)
        starts, arrivals = [], []
        for a in range(n):
            for k in range(4):
                starts.append(_remote(refs[a].at[2 * k + 1 - c], refs[n + a].at[k], send_sems, recv_sems, 4 * a + k, sibling))
                blk = refs[n + a].at[k]
                arrivals.append(_remote(blk, blk, send_sems, recv_sems, 4 * a + k, me))
        return starts, arrivals, []
    return plan


def _plan_scatter_ici(n):
    def plan(refs, send_sems, recv_sems, local_sems):
        x, y, c = _me()
        me = (x, y, c)
        starts, arrivals = [], []
        for a in range(n):
            for j, (cx, cy) in enumerate(_relations(x, y)[1:]):
                starts.append(_remote(refs[a].at[2 * cx + cy], refs[n + a].at[j], send_sems, recv_sems, 3 * a + j, (cx, cy, c)))
                blk = refs[n + a].at[j]
                arrivals.append(_remote(blk, blk, send_sems, recv_sems, 3 * a + j, me))
        return starts, arrivals, []
    return plan


SMALL = ("conv_a_w", "conv_c_w", "conv_c_b", "gate_r_w", "gate_r_b", "gate_i_w", "gate_i_b", "rg_lambda",
         "norm_a", "norm_b", "norm_c", "sinks", "ln_g", "ln_b")
PACK_COLS = 1024


def _pack(arrs):
    flat = jnp.concatenate([a.reshape(-1) for a in arrs])
    pad = (-flat.shape[0]) % (SUBLANES * PACK_COLS)
    return jnp.pad(flat, (0, pad)).reshape(-1, PACK_COLS)


def _unpack(packed, shapes):
    flat = packed.reshape(-1)
    out, off = [], 0
    for s in shapes:
        size = 1
        for dim in s:
            size *= dim
        out.append(flat[off:off + size].reshape(s))
        off += size
    return out


def kernel(x, w_in, conv_a_w, sinks, conv_c_w, conv_c_b, gate_r_w, gate_r_b, gate_i_w, gate_i_b, rg_lambda, norm_a, norm_b, norm_c, w_out, ln_g, ln_b, loss_target, m_w_in, m_conv_a_w, m_sinks, m_conv_c_w, m_conv_c_b, m_gate_r_w, m_gate_r_b, m_gate_i_w, m_gate_i_b, m_rg_lambda, m_norm_a, m_norm_b, m_norm_c, m_w_out, m_ln_g, m_ln_b, v_w_in, v_conv_a_w, v_sinks, v_conv_c_w, v_conv_c_b, v_gate_r_w, v_gate_r_b, v_gate_i_w, v_gate_i_b, v_rg_lambda, v_norm_a, v_norm_b, v_norm_c, v_w_out, v_ln_g, v_ln_b):
    b_loc, seq, d = x.shape
    depth = w_in.shape[0]
    dm = _Dims(d, gate_r_w.shape[1])
    t = b_loc * seq
    nblk = seq // ATT_BLOCK
    alpha = (2.0 * depth) ** 0.25
    ch = dm.wa // N_DEV
    dev = _dev(*_me())

    wt_shard = jnp.swapaxes(w_in, 1, 2).astype(BF16)
    wo_shard = w_out.astype(BF16)
    conv_shard = jnp.concatenate([conv_a_w.reshape(depth * CONV_A, ch), conv_c_w.reshape(depth * CONV_C, ch)], axis=0)
    conv_all = _all_gather([jnp.pad(conv_shard, ((0, (-conv_shard.shape[0]) % SUBLANES), (0, 0)))], "ag_conv")[0]
    conv_all = jnp.swapaxes(conv_all, 0, 1).reshape(conv_all.shape[1], dm.wa)
    conv_a_full = conv_all[:depth * CONV_A].reshape(depth, CONV_A, dm.wa)
    conv_c_full = conv_all[depth * CONV_A:depth * (CONV_A + CONV_C)].reshape(depth, CONV_C, dm.wc)
    sinks_wide = jnp.broadcast_to(sinks[:, :, None], (depth, dm.nq, 2 * ATT_BLOCK))

    def layer_params(l):
        return (conv_a_full[l], conv_c_full[l], conv_c_b[l][None], gate_r_w[l], gate_r_b[l][None], gate_i_w[l],
                gate_i_b[l][None], rg_lambda[l][None], norm_a[l][None], norm_b[l][None], norm_c[l][None], sinks_wide[l])

    tm = _tile(t, 1024)
    xs = x.reshape(t, d)
    xb = xs.astype(BF16)
    saved = []
    bias = _attention_bias()
    rows_t, rows_o = dm.in_w // N_DEV, d // N_DEV
    wt, wo = _all_gather([wt_shard[0], wo_shard[0]], "ag_weights")
    for l in range(depth):
        token = _no_token()
        if l + 1 < depth:
            lands = [_landing((N_DEV, rows_t, d), BF16), _landing((N_DEV, rows_o, d), BF16)]
            ici, token = _exchange_start("ag_ici_start_%d" % l, [wt_shard[l + 1], wo_shard[l + 1]] + lands, _plan_gather_ici(2), 8, 2, wt)
        h = _matmul(xb, wt.reshape(dm.in_w, d), dims=NT, tm=tm, tn=_tile(dm.in_w, 512), tk=d, out_dtype=F32, name="mm_in",
                    token=token)
        mix, sst = _mixer_fwd(dm, h, bias, layer_params(l), b_loc, nblk)
        z = _matmul(mix, wo.reshape(d, d), dims=NN, tm=tm, tn=_tile(d, 512), tk=d, out_dtype=F32, name="mm_out", addend=xs,
                    alpha=alpha)
        saved.append((xb, h, sst, mix, z, wt, wo))
        token = _no_token()
        if l + 1 < depth:
            lands = _exchange_wait("ag_ici_wait_%d" % l, ici, z)[2:]
            d2d, token = _exchange_start("ag_d2d_start_%d" % l, lands, _plan_gather_d2d(2), 6, 0, z)
        xs, xb = _ln_fwd(z, ln_g[l][None], ln_b[l][None], token)
        if l + 1 < depth:
            wt, wo = _exchange_wait("ag_d2d_wait_%d" % l, d2d, xb)

    dy, loss_part = _loss_head(xs, loss_target.reshape(t, d))
    loss = lax.psum(loss_part[0, 0], ("x", "y", "c"))

    g_wt, g_wo, small = [None] * depth, [None] * depth, [None] * depth
    ici, token_ici = None, _no_token()

    def finish_scatter(l, ici, *after):
        done = _exchange_wait("rs_ici_wait_%d" % l, ici, *after)
        g_wt[l], g_wo[l] = [_chip_sum(s, f, "rs_sum_%d" % i) for i, (s, f) in enumerate(zip(done[:2], done[2:]))]

    def scatter_add_start(l, d2d, *after):
        done = _exchange_wait("rs_d2d_wait_%d" % l, d2d, *after)
        sums = [_pair_add(p, g, "rs_add_%d" % i) for i, (p, g) in enumerate(zip(done[:2], done[2:]))]
        lands = [_landing((3, rows_t, d), BF16), _landing((3, rows_o, d), BF16)]
        return _exchange_start("rs_ici_start_%d" % l, sums + lands, _plan_scatter_ici(2), 6, 0, after[0])

    for l in reversed(range(depth)):
        xb, h, sst, mix, z, wt, wo = saved[l]
        dz, dzb, d_lng, d_lnb = _ln_bwd(dy, z, ln_g[l][None], token_ici)
        dmix = _matmul(dzb, wo.reshape(d, d), dims=NT, tm=tm, tn=_tile(d, 512), tk=d, out_dtype=F32, name="mm_dmix")
        dwo = _matmul(mix, dzb, dims=TN, tm=_tile(d, 1024), tn=_tile(d, 1024), tk=_tile(t, 1024), out_dtype=BF16, name="mm_dwo")
        dh, sm = _mixer_bwd(dm, h, sst, dmix, bias, layer_params(l), b_loc, nblk)
        (d_caw, d_ccw, d_ccb, d_grw, d_grb, d_giw, d_gib, d_lam, d_na, d_nb, d_nc, d_snk) = sm
        small[l] = dict(conv_a_w=d_caw, conv_c_w=d_ccw, conv_c_b=d_ccb[0], gate_r_w=d_grw, gate_r_b=d_grb[0], gate_i_w=d_giw,
                        gate_i_b=d_gib[0], rg_lambda=d_lam[0], norm_a=d_na[0], norm_b=d_nb[0], norm_c=d_nc[0],
                        sinks=d_snk[:, 0], ln_g=d_lng[0], ln_b=d_lnb[0])
        token = _no_token()
        if l == 0:
            g_local = _pack([jnp.stack([small[i][n] for i in range(depth)]) for n in SMALL])
            small_ici, token = _exchange_start("ag_small_ici_start", [g_local, _landing((N_DEV,) + g_local.shape, F32)],
                                               _plan_gather_ici(1), 4, 1, dh)
        dwt = _matmul(dh, xb, dims=TN, tm=_tile(dm.in_w, 1536), tn=_tile(d, 1024), tk=_tile(t, 1024), out_dtype=BF16,
                      name="mm_dwt", token=token)
        if ici is not None:
            finish_scatter(l + 1, ici, dh)
        parts = [dwt.reshape(N_DEV, rows_t, d), dwo.reshape(N_DEV, rows_o, d)]
        lands = [_landing((4, rows_t, d), BF16), _landing((4, rows_o, d), BF16)]
        d2d, token = _exchange_start("rs_d2d_start_%d" % l, parts + lands, _plan_scatter_d2d(2), 8, 0, dh)
        if l > 0:
            dy = _matmul(dh, wt.reshape(dm.in_w, d), dims=NN, tm=tm, tn=_tile(d, 512), tk=_tile(dm.in_w, 3584), out_dtype=F32,
                         name="mm_dx", addend=dz, alpha=alpha, token=token)
            ici, token_ici = scatter_add_start(l, d2d, dy)
        else:
            small_land = _exchange_wait("ag_small_ici_wait", small_ici, token)[1:]
            small_d2d, token = _exchange_start("ag_small_d2d_start", small_land, _plan_gather_d2d(1), 3, 0, token)
            ici, token_ici = scatter_add_start(l, d2d, token)
            dy = _matmul(dh, wt.reshape(dm.in_w, d), dims=NN, tm=tm, tn=_tile(d, 512), tk=_tile(dm.in_w, 3584), out_dtype=F32,
                         name="mm_dx", addend=dz, alpha=alpha, token=token_ici)
            g_all = _exchange_wait("ag_small_d2d_wait", small_d2d, dy)[0]
    grad_x = dy.reshape(b_loc, seq, d)

    def adamw_group(first, last, prev_in, prev_out):
        gin = jnp.swapaxes(jnp.stack(g_wt[first:last]), 1, 2)
        return (_adamw_layers(w_in, gin, m_w_in, v_w_in, first, prev_in, "adamw_in_%d" % first),
                _adamw_layers(w_out, jnp.stack(g_wo[first:last]), m_w_out, v_w_out, first, prev_out, "adamw_out_%d" % first))

    res_in, res_out = None, None
    if depth > 1:
        res_in, res_out = adamw_group(1, depth, None, None)
        finish_scatter(0, ici, dy, res_in[0], res_out[0])
    else:
        finish_scatter(0, ici, dy)
    (gw_in, dl_in, nm_in, nv_in), (gw_out, dl_out, nm_out, nv_out) = adamw_group(0, 1, res_in, res_out)

    given = dict(conv_a_w=(conv_a_w, m_conv_a_w, v_conv_a_w), conv_c_w=(conv_c_w, m_conv_c_w, v_conv_c_w),
                 conv_c_b=(conv_c_b, m_conv_c_b, v_conv_c_b), gate_r_w=(gate_r_w, m_gate_r_w, v_gate_r_w),
                 gate_r_b=(gate_r_b, m_gate_r_b, v_gate_r_b), gate_i_w=(gate_i_w, m_gate_i_w, v_gate_i_w),
                 gate_i_b=(gate_i_b, m_gate_i_b, v_gate_i_b), rg_lambda=(rg_lambda, m_rg_lambda, v_rg_lambda),
                 norm_a=(norm_a, m_norm_a, v_norm_a), norm_b=(norm_b, m_norm_b, v_norm_b), norm_c=(norm_c, m_norm_c, v_norm_c),
                 sinks=(sinks, m_sinks, v_sinks), ln_g=(ln_g, m_ln_g, v_ln_g), ln_b=(ln_b, m_ln_b, v_ln_b))
    full_shapes = [jnp.stack([small[l][n] for l in range(depth)]).shape for n in SMALL]

    def mine_of(n, a):
        if n in ("conv_a_w", "conv_c_w"):
            return lax.dynamic_update_slice(jnp.zeros(a.shape[:2] + (dm.wa,), F32), a, (0, 0, dev * ch))
        return a

    packs = [_pack([mine_of(n, given[n][i]) for n in SMALL]) for i in range(3)]
    outs = _adamw(packs[0], g_all, packs[1], packs[2], "adamw_small")
    res = {}
    for kind, packed in zip(("grad", "delta", "new_m", "new_v"), outs):
        for n, a in zip(SMALL, _unpack(packed, full_shapes)):
            if n in ("conv_a_w", "conv_c_w"):
                a = lax.dynamic_slice(a, (0, 0, dev * ch), a.shape[:2] + (ch,))
            res[kind, n] = a
    res.update({("grad", "w_in"): gw_in, ("delta", "w_in"): dl_in, ("new_m", "w_in"): nm_in, ("new_v", "w_in"): nv_in,
                ("grad", "w_out"): gw_out, ("delta", "w_out"): dl_out, ("new_m", "w_out"): nm_out, ("new_v", "w_out"): nv_out})
    order = ("w_in", "conv_a_w", "sinks", "conv_c_w", "conv_c_b", "gate_r_w", "gate_r_b", "gate_i_w", "gate_i_b", "rg_lambda",
             "norm_a", "norm_b", "norm_c", "w_out", "ln_g", "ln_b")
    return (loss, grad_x, *[res[kind, n] for kind in ("grad", "delta", "new_m", "new_v") for n in order])
```

```python
import functools

import jax
import jax.numpy as jnp
from jax import lax
from jax.experimental import pallas as pl
from jax.experimental.pallas import tpu as pltpu

F32 = jnp.float32
BF16 = jnp.bfloat16
MESH = pl.DeviceIdType.MESH
ANY = pl.BlockSpec(memory_space=pl.ANY)

N_DEV = 8
LANES = 128
SUBLANES = 8
HEAD_DIM = 64
KV_GROUP = 8
ATT_BLOCK = 128
CONV_A = 3
CONV_C = 4
RG_C = 8.0
LN_EPS = 1e-5
RMS_EPS = 1e-6
NEG_INF = -1e30
ADAM_LR = 0.001
ADAM_B1 = 0.9
ADAM_B2 = 0.999
ADAM_EPS = 1e-08
ADAM_WD = 0.01
ADAM_STEP = 10
VMEM_LIMIT = 56 * 1024 * 1024

NN = ((1,), (0,))
NT = ((1,), (1,))
TN = ((0,), (0,))


def _pcall(body, **kw):
    return pl.pallas_call(body, **kw)


def _roll(x, shift, axis):
    return pltpu.roll(x, shift, axis)


def _params(sem=None, vmem=VMEM_LIMIT):
    return pltpu.CompilerParams(dimension_semantics=sem, vmem_limit_bytes=vmem)


def _dot(a, b, dims):
    return lax.dot_general(a.astype(BF16), b.astype(BF16), (dims, ((), ())), preferred_element_type=F32)


@jax.custom_vjp
def _mm(a, b):
    return _dot(a, b, NN)


def _mm_fwd(a, b):
    return _dot(a, b, NN), (a.astype(BF16), b.astype(BF16))


def _mm_bwd(res, g):
    a, b = res
    return _dot(g, b, NT), _dot(a, g, TN)


_mm.defvjp(_mm_fwd, _mm_bwd)


@jax.custom_vjp
def _mm_nt(a, b):
    return _dot(a, b, NT)


def _mm_nt_fwd(a, b):
    return _dot(a, b, NT), (a.astype(BF16), b.astype(BF16))


def _mm_nt_bwd(res, g):
    a, b = res
    return _dot(g, b, NN), _dot(g, a, TN)


_mm_nt.defvjp(_mm_nt_fwd, _mm_nt_bwd)


def _rows(shape):
    return lax.broadcasted_iota(jnp.int32, shape, 0)


@functools.partial(jax.custom_vjp, nondiff_argnums=(2,))
def _shift_halo(u, prev, k):
    r, c = u.shape
    fill = jnp.concatenate([_roll(prev, k, 0), jnp.zeros((r - SUBLANES, c), u.dtype)], axis=0)
    return jnp.where(_rows((r, c)) < k, fill, _roll(u, k, 0))


def _shift_halo_fwd(u, prev, k):
    return _shift_halo(u, prev, k), None


def _shift_halo_bwd(k, _, g):
    r, c = g.shape
    du = jnp.where(_rows((r, c)) < r - k, _roll(g, r - k, 0), 0.0)
    dprev = jnp.where(_rows((SUBLANES, c)) >= SUBLANES - k, _roll(g[0:SUBLANES], SUBLANES - k, 0), 0.0)
    return du, dprev


_shift_halo.defvjp(_shift_halo_fwd, _shift_halo_bwd)


@functools.partial(jax.custom_vjp, nondiff_argnums=(1, 2))
def _shift_fill(u, k, fill):
    return jnp.where(_rows(u.shape) < k, fill, _roll(u, k, 0))


def _shift_fill_fwd(u, k, fill):
    return _shift_fill(u, k, fill), None


def _shift_fill_bwd(k, fill, _, g):
    r = g.shape[0]
    return (jnp.where(_rows(g.shape) < r - k, _roll(g, r - k, 0), 0.0),)


_shift_fill.defvjp(_shift_fill_fwd, _shift_fill_bwd)


@jax.custom_vjp
def _swap_halves(x):
    return _roll(x, HEAD_DIM, 1)


_swap_halves.defvjp(lambda x: (_roll(x, HEAD_DIM, 1), None), lambda _, g: (_roll(g, HEAD_DIM, 1),))


@functools.partial(jax.custom_vjp, nondiff_argnums=(1,))
def _split_rows(x, n):
    r = x.shape[0] // n
    return tuple(x[i * r:(i + 1) * r] for i in range(n))


def _split_rows_fwd(x, n):
    return _split_rows(x, n), None


def _split_rows_bwd(n, _, gs):
    return (jnp.concatenate(list(gs), axis=0),)


_split_rows.defvjp(_split_rows_fwd, _split_rows_bwd)


def _logistic(x):
    return 1.0 / (1.0 + jnp.exp(-x))


@jax.custom_vjp
def _sigmoid(x):
    return _logistic(x)


def _sigmoid_fwd(x):
    s = _logistic(x)
    return s, s


_sigmoid.defvjp(_sigmoid_fwd, lambda s, g: (g * s * (1.0 - s),))


@jax.custom_vjp
def _silu(x):
    return x * _logistic(x)


def _silu_fwd(x):
    s = _logistic(x)
    return x * s, (x, s)


_silu.defvjp(_silu_fwd, lambda res, g: (g * res[1] * (1.0 + res[0] * (1.0 - res[1])),))


def _log_sigmoid(x):
    return -(jnp.maximum(-x, 0.0) + jnp.log1p(jnp.exp(-jnp.abs(x))))


@jax.custom_vjp
def _neg_expm1(x):
    series = x * (1 + x * (1 / 2) * (1 + x * (1 / 3) * (1 + x * (1 / 4) * (1 + x * (1 / 5) * (1 + x * (1 / 6) * (1 + x * (1 / 7)))))))
    return -jnp.where(jnp.abs(x) < 0.25, series, jnp.exp(x) - 1.0)


_neg_expm1.defvjp(lambda x: (_neg_expm1(x), x), lambda x, g: (-g * jnp.exp(x),))


def _scan_block(a, u, s_prev):
    acc_a, acc_b = a, u
    d = 1
    while d < a.shape[0]:
        acc_b = acc_a * _shift_fill(acc_b, d, 0.0) + acc_b
        acc_a = acc_a * _shift_fill(acc_a, d, 1.0)
        d *= 2
    return acc_a * s_prev + acc_b


def _last_row(h):
    return jnp.sum(jnp.where(_rows(h.shape) == h.shape[0] - 1, h, 0.0), axis=0, keepdims=True)


def _branch_a(ab, ac, ax, ag, acp, axp, w0, w1, w2, na):
    u = ac * ax
    up = acp * axp
    ya = ab * (w2 * u + w1 * _shift_halo(u, up, 1) + w0 * _shift_halo(u, up, 2))
    ms = jnp.sum(ya * ya, axis=1, keepdims=True) * (1.0 / ya.shape[1])
    return ya * lax.rsqrt(ms + RMS_EPS) * na * _silu(ag)


def _branch_c(cx, cg, cxp, sp, wc, bc, wr, br, wi, bi, lam, nc):
    hs, lasts = [], []
    for j in range(len(cx)):
        xc = (wc[3][j] * cx[j] + wc[2][j] * _shift_halo(cx[j], cxp[j], 1) + wc[1][j] * _shift_halo(cx[j], cxp[j], 2)
              + wc[0][j] * _shift_halo(cx[j], cxp[j], 3) + bc[j])
        r = _sigmoid(_mm(xc, wr[j]) + br[j])
        i = _sigmoid(_mm(xc, wi[j]) + bi[j])
        log_a = RG_C * r * _log_sigmoid(lam[j])
        a = jnp.exp(log_a)
        u = jnp.sqrt(_neg_expm1(2.0 * log_a)) * (i * xc)
        h = _scan_block(a, u, sp[j])
        hs.append(h)
        lasts.append(_last_row(h))
    width = LANES * len(cx)
    ms = sum(jnp.sum(h * h, axis=1, keepdims=True) for h in hs) * (1.0 / width)
    inv = lax.rsqrt(ms + RMS_EPS)
    return [hs[j] * inv * nc[j] * _silu(cg[j]) for j in range(len(cx))], lasts


def _attention_bias():
    qi = (jnp.arange(KV_GROUP * ATT_BLOCK) % ATT_BLOCK)[:, None]
    kj = jnp.arange(2 * ATT_BLOCK)[None, :]
    dist = qi + ATT_BLOCK - kj
    band = (dist >= 0) & (dist < ATT_BLOCK)
    return jnp.where(jnp.stack([band & (kj >= ATT_BLOCK), band]), 0.0, NEG_INF).astype(F32)


def _branch_b(q, k, v, kp, vp, bg, snk, nb, bias):
    rows = ATT_BLOCK
    n_kv = 2 * len(k)
    lane0 = lax.broadcasted_iota(jnp.int32, (KV_GROUP * rows, 2 * rows), 1) == 0
    upper = lax.broadcasted_iota(jnp.int32, (2 * rows, LANES), 1) >= HEAD_DIM
    heads = [None] * (n_kv * KV_GROUP)
    for g in range(n_kv):
        half = g % 2
        keep = upper if half else jnp.logical_not(upper)
        kc = jnp.where(keep, jnp.concatenate([kp[g // 2], k[g // 2]], axis=0), 0.0)
        vc = jnp.where(keep, jnp.concatenate([vp[g // 2], v[g // 2]], axis=0), 0.0)
        hs = range(g * KV_GROUP, (g + 1) * KV_GROUP)
        qg = jnp.concatenate([q[h // 2] if h % 2 == half else _swap_halves(q[h // 2]) for h in hs], axis=0)
        s = _mm_nt(qg, kc) * (HEAD_DIM ** -0.5) + bias
        sink = jnp.concatenate([jnp.broadcast_to(snk[h], (rows, 2 * rows)) for h in hs], axis=0)
        m = lax.stop_gradient(jnp.maximum(jnp.max(s, axis=1, keepdims=True), sink))
        p = jnp.exp(s - m)
        e_sink = jnp.sum(jnp.where(lane0, jnp.exp(sink - m), 0.0), axis=1, keepdims=True)
        inv = 1.0 / (jnp.sum(p, axis=1, keepdims=True) + e_sink)
        o = _split_rows(_mm(p * inv, vc), KV_GROUP)
        for i, h in enumerate(hs):
            heads[h] = o[i] if h % 2 == half else _swap_halves(o[i])
    yb = [heads[2 * j] + heads[2 * j + 1] for j in range(len(q))]
    width = LANES * len(q)
    ms = sum(jnp.sum(y * y, axis=1, keepdims=True) for y in yb) * (1.0 / width)
    inv = lax.rsqrt(ms + RMS_EPS)
    return [yb[j] * inv * nb[j] * _silu(bg[j]) for j in range(len(q))]


class _Dims:
    def __init__(self, d_model, n_rg_heads):
        self.d = d_model
        self.wa = d_model // 4
        self.wb = d_model // 2
        self.wc = d_model // 4
        self.kvw = self.wb // KV_GROUP
        self.nq = self.wb // HEAD_DIM
        self.in_w = 4 * self.wa + 2 * self.wb + 2 * self.kvw + 2 * self.wc
        self.o_q = 4 * self.wa
        self.o_k = self.o_q + self.wb
        self.o_v = self.o_k + self.kvw
        self.o_bg = self.o_v + self.kvw
        self.o_cx = self.o_bg + self.wb
        self.o_cg = self.o_cx + self.wc
        self.nh = n_rg_heads
        assert self.wc // n_rg_heads == LANES and self.kvw % LANES == 0
        assert self.o_k % self.kvw == 0 and self.o_cx % (self.wc // 2) == 0


def _chunks(ref, rows, off, width):
    return [ref[rows, off + LANES * j: off + LANES * (j + 1)] for j in range(width // LANES)]


def _read_params(dm, caw, ccw, ccb, grw, grb, giw, gib, lam, na, nb, nc, snk):
    row = slice(0, 1)
    return dict(
        wa=[caw[k:k + 1, :] for k in range(CONV_A)], na=na[...],
        wc=[_chunks(ccw, slice(k, k + 1), 0, dm.wc) for k in range(CONV_C)], bc=_chunks(ccb, row, 0, dm.wc),
        wr=[grw[j] for j in range(dm.nh)], br=_chunks(grb, row, 0, dm.wc),
        wi=[giw[j] for j in range(dm.nh)], bi=_chunks(gib, row, 0, dm.wc),
        lam=_chunks(lam, row, 0, dm.wc), nc=_chunks(nc, row, 0, dm.wc),
        nb=_chunks(nb, row, 0, dm.wb), snk=[snk[h:h + 1, :] for h in range(dm.nq)])


def _param_specs(dm):
    shapes = [(CONV_A, dm.wa), (CONV_C, dm.wc), (1, dm.wc), (dm.nh, LANES, LANES), (1, dm.wc), (dm.nh, LANES, LANES),
              (1, dm.wc), (1, dm.wc), (1, dm.wa), (1, dm.wb), (1, dm.wc), (dm.nq, 2 * ATT_BLOCK)]
    specs = [pl.BlockSpec(s, (lambda b, n, _r=len(s): (0,) * _r)) for s in shapes]
    return shapes, specs


def _bias_spec():
    return pl.BlockSpec((2, KV_GROUP * ATT_BLOCK, 2 * ATT_BLOCK), lambda *_: (0, 0, 0))


def _mixer_fwd(dm, h, bias, prm, b_loc, nblk):
    t = h.shape[0]
    r = ATT_BLOCK
    tail = slice(r - SUBLANES, r)

    def body(h_ref, bias_ref, caw, ccw, ccb, grw, grb, giw, gib, lam, na, nb, nc, snk, mix_ref, sst_ref, kp, vp, acp, axp, cxp, sp):
        n = pl.program_id(1)

        @pl.when(n == 0)
        def _():
            for ref in (kp, vp, acp, axp, cxp, sp):
                ref[...] = jnp.zeros(ref.shape, ref.dtype)

        p = _read_params(dm, caw, ccw, ccb, grw, grb, giw, gib, lam, na, nb, nc, snk)
        full = slice(None)
        mix_a = _branch_a(h_ref[:, 0:dm.wa], h_ref[:, dm.wa:2 * dm.wa], h_ref[:, 2 * dm.wa:3 * dm.wa],
                          h_ref[:, 3 * dm.wa:4 * dm.wa], acp[...], axp[...], p["wa"][0], p["wa"][1], p["wa"][2], p["na"])
        mix_ref[:, 0:dm.wa] = mix_a.astype(BF16)
        bias = bias_ref[jnp.where(n == 0, 0, 1)]
        mix_b = _branch_b(_chunks(h_ref, full, dm.o_q, dm.wb), _chunks(h_ref, full, dm.o_k, dm.kvw),
                          _chunks(h_ref, full, dm.o_v, dm.kvw), _chunks(kp, full, 0, dm.kvw), _chunks(vp, full, 0, dm.kvw),
                          _chunks(h_ref, full, dm.o_bg, dm.wb), p["snk"], p["nb"], bias)
        for j, mb in enumerate(mix_b):
            mix_ref[:, dm.wa + LANES * j: dm.wa + LANES * (j + 1)] = mb.astype(BF16)
        sst_ref[0] = sp[...]
        mix_c, lasts = _branch_c(_chunks(h_ref, full, dm.o_cx, dm.wc), _chunks(h_ref, full, dm.o_cg, dm.wc),
                                 _chunks(cxp, full, 0, dm.wc), _chunks(sp, slice(0, 1), 0, dm.wc), p["wc"], p["bc"],
                                 p["wr"], p["br"], p["wi"], p["bi"], p["lam"], p["nc"])
        o_c = dm.wa + dm.wb
        for j, mc in enumerate(mix_c):
            mix_ref[:, o_c + LANES * j: o_c + LANES * (j + 1)] = mc.astype(BF16)
            sp[:, LANES * j: LANES * (j + 1)] = jnp.broadcast_to(lasts[j], (SUBLANES, LANES))
        kp[...] = h_ref[:, dm.o_k:dm.o_k + dm.kvw]
        vp[...] = h_ref[:, dm.o_v:dm.o_v + dm.kvw]
        acp[...] = h_ref[tail, dm.wa:2 * dm.wa]
        axp[...] = h_ref[tail, 2 * dm.wa:3 * dm.wa]
        cxp[...] = h_ref[tail, dm.o_cx:dm.o_cx + dm.wc]

    _, pspecs = _param_specs(dm)
    return _pcall(
        body, name="mixer_fwd", grid=(b_loc, nblk),
        in_specs=[pl.BlockSpec((r, dm.in_w), lambda b, n: (b * nblk + n, 0)), _bias_spec()] + pspecs,
        out_specs=[pl.BlockSpec((r, dm.d), lambda b, n: (b * nblk + n, 0)),
                   pl.BlockSpec((1, SUBLANES, dm.wc), lambda b, n: (b * nblk + n, 0, 0))],
        out_shape=[jax.ShapeDtypeStruct((t, dm.d), BF16), jax.ShapeDtypeStruct((b_loc * nblk, SUBLANES, dm.wc), F32)],
        scratch_shapes=[pltpu.VMEM((r, dm.kvw), F32), pltpu.VMEM((r, dm.kvw), F32), pltpu.VMEM((SUBLANES, dm.wa), F32),
                        pltpu.VMEM((SUBLANES, dm.wa), F32), pltpu.VMEM((SUBLANES, dm.wc), F32),
                        pltpu.VMEM((SUBLANES, dm.wc), F32)],
        compiler_params=_params(("arbitrary", "arbitrary")),
    )(h, bias, *prm)


def _mixer_bwd(dm, h, sst, dmix, bias, prm, b_loc, nblk):
    t = h.shape[0]
    r = ATT_BLOCK
    rb8 = r // SUBLANES
    n_small = 12

    def body(h_ref, kp_ref, vp_ref, acp_ref, axp_ref, cxp0_ref, cxp1_ref, sst_ref, dmix_ref, bias_ref,
             caw, ccw, ccb, grw, grb, giw, gib, lam, na, nb, nc, snk,
             dh_ref, d_caw, d_ccw, d_ccb, d_grw, d_grb, d_giw, d_gib, d_lam, d_na, d_nb, d_nc, d_snk,
             dkp, dvp, dacp, daxp, dcxp, dsp):
        step = pl.program_id(1)
        n = nblk - 1 - step

        @pl.when(step == 0)
        def _():
            for ref in (dkp, dvp, dacp, daxp, dcxp, dsp):
                ref[...] = jnp.zeros(ref.shape, ref.dtype)

        @pl.when((step == 0) & (pl.program_id(0) == 0))
        def _():
            for ref in (d_caw, d_ccw, d_ccb, d_grw, d_grb, d_giw, d_gib, d_lam, d_na, d_nb, d_nc, d_snk):
                ref[...] = jnp.zeros(ref.shape, ref.dtype)

        p = _read_params(dm, caw, ccw, ccb, grw, grb, giw, gib, lam, na, nb, nc, snk)
        has_prev = jnp.where(n > 0, 1.0, 0.0)
        full = slice(None)
        pad = jnp.zeros((r - SUBLANES, LANES), F32)

        def with_tail(own, carry):
            z = jnp.zeros((r - SUBLANES, own.shape[1]), F32)
            return own + jnp.concatenate([z, carry], axis=0)

        a_in = (h_ref[:, 0:dm.wa], h_ref[:, dm.wa:2 * dm.wa], h_ref[:, 2 * dm.wa:3 * dm.wa], h_ref[:, 3 * dm.wa:4 * dm.wa],
                acp_ref[...] * has_prev, axp_ref[...] * has_prev, p["wa"][0], p["wa"][1], p["wa"][2], p["na"])
        _, vjp_a = jax.vjp(_branch_a, *a_in)
        g_ab, g_ac, g_ax, g_ag, g_acp, g_axp, g_w0, g_w1, g_w2, g_na = vjp_a(dmix_ref[:, 0:dm.wa])
        dh_ref[:, 0:dm.wa] = g_ab.astype(BF16)
        dh_ref[:, dm.wa:2 * dm.wa] = with_tail(g_ac, dacp[...]).astype(BF16)
        dh_ref[:, 2 * dm.wa:3 * dm.wa] = with_tail(g_ax, daxp[...]).astype(BF16)
        dh_ref[:, 3 * dm.wa:4 * dm.wa] = g_ag.astype(BF16)
        dacp[...] = g_acp
        daxp[...] = g_axp
        for k, gw in enumerate((g_w0, g_w1, g_w2)):
            d_caw[k:k + 1, :] += gw
        d_na[...] += g_na

        bias = bias_ref[jnp.where(n == 0, 0, 1)]
        kp_in = [c * has_prev for c in _chunks(kp_ref, full, 0, dm.kvw)]
        vp_in = [c * has_prev for c in _chunks(vp_ref, full, 0, dm.kvw)]
        b_in = (_chunks(h_ref, full, dm.o_q, dm.wb), _chunks(h_ref, full, dm.o_k, dm.kvw), _chunks(h_ref, full, dm.o_v, dm.kvw),
                kp_in, vp_in, _chunks(h_ref, full, dm.o_bg, dm.wb), p["snk"], p["nb"])
        _, vjp_b = jax.vjp(lambda *a: _branch_b(*a, bias), *b_in)
        g_q, g_k, g_v, g_kp, g_vp, g_bg, g_snk, g_nb = vjp_b(_chunks(dmix_ref, full, dm.wa, dm.wb))
        for j in range(len(g_q)):
            dh_ref[:, dm.o_q + LANES * j: dm.o_q + LANES * (j + 1)] = g_q[j].astype(BF16)
            dh_ref[:, dm.o_bg + LANES * j: dm.o_bg + LANES * (j + 1)] = g_bg[j].astype(BF16)
            d_nb[:, LANES * j: LANES * (j + 1)] += g_nb[j]
        for j in range(len(g_k)):
            cols = slice(LANES * j, LANES * (j + 1))
            dh_ref[:, dm.o_k + LANES * j: dm.o_k + LANES * (j + 1)] = (g_k[j] + dkp[:, cols]).astype(BF16)
            dh_ref[:, dm.o_v + LANES * j: dm.o_v + LANES * (j + 1)] = (g_v[j] + dvp[:, cols]).astype(BF16)
            dkp[:, cols] = g_kp[j]
            dvp[:, cols] = g_vp[j]
        for hd in range(dm.nq):
            d_snk[hd:hd + 1, :] += g_snk[hd]

        half_c = dm.wc // 2
        cxp_in = ([c * has_prev for c in _chunks(cxp0_ref, full, 0, half_c)]
                  + [c * has_prev for c in _chunks(cxp1_ref, full, 0, half_c)])
        c_in = (_chunks(h_ref, full, dm.o_cx, dm.wc), _chunks(h_ref, full, dm.o_cg, dm.wc), cxp_in,
                [sst_ref[0, 0:1, LANES * j: LANES * (j + 1)] for j in range(dm.nh)], p["wc"], p["bc"], p["wr"], p["br"], p["wi"], p["bi"],
                p["lam"], p["nc"])
        _, vjp_c = jax.vjp(_branch_c, *c_in)
        ct_last = [dsp[0:1, LANES * j: LANES * (j + 1)] for j in range(dm.nh)]
        g_cx, g_cg, g_cxp, g_sp, g_wc, g_bc, g_wr, g_br, g_wi, g_bi, g_lam, g_nc = vjp_c(
            (_chunks(dmix_ref, full, dm.wa + dm.wb, dm.wc), ct_last))
        for j in range(dm.nh):
            cols = slice(LANES * j, LANES * (j + 1))
            tot = g_cx[j] + jnp.concatenate([pad, dcxp[:, cols]], axis=0)
            dh_ref[:, dm.o_cx + LANES * j: dm.o_cx + LANES * (j + 1)] = tot.astype(BF16)
            dh_ref[:, dm.o_cg + LANES * j: dm.o_cg + LANES * (j + 1)] = g_cg[j].astype(BF16)
            dcxp[:, cols] = g_cxp[j]
            dsp[:, cols] = jnp.broadcast_to(g_sp[j], (SUBLANES, LANES))
            for k in range(CONV_C):
                d_ccw[k:k + 1, cols] += g_wc[k][j]
            d_ccb[:, cols] += g_bc[j]
            d_grw[j] += g_wr[j]
            d_grb[:, cols] += g_br[j]
            d_giw[j] += g_wi[j]
            d_gib[:, cols] += g_bi[j]
            d_lam[:, cols] += g_lam[j]
            d_nc[:, cols] += g_nc[j]

    def blk(b, s):
        return b * nblk + (nblk - 1 - s)

    def prev_rows8(b, s):
        return jnp.maximum(blk(b, s) * rb8 - 1, 0)

    pshapes, pspecs = _param_specs(dm)
    half_c = dm.wc // 2
    in_specs = [
        pl.BlockSpec((r, dm.in_w), lambda b, s: (blk(b, s), 0)),
        pl.BlockSpec((r, dm.kvw), lambda b, s: (jnp.maximum(blk(b, s) - 1, 0), dm.o_k // dm.kvw)),
        pl.BlockSpec((r, dm.kvw), lambda b, s: (jnp.maximum(blk(b, s) - 1, 0), dm.o_v // dm.kvw)),
        pl.BlockSpec((SUBLANES, dm.wa), lambda b, s: (prev_rows8(b, s), 1)),
        pl.BlockSpec((SUBLANES, dm.wa), lambda b, s: (prev_rows8(b, s), 2)),
        pl.BlockSpec((SUBLANES, half_c), lambda b, s: (prev_rows8(b, s), dm.o_cx // half_c)),
        pl.BlockSpec((SUBLANES, half_c), lambda b, s: (prev_rows8(b, s), dm.o_cx // half_c + 1)),
        pl.BlockSpec((1, SUBLANES, dm.wc), lambda b, s: (blk(b, s), 0, 0)),
        pl.BlockSpec((r, dm.d), lambda b, s: (blk(b, s), 0)),
        _bias_spec(),
    ] + pspecs
    outs = _pcall(
        body, name="mixer_bwd", grid=(b_loc, nblk), in_specs=in_specs,
        out_specs=[pl.BlockSpec((r, dm.in_w), lambda b, s: (blk(b, s), 0))] + pspecs,
        out_shape=[jax.ShapeDtypeStruct((t, dm.in_w), BF16)] + [jax.ShapeDtypeStruct(s, F32) for s in pshapes],
        scratch_shapes=[pltpu.VMEM((r, dm.kvw), F32), pltpu.VMEM((r, dm.kvw), F32), pltpu.VMEM((SUBLANES, dm.wa), F32),
                        pltpu.VMEM((SUBLANES, dm.wa), F32), pltpu.VMEM((SUBLANES, dm.wc), F32),
                        pltpu.VMEM((SUBLANES, dm.wc), F32)],
        compiler_params=_params(("arbitrary", "arbitrary")),
    )(h, h, h, h, h, h, h, sst, dmix, bias, *prm)
    assert len(outs) == 1 + n_small
    return outs[0], outs[1:]


def _token_spec():
    return pl.BlockSpec((SUBLANES, LANES), lambda *_: (0, 0))


def _no_token():
    return jnp.zeros((SUBLANES, LANES), F32)


def _matmul(a, b, *, dims, tm, tn, tk, out_dtype, name, addend=None, alpha=None, token=None):
    if dims == TN:
        (k_dim, m), n_dim = a.shape, b.shape[1]
        a_spec = pl.BlockSpec((tk, tm), lambda i, j, k: (k, i))
    else:
        (m, k_dim), n_dim = a.shape, (b.shape[0] if dims == NT else b.shape[1])
        a_spec = pl.BlockSpec((tm, tk), lambda i, j, k: (i, k))
    b_spec = pl.BlockSpec((tn, tk), lambda i, j, k: (j, k)) if dims == NT else pl.BlockSpec((tk, tn), lambda i, j, k: (k, j))
    assert m % tm == 0 and n_dim % tn == 0 and k_dim % tk == 0, (a.shape, b.shape, tm, tn, tk)
    nk = k_dim // tk
    o_spec = pl.BlockSpec((tm, tn), lambda i, j, k: (i, j))

    def body(*refs):
        a_ref, b_ref = refs[0], refs[1]
        add_ref = refs[2] if addend is not None else None
        o_ref, acc_ref = refs[-2], refs[-1]
        k = pl.program_id(2)
        part = lax.dot_general(a_ref[...], b_ref[...], (dims, ((), ())), preferred_element_type=F32)

        def finish(acc):
            if add_ref is not None:
                acc = acc + alpha * add_ref[...]
            o_ref[...] = acc.astype(out_dtype)

        if nk == 1:
            finish(part)
        else:
            @pl.when(k == 0)
            def _():
                acc_ref[...] = part

            @pl.when((k > 0) & (k < nk - 1))
            def _():
                acc_ref[...] += part

            @pl.when(k == nk - 1)
            def _():
                finish(acc_ref[...] + part)

    ins = [a, b] + ([addend] if addend is not None else []) + ([token] if token is not None else [])
    in_specs = [a_spec, b_spec] + ([o_spec] if addend is not None else []) + ([_token_spec()] if token is not None else [])
    return _pcall(
        body, name=name, grid=(m // tm, n_dim // tn, nk), in_specs=in_specs, out_specs=o_spec,
        out_shape=jax.ShapeDtypeStruct((m, n_dim), out_dtype),
        scratch_shapes=[pltpu.VMEM((tm, tn) if nk > 1 else (SUBLANES, LANES), F32)],
        compiler_params=_params(("parallel", "parallel", "arbitrary")),
    )(*ins)


def _tile(n, want, quantum=LANES):
    if n <= want:
        return n
    for cand in range(want - want % quantum, 0, -quantum):
        if n % cand == 0:
            return cand
    return n


def _row_tile(t, d):
    return _tile(t, max(2 * SUBLANES, (1 << 19) // d), 2 * SUBLANES)


def _ln_fwd(z, g, b, token):
    t, d = z.shape
    tr = _row_tile(t, d)

    def body(z_ref, g_ref, b_ref, _, y_ref, yb_ref):
        zz = z_ref[...]
        mu = jnp.mean(zz, axis=1, keepdims=True)
        zc = zz - mu
        var = jnp.mean(zc * zc, axis=1, keepdims=True)
        y = zc * lax.rsqrt(var + LN_EPS) * g_ref[...] + b_ref[...]
        y_ref[...] = y
        yb_ref[...] = y.astype(BF16)

    row = pl.BlockSpec((tr, d), lambda i: (i, 0))
    vec = pl.BlockSpec((1, d), lambda i: (0, 0))
    return _pcall(body, name="ln_fwd", grid=(t // tr,), in_specs=[row, vec, vec, _token_spec()], out_specs=[row, row],
                  out_shape=[jax.ShapeDtypeStruct((t, d), F32), jax.ShapeDtypeStruct((t, d), BF16)],
                  compiler_params=_params(("parallel",)))(z, g, b, token)


def _ln_bwd(dy, z, g, token):
    t, d = z.shape
    tr = _row_tile(t, d)

    def body(dy_ref, z_ref, g_ref, _, dz_ref, dzb_ref, dg_ref, db_ref):
        @pl.when(pl.program_id(0) == 0)
        def _():
            dg_ref[...] = jnp.zeros(dg_ref.shape, F32)
            db_ref[...] = jnp.zeros(db_ref.shape, F32)

        zz = z_ref[...]
        dyy = dy_ref[...]
        mu = jnp.mean(zz, axis=1, keepdims=True)
        zc = zz - mu
        rstd = lax.rsqrt(jnp.mean(zc * zc, axis=1, keepdims=True) + LN_EPS)
        xhat = zc * rstd
        dyg = dyy * g_ref[...]
        dz = rstd * (dyg - jnp.mean(dyg, axis=1, keepdims=True) - xhat * jnp.mean(dyg * xhat, axis=1, keepdims=True))
        dz_ref[...] = dz
        dzb_ref[...] = dz.astype(BF16)
        dg_ref[...] += jnp.sum(dyy * xhat, axis=0, keepdims=True)
        db_ref[...] += jnp.sum(dyy, axis=0, keepdims=True)

    row = pl.BlockSpec((tr, d), lambda i: (i, 0))
    vec = pl.BlockSpec((1, d), lambda i: (0, 0))
    return _pcall(body, name="ln_bwd", grid=(t // tr,), in_specs=[row, row, vec, _token_spec()], out_specs=[row, row, vec, vec],
                  out_shape=[jax.ShapeDtypeStruct((t, d), F32), jax.ShapeDtypeStruct((t, d), BF16),
                             jax.ShapeDtypeStruct((1, d), F32), jax.ShapeDtypeStruct((1, d), F32)],
                  compiler_params=_params(("arbitrary",)))(dy, z, g, token)


def _loss_head(y, target):
    t, d = y.shape
    tr = _row_tile(t, d)

    def body(y_ref, t_ref, dy_ref, loss_ref):
        @pl.when(pl.program_id(0) == 0)
        def _():
            loss_ref[...] = jnp.zeros(loss_ref.shape, F32)

        err = y_ref[...] - t_ref[...]
        dy_ref[...] = err * (1.0 / d)
        per_token = jnp.sum(err * err, axis=1, keepdims=True) * (1.0 / d)
        loss_ref[...] += 0.5 * jnp.sum(per_token, axis=0, keepdims=True)

    row = pl.BlockSpec((tr, d), lambda i: (i, 0))
    one = pl.BlockSpec((1, 1), lambda i: (0, 0))
    return _pcall(body, name="loss_head", grid=(t // tr,), in_specs=[row, row], out_specs=[row, one],
                  out_shape=[jax.ShapeDtypeStruct((t, d), F32), jax.ShapeDtypeStruct((1, 1), F32)],
                  compiler_params=_params(("arbitrary",)))(y, target)


def _adamw_scattered(w, m, v, layer, sums, far, prev, name):
    n_layers, r, c = w.shape
    tr = _row_tile(r, c)

    def body(*refs):
        w_ref, m_ref, v_ref, s_ref, f_ref = refs[:5]
        go_ref, d_ref, mo_ref, vo_ref = refs[-4:]
        gg = s_ref[...].astype(F32)
        for i in range(3):
            gg = gg + f_ref[i:i + 1].astype(F32)
        m_new = ADAM_B1 * m_ref[...] + (1.0 - ADAM_B1) * gg
        v_new = ADAM_B2 * v_ref[...] + (1.0 - ADAM_B2) * (gg * gg)
        m_hat = m_new / (1.0 - ADAM_B1 ** ADAM_STEP)
        v_hat = v_new / (1.0 - ADAM_B2 ** ADAM_STEP)
        go_ref[...] = gg
        d_ref[...] = -ADAM_LR * (m_hat / (jnp.sqrt(v_hat) + ADAM_EPS) + ADAM_WD * w_ref[...])
        mo_ref[...] = m_new
        vo_ref[...] = v_new

    own = pl.BlockSpec((1, tr, c), lambda i: (layer, i, 0))
    in_specs = [own, own, own, pl.BlockSpec((1, tr, c), lambda i: (2 * lax.axis_index("x") + lax.axis_index("y"), i, 0)),
                pl.BlockSpec((3, tr, c), lambda i: (0, i, 0))] + ([ANY] * 4 if prev is not None else [])
    return _pcall(body, name=name, grid=(r // tr,), in_specs=in_specs, out_specs=[own] * 4,
                  out_shape=[jax.ShapeDtypeStruct((n_layers, r, c), F32)] * 4,
                  input_output_aliases={5 + i: i for i in range(4)} if prev is not None else {},
                  compiler_params=_params(("parallel",)))(w, m, v, sums, far, *(prev if prev is not None else []))


def _pair_add(a, b, name):
    p, r, c = b.shape
    tr = _row_tile(r, c)

    def body(a_ref, b_ref, o_ref):
        o_ref[...] = (a_ref[...].astype(F32) + b_ref[...].astype(F32)).astype(BF16)

    spec = pl.BlockSpec((1, tr, c), lambda q, i: (q, i, 0))
    return _pcall(body, name=name, grid=(p, r // tr),
                  in_specs=[pl.BlockSpec((1, tr, c), lambda q, i: (2 * q + lax.axis_index("c"), i, 0)), spec], out_specs=spec,
                  out_shape=jax.ShapeDtypeStruct((p, r, c), BF16), compiler_params=_params(("parallel", "parallel")))(a, b)


def _adamw(w, g_parts, m, v, name):
    r, c = w.shape
    n_parts = g_parts.shape[0]
    tr = _row_tile(r, c) if r % SUBLANES == 0 else r

    def body(w_ref, g_ref, m_ref, v_ref, go_ref, d_ref, mo_ref, vo_ref):
        g = g_ref[0].astype(F32)
        for i in range(1, n_parts):
            g = g + g_ref[i].astype(F32)
        m_new = ADAM_B1 * m_ref[...] + (1.0 - ADAM_B1) * g
        v_new = ADAM_B2 * v_ref[...] + (1.0 - ADAM_B2) * (g * g)
        m_hat = m_new / (1.0 - ADAM_B1 ** ADAM_STEP)
        v_hat = v_new / (1.0 - ADAM_B2 ** ADAM_STEP)
        go_ref[...] = g
        d_ref[...] = -ADAM_LR * (m_hat / (jnp.sqrt(v_hat) + ADAM_EPS) + ADAM_WD * w_ref[...])
        mo_ref[...] = m_new
        vo_ref[...] = v_new

    spec = pl.BlockSpec((tr, c), lambda i: (i, 0))
    shape = jax.ShapeDtypeStruct((r, c), F32)
    return _pcall(body, name=name, grid=(r // tr,),
                  in_specs=[spec, pl.BlockSpec((n_parts, tr, c), lambda i: (0, i, 0)), spec, spec],
                  out_specs=[spec] * 4, out_shape=[shape] * 4, compiler_params=_params(("parallel",)))(w, g_parts, m, v)


def _me():
    return lax.axis_index("x"), lax.axis_index("y"), lax.axis_index("c")


def _dev(px, py, pc):
    return 4 * px + 2 * py + pc


def _remote(src, dst, send_sems, recv_sems, k, to):
    return pltpu.make_async_remote_copy(src_ref=src, dst_ref=dst, send_sem=send_sems.at[k], recv_sem=recv_sems.at[k],
                                        device_id=to, device_id_type=MESH)


def _all_gather(arrs, name):
    n = len(arrs)

    def body(*refs):
        ins, outs = refs[:n], refs[n:2 * n]
        send_sems, recv_sems, local_sems = refs[2 * n:]
        x, y, c = _me()
        me, sibling = (x, y, c), (x, y, 1 - c)
        chips = [(1 - x, y), (x, 1 - y), (1 - x, 1 - y)]
        pending = []
        for a in range(n):
            mine = pltpu.make_async_copy(ins[a], outs[a].at[_dev(*me)], local_sems.at[a])
            mine.start()
            pending.append(mine)
        sends = []
        for a in range(n):
            dst = outs[a].at[_dev(*me)]
            sends.append(_remote(ins[a], dst, send_sems, recv_sems, 7 * a, sibling))
            sends += [_remote(ins[a], dst, send_sems, recv_sems, 7 * a + 1 + j, (*chip, c)) for j, chip in enumerate(chips)]
        for cp in sends:
            cp.start()
        for a in range(n):
            for j, chip in enumerate(chips):
                blk = outs[a].at[_dev(*chip, c)]
                _remote(blk, blk, send_sems, recv_sems, 7 * a + 1 + j, me).wait_recv()
                fwd = _remote(blk, blk, send_sems, recv_sems, 7 * a + 4 + j, sibling)
                fwd.start()
                sends.append(fwd)
        for a in range(n):
            blk = outs[a].at[_dev(*sibling)]
            _remote(blk, blk, send_sems, recv_sems, 7 * a, me).wait_recv()
            for j, chip in enumerate(chips):
                blk = outs[a].at[_dev(*chip, 1 - c)]
                _remote(blk, blk, send_sems, recv_sems, 7 * a + 4 + j, me).wait_recv()
        for cp in sends:
            cp.wait_send()
        for cp in pending:
            cp.wait()

    return _pcall(
        body, name=name, in_specs=[ANY] * n, out_specs=[ANY] * n,
        out_shape=[jax.ShapeDtypeStruct((N_DEV,) + a.shape, a.dtype) for a in arrs],
        scratch_shapes=[pltpu.SemaphoreType.DMA((7 * n,)), pltpu.SemaphoreType.DMA((7 * n,)), pltpu.SemaphoreType.DMA((n,))],
    )(*arrs)


def _relations(x, y):
    return [(x, y), (1 - x, y), (x, 1 - y), (1 - x, 1 - y)]


HBM_SPEC = pl.BlockSpec(memory_space=pltpu.HBM)
SEM_SPEC = pl.BlockSpec(memory_space=pltpu.SEMAPHORE)
DATAFLOW = pltpu.SideEffectType.DATAFLOW_SIDE_EFFECTING


def _exchange_start(name, bufs, plan, n_remote, n_local, dep):
    nb = len(bufs)
    sem_shapes = [pltpu.SemaphoreType.DMA((n_remote,)), pltpu.SemaphoreType.DMA((n_remote,))]
    if n_local:
        sem_shapes.append(pltpu.SemaphoreType.DMA((n_local,)))
    ns = len(sem_shapes)

    def body(*refs):
        ins, sems, token = refs[:nb], refs[nb + 1:nb + 1 + ns], refs[-1]
        starts, _, local = plan(ins, sems[0], sems[1], sems[2] if n_local else None)
        for cp in local + starts:
            cp.start()
        token[...] = jnp.zeros(token.shape, F32)

    outs = _pcall(
        body, name=name, in_specs=[HBM_SPEC] * nb + [ANY],
        out_specs=[SEM_SPEC] * ns + [HBM_SPEC] * nb + [pl.BlockSpec(memory_space=pltpu.VMEM)],
        out_shape=sem_shapes + [pltpu.HBM(b.shape, b.dtype) for b in bufs] + [jax.ShapeDtypeStruct((SUBLANES, LANES), F32)],
        input_output_aliases={i: ns + i for i in range(nb)}, compiler_params=pltpu.CompilerParams(has_side_effects=DATAFLOW),
    )(*[pltpu.with_memory_space_constraint(b, pltpu.HBM) for b in bufs], dep)
    return dict(sems=outs[:ns], thru=outs[ns:ns + nb], plan=plan, n_local=n_local), outs[-1]


def _exchange_wait(name, handle, *after):
    thru, sems, plan, n_local = handle["thru"], handle["sems"], handle["plan"], handle["n_local"]
    nb, ns = len(thru), len(sems)

    def body(*refs):
        ins, sem_refs = refs[:nb], refs[nb:nb + ns]
        starts, arrivals, local = plan(ins, sem_refs[0], sem_refs[1], sem_refs[2] if n_local else None)
        for cp in starts:
            cp.wait_send()
        for cp in arrivals:
            cp.wait_recv()
        for cp in local:
            cp.wait()

    return _pcall(
        body, name=name, in_specs=[HBM_SPEC] * nb + [SEM_SPEC] * ns + [ANY] * len(after), out_specs=[HBM_SPEC] * nb,
        out_shape=[pltpu.HBM(b.shape, b.dtype) for b in thru], input_output_aliases={i: i for i in range(nb)},
        compiler_params=pltpu.CompilerParams(has_side_effects=DATAFLOW),
    )(*thru, *sems, *after)


def _landing(shape, dtype):
    return lax.empty(shape, dtype)


def _plan_gather_ici(n):
    def plan(refs, send_sems, recv_sems, local_sems):
        x, y, c = _me()
        me, sibling = (x, y, c), (x, y, 1 - c)
        chips = _relations(x, y)[1:]
        starts, arrivals, local = [], [], []
        for a in range(n):
            shard, land = refs[a], refs[n + a]
            own = land.at[_dev(*me)]
            local.append(pltpu.make_async_copy(shard, own, local_sems.at[a]))
            starts.append(_remote(shard, own, send_sems, recv_sems, 4 * a, sibling))
            blk = land.at[_dev(*sibling)]
            arrivals.append(_remote(blk, blk, send_sems, recv_sems, 4 * a, me))
            for j, chip in enumerate(chips):
                starts.append(_remote(shard, own, send_sems, recv_sems, 4 * a + 1 + j, (*chip, c)))
                blk = land.at[_dev(*chip, c)]
                arrivals.append(_remote(blk, blk, send_sems, recv_sems, 4 * a + 1 + j, me))
        return starts, arrivals, local
    return plan


def _plan_gather_d2d(n):
    def plan(refs, send_sems, recv_sems, local_sems):
        x, y, c = _me()
        me, sibling = (x, y, c), (x, y, 1 - c)
        starts, arrivals = [], []
        for a in range(n):
            for j, chip in enumerate(_relations(x, y)[1:]):
                blk = refs[a].at[_dev(*chip, c)]
                starts.append(_remote(blk, blk, send_sems, recv_sems, 3 * a + j, sibling))
                blk = refs[a].at[_dev(*chip, 1 - c)]
                arrivals.append(_remote(blk, blk, send_sems, recv_sems, 3 * a + j, me))
        return starts, arrivals, []
    return plan


def _plan_scatter_d2d(n):
    def plan(refs, send_sems, recv_sems, local_sems):
        x, y, c = _me()
        me, sibling = (x, y, c), (x, y, 1 - c)
        starts, arrivals = [], []
        for a in range(n):
            for k in range(4):
                starts.append(_remote(refs[a].at[2 * k + 1 - c], refs[n + a].at[k], send_sems, recv_sems, 4 * a + k, sibling))
                blk = refs[n + a].at[k]
                arrivals.append(_remote(blk, blk, send_sems, recv_sems, 4 * a + k, me))
        return starts, arrivals, []
    return plan


def _plan_scatter_ici(n):
    def plan(refs, send_sems, recv_sems, local_sems):
        x, y, c = _me()
        me = (x, y, c)
        starts, arrivals = [], []
        for a in range(n):
            for j, (cx, cy) in enumerate(_relations(x, y)[1:]):
                starts.append(_remote(refs[a].at[2 * cx + cy], refs[n + a].at[j], send_sems, recv_sems, 3 * a + j, (cx, cy, c)))
                blk = refs[n + a].at[j]
                arrivals.append(_remote(blk, blk, send_sems, recv_sems, 3 * a + j, me))
        return starts, arrivals, []
    return plan


SMALL = ("conv_a_w", "conv_c_w", "conv_c_b", "gate_r_w", "gate_r_b", "gate_i_w", "gate_i_b", "rg_lambda",
         "norm_a", "norm_b", "norm_c", "sinks", "ln_g", "ln_b")
PACK_COLS = 1024


def _pack(arrs):
    flat = jnp.concatenate([a.reshape(-1) for a in arrs])
    pad = (-flat.shape[0]) % (SUBLANES * PACK_COLS)
    return jnp.pad(flat, (0, pad)).reshape(-1, PACK_COLS)


def _unpack(packed, shapes):
    flat = packed.reshape(-1)
    out, off = [], 0
    for s in shapes:
        size = 1
        for dim in s:
            size *= dim
        out.append(flat[off:off + size].reshape(s))
        off += size
    return out


def kernel(x, w_in, conv_a_w, sinks, conv_c_w, conv_c_b, gate_r_w, gate_r_b, gate_i_w, gate_i_b, rg_lambda, norm_a, norm_b, norm_c, w_out, ln_g, ln_b, loss_target, m_w_in, m_conv_a_w, m_sinks, m_conv_c_w, m_conv_c_b, m_gate_r_w, m_gate_r_b, m_gate_i_w, m_gate_i_b, m_rg_lambda, m_norm_a, m_norm_b, m_norm_c, m_w_out, m_ln_g, m_ln_b, v_w_in, v_conv_a_w, v_sinks, v_conv_c_w, v_conv_c_b, v_gate_r_w, v_gate_r_b, v_gate_i_w, v_gate_i_b, v_rg_lambda, v_norm_a, v_norm_b, v_norm_c, v_w_out, v_ln_g, v_ln_b):
    b_loc, seq, d = x.shape
    depth = w_in.shape[0]
    dm = _Dims(d, gate_r_w.shape[1])
    t = b_loc * seq
    nblk = seq // ATT_BLOCK
    alpha = (2.0 * depth) ** 0.25
    ch = dm.wa // N_DEV
    dev = _dev(*_me())

    wt_shard = jnp.swapaxes(w_in, 1, 2).astype(BF16)
    wo_shard = w_out.astype(BF16)
    conv_shard = jnp.concatenate([conv_a_w.reshape(depth * CONV_A, ch), conv_c_w.reshape(depth * CONV_C, ch)], axis=0)
    conv_all = _all_gather([jnp.pad(conv_shard, ((0, (-conv_shard.shape[0]) % SUBLANES), (0, 0)))], "ag_conv")[0]
    conv_all = jnp.swapaxes(conv_all, 0, 1).reshape(conv_all.shape[1], dm.wa)
    conv_a_full = conv_all[:depth * CONV_A].reshape(depth, CONV_A, dm.wa)
    conv_c_full = conv_all[depth * CONV_A:depth * (CONV_A + CONV_C)].reshape(depth, CONV_C, dm.wc)
    sinks_wide = jnp.broadcast_to(sinks[:, :, None], (depth, dm.nq, 2 * ATT_BLOCK))

    def layer_params(l):
        return (conv_a_full[l], conv_c_full[l], conv_c_b[l][None], gate_r_w[l], gate_r_b[l][None], gate_i_w[l],
                gate_i_b[l][None], rg_lambda[l][None], norm_a[l][None], norm_b[l][None], norm_c[l][None], sinks_wide[l])

    tm = _tile(t, 1024)
    xs = x.reshape(t, d)
    xb = xs.astype(BF16)
    saved = []
    bias = _attention_bias()
    rows_t, rows_o = dm.in_w // N_DEV, d // N_DEV
    wt, wo = _all_gather([wt_shard[0], wo_shard[0]], "ag_weights")
    for l in range(depth):
        token = _no_token()
        if l + 1 < depth:
            lands = [_landing((N_DEV, rows_t, d), BF16), _landing((N_DEV, rows_o, d), BF16)]
            ici, token = _exchange_start("ag_ici_start_%d" % l, [wt_shard[l + 1], wo_shard[l + 1]] + lands, _plan_gather_ici(2), 8, 2, wt)
        h = _matmul(xb, wt.reshape(dm.in_w, d), dims=NT, tm=tm, tn=_tile(dm.in_w, 512), tk=d, out_dtype=F32, name="mm_in",
                    token=token)
        mix, sst = _mixer_fwd(dm, h, bias, layer_params(l), b_loc, nblk)
        z = _matmul(mix, wo.reshape(d, d), dims=NN, tm=tm, tn=_tile(d, 512), tk=d, out_dtype=F32, name="mm_out", addend=xs,
                    alpha=alpha)
        saved.append((xb, h, sst, mix, z, wt, wo))
        token = _no_token()
        if l + 1 < depth:
            lands = _exchange_wait("ag_ici_wait_%d" % l, ici, z)[2:]
            d2d, token = _exchange_start("ag_d2d_start_%d" % l, lands, _plan_gather_d2d(2), 6, 0, z)
        xs, xb = _ln_fwd(z, ln_g[l][None], ln_b[l][None], token)
        if l + 1 < depth:
            wt, wo = _exchange_wait("ag_d2d_wait_%d" % l, d2d, xb)

    dy, loss_part = _loss_head(xs, loss_target.reshape(t, d))
    loss = lax.psum(loss_part[0, 0], ("x", "y", "c"))

    scattered, small = [None] * depth, [None] * depth
    ici, token_ici = None, _no_token()

    def finish_scatter(l, ici, *after):
        scattered[l] = _exchange_wait("rs_ici_wait_%d" % l, ici, *after)

    def scatter_add_start(l, d2d, *after):
        done = _exchange_wait("rs_d2d_wait_%d" % l, d2d, *after)
        sums = [_pair_add(p, g, "rs_add_%d" % i) for i, (p, g) in enumerate(zip(done[:2], done[2:]))]
        lands = [_landing((3, rows_t, d), BF16), _landing((3, rows_o, d), BF16)]
        return _exchange_start("rs_ici_start_%d" % l, sums + lands, _plan_scatter_ici(2), 6, 0, after[0])

    for l in reversed(range(depth)):
        xb, h, sst, mix, z, wt, wo = saved[l]
        dz, dzb, d_lng, d_lnb = _ln_bwd(dy, z, ln_g[l][None], token_ici)
        dmix = _matmul(dzb, wo.reshape(d, d), dims=NT, tm=tm, tn=_tile(d, 512), tk=d, out_dtype=F32, name="mm_dmix")
        dwo = _matmul(mix, dzb, dims=TN, tm=_tile(d, 1024), tn=_tile(d, 1024), tk=_tile(t, 1024), out_dtype=BF16, name="mm_dwo")
        dh, sm = _mixer_bwd(dm, h, sst, dmix, bias, layer_params(l), b_loc, nblk)
        (d_caw, d_ccw, d_ccb, d_grw, d_grb, d_giw, d_gib, d_lam, d_na, d_nb, d_nc, d_snk) = sm
        small[l] = dict(conv_a_w=d_caw, conv_c_w=d_ccw, conv_c_b=d_ccb[0], gate_r_w=d_grw, gate_r_b=d_grb[0], gate_i_w=d_giw,
                        gate_i_b=d_gib[0], rg_lambda=d_lam[0], norm_a=d_na[0], norm_b=d_nb[0], norm_c=d_nc[0],
                        sinks=d_snk[:, 0], ln_g=d_lng[0], ln_b=d_lnb[0])
        token = _no_token()
        if l == 0:
            g_local = _pack([jnp.stack([small[i][n] for i in range(depth)]) for n in SMALL])
            small_ici, token = _exchange_start("ag_small_ici_start", [g_local, _landing((N_DEV,) + g_local.shape, F32)],
                                               _plan_gather_ici(1), 4, 1, dh)
        dwt = _matmul(dh, xb, dims=TN, tm=_tile(dm.in_w, 1536), tn=_tile(d, 1024), tk=_tile(t, 1024), out_dtype=BF16,
                      name="mm_dwt", token=token)
        if ici is not None:
            finish_scatter(l + 1, ici, dh)
        parts = [dwt.reshape(N_DEV, rows_t, d), dwo.reshape(N_DEV, rows_o, d)]
        lands = [_landing((4, rows_t, d), BF16), _landing((4, rows_o, d), BF16)]
        d2d, token = _exchange_start("rs_d2d_start_%d" % l, parts + lands, _plan_scatter_d2d(2), 8, 0, dh)
        if l > 0:
            dy = _matmul(dh, wt.reshape(dm.in_w, d), dims=NN, tm=tm, tn=_tile(d, 512), tk=_tile(dm.in_w, 3584), out_dtype=F32,
                         name="mm_dx", addend=dz, alpha=alpha, token=token)
            ici, token_ici = scatter_add_start(l, d2d, dy)
        else:
            small_land = _exchange_wait("ag_small_ici_wait", small_ici, token)[1:]
            small_d2d, token = _exchange_start("ag_small_d2d_start", small_land, _plan_gather_d2d(1), 3, 0, token)
            ici, token_ici = scatter_add_start(l, d2d, token)
            dy = _matmul(dh, wt.reshape(dm.in_w, d), dims=NN, tm=tm, tn=_tile(d, 512), tk=_tile(dm.in_w, 3584), out_dtype=F32,
                         name="mm_dx", addend=dz, alpha=alpha, token=token_ici)
            g_all = _exchange_wait("ag_small_d2d_wait", small_d2d, dy)[0]
    grad_x = dy.reshape(b_loc, seq, d)

    w_t, m_t, v_t = [jnp.swapaxes(a, 1, 2) for a in (w_in, m_w_in, v_w_in)]
    res_in, res_out = None, None
    for l in reversed(range(depth)):
        if l == 0:
            finish_scatter(0, ici, dy, *([res_in[0], res_out[0]] if depth > 1 else []))
        sums_t, sums_o, far_t, far_o = scattered[l]
        res_in = _adamw_scattered(w_t, m_t, v_t, l, sums_t, far_t, res_in, "adamw_in_%d" % l)
        res_out = _adamw_scattered(w_out, m_w_out, v_w_out, l, sums_o, far_o, res_out, "adamw_out_%d" % l)
    gw_in, dl_in, nm_in, nv_in = [jnp.swapaxes(a, 1, 2) for a in res_in]
    gw_out, dl_out, nm_out, nv_out = res_out

    given = dict(conv_a_w=(conv_a_w, m_conv_a_w, v_conv_a_w), conv_c_w=(conv_c_w, m_conv_c_w, v_conv_c_w),
                 conv_c_b=(conv_c_b, m_conv_c_b, v_conv_c_b), gate_r_w=(gate_r_w, m_gate_r_w, v_gate_r_w),
                 gate_r_b=(gate_r_b, m_gate_r_b, v_gate_r_b), gate_i_w=(gate_i_w, m_gate_i_w, v_gate_i_w),
                 gate_i_b=(gate_i_b, m_gate_i_b, v_gate_i_b), rg_lambda=(rg_lambda, m_rg_lambda, v_rg_lambda),
                 norm_a=(norm_a, m_norm_a, v_norm_a), norm_b=(norm_b, m_norm_b, v_norm_b), norm_c=(norm_c, m_norm_c, v_norm_c),
                 sinks=(sinks, m_sinks, v_sinks), ln_g=(ln_g, m_ln_g, v_ln_g), ln_b=(ln_b, m_ln_b, v_ln_b))
    full_shapes = [jnp.stack([small[l][n] for l in range(depth)]).shape for n in SMALL]

    def mine_of(n, a):
        if n in ("conv_a_w", "conv_c_w"):
            return lax.dynamic_update_slice(jnp.zeros(a.shape[:2] + (dm.wa,), F32), a, (0, 0, dev * ch))
        return a

    packs = [_pack([mine_of(n, given[n][i]) for n in SMALL]) for i in range(3)]
    outs = _adamw(packs[0], g_all, packs[1], packs[2], "adamw_small")
    res = {}
    for kind, packed in zip(("grad", "delta", "new_m", "new_v"), outs):
        for n, a in zip(SMALL, _unpack(packed, full_shapes)):
            if n in ("conv_a_w", "conv_c_w"):
                a = lax.dynamic_slice(a, (0, 0, dev * ch), a.shape[:2] + (ch,))
            res[kind, n] = a
    res.update({("grad", "w_in"): gw_in, ("delta", "w_in"): dl_in, ("new_m", "w_in"): nm_in, ("new_v", "w_in"): nv_in,
                ("grad", "w_out"): gw_out, ("delta", "w_out"): dl_out, ("new_m", "w_out"): nm_out, ("new_v", "w_out"): nv_out})
    order = ("w_in", "conv_a_w", "sinks", "conv_c_w", "conv_c_b", "gate_r_w", "gate_r_b", "gate_i_w", "gate_i_b", "rg_lambda",
             "norm_a", "norm_b", "norm_c", "w_out", "ln_g", "ln_b")
    return (loss, grad_x, *[res[kind, n] for kind in ("grad", "delta", "new_m", "new_v") for n in order])
```

```python
import functools

import jax
import jax.numpy as jnp
from jax import lax
from jax.experimental import pallas as pl
from jax.experimental.pallas import tpu as pltpu

F32 = jnp.float32
BF16 = jnp.bfloat16
MESH = pl.DeviceIdType.MESH
ANY = pl.BlockSpec(memory_space=pl.ANY)

N_DEV = 8
LANES = 128
SUBLANES = 8
HEAD_DIM = 64
KV_GROUP = 8
ATT_BLOCK = 128
CONV_A = 3
CONV_C = 4
RG_C = 8.0
LN_EPS = 1e-5
RMS_EPS = 1e-6
NEG_INF = -1e30
ADAM_LR = 0.001
ADAM_B1 = 0.9
ADAM_B2 = 0.999
ADAM_EPS = 1e-08
ADAM_WD = 0.01
ADAM_STEP = 10
VMEM_LIMIT = 56 * 1024 * 1024

NN = ((1,), (0,))
NT = ((1,), (1,))
TN = ((0,), (0,))


def _pcall(body, **kw):
    return pl.pallas_call(body, **kw)


def _roll(x, shift, axis):
    return pltpu.roll(x, shift, axis)


def _params(sem=None, vmem=VMEM_LIMIT):
    return pltpu.CompilerParams(dimension_semantics=sem, vmem_limit_bytes=vmem)


def _dot(a, b, dims):
    return lax.dot_general(a.astype(BF16), b.astype(BF16), (dims, ((), ())), preferred_element_type=F32)


@jax.custom_vjp
def _mm(a, b):
    return _dot(a, b, NN)


def _mm_fwd(a, b):
    return _dot(a, b, NN), (a.astype(BF16), b.astype(BF16))


def _mm_bwd(res, g):
    a, b = res
    return _dot(g, b, NT), _dot(a, g, TN)


_mm.defvjp(_mm_fwd, _mm_bwd)


@jax.custom_vjp
def _mm_nt(a, b):
    return _dot(a, b, NT)


def _mm_nt_fwd(a, b):
    return _dot(a, b, NT), (a.astype(BF16), b.astype(BF16))


def _mm_nt_bwd(res, g):
    a, b = res
    return _dot(g, b, NN), _dot(g, a, TN)


_mm_nt.defvjp(_mm_nt_fwd, _mm_nt_bwd)


def _rows(shape):
    return lax.broadcasted_iota(jnp.int32, shape, 0)


@functools.partial(jax.custom_vjp, nondiff_argnums=(2,))
def _shift_halo(u, prev, k):
    r, c = u.shape
    fill = jnp.concatenate([_roll(prev, k, 0), jnp.zeros((r - SUBLANES, c), u.dtype)], axis=0)
    return jnp.where(_rows((r, c)) < k, fill, _roll(u, k, 0))


def _shift_halo_fwd(u, prev, k):
    return _shift_halo(u, prev, k), None


def _shift_halo_bwd(k, _, g):
    r, c = g.shape
    du = jnp.where(_rows((r, c)) < r - k, _roll(g, r - k, 0), 0.0)
    dprev = jnp.where(_rows((SUBLANES, c)) >= SUBLANES - k, _roll(g[0:SUBLANES], SUBLANES - k, 0), 0.0)
    return du, dprev


_shift_halo.defvjp(_shift_halo_fwd, _shift_halo_bwd)


@functools.partial(jax.custom_vjp, nondiff_argnums=(1, 2))
def _shift_fill(u, k, fill):
    return jnp.where(_rows(u.shape) < k, fill, _roll(u, k, 0))


def _shift_fill_fwd(u, k, fill):
    return _shift_fill(u, k, fill), None


def _shift_fill_bwd(k, fill, _, g):
    r = g.shape[0]
    return (jnp.where(_rows(g.shape) < r - k, _roll(g, r - k, 0), 0.0),)


_shift_fill.defvjp(_shift_fill_fwd, _shift_fill_bwd)


@jax.custom_vjp
def _swap_halves(x):
    return _roll(x, HEAD_DIM, 1)


_swap_halves.defvjp(lambda x: (_roll(x, HEAD_DIM, 1), None), lambda _, g: (_roll(g, HEAD_DIM, 1),))


@functools.partial(jax.custom_vjp, nondiff_argnums=(1,))
def _split_rows(x, n):
    r = x.shape[0] // n
    return tuple(x[i * r:(i + 1) * r] for i in range(n))


def _split_rows_fwd(x, n):
    return _split_rows(x, n), None


def _split_rows_bwd(n, _, gs):
    return (jnp.concatenate(list(gs), axis=0),)


_split_rows.defvjp(_split_rows_fwd, _split_rows_bwd)


def _logistic(x):
    return 1.0 / (1.0 + jnp.exp(-x))


@jax.custom_vjp
def _sigmoid(x):
    return _logistic(x)


def _sigmoid_fwd(x):
    s = _logistic(x)
    return s, s


_sigmoid.defvjp(_sigmoid_fwd, lambda s, g: (g * s * (1.0 - s),))


@jax.custom_vjp
def _silu(x):
    return x * _logistic(x)


def _silu_fwd(x):
    s = _logistic(x)
    return x * s, (x, s)


_silu.defvjp(_silu_fwd, lambda res, g: (g * res[1] * (1.0 + res[0] * (1.0 - res[1])),))


def _log_sigmoid(x):
    return -(jnp.maximum(-x, 0.0) + jnp.log1p(jnp.exp(-jnp.abs(x))))


@jax.custom_vjp
def _neg_expm1(x):
    series = x * (1 + x * (1 / 2) * (1 + x * (1 / 3) * (1 + x * (1 / 4) * (1 + x * (1 / 5) * (1 + x * (1 / 6) * (1 + x * (1 / 7)))))))
    return -jnp.where(jnp.abs(x) < 0.25, series, jnp.exp(x) - 1.0)


_neg_expm1.defvjp(lambda x: (_neg_expm1(x), x), lambda x, g: (-g * jnp.exp(x),))


def _scan_block(a, u, s_prev):
    acc_a, acc_b = a, u
    d = 1
    while d < a.shape[0]:
        acc_b = acc_a * _shift_fill(acc_b, d, 0.0) + acc_b
        acc_a = acc_a * _shift_fill(acc_a, d, 1.0)
        d *= 2
    return acc_a * s_prev + acc_b


def _last_row(h):
    return jnp.sum(jnp.where(_rows(h.shape) == h.shape[0] - 1, h, 0.0), axis=0, keepdims=True)


def _branch_a(ab, ac, ax, ag, acp, axp, w0, w1, w2, na):
    u = ac * ax
    up = acp * axp
    ya = ab * (w2 * u + w1 * _shift_halo(u, up, 1) + w0 * _shift_halo(u, up, 2))
    ms = jnp.sum(ya * ya, axis=1, keepdims=True) * (1.0 / ya.shape[1])
    return ya * lax.rsqrt(ms + RMS_EPS) * na * _silu(ag)


def _branch_c(cx, cg, cxp, sp, wc, bc, wr, br, wi, bi, lam, nc):
    hs, lasts = [], []
    for j in range(len(cx)):
        xc = (wc[3][j] * cx[j] + wc[2][j] * _shift_halo(cx[j], cxp[j], 1) + wc[1][j] * _shift_halo(cx[j], cxp[j], 2)
              + wc[0][j] * _shift_halo(cx[j], cxp[j], 3) + bc[j])
        r = _sigmoid(_mm(xc, wr[j]) + br[j])
        i = _sigmoid(_mm(xc, wi[j]) + bi[j])
        log_a = RG_C * r * _log_sigmoid(lam[j])
        a = jnp.exp(log_a)
        u = jnp.sqrt(_neg_expm1(2.0 * log_a)) * (i * xc)
        h = _scan_block(a, u, sp[j])
        hs.append(h)
        lasts.append(_last_row(h))
    width = LANES * len(cx)
    ms = sum(jnp.sum(h * h, axis=1, keepdims=True) for h in hs) * (1.0 / width)
    inv = lax.rsqrt(ms + RMS_EPS)
    return [hs[j] * inv * nc[j] * _silu(cg[j]) for j in range(len(cx))], lasts


def _attention_bias():
    qi = (jnp.arange(KV_GROUP * ATT_BLOCK) % ATT_BLOCK)[:, None]
    kj = jnp.arange(2 * ATT_BLOCK)[None, :]
    dist = qi + ATT_BLOCK - kj
    band = (dist >= 0) & (dist < ATT_BLOCK)
    return jnp.where(jnp.stack([band & (kj >= ATT_BLOCK), band]), 0.0, NEG_INF).astype(F32)


def _branch_b(q, k, v, kp, vp, bg, snk, nb, bias):
    rows = ATT_BLOCK
    n_kv = 2 * len(k)
    lane0 = lax.broadcasted_iota(jnp.int32, (KV_GROUP * rows, 2 * rows), 1) == 0
    upper = lax.broadcasted_iota(jnp.int32, (2 * rows, LANES), 1) >= HEAD_DIM
    heads = [None] * (n_kv * KV_GROUP)
    for g in range(n_kv):
        half = g % 2
        keep = upper if half else jnp.logical_not(upper)
        kc = jnp.where(keep, jnp.concatenate([kp[g // 2], k[g // 2]], axis=0), 0.0)
        vc = jnp.where(keep, jnp.concatenate([vp[g // 2], v[g // 2]], axis=0), 0.0)
        hs = range(g * KV_GROUP, (g + 1) * KV_GROUP)
        qg = jnp.concatenate([q[h // 2] if h % 2 == half else _swap_halves(q[h // 2]) for h in hs], axis=0)
        s = _mm_nt(qg, kc) * (HEAD_DIM ** -0.5) + bias
        sink = jnp.concatenate([jnp.broadcast_to(snk[h], (rows, 2 * rows)) for h in hs], axis=0)
        m = lax.stop_gradient(jnp.maximum(jnp.max(s, axis=1, keepdims=True), sink))
        p = jnp.exp(s - m)
        e_sink = jnp.sum(jnp.where(lane0, jnp.exp(sink - m), 0.0), axis=1, keepdims=True)
        inv = 1.0 / (jnp.sum(p, axis=1, keepdims=True) + e_sink)
        o = _split_rows(_mm(p * inv, vc), KV_GROUP)
        for i, h in enumerate(hs):
            heads[h] = o[i] if h % 2 == half else _swap_halves(o[i])
    yb = [heads[2 * j] + heads[2 * j + 1] for j in range(len(q))]
    width = LANES * len(q)
    ms = sum(jnp.sum(y * y, axis=1, keepdims=True) for y in yb) * (1.0 / width)
    inv = lax.rsqrt(ms + RMS_EPS)
    return [yb[j] * inv * nb[j] * _silu(bg[j]) for j in range(len(q))]


class _Dims:
    def __init__(self, d_model, n_rg_heads):
        self.d = d_model
        self.wa = d_model // 4
        self.wb = d_model // 2
        self.wc = d_model // 4
        self.kvw = self.wb // KV_GROUP
        self.nq = self.wb // HEAD_DIM
        self.in_w = 4 * self.wa + 2 * self.wb + 2 * self.kvw + 2 * self.wc
        self.o_q = 4 * self.wa
        self.o_k = self.o_q + self.wb
        self.o_v = self.o_k + self.kvw
        self.o_bg = self.o_v + self.kvw
        self.o_cx = self.o_bg + self.wb
        self.o_cg = self.o_cx + self.wc
        self.nh = n_rg_heads
        assert self.wc // n_rg_heads == LANES and self.kvw % LANES == 0
        assert self.o_k % self.kvw == 0 and self.o_cx % (self.wc // 2) == 0


def _chunks(ref, rows, off, width):
    return [ref[rows, off + LANES * j: off + LANES * (j + 1)] for j in range(width // LANES)]


def _read_params(dm, caw, ccw, ccb, grw, grb, giw, gib, lam, na, nb, nc, snk):
    row = slice(0, 1)
    return dict(
        wa=[caw[k:k + 1, :] for k in range(CONV_A)], na=na[...],
        wc=[_chunks(ccw, slice(k, k + 1), 0, dm.wc) for k in range(CONV_C)], bc=_chunks(ccb, row, 0, dm.wc),
        wr=[grw[j] for j in range(dm.nh)], br=_chunks(grb, row, 0, dm.wc),
        wi=[giw[j] for j in range(dm.nh)], bi=_chunks(gib, row, 0, dm.wc),
        lam=_chunks(lam, row, 0, dm.wc), nc=_chunks(nc, row, 0, dm.wc),
        nb=_chunks(nb, row, 0, dm.wb), snk=[snk[h:h + 1, :] for h in range(dm.nq)])


def _param_specs(dm):
    shapes = [(CONV_A, dm.wa), (CONV_C, dm.wc), (1, dm.wc), (dm.nh, LANES, LANES), (1, dm.wc), (dm.nh, LANES, LANES),
              (1, dm.wc), (1, dm.wc), (1, dm.wa), (1, dm.wb), (1, dm.wc), (dm.nq, 2 * ATT_BLOCK)]
    specs = [pl.BlockSpec(s, (lambda b, n, _r=len(s): (0,) * _r)) for s in shapes]
    return shapes, specs


def _bias_spec():
    return pl.BlockSpec((2, KV_GROUP * ATT_BLOCK, 2 * ATT_BLOCK), lambda *_: (0, 0, 0))


def _mixer_fwd(dm, h, bias, prm, b_loc, nblk):
    t = h.shape[0]
    r = ATT_BLOCK
    tail = slice(r - SUBLANES, r)

    def body(h_ref, bias_ref, caw, ccw, ccb, grw, grb, giw, gib, lam, na, nb, nc, snk, mix_ref, sst_ref, kp, vp, acp, axp, cxp, sp):
        n = pl.program_id(1)

        @pl.when(n == 0)
        def _():
            for ref in (kp, vp, acp, axp, cxp, sp):
                ref[...] = jnp.zeros(ref.shape, ref.dtype)

        p = _read_params(dm, caw, ccw, ccb, grw, grb, giw, gib, lam, na, nb, nc, snk)
        full = slice(None)
        mix_a = _branch_a(h_ref[:, 0:dm.wa], h_ref[:, dm.wa:2 * dm.wa], h_ref[:, 2 * dm.wa:3 * dm.wa],
                          h_ref[:, 3 * dm.wa:4 * dm.wa], acp[...], axp[...], p["wa"][0], p["wa"][1], p["wa"][2], p["na"])
        mix_ref[:, 0:dm.wa] = mix_a.astype(BF16)
        bias = bias_ref[jnp.where(n == 0, 0, 1)]
        mix_b = _branch_b(_chunks(h_ref, full, dm.o_q, dm.wb), _chunks(h_ref, full, dm.o_k, dm.kvw),
                          _chunks(h_ref, full, dm.o_v, dm.kvw), _chunks(kp, full, 0, dm.kvw), _chunks(vp, full, 0, dm.kvw),
                          _chunks(h_ref, full, dm.o_bg, dm.wb), p["snk"], p["nb"], bias)
        for j, mb in enumerate(mix_b):
            mix_ref[:, dm.wa + LANES * j: dm.wa + LANES * (j + 1)] = mb.astype(BF16)
        sst_ref[0] = sp[...]
        mix_c, lasts = _branch_c(_chunks(h_ref, full, dm.o_cx, dm.wc), _chunks(h_ref, full, dm.o_cg, dm.wc),
                                 _chunks(cxp, full, 0, dm.wc), _chunks(sp, slice(0, 1), 0, dm.wc), p["wc"], p["bc"],
                                 p["wr"], p["br"], p["wi"], p["bi"], p["lam"], p["nc"])
        o_c = dm.wa + dm.wb
        for j, mc in enumerate(mix_c):
            mix_ref[:, o_c + LANES * j: o_c + LANES * (j + 1)] = mc.astype(BF16)
            sp[:, LANES * j: LANES * (j + 1)] = jnp.broadcast_to(lasts[j], (SUBLANES, LANES))
        kp[...] = h_ref[:, dm.o_k:dm.o_k + dm.kvw]
        vp[...] = h_ref[:, dm.o_v:dm.o_v + dm.kvw]
        acp[...] = h_ref[tail, dm.wa:2 * dm.wa]
        axp[...] = h_ref[tail, 2 * dm.wa:3 * dm.wa]
        cxp[...] = h_ref[tail, dm.o_cx:dm.o_cx + dm.wc]

    _, pspecs = _param_specs(dm)
    return _pcall(
        body, name="mixer_fwd", grid=(b_loc, nblk),
        in_specs=[pl.BlockSpec((r, dm.in_w), lambda b, n: (b * nblk + n, 0)), _bias_spec()] + pspecs,
        out_specs=[pl.BlockSpec((r, dm.d), lambda b, n: (b * nblk + n, 0)),
                   pl.BlockSpec((1, SUBLANES, dm.wc), lambda b, n: (b * nblk + n, 0, 0))],
        out_shape=[jax.ShapeDtypeStruct((t, dm.d), BF16), jax.ShapeDtypeStruct((b_loc * nblk, SUBLANES, dm.wc), F32)],
        scratch_shapes=[pltpu.VMEM((r, dm.kvw), F32), pltpu.VMEM((r, dm.kvw), F32), pltpu.VMEM((SUBLANES, dm.wa), F32),
                        pltpu.VMEM((SUBLANES, dm.wa), F32), pltpu.VMEM((SUBLANES, dm.wc), F32),
                        pltpu.VMEM((SUBLANES, dm.wc), F32)],
        compiler_params=_params(("arbitrary", "arbitrary")),
    )(h, bias, *prm)


def _mixer_bwd(dm, h, sst, dmix, bias, prm, b_loc, nblk):
    t = h.shape[0]
    r = ATT_BLOCK
    rb8 = r // SUBLANES
    n_small = 12

    def body(h_ref, kp_ref, vp_ref, acp_ref, axp_ref, cxp0_ref, cxp1_ref, sst_ref, dmix_ref, bias_ref,
             caw, ccw, ccb, grw, grb, giw, gib, lam, na, nb, nc, snk,
             dh_ref, d_caw, d_ccw, d_ccb, d_grw, d_grb, d_giw, d_gib, d_lam, d_na, d_nb, d_nc, d_snk,
             dkp, dvp, dacp, daxp, dcxp, dsp):
        step = pl.program_id(1)
        n = nblk - 1 - step

        @pl.when(step == 0)
        def _():
            for ref in (dkp, dvp, dacp, daxp, dcxp, dsp):
                ref[...] = jnp.zeros(ref.shape, ref.dtype)

        @pl.when((step == 0) & (pl.program_id(0) == 0))
        def _():
            for ref in (d_caw, d_ccw, d_ccb, d_grw, d_grb, d_giw, d_gib, d_lam, d_na, d_nb, d_nc, d_snk):
                ref[...] = jnp.zeros(ref.shape, ref.dtype)

        p = _read_params(dm, caw, ccw, ccb, grw, grb, giw, gib, lam, na, nb, nc, snk)
        has_prev = jnp.where(n > 0, 1.0, 0.0)
        full = slice(None)
        pad = jnp.zeros((r - SUBLANES, LANES), F32)

        def with_tail(own, carry):
            z = jnp.zeros((r - SUBLANES, own.shape[1]), F32)
            return own + jnp.concatenate([z, carry], axis=0)

        a_in = (h_ref[:, 0:dm.wa], h_ref[:, dm.wa:2 * dm.wa], h_ref[:, 2 * dm.wa:3 * dm.wa], h_ref[:, 3 * dm.wa:4 * dm.wa],
                acp_ref[...] * has_prev, axp_ref[...] * has_prev, p["wa"][0], p["wa"][1], p["wa"][2], p["na"])
        _, vjp_a = jax.vjp(_branch_a, *a_in)
        g_ab, g_ac, g_ax, g_ag, g_acp, g_axp, g_w0, g_w1, g_w2, g_na = vjp_a(dmix_ref[:, 0:dm.wa])
        dh_ref[:, 0:dm.wa] = g_ab.astype(BF16)
        dh_ref[:, dm.wa:2 * dm.wa] = with_tail(g_ac, dacp[...]).astype(BF16)
        dh_ref[:, 2 * dm.wa:3 * dm.wa] = with_tail(g_ax, daxp[...]).astype(BF16)
        dh_ref[:, 3 * dm.wa:4 * dm.wa] = g_ag.astype(BF16)
        dacp[...] = g_acp
        daxp[...] = g_axp
        for k, gw in enumerate((g_w0, g_w1, g_w2)):
            d_caw[k:k + 1, :] += gw
        d_na[...] += g_na

        bias = bias_ref[jnp.where(n == 0, 0, 1)]
        kp_in = [c * has_prev for c in _chunks(kp_ref, full, 0, dm.kvw)]
        vp_in = [c * has_prev for c in _chunks(vp_ref, full, 0, dm.kvw)]
        b_in = (_chunks(h_ref, full, dm.o_q, dm.wb), _chunks(h_ref, full, dm.o_k, dm.kvw), _chunks(h_ref, full, dm.o_v, dm.kvw),
                kp_in, vp_in, _chunks(h_ref, full, dm.o_bg, dm.wb), p["snk"], p["nb"])
        _, vjp_b = jax.vjp(lambda *a: _branch_b(*a, bias), *b_in)
        g_q, g_k, g_v, g_kp, g_vp, g_bg, g_snk, g_nb = vjp_b(_chunks(dmix_ref, full, dm.wa, dm.wb))
        for j in range(len(g_q)):
            dh_ref[:, dm.o_q + LANES * j: dm.o_q + LANES * (j + 1)] = g_q[j].astype(BF16)
            dh_ref[:, dm.o_bg + LANES * j: dm.o_bg + LANES * (j + 1)] = g_bg[j].astype(BF16)
            d_nb[:, LANES * j: LANES * (j + 1)] += g_nb[j]
        for j in range(len(g_k)):
            cols = slice(LANES * j, LANES * (j + 1))
            dh_ref[:, dm.o_k + LANES * j: dm.o_k + LANES * (j + 1)] = (g_k[j] + dkp[:, cols]).astype(BF16)
            dh_ref[:, dm.o_v + LANES * j: dm.o_v + LANES * (j + 1)] = (g_v[j] + dvp[:, cols]).astype(BF16)
            dkp[:, cols] = g_kp[j]
            dvp[:, cols] = g_vp[j]
        for hd in range(dm.nq):
            d_snk[hd:hd + 1, :] += g_snk[hd]

        half_c = dm.wc // 2
        cxp_in = ([c * has_prev for c in _chunks(cxp0_ref, full, 0, half_c)]
                  + [c * has_prev for c in _chunks(cxp1_ref, full, 0, half_c)])
        c_in = (_chunks(h_ref, full, dm.o_cx, dm.wc), _chunks(h_ref, full, dm.o_cg, dm.wc), cxp_in,
                [sst_ref[0, 0:1, LANES * j: LANES * (j + 1)] for j in range(dm.nh)], p["wc"], p["bc"], p["wr"], p["br"], p["wi"], p["bi"],
                p["lam"], p["nc"])
        _, vjp_c = jax.vjp(_branch_c, *c_in)
        ct_last = [dsp[0:1, LANES * j: LANES * (j + 1)] for j in range(dm.nh)]
        g_cx, g_cg, g_cxp, g_sp, g_wc, g_bc, g_wr, g_br, g_wi, g_bi, g_lam, g_nc = vjp_c(
            (_chunks(dmix_ref, full, dm.wa + dm.wb, dm.wc), ct_last))
        for j in range(dm.nh):
            cols = slice(LANES * j, LANES * (j + 1))
            tot = g_cx[j] + jnp.concatenate([pad, dcxp[:, cols]], axis=0)
            dh_ref[:, dm.o_cx + LANES * j: dm.o_cx + LANES * (j + 1)] = tot.astype(BF16)
            dh_ref[:, dm.o_cg + LANES * j: dm.o_cg + LANES * (j + 1)] = g_cg[j].astype(BF16)
            dcxp[:, cols] = g_cxp[j]
            dsp[:, cols] = jnp.broadcast_to(g_sp[j], (SUBLANES, LANES))
            for k in range(CONV_C):
                d_ccw[k:k + 1, cols] += g_wc[k][j]
            d_ccb[:, cols] += g_bc[j]
            d_grw[j] += g_wr[j]
            d_grb[:, cols] += g_br[j]
            d_giw[j] += g_wi[j]
            d_gib[:, cols] += g_bi[j]
            d_lam[:, cols] += g_lam[j]
            d_nc[:, cols] += g_nc[j]

    def blk(b, s):
        return b * nblk + (nblk - 1 - s)

    def prev_rows8(b, s):
        return jnp.maximum(blk(b, s) * rb8 - 1, 0)

    pshapes, pspecs = _param_specs(dm)
    half_c = dm.wc // 2
    in_specs = [
        pl.BlockSpec((r, dm.in_w), lambda b, s: (blk(b, s), 0)),
        pl.BlockSpec((r, dm.kvw), lambda b, s: (jnp.maximum(blk(b, s) - 1, 0), dm.o_k // dm.kvw)),
        pl.BlockSpec((r, dm.kvw), lambda b, s: (jnp.maximum(blk(b, s) - 1, 0), dm.o_v // dm.kvw)),
        pl.BlockSpec((SUBLANES, dm.wa), lambda b, s: (prev_rows8(b, s), 1)),
        pl.BlockSpec((SUBLANES, dm.wa), lambda b, s: (prev_rows8(b, s), 2)),
        pl.BlockSpec((SUBLANES, half_c), lambda b, s: (prev_rows8(b, s), dm.o_cx // half_c)),
        pl.BlockSpec((SUBLANES, half_c), lambda b, s: (prev_rows8(b, s), dm.o_cx // half_c + 1)),
        pl.BlockSpec((1, SUBLANES, dm.wc), lambda b, s: (blk(b, s), 0, 0)),
        pl.BlockSpec((r, dm.d), lambda b, s: (blk(b, s), 0)),
        _bias_spec(),
    ] + pspecs
    outs = _pcall(
        body, name="mixer_bwd", grid=(b_loc, nblk), in_specs=in_specs,
        out_specs=[pl.BlockSpec((r, dm.in_w), lambda b, s: (blk(b, s), 0))] + pspecs,
        out_shape=[jax.ShapeDtypeStruct((t, dm.in_w), BF16)] + [jax.ShapeDtypeStruct(s, F32) for s in pshapes],
        scratch_shapes=[pltpu.VMEM((r, dm.kvw), F32), pltpu.VMEM((r, dm.kvw), F32), pltpu.VMEM((SUBLANES, dm.wa), F32),
                        pltpu.VMEM((SUBLANES, dm.wa), F32), pltpu.VMEM((SUBLANES, dm.wc), F32),
                        pltpu.VMEM((SUBLANES, dm.wc), F32)],
        compiler_params=_params(("arbitrary", "arbitrary")),
    )(h, h, h, h, h, h, h, sst, dmix, bias, *prm)
    assert len(outs) == 1 + n_small
    return outs[0], outs[1:]


def _token_spec():
    return pl.BlockSpec((SUBLANES, LANES), lambda *_: (0, 0))


def _no_token():
    return jnp.zeros((SUBLANES, LANES), F32)


def _matmul(a, b, *, dims, tm, tn, tk, out_dtype, name, addend=None, alpha=None, token=None):
    if dims == TN:
        (k_dim, m), n_dim = a.shape, b.shape[1]
        a_spec = pl.BlockSpec((tk, tm), lambda i, j, k: (k, i))
    else:
        (m, k_dim), n_dim = a.shape, (b.shape[0] if dims == NT else b.shape[1])
        a_spec = pl.BlockSpec((tm, tk), lambda i, j, k: (i, k))
    b_spec = pl.BlockSpec((tn, tk), lambda i, j, k: (j, k)) if dims == NT else pl.BlockSpec((tk, tn), lambda i, j, k: (k, j))
    assert m % tm == 0 and n_dim % tn == 0 and k_dim % tk == 0, (a.shape, b.shape, tm, tn, tk)
    nk = k_dim // tk
    o_spec = pl.BlockSpec((tm, tn), lambda i, j, k: (i, j))

    def body(*refs):
        a_ref, b_ref = refs[0], refs[1]
        add_ref = refs[2] if addend is not None else None
        o_ref, acc_ref = refs[-2], refs[-1]
        k = pl.program_id(2)
        part = lax.dot_general(a_ref[...], b_ref[...], (dims, ((), ())), preferred_element_type=F32)

        def finish(acc):
            if add_ref is not None:
                acc = acc + alpha * add_ref[...]
            o_ref[...] = acc.astype(out_dtype)

        if nk == 1:
            finish(part)
        else:
            @pl.when(k == 0)
            def _():
                acc_ref[...] = part

            @pl.when((k > 0) & (k < nk - 1))
            def _():
                acc_ref[...] += part

            @pl.when(k == nk - 1)
            def _():
                finish(acc_ref[...] + part)

    ins = [a, b] + ([addend] if addend is not None else []) + ([token] if token is not None else [])
    in_specs = [a_spec, b_spec] + ([o_spec] if addend is not None else []) + ([_token_spec()] if token is not None else [])
    return _pcall(
        body, name=name, grid=(m // tm, n_dim // tn, nk), in_specs=in_specs, out_specs=o_spec,
        out_shape=jax.ShapeDtypeStruct((m, n_dim), out_dtype),
        scratch_shapes=[pltpu.VMEM((tm, tn) if nk > 1 else (SUBLANES, LANES), F32)],
        compiler_params=_params(("parallel", "parallel", "arbitrary")),
    )(*ins)


def _tile(n, want, quantum=LANES):
    if n <= want:
        return n
    for cand in range(want - want % quantum, 0, -quantum):
        if n % cand == 0:
            return cand
    return n


def _row_tile(t, d):
    return _tile(t, max(2 * SUBLANES, (1 << 19) // d), 2 * SUBLANES)


def _ln_fwd(z, g, b, token):
    t, d = z.shape
    tr = _row_tile(t, d)

    def body(z_ref, g_ref, b_ref, _, y_ref, yb_ref):
        zz = z_ref[...]
        mu = jnp.mean(zz, axis=1, keepdims=True)
        zc = zz - mu
        var = jnp.mean(zc * zc, axis=1, keepdims=True)
        y = zc * lax.rsqrt(var + LN_EPS) * g_ref[...] + b_ref[...]
        y_ref[...] = y
        yb_ref[...] = y.astype(BF16)

    row = pl.BlockSpec((tr, d), lambda i: (i, 0))
    vec = pl.BlockSpec((1, d), lambda i: (0, 0))
    return _pcall(body, name="ln_fwd", grid=(t // tr,), in_specs=[row, vec, vec, _token_spec()], out_specs=[row, row],
                  out_shape=[jax.ShapeDtypeStruct((t, d), F32), jax.ShapeDtypeStruct((t, d), BF16)],
                  compiler_params=_params(("parallel",)))(z, g, b, token)


def _ln_bwd(dy, z, g, token):
    t, d = z.shape
    tr = _row_tile(t, d)

    def body(dy_ref, z_ref, g_ref, _, dz_ref, dzb_ref, dg_ref, db_ref):
        @pl.when(pl.program_id(0) == 0)
        def _():
            dg_ref[...] = jnp.zeros(dg_ref.shape, F32)
            db_ref[...] = jnp.zeros(db_ref.shape, F32)

        zz = z_ref[...]
        dyy = dy_ref[...]
        mu = jnp.mean(zz, axis=1, keepdims=True)
        zc = zz - mu
        rstd = lax.rsqrt(jnp.mean(zc * zc, axis=1, keepdims=True) + LN_EPS)
        xhat = zc * rstd
        dyg = dyy * g_ref[...]
        dz = rstd * (dyg - jnp.mean(dyg, axis=1, keepdims=True) - xhat * jnp.mean(dyg * xhat, axis=1, keepdims=True))
        dz_ref[...] = dz
        dzb_ref[...] = dz.astype(BF16)
        dg_ref[...] += jnp.sum(dyy * xhat, axis=0, keepdims=True)
        db_ref[...] += jnp.sum(dyy, axis=0, keepdims=True)

    row = pl.BlockSpec((tr, d), lambda i: (i, 0))
    vec = pl.BlockSpec((1, d), lambda i: (0, 0))
    return _pcall(body, name="ln_bwd", grid=(t // tr,), in_specs=[row, row, vec, _token_spec()], out_specs=[row, row, vec, vec],
                  out_shape=[jax.ShapeDtypeStruct((t, d), F32), jax.ShapeDtypeStruct((t, d), BF16),
                             jax.ShapeDtypeStruct((1, d), F32), jax.ShapeDtypeStruct((1, d), F32)],
                  compiler_params=_params(("arbitrary",)))(dy, z, g, token)


def _loss_head(y, target):
    t, d = y.shape
    tr = _row_tile(t, d)

    def body(y_ref, t_ref, dy_ref, loss_ref):
        @pl.when(pl.program_id(0) == 0)
        def _():
            loss_ref[...] = jnp.zeros(loss_ref.shape, F32)

        err = y_ref[...] - t_ref[...]
        dy_ref[...] = err * (1.0 / d)
        per_token = jnp.sum(err * err, axis=1, keepdims=True) * (1.0 / d)
        loss_ref[...] += 0.5 * jnp.sum(per_token, axis=0, keepdims=True)

    row = pl.BlockSpec((tr, d), lambda i: (i, 0))
    one = pl.BlockSpec((1, 1), lambda i: (0, 0))
    return _pcall(body, name="loss_head", grid=(t // tr,), in_specs=[row, row], out_specs=[row, one],
                  out_shape=[jax.ShapeDtypeStruct((t, d), F32), jax.ShapeDtypeStruct((1, 1), F32)],
                  compiler_params=_params(("arbitrary",)))(y, target)


def _adamw_scattered(w, m, v, layer, sums, far, prev, name):
    n_layers, r, c = w.shape
    tr = _row_tile(r, c)

    def body(*refs):
        w_ref, m_ref, v_ref, s_ref, f_ref = refs[:5]
        go_ref, d_ref, mo_ref, vo_ref = refs[-4:]
        gg = s_ref[...].astype(F32)
        for i in range(3):
            gg = gg + f_ref[i:i + 1].astype(F32)
        m_new = ADAM_B1 * m_ref[...] + (1.0 - ADAM_B1) * gg
        v_new = ADAM_B2 * v_ref[...] + (1.0 - ADAM_B2) * (gg * gg)
        m_hat = m_new / (1.0 - ADAM_B1 ** ADAM_STEP)
        v_hat = v_new / (1.0 - ADAM_B2 ** ADAM_STEP)
        go_ref[...] = gg
        d_ref[...] = -ADAM_LR * (m_hat / (jnp.sqrt(v_hat) + ADAM_EPS) + ADAM_WD * w_ref[...])
        mo_ref[...] = m_new
        vo_ref[...] = v_new

    own = pl.BlockSpec((1, tr, c), lambda i: (layer, i, 0))
    in_specs = [own, own, own, pl.BlockSpec((1, tr, c), lambda i: (2 * lax.axis_index("x") + lax.axis_index("y"), i, 0)),
                pl.BlockSpec((3, tr, c), lambda i: (0, i, 0))] + ([ANY] * 4 if prev is not None else [])
    return _pcall(body, name=name, grid=(r // tr,), in_specs=in_specs, out_specs=[own] * 4,
                  out_shape=[jax.ShapeDtypeStruct((n_layers, r, c), F32)] * 4,
                  input_output_aliases={5 + i: i for i in range(4)} if prev is not None else {},
                  compiler_params=_params(("parallel",)))(w, m, v, sums, far, *(prev if prev is not None else []))


def _pair_add(a, b, name):
    p, r, c = b.shape
    tr = _row_tile(r, c)

    def body(a_ref, b_ref, o_ref):
        o_ref[...] = (a_ref[...].astype(F32) + b_ref[...].astype(F32)).astype(BF16)

    spec = pl.BlockSpec((1, tr, c), lambda q, i: (q, i, 0))
    return _pcall(body, name=name, grid=(p, r // tr),
                  in_specs=[pl.BlockSpec((1, tr, c), lambda q, i: (2 * q + lax.axis_index("c"), i, 0)), spec], out_specs=spec,
                  out_shape=jax.ShapeDtypeStruct((p, r, c), BF16), compiler_params=_params(("parallel", "parallel")))(a, b)


def _adamw(w, g_parts, m, v, name):
    r, c = w.shape
    n_parts = g_parts.shape[0]
    tr = _row_tile(r, c) if r % SUBLANES == 0 else r

    def body(w_ref, g_ref, m_ref, v_ref, go_ref, d_ref, mo_ref, vo_ref):
        g = g_ref[0].astype(F32)
        for i in range(1, n_parts):
            g = g + g_ref[i].astype(F32)
        m_new = ADAM_B1 * m_ref[...] + (1.0 - ADAM_B1) * g
        v_new = ADAM_B2 * v_ref[...] + (1.0 - ADAM_B2) * (g * g)
        m_hat = m_new / (1.0 - ADAM_B1 ** ADAM_STEP)
        v_hat = v_new / (1.0 - ADAM_B2 ** ADAM_STEP)
        go_ref[...] = g
        d_ref[...] = -ADAM_LR * (m_hat / (jnp.sqrt(v_hat) + ADAM_EPS) + ADAM_WD * w_ref[...])
        mo_ref[...] = m_new
        vo_ref[...] = v_new

    spec = pl.BlockSpec((tr, c), lambda i: (i, 0))
    shape = jax.ShapeDtypeStruct((r, c), F32)
    return _pcall(body, name=name, grid=(r // tr,),
                  in_specs=[spec, pl.BlockSpec((n_parts, tr, c), lambda i: (0, i, 0)), spec, spec],
                  out_specs=[spec] * 4, out_shape=[shape] * 4, compiler_params=_params(("parallel",)))(w, g_parts, m, v)


def _me():
    return lax.axis_index("x"), lax.axis_index("y"), lax.axis_index("c")


def _dev(px, py, pc):
    return 4 * px + 2 * py + pc


def _remote(src, dst, send_sems, recv_sems, k, to):
    return pltpu.make_async_remote_copy(src_ref=src, dst_ref=dst, send_sem=send_sems.at[k], recv_sem=recv_sems.at[k],
                                        device_id=to, device_id_type=MESH)


def _all_gather(arrs, name):
    n = len(arrs)

    def body(*refs):
        ins, outs = refs[:n], refs[n:2 * n]
        send_sems, recv_sems, local_sems = refs[2 * n:]
        x, y, c = _me()
        me, sibling = (x, y, c), (x, y, 1 - c)
        chips = [(1 - x, y), (x, 1 - y), (1 - x, 1 - y)]
        pending = []
        for a in range(n):
            mine = pltpu.make_async_copy(ins[a], outs[a].at[_dev(*me)], local_sems.at[a])
            mine.start()
            pending.append(mine)
        sends = []
        for a in range(n):
            dst = outs[a].at[_dev(*me)]
            sends.append(_remote(ins[a], dst, send_sems, recv_sems, 7 * a, sibling))
            sends += [_remote(ins[a], dst, send_sems, recv_sems, 7 * a + 1 + j, (*chip, c)) for j, chip in enumerate(chips)]
        for cp in sends:
            cp.start()
        for a in range(n):
            for j, chip in enumerate(chips):
                blk = outs[a].at[_dev(*chip, c)]
                _remote(blk, blk, send_sems, recv_sems, 7 * a + 1 + j, me).wait_recv()
                fwd = _remote(blk, blk, send_sems, recv_sems, 7 * a + 4 + j, sibling)
                fwd.start()
                sends.append(fwd)
        for a in range(n):
            blk = outs[a].at[_dev(*sibling)]
            _remote(blk, blk, send_sems, recv_sems, 7 * a, me).wait_recv()
            for j, chip in enumerate(chips):
                blk = outs[a].at[_dev(*chip, 1 - c)]
                _remote(blk, blk, send_sems, recv_sems, 7 * a + 4 + j, me).wait_recv()
        for cp in sends:
            cp.wait_send()
        for cp in pending:
            cp.wait()

    return _pcall(
        body, name=name, in_specs=[ANY] * n, out_specs=[ANY] * n,
        out_shape=[jax.ShapeDtypeStruct((N_DEV,) + a.shape, a.dtype) for a in arrs],
        scratch_shapes=[pltpu.SemaphoreType.DMA((7 * n,)), pltpu.SemaphoreType.DMA((7 * n,)), pltpu.SemaphoreType.DMA((n,))],
    )(*arrs)


def _relations(x, y):
    return [(x, y), (1 - x, y), (x, 1 - y), (1 - x, 1 - y)]


HBM_SPEC = pl.BlockSpec(memory_space=pltpu.HBM)
SEM_SPEC = pl.BlockSpec(memory_space=pltpu.SEMAPHORE)
DATAFLOW = pltpu.SideEffectType.DATAFLOW_SIDE_EFFECTING


def _exchange_start(name, bufs, plan, n_remote, n_local, dep):
    nb = len(bufs)
    sem_shapes = [pltpu.SemaphoreType.DMA((n_remote,)), pltpu.SemaphoreType.DMA((n_remote,))]
    if n_local:
        sem_shapes.append(pltpu.SemaphoreType.DMA((n_local,)))
    ns = len(sem_shapes)

    def body(*refs):
        ins, sems, token = refs[:nb], refs[nb + 1:nb + 1 + ns], refs[-1]
        starts, _, local = plan(ins, sems[0], sems[1], sems[2] if n_local else None)
        for cp in local + starts:
            cp.start()
        token[...] = jnp.zeros(token.shape, F32)

    outs = _pcall(
        body, name=name, in_specs=[HBM_SPEC] * nb + [ANY],
        out_specs=[SEM_SPEC] * ns + [HBM_SPEC] * nb + [pl.BlockSpec(memory_space=pltpu.VMEM)],
        out_shape=sem_shapes + [pltpu.HBM(b.shape, b.dtype) for b in bufs] + [jax.ShapeDtypeStruct((SUBLANES, LANES), F32)],
        input_output_aliases={i: ns + i for i in range(nb)}, compiler_params=pltpu.CompilerParams(has_side_effects=DATAFLOW),
    )(*[pltpu.with_memory_space_constraint(b, pltpu.HBM) for b in bufs], dep)
    return dict(sems=outs[:ns], thru=outs[ns:ns + nb], plan=plan, n_local=n_local), outs[-1]


def _exchange_wait(name, handle, *after):
    thru, sems, plan, n_local = handle["thru"], handle["sems"], handle["plan"], handle["n_local"]
    nb, ns = len(thru), len(sems)

    def body(*refs):
        ins, sem_refs = refs[:nb], refs[nb:nb + ns]
        starts, arrivals, local = plan(ins, sem_refs[0], sem_refs[1], sem_refs[2] if n_local else None)
        for cp in starts:
            cp.wait_send()
        for cp in arrivals:
            cp.wait_recv()
        for cp in local:
            cp.wait()

    return _pcall(
        body, name=name, in_specs=[HBM_SPEC] * nb + [SEM_SPEC] * ns + [ANY] * len(after), out_specs=[HBM_SPEC] * nb,
        out_shape=[pltpu.HBM(b.shape, b.dtype) for b in thru], input_output_aliases={i: i for i in range(nb)},
        compiler_params=pltpu.CompilerParams(has_side_effects=DATAFLOW),
    )(*thru, *sems, *after)


def _landing(shape, dtype):
    return lax.empty(shape, dtype)


def _plan_gather_ici(n):
    def plan(refs, send_sems, recv_sems, local_sems):
        x, y, c = _me()
        me, sibling = (x, y, c), (x, y, 1 - c)
        chips = _relations(x, y)[1:]
        starts, arrivals, local = [], [], []
        for a in range(n):
            shard, land = refs[a], refs[n + a]
            own = land.at[_dev(*me)]
            local.append(pltpu.make_async_copy(shard, own, local_sems.at[a]))
            starts.append(_remote(shard, own, send_sems, recv_sems, 4 * a, sibling))
            blk = land.at[_dev(*sibling)]
            arrivals.append(_remote(blk, blk, send_sems, recv_sems, 4 * a, me))
            for j, chip in enumerate(chips):
                starts.append(_remote(shard, own, send_sems, recv_sems, 4 * a + 1 + j, (*chip, c)))
                blk = land.at[_dev(*chip, c)]
                arrivals.append(_remote(blk, blk, send_sems, recv_sems, 4 * a + 1 + j, me))
        return starts, arrivals, local
    return plan


def _plan_gather_d2d(n):
    def plan(refs, send_sems, recv_sems, local_sems):
        x, y, c = _me()
        me, sibling = (x, y, c), (x, y, 1 - c)
        starts, arrivals = [], []
        for a in range(n):
            for j, chip in enumerate(_relations(x, y)[1:]):
                blk = refs[a].at[_dev(*chip, c)]
                starts.append(_remote(blk, blk, send_sems, recv_sems, 3 * a + j, sibling))
                blk = refs[a].at[_dev(*chip, 1 - c)]
                arrivals.append(_remote(blk, blk, send_sems, recv_sems, 3 * a + j, me))
        return starts, arrivals, []
    return plan


def _plan_scatter_d2d(n):
    def plan(refs, send_sems, recv_sems, local_sems):
        x, y, c = _me()
        me, sibling = (x, y, c), (x, y, 1 - c)
        starts, arrivals = [], []
        for a in range(n):
            for k in range(4):
                starts.append(_remote(refs[a].at[2 * k + 1 - c], refs[n + a].at[k], send_sems, recv_sems, 4 * a + k, sibling))
                blk = refs[n + a].at[k]
                arrivals.append(_remote(blk, blk, send_sems, recv_sems, 4 * a + k, me))
        return starts, arrivals, []
    return plan


def _plan_scatter_ici(n):
    def plan(refs, send_sems, recv_sems, local_sems):
        x, y, c = _me()
        me = (x, y, c)
        starts, arrivals = [], []
        for a in range(n):
            for j, (cx, cy) in enumerate(_relations(x, y)[1:]):
                starts.append(_remote(refs[a].at[2 * cx + cy], refs[n + a].at[j], send_sems, recv_sems, 3 * a + j, (cx, cy, c)))
                blk = refs[n + a].at[j]
                arrivals.append(_remote(blk, blk, send_sems, recv_sems, 3 * a + j, me))
        return starts, arrivals, []
    return plan


SMALL = ("conv_a_w", "conv_c_w", "conv_c_b", "gate_r_w", "gate_r_b", "gate_i_w", "gate_i_b", "rg_lambda",
         "norm_a", "norm_b", "norm_c", "sinks", "ln_g", "ln_b")
PACK_COLS = 1024


def _pack(arrs):
    flat = jnp.concatenate([a.reshape(-1) for a in arrs])
    pad = (-flat.shape[0]) % (SUBLANES * PACK_COLS)
    return jnp.pad(flat, (0, pad)).reshape(-1, PACK_COLS)


def _unpack(packed, shapes):
    flat = packed.reshape(-1)
    out, off = [], 0
    for s in shapes:
        size = 1
        for dim in s:
            size *= dim
        out.append(flat[off:off + size].reshape(s))
        off += size
    return out


def kernel(x, w_in, conv_a_w, sinks, conv_c_w, conv_c_b, gate_r_w, gate_r_b, gate_i_w, gate_i_b, rg_lambda, norm_a, norm_b, norm_c, w_out, ln_g, ln_b, loss_target, m_w_in, m_conv_a_w, m_sinks, m_conv_c_w, m_conv_c_b, m_gate_r_w, m_gate_r_b, m_gate_i_w, m_gate_i_b, m_rg_lambda, m_norm_a, m_norm_b, m_norm_c, m_w_out, m_ln_g, m_ln_b, v_w_in, v_conv_a_w, v_sinks, v_conv_c_w, v_conv_c_b, v_gate_r_w, v_gate_r_b, v_gate_i_w, v_gate_i_b, v_rg_lambda, v_norm_a, v_norm_b, v_norm_c, v_w_out, v_ln_g, v_ln_b):
    b_loc, seq, d = x.shape
    depth = w_in.shape[0]
    dm = _Dims(d, gate_r_w.shape[1])
    t = b_loc * seq
    nblk = seq // ATT_BLOCK
    alpha = (2.0 * depth) ** 0.25
    ch = dm.wa // N_DEV
    dev = _dev(*_me())

    wt_shard = jnp.swapaxes(w_in, 1, 2).astype(BF16)
    wo_shard = w_out.astype(BF16)
    conv_shard = jnp.concatenate([conv_a_w.reshape(depth * CONV_A, ch), conv_c_w.reshape(depth * CONV_C, ch)], axis=0)
    conv_shard = jnp.pad(conv_shard, ((0, (-conv_shard.shape[0]) % SUBLANES), (0, 0)))
    wt, wo, conv_all = _all_gather([wt_shard[0], wo_shard[0], conv_shard], "ag_weights")
    conv_all = jnp.swapaxes(conv_all, 0, 1).reshape(conv_all.shape[1], dm.wa)
    conv_a_full = conv_all[:depth * CONV_A].reshape(depth, CONV_A, dm.wa)
    conv_c_full = conv_all[depth * CONV_A:depth * (CONV_A + CONV_C)].reshape(depth, CONV_C, dm.wc)
    sinks_wide = jnp.broadcast_to(sinks[:, :, None], (depth, dm.nq, 2 * ATT_BLOCK))

    def layer_params(l):
        return (conv_a_full[l], conv_c_full[l], conv_c_b[l][None], gate_r_w[l], gate_r_b[l][None], gate_i_w[l],
                gate_i_b[l][None], rg_lambda[l][None], norm_a[l][None], norm_b[l][None], norm_c[l][None], sinks_wide[l])

    tm = _tile(t, 1024)
    xs = x.reshape(t, d)
    xb = xs.astype(BF16)
    saved = []
    bias = _attention_bias()
    rows_t, rows_o = dm.in_w // N_DEV, d // N_DEV
    for l in range(depth):
        token = _no_token()
        if l + 1 < depth:
            lands = [_landing((N_DEV, rows_t, d), BF16), _landing((N_DEV, rows_o, d), BF16)]
            ici, token = _exchange_start("ag_ici_start_%d" % l, [wt_shard[l + 1], wo_shard[l + 1]] + lands, _plan_gather_ici(2), 8, 2, wt)
        h = _matmul(xb, wt.reshape(dm.in_w, d), dims=NT, tm=tm, tn=_tile(dm.in_w, 512), tk=d, out_dtype=F32, name="mm_in",
                    token=token)
        mix, sst = _mixer_fwd(dm, h, bias, layer_params(l), b_loc, nblk)
        z = _matmul(mix, wo.reshape(d, d), dims=NN, tm=tm, tn=_tile(d, 512), tk=d, out_dtype=F32, name="mm_out", addend=xs,
                    alpha=alpha)
        saved.append((xb, h, sst, mix, z, wt, wo))
        token = _no_token()
        if l + 1 < depth:
            lands = _exchange_wait("ag_ici_wait_%d" % l, ici, z)[2:]
            d2d, token = _exchange_start("ag_d2d_start_%d" % l, lands, _plan_gather_d2d(2), 6, 0, z)
        xs, xb = _ln_fwd(z, ln_g[l][None], ln_b[l][None], token)
        if l + 1 < depth:
            wt, wo = _exchange_wait("ag_d2d_wait_%d" % l, d2d, xb)

    dy, loss_part = _loss_head(xs, loss_target.reshape(t, d))
    loss = lax.psum(loss_part[0, 0], ("x", "y", "c"))

    scattered, small = [None] * depth, [None] * depth
    ici, token_ici = None, _no_token()

    def finish_scatter(l, ici, *after):
        scattered[l] = _exchange_wait("rs_ici_wait_%d" % l, ici, *after)

    def scatter_add_start(l, d2d, *after):
        done = _exchange_wait("rs_d2d_wait_%d" % l, d2d, *after)
        sums = [_pair_add(p, g, "rs_add_%d" % i) for i, (p, g) in enumerate(zip(done[:2], done[2:]))]
        lands = [_landing((3, rows_t, d), BF16), _landing((3, rows_o, d), BF16)]
        return _exchange_start("rs_ici_start_%d" % l, sums + lands, _plan_scatter_ici(2), 6, 0, after[0])

    for l in reversed(range(depth)):
        xb, h, sst, mix, z, wt, wo = saved[l]
        dz, dzb, d_lng, d_lnb = _ln_bwd(dy, z, ln_g[l][None], token_ici)
        dmix = _matmul(dzb, wo.reshape(d, d), dims=NT, tm=tm, tn=_tile(d, 512), tk=d, out_dtype=F32, name="mm_dmix")
        dwo = _matmul(mix, dzb, dims=TN, tm=_tile(d, 1024), tn=_tile(d, 512), tk=t, out_dtype=BF16, name="mm_dwo")
        dh, sm = _mixer_bwd(dm, h, sst, dmix, bias, layer_params(l), b_loc, nblk)
        (d_caw, d_ccw, d_ccb, d_grw, d_grb, d_giw, d_gib, d_lam, d_na, d_nb, d_nc, d_snk) = sm
        small[l] = dict(conv_a_w=d_caw, conv_c_w=d_ccw, conv_c_b=d_ccb[0], gate_r_w=d_grw, gate_r_b=d_grb[0], gate_i_w=d_giw,
                        gate_i_b=d_gib[0], rg_lambda=d_lam[0], norm_a=d_na[0], norm_b=d_nb[0], norm_c=d_nc[0],
                        sinks=d_snk[:, 0], ln_g=d_lng[0], ln_b=d_lnb[0])
        token = _no_token()
        if l == 0:
            g_local = _pack([jnp.stack([small[i][n] for i in range(depth)]) for n in SMALL])
            small_ici, token = _exchange_start("ag_small_ici_start", [g_local, _landing((N_DEV,) + g_local.shape, F32)],
                                               _plan_gather_ici(1), 4, 1, dh)
        dwt = _matmul(dh, xb, dims=TN, tm=_tile(dm.in_w, 1536), tn=_tile(d, 512), tk=t, out_dtype=BF16, name="mm_dwt",
                      token=token)
        if ici is not None:
            finish_scatter(l + 1, ici, dh)
        parts = [dwt.reshape(N_DEV, rows_t, d), dwo.reshape(N_DEV, rows_o, d)]
        lands = [_landing((4, rows_t, d), BF16), _landing((4, rows_o, d), BF16)]
        d2d, token = _exchange_start("rs_d2d_start_%d" % l, parts + lands, _plan_scatter_d2d(2), 8, 0, dh)
        if l > 0:
            dy = _matmul(dh, wt.reshape(dm.in_w, d), dims=NN, tm=tm, tn=_tile(d, 1024), tk=_tile(dm.in_w, 1792), out_dtype=F32,
                         name="mm_dx", addend=dz, alpha=alpha, token=token)
            ici, token_ici = scatter_add_start(l, d2d, dy)
        else:
            small_land = _exchange_wait("ag_small_ici_wait", small_ici, token)[1:]
            small_d2d, token = _exchange_start("ag_small_d2d_start", small_land, _plan_gather_d2d(1), 3, 0, token)
            ici, token_ici = scatter_add_start(l, d2d, token)
            dy = _matmul(dh, wt.reshape(dm.in_w, d), dims=NN, tm=tm, tn=_tile(d, 1024), tk=_tile(dm.in_w, 1792), out_dtype=F32,
                         name="mm_dx", addend=dz, alpha=alpha, token=token_ici)
            g_all = _exchange_wait("ag_small_d2d_wait", small_d2d, dy)[0]
    grad_x = dy.reshape(b_loc, seq, d)

    w_t, m_t, v_t = [jnp.swapaxes(a, 1, 2) for a in (w_in, m_w_in, v_w_in)]
    res_in, res_out = None, None
    for l in reversed(range(depth)):
        if l == 0:
            finish_scatter(0, ici, dy, *([res_in[0], res_out[0]] if depth > 1 else []))
        sums_t, sums_o, far_t, far_o = scattered[l]
        res_in = _adamw_scattered(w_t, m_t, v_t, l, sums_t, far_t, res_in, "adamw_in_%d" % l)
        res_out = _adamw_scattered(w_out, m_w_out, v_w_out, l, sums_o, far_o, res_out, "adamw_out_%d" % l)
    gw_in, dl_in, nm_in, nv_in = [jnp.swapaxes(a, 1, 2) for a in res_in]
    gw_out, dl_out, nm_out, nv_out = res_out

    given = dict(conv_a_w=(conv_a_w, m_conv_a_w, v_conv_a_w), conv_c_w=(conv_c_w, m_conv_c_w, v_conv_c_w),
                 conv_c_b=(conv_c_b, m_conv_c_b, v_conv_c_b), gate_r_w=(gate_r_w, m_gate_r_w, v_gate_r_w),
                 gate_r_b=(gate_r_b, m_gate_r_b, v_gate_r_b), gate_i_w=(gate_i_w, m_gate_i_w, v_gate_i_w),
                 gate_i_b=(gate_i_b, m_gate_i_b, v_gate_i_b), rg_lambda=(rg_lambda, m_rg_lambda, v_rg_lambda),
                 norm_a=(norm_a, m_norm_a, v_norm_a), norm_b=(norm_b, m_norm_b, v_norm_b), norm_c=(norm_c, m_norm_c, v_norm_c),
                 sinks=(sinks, m_sinks, v_sinks), ln_g=(ln_g, m_ln_g, v_ln_g), ln_b=(ln_b, m_ln_b, v_ln_b))
    full_shapes = [jnp.stack([small[l][n] for l in range(depth)]).shape for n in SMALL]

    def mine_of(n, a):
        if n in ("conv_a_w", "conv_c_w"):
            return lax.dynamic_update_slice(jnp.zeros(a.shape[:2] + (dm.wa,), F32), a, (0, 0, dev * ch))
        return a

    packs = [_pack([mine_of(n, given[n][i]) for n in SMALL]) for i in range(3)]
    outs = _adamw(packs[0], g_all, packs[1], packs[2], "adamw_small")
    res = {}
    for kind, packed in zip(("grad", "delta", "new_m", "new_v"), outs):
        for n, a in zip(SMALL, _unpack(packed, full_shapes)):
            if n in ("conv_a_w", "conv_c_w"):
                a = lax.dynamic_slice(a, (0, 0, dev * ch), a.shape[:2] + (ch,))
            res[kind, n] = a
    res.update({("grad", "w_in"): gw_in, ("delta", "w_in"): dl_in, ("new_m", "w_in"): nm_in, ("new_v", "w_in"): nv_in,
                ("grad", "w_out"): gw_out, ("delta", "w_out"): dl_out, ("new_m", "w_out"): nm_out, ("new_v", "w_out"): nv_out})
    order = ("w_in", "conv_a_w", "sinks", "conv_c_w", "conv_c_b", "gate_r_w", "gate_r_b", "gate_i_w", "gate_i_b", "rg_lambda",
             "norm_a", "norm_b", "norm_c", "w_out", "ln_g", "ln_b")
    return (loss, grad_x, *[res[kind, n] for kind in ("grad", "delta", "new_m", "new_v") for n in order])
```

```python
import functools

import jax
import jax.numpy as jnp
from jax import lax
from jax.experimental import pallas as pl
from jax.experimental.pallas import tpu as pltpu

F32 = jnp.float32
BF16 = jnp.bfloat16
MESH = pl.DeviceIdType.MESH
ANY = pl.BlockSpec(memory_space=pl.ANY)

N_DEV = 8
LANES = 128
SUBLANES = 8
HEAD_DIM = 64
KV_GROUP = 8
ATT_BLOCK = 128
ATT_STACK = 8
CONV_A = 3
CONV_C = 4
RG_C = 8.0
LN_EPS = 1e-5
RMS_EPS = 1e-6
NEG_INF = -1e30
ADAM_LR = 0.001
ADAM_B1 = 0.9
ADAM_B2 = 0.999
ADAM_EPS = 1e-08
ADAM_WD = 0.01
ADAM_STEP = 10
VMEM_LIMIT = 56 * 1024 * 1024

NN = ((1,), (0,))
NT = ((1,), (1,))
TN = ((0,), (0,))


def _pcall(body, **kw):
    return pl.pallas_call(body, **kw)


def _roll(x, shift, axis):
    return pltpu.roll(x, shift, axis)


def _params(sem=None, vmem=VMEM_LIMIT):
    return pltpu.CompilerParams(dimension_semantics=sem, vmem_limit_bytes=vmem)


def _dot(a, b, dims):
    return lax.dot_general(a.astype(BF16), b.astype(BF16), (dims, ((), ())), preferred_element_type=F32)


@jax.custom_vjp
def _mm(a, b):
    return _dot(a, b, NN)


def _mm_fwd(a, b):
    return _dot(a, b, NN), (a.astype(BF16), b.astype(BF16))


def _mm_bwd(res, g):
    a, b = res
    return _dot(g, b, NT), _dot(a, g, TN)


_mm.defvjp(_mm_fwd, _mm_bwd)


@jax.custom_vjp
def _mm_nt(a, b):
    return _dot(a, b, NT)


def _mm_nt_fwd(a, b):
    return _dot(a, b, NT), (a.astype(BF16), b.astype(BF16))


def _mm_nt_bwd(res, g):
    a, b = res
    return _dot(g, b, NN), _dot(g, a, TN)


_mm_nt.defvjp(_mm_nt_fwd, _mm_nt_bwd)


def _rows(shape):
    return lax.broadcasted_iota(jnp.int32, shape, 0)


@functools.partial(jax.custom_vjp, nondiff_argnums=(2,))
def _shift_halo(u, prev, k):
    r, c = u.shape
    fill = jnp.concatenate([_roll(prev, k, 0), jnp.zeros((r - SUBLANES, c), u.dtype)], axis=0)
    return jnp.where(_rows((r, c)) < k, fill, _roll(u, k, 0))


def _shift_halo_fwd(u, prev, k):
    return _shift_halo(u, prev, k), None


def _shift_halo_bwd(k, _, g):
    r, c = g.shape
    du = jnp.where(_rows((r, c)) < r - k, _roll(g, r - k, 0), 0.0)
    dprev = jnp.where(_rows((SUBLANES, c)) >= SUBLANES - k, _roll(g[0:SUBLANES], SUBLANES - k, 0), 0.0)
    return du, dprev


_shift_halo.defvjp(_shift_halo_fwd, _shift_halo_bwd)


@functools.partial(jax.custom_vjp, nondiff_argnums=(1, 2))
def _shift_fill(u, k, fill):
    return jnp.where(_rows(u.shape) < k, fill, _roll(u, k, 0))


def _shift_fill_fwd(u, k, fill):
    return _shift_fill(u, k, fill), None


def _shift_fill_bwd(k, fill, _, g):
    r = g.shape[0]
    return (jnp.where(_rows(g.shape) < r - k, _roll(g, r - k, 0), 0.0),)


_shift_fill.defvjp(_shift_fill_fwd, _shift_fill_bwd)


@jax.custom_vjp
def _swap_halves(x):
    return _roll(x, HEAD_DIM, 1)


_swap_halves.defvjp(lambda x: (_roll(x, HEAD_DIM, 1), None), lambda _, g: (_roll(g, HEAD_DIM, 1),))


@functools.partial(jax.custom_vjp, nondiff_argnums=(1,))
def _split_rows(x, n):
    r = x.shape[0] // n
    return tuple(x[i * r:(i + 1) * r] for i in range(n))


def _split_rows_fwd(x, n):
    return _split_rows(x, n), None


def _split_rows_bwd(n, _, gs):
    return (jnp.concatenate(list(gs), axis=0),)


_split_rows.defvjp(_split_rows_fwd, _split_rows_bwd)


def _logistic(x):
    return 1.0 / (1.0 + jnp.exp(-x))


@jax.custom_vjp
def _sigmoid(x):
    return _logistic(x)


def _sigmoid_fwd(x):
    s = _logistic(x)
    return s, s


_sigmoid.defvjp(_sigmoid_fwd, lambda s, g: (g * s * (1.0 - s),))


@jax.custom_vjp
def _silu(x):
    return x * _logistic(x)


def _silu_fwd(x):
    s = _logistic(x)
    return x * s, (x, s)


_silu.defvjp(_silu_fwd, lambda res, g: (g * res[1] * (1.0 + res[0] * (1.0 - res[1])),))


def _log_sigmoid(x):
    return -(jnp.maximum(-x, 0.0) + jnp.log1p(jnp.exp(-jnp.abs(x))))


@jax.custom_vjp
def _neg_expm1(x):
    series = x * (1 + x * (1 / 2) * (1 + x * (1 / 3) * (1 + x * (1 / 4) * (1 + x * (1 / 5) * (1 + x * (1 / 6) * (1 + x * (1 / 7)))))))
    return -jnp.where(jnp.abs(x) < 0.25, series, jnp.exp(x) - 1.0)


_neg_expm1.defvjp(lambda x: (_neg_expm1(x), x), lambda x, g: (-g * jnp.exp(x),))


def _shift_up(x, k, fill):
    r = x.shape[0]
    return jnp.where(_rows(x.shape) < r - k, _roll(x, r - k, 0), fill)


@jax.custom_vjp
def _scan_block(a, u, s_prev):
    acc_a, acc_b = a, u
    d = 1
    while d < a.shape[0]:
        acc_b = acc_a * _shift_fill(acc_b, d, 0.0) + acc_b
        acc_a = acc_a * _shift_fill(acc_a, d, 1.0)
        d *= 2
    return acc_a * s_prev + acc_b


def _scan_block_fwd(a, u, s_prev):
    h = _scan_block(a, u, s_prev)
    return h, (a, h, s_prev)


def _scan_block_bwd(res, dh):
    a, h, s_prev = res
    acc_a, acc_g = _shift_up(a, 1, 0.0), dh
    d = 1
    while d < a.shape[0]:
        acc_g = acc_a * _shift_up(acc_g, d, 0.0) + acc_g
        acc_a = acc_a * _shift_up(acc_a, d, 1.0)
        d *= 2
    h_prev = jnp.where(_rows(h.shape) < 1, s_prev, _roll(h, 1, 0))
    first = jnp.sum(jnp.where(_rows(h.shape) < 1, a * acc_g, 0.0), axis=0, keepdims=True)
    return acc_g * h_prev, acc_g, first


_scan_block.defvjp(_scan_block_fwd, _scan_block_bwd)


def _last_row(h):
    return jnp.sum(jnp.where(_rows(h.shape) == h.shape[0] - 1, h, 0.0), axis=0, keepdims=True)


def _branch_a(ab, ac, ax, ag, acp, axp, w0, w1, w2, na):
    u = ac * ax
    up = acp * axp
    ya = ab * (w2 * u + w1 * _shift_halo(u, up, 1) + w0 * _shift_halo(u, up, 2))
    ms = jnp.sum(ya * ya, axis=1, keepdims=True) * (1.0 / ya.shape[1])
    return ya * lax.rsqrt(ms + RMS_EPS) * na * _silu(ag)


def _branch_c(cx, cg, cxp, sp, wc, bc, wr, br, wi, bi, lam, nc):
    hs, lasts = [], []
    for j in range(len(cx)):
        xc = (wc[3][j] * cx[j] + wc[2][j] * _shift_halo(cx[j], cxp[j], 1) + wc[1][j] * _shift_halo(cx[j], cxp[j], 2)
              + wc[0][j] * _shift_halo(cx[j], cxp[j], 3) + bc[j])
        r = _sigmoid(_mm(xc, wr[j]) + br[j])
        i = _sigmoid(_mm(xc, wi[j]) + bi[j])
        log_a = RG_C * r * _log_sigmoid(lam[j])
        a = jnp.exp(log_a)
        u = jnp.sqrt(_neg_expm1(2.0 * log_a)) * (i * xc)
        h = _scan_block(a, u, sp[j])
        hs.append(h)
        lasts.append(_last_row(h))
    width = LANES * len(cx)
    ms = sum(jnp.sum(h * h, axis=1, keepdims=True) for h in hs) * (1.0 / width)
    inv = lax.rsqrt(ms + RMS_EPS)
    return [hs[j] * inv * nc[j] * _silu(cg[j]) for j in range(len(cx))], lasts


def _attention_bias():
    qi = (jnp.arange(ATT_STACK * ATT_BLOCK) % ATT_BLOCK)[:, None]
    kj = jnp.arange(2 * ATT_BLOCK)[None, :]
    dist = qi + ATT_BLOCK - kj
    band = (dist >= 0) & (dist < ATT_BLOCK)
    return jnp.where(jnp.stack([band & (kj >= ATT_BLOCK), band]), 0.0, NEG_INF).astype(F32)


def _branch_b(q, k, v, kp, vp, bg, snk, nb, bias):
    rows = ATT_BLOCK
    n_kv = 2 * len(k)
    lane0 = lax.broadcasted_iota(jnp.int32, (ATT_STACK * rows, LANES), 1) == 0
    upper = lax.broadcasted_iota(jnp.int32, (2 * rows, LANES), 1) >= HEAD_DIM
    heads = [None] * (n_kv * KV_GROUP)
    for g in range(n_kv):
        half = g % 2
        keep = upper if half else jnp.logical_not(upper)
        kc = jnp.where(keep, jnp.concatenate([kp[g // 2], k[g // 2]], axis=0), 0.0)
        vc = jnp.where(keep, jnp.concatenate([vp[g // 2], v[g // 2]], axis=0), 0.0)
        for first in range(g * KV_GROUP, (g + 1) * KV_GROUP, ATT_STACK):
            hs = range(first, first + ATT_STACK)
            qg = jnp.concatenate([q[h // 2] if h % 2 == half else _swap_halves(q[h // 2]) for h in hs], axis=0)
            s = _mm_nt(qg * (HEAD_DIM ** -0.5), kc) + bias
            sink = jnp.concatenate([jnp.broadcast_to(snk[h], (rows, LANES)) for h in hs], axis=0)
            sink = jnp.sum(jnp.where(lane0, sink, 0.0), axis=1, keepdims=True)
            m = lax.stop_gradient(jnp.maximum(jnp.max(s, axis=1, keepdims=True), sink))
            p = jnp.exp(s - m)
            inv = 1.0 / (jnp.sum(p, axis=1, keepdims=True) + jnp.exp(sink - m))
            o = _split_rows(_mm(p * inv, vc), ATT_STACK)
            for i, h in enumerate(hs):
                heads[h] = o[i] if h % 2 == half else _swap_halves(o[i])
    yb = [heads[2 * j] + heads[2 * j + 1] for j in range(len(q))]
    width = LANES * len(q)
    ms = sum(jnp.sum(y * y, axis=1, keepdims=True) for y in yb) * (1.0 / width)
    inv = lax.rsqrt(ms + RMS_EPS)
    return [yb[j] * inv * nb[j] * _silu(bg[j]) for j in range(len(q))]


class _Dims:
    def __init__(self, d_model, n_rg_heads):
        self.d = d_model
        self.wa = d_model // 4
        self.wb = d_model // 2
        self.wc = d_model // 4
        self.kvw = self.wb // KV_GROUP
        self.nq = self.wb // HEAD_DIM
        self.in_w = 4 * self.wa + 2 * self.wb + 2 * self.kvw + 2 * self.wc
        self.o_q = 4 * self.wa
        self.o_k = self.o_q + self.wb
        self.o_v = self.o_k + self.kvw
        self.o_bg = self.o_v + self.kvw
        self.o_cx = self.o_bg + self.wb
        self.o_cg = self.o_cx + self.wc
        self.nh = n_rg_heads
        assert self.wc // n_rg_heads == LANES and self.kvw % LANES == 0
        assert self.o_k % self.kvw == 0 and self.o_cx % (self.wc // 2) == 0


def _chunks(ref, rows, off, width):
    return [ref[rows, off + LANES * j: off + LANES * (j + 1)] for j in range(width // LANES)]


def _read_params(dm, caw, ccw, ccb, grw, grb, giw, gib, lam, na, nb, nc, snk):
    row = slice(0, 1)
    return dict(
        wa=[caw[k:k + 1, :] for k in range(CONV_A)], na=na[...],
        wc=[_chunks(ccw, slice(k, k + 1), 0, dm.wc) for k in range(CONV_C)], bc=_chunks(ccb, row, 0, dm.wc),
        wr=[grw[j] for j in range(dm.nh)], br=_chunks(grb, row, 0, dm.wc),
        wi=[giw[j] for j in range(dm.nh)], bi=_chunks(gib, row, 0, dm.wc),
        lam=_chunks(lam, row, 0, dm.wc), nc=_chunks(nc, row, 0, dm.wc),
        nb=_chunks(nb, row, 0, dm.wb), snk=[snk[h:h + 1, :] for h in range(dm.nq)])


def _param_specs(dm):
    shapes = [(CONV_A, dm.wa), (CONV_C, dm.wc), (1, dm.wc), (dm.nh, LANES, LANES), (1, dm.wc), (dm.nh, LANES, LANES),
              (1, dm.wc), (1, dm.wc), (1, dm.wa), (1, dm.wb), (1, dm.wc), (dm.nq, LANES)]
    specs = [pl.BlockSpec(s, (lambda b, n, _r=len(s): (0,) * _r)) for s in shapes]
    return shapes, specs


def _bias_spec():
    return pl.BlockSpec((2, ATT_STACK * ATT_BLOCK, 2 * ATT_BLOCK), lambda *_: (0, 0, 0))


def _mixer_fwd(dm, h, bias, prm, b_loc, nblk):
    t = h.shape[0]
    r = ATT_BLOCK
    tail = slice(r - SUBLANES, r)

    def body(h_ref, bias_ref, caw, ccw, ccb, grw, grb, giw, gib, lam, na, nb, nc, snk, mix_ref, sst_ref, kp, vp, acp, axp, cxp, sp):
        n = pl.program_id(1)

        @pl.when(n == 0)
        def _():
            for ref in (kp, vp, acp, axp, cxp, sp):
                ref[...] = jnp.zeros(ref.shape, ref.dtype)

        p = _read_params(dm, caw, ccw, ccb, grw, grb, giw, gib, lam, na, nb, nc, snk)
        full = slice(None)
        mix_a = _branch_a(h_ref[:, 0:dm.wa], h_ref[:, dm.wa:2 * dm.wa], h_ref[:, 2 * dm.wa:3 * dm.wa],
                          h_ref[:, 3 * dm.wa:4 * dm.wa], acp[...], axp[...], p["wa"][0], p["wa"][1], p["wa"][2], p["na"])
        mix_ref[:, 0:dm.wa] = mix_a.astype(BF16)
        bias = bias_ref[jnp.where(n == 0, 0, 1)]
        mix_b = _branch_b(_chunks(h_ref, full, dm.o_q, dm.wb), _chunks(h_ref, full, dm.o_k, dm.kvw),
                          _chunks(h_ref, full, dm.o_v, dm.kvw), _chunks(kp, full, 0, dm.kvw), _chunks(vp, full, 0, dm.kvw),
                          _chunks(h_ref, full, dm.o_bg, dm.wb), p["snk"], p["nb"], bias)
        for j, mb in enumerate(mix_b):
            mix_ref[:, dm.wa + LANES * j: dm.wa + LANES * (j + 1)] = mb.astype(BF16)
        sst_ref[0] = sp[...]
        mix_c, lasts = _branch_c(_chunks(h_ref, full, dm.o_cx, dm.wc), _chunks(h_ref, full, dm.o_cg, dm.wc),
                                 _chunks(cxp, full, 0, dm.wc), _chunks(sp, slice(0, 1), 0, dm.wc), p["wc"], p["bc"],
                                 p["wr"], p["br"], p["wi"], p["bi"], p["lam"], p["nc"])
        o_c = dm.wa + dm.wb
        for j, mc in enumerate(mix_c):
            mix_ref[:, o_c + LANES * j: o_c + LANES * (j + 1)] = mc.astype(BF16)
            sp[:, LANES * j: LANES * (j + 1)] = jnp.broadcast_to(lasts[j], (SUBLANES, LANES))
        kp[...] = h_ref[:, dm.o_k:dm.o_k + dm.kvw]
        vp[...] = h_ref[:, dm.o_v:dm.o_v + dm.kvw]
        acp[...] = h_ref[tail, dm.wa:2 * dm.wa]
        axp[...] = h_ref[tail, 2 * dm.wa:3 * dm.wa]
        cxp[...] = h_ref[tail, dm.o_cx:dm.o_cx + dm.wc]

    _, pspecs = _param_specs(dm)
    return _pcall(
        body, name="mixer_fwd", grid=(b_loc, nblk),
        in_specs=[pl.BlockSpec((r, dm.in_w), lambda b, n: (b * nblk + n, 0)), _bias_spec()] + pspecs,
        out_specs=[pl.BlockSpec((r, dm.d), lambda b, n: (b * nblk + n, 0)),
                   pl.BlockSpec((1, SUBLANES, dm.wc), lambda b, n: (b * nblk + n, 0, 0))],
        out_shape=[jax.ShapeDtypeStruct((t, dm.d), BF16), jax.ShapeDtypeStruct((b_loc * nblk, SUBLANES, dm.wc), F32)],
        scratch_shapes=[pltpu.VMEM((r, dm.kvw), F32), pltpu.VMEM((r, dm.kvw), F32), pltpu.VMEM((SUBLANES, dm.wa), F32),
                        pltpu.VMEM((SUBLANES, dm.wa), F32), pltpu.VMEM((SUBLANES, dm.wc), F32),
                        pltpu.VMEM((SUBLANES, dm.wc), F32)],
        compiler_params=_params(("arbitrary", "arbitrary")),
    )(h, bias, *prm)


def _mixer_bwd(dm, h, sst, dmix, bias, prm, b_loc, nblk):
    t = h.shape[0]
    r = ATT_BLOCK
    rb8 = r // SUBLANES
    n_small = 12

    def body(h_ref, kp_ref, vp_ref, acp_ref, axp_ref, cxp0_ref, cxp1_ref, sst_ref, dmix_ref, bias_ref,
             caw, ccw, ccb, grw, grb, giw, gib, lam, na, nb, nc, snk,
             dh_ref, d_caw, d_ccw, d_ccb, d_grw, d_grb, d_giw, d_gib, d_lam, d_na, d_nb, d_nc, d_snk,
             dkp, dvp, dacp, daxp, dcxp, dsp):
        step = pl.program_id(1)
        n = nblk - 1 - step

        @pl.when(step == 0)
        def _():
            for ref in (dkp, dvp, dacp, daxp, dcxp, dsp):
                ref[...] = jnp.zeros(ref.shape, ref.dtype)

        @pl.when((step == 0) & (pl.program_id(0) == 0))
        def _():
            for ref in (d_caw, d_ccw, d_ccb, d_grw, d_grb, d_giw, d_gib, d_lam, d_na, d_nb, d_nc, d_snk):
                ref[...] = jnp.zeros(ref.shape, ref.dtype)

        p = _read_params(dm, caw, ccw, ccb, grw, grb, giw, gib, lam, na, nb, nc, snk)
        has_prev = jnp.where(n > 0, 1.0, 0.0)
        full = slice(None)
        pad = jnp.zeros((r - SUBLANES, LANES), F32)

        def with_tail(own, carry):
            z = jnp.zeros((r - SUBLANES, own.shape[1]), F32)
            return own + jnp.concatenate([z, carry], axis=0)

        a_in = (h_ref[:, 0:dm.wa], h_ref[:, dm.wa:2 * dm.wa], h_ref[:, 2 * dm.wa:3 * dm.wa], h_ref[:, 3 * dm.wa:4 * dm.wa],
                acp_ref[...] * has_prev, axp_ref[...] * has_prev, p["wa"][0], p["wa"][1], p["wa"][2], p["na"])
        _, vjp_a = jax.vjp(_branch_a, *a_in)
        g_ab, g_ac, g_ax, g_ag, g_acp, g_axp, g_w0, g_w1, g_w2, g_na = vjp_a(dmix_ref[:, 0:dm.wa])
        dh_ref[:, 0:dm.wa] = g_ab.astype(BF16)
        dh_ref[:, dm.wa:2 * dm.wa] = with_tail(g_ac, dacp[...]).astype(BF16)
        dh_ref[:, 2 * dm.wa:3 * dm.wa] = with_tail(g_ax, daxp[...]).astype(BF16)
        dh_ref[:, 3 * dm.wa:4 * dm.wa] = g_ag.astype(BF16)
        dacp[...] = g_acp
        daxp[...] = g_axp
        for k, gw in enumerate((g_w0, g_w1, g_w2)):
            d_caw[k:k + 1, :] += gw
        d_na[...] += g_na

        bias = bias_ref[jnp.where(n == 0, 0, 1)]
        kp_in = [c * has_prev for c in _chunks(kp_ref, full, 0, dm.kvw)]
        vp_in = [c * has_prev for c in _chunks(vp_ref, full, 0, dm.kvw)]
        b_in = (_chunks(h_ref, full, dm.o_q, dm.wb), _chunks(h_ref, full, dm.o_k, dm.kvw), _chunks(h_ref, full, dm.o_v, dm.kvw),
                kp_in, vp_in, _chunks(h_ref, full, dm.o_bg, dm.wb), p["snk"], p["nb"])
        _, vjp_b = jax.vjp(lambda *a: _branch_b(*a, bias), *b_in)
        g_q, g_k, g_v, g_kp, g_vp, g_bg, g_snk, g_nb = vjp_b(_chunks(dmix_ref, full, dm.wa, dm.wb))
        for j in range(len(g_q)):
            dh_ref[:, dm.o_q + LANES * j: dm.o_q + LANES * (j + 1)] = g_q[j].astype(BF16)
            dh_ref[:, dm.o_bg + LANES * j: dm.o_bg + LANES * (j + 1)] = g_bg[j].astype(BF16)
            d_nb[:, LANES * j: LANES * (j + 1)] += g_nb[j]
        for j in range(len(g_k)):
            cols = slice(LANES * j, LANES * (j + 1))
            dh_ref[:, dm.o_k + LANES * j: dm.o_k + LANES * (j + 1)] = (g_k[j] + dkp[:, cols]).astype(BF16)
            dh_ref[:, dm.o_v + LANES * j: dm.o_v + LANES * (j + 1)] = (g_v[j] + dvp[:, cols]).astype(BF16)
            dkp[:, cols] = g_kp[j]
            dvp[:, cols] = g_vp[j]
        for hd in range(dm.nq):
            d_snk[hd:hd + 1, :] += g_snk[hd]

        half_c = dm.wc // 2
        cxp_in = ([c * has_prev for c in _chunks(cxp0_ref, full, 0, half_c)]
                  + [c * has_prev for c in _chunks(cxp1_ref, full, 0, half_c)])
        c_in = (_chunks(h_ref, full, dm.o_cx, dm.wc), _chunks(h_ref, full, dm.o_cg, dm.wc), cxp_in,
                [sst_ref[0, 0:1, LANES * j: LANES * (j + 1)] for j in range(dm.nh)], p["wc"], p["bc"], p["wr"], p["br"], p["wi"], p["bi"],
                p["lam"], p["nc"])
        _, vjp_c = jax.vjp(_branch_c, *c_in)
        ct_last = [dsp[0:1, LANES * j: LANES * (j + 1)] for j in range(dm.nh)]
        g_cx, g_cg, g_cxp, g_sp, g_wc, g_bc, g_wr, g_br, g_wi, g_bi, g_lam, g_nc = vjp_c(
            (_chunks(dmix_ref, full, dm.wa + dm.wb, dm.wc), ct_last))
        for j in range(dm.nh):
            cols = slice(LANES * j, LANES * (j + 1))
            tot = g_cx[j] + jnp.concatenate([pad, dcxp[:, cols]], axis=0)
            dh_ref[:, dm.o_cx + LANES * j: dm.o_cx + LANES * (j + 1)] = tot.astype(BF16)
            dh_ref[:, dm.o_cg + LANES * j: dm.o_cg + LANES * (j + 1)] = g_cg[j].astype(BF16)
            dcxp[:, cols] = g_cxp[j]
            dsp[:, cols] = jnp.broadcast_to(g_sp[j], (SUBLANES, LANES))
            for k in range(CONV_C):
                d_ccw[k:k + 1, cols] += g_wc[k][j]
            d_ccb[:, cols] += g_bc[j]
            d_grw[j] += g_wr[j]
            d_grb[:, cols] += g_br[j]
            d_giw[j] += g_wi[j]
            d_gib[:, cols] += g_bi[j]
            d_lam[:, cols] += g_lam[j]
            d_nc[:, cols] += g_nc[j]

    def blk(b, s):
        return b * nblk + (nblk - 1 - s)

    def prev_rows8(b, s):
        return jnp.maximum(blk(b, s) * rb8 - 1, 0)

    pshapes, pspecs = _param_specs(dm)
    half_c = dm.wc // 2
    in_specs = [
        pl.BlockSpec((r, dm.in_w), lambda b, s: (blk(b, s), 0)),
        pl.BlockSpec((r, dm.kvw), lambda b, s: (jnp.maximum(blk(b, s) - 1, 0), dm.o_k // dm.kvw)),
        pl.BlockSpec((r, dm.kvw), lambda b, s: (jnp.maximum(blk(b, s) - 1, 0), dm.o_v // dm.kvw)),
        pl.BlockSpec((SUBLANES, dm.wa), lambda b, s: (prev_rows8(b, s), 1)),
        pl.BlockSpec((SUBLANES, dm.wa), lambda b, s: (prev_rows8(b, s), 2)),
        pl.BlockSpec((SUBLANES, half_c), lambda b, s: (prev_rows8(b, s), dm.o_cx // half_c)),
        pl.BlockSpec((SUBLANES, half_c), lambda b, s: (prev_rows8(b, s), dm.o_cx // half_c + 1)),
        pl.BlockSpec((1, SUBLANES, dm.wc), lambda b, s: (blk(b, s), 0, 0)),
        pl.BlockSpec((r, dm.d), lambda b, s: (blk(b, s), 0)),
        _bias_spec(),
    ] + pspecs
    outs = _pcall(
        body, name="mixer_bwd", grid=(b_loc, nblk), in_specs=in_specs,
        out_specs=[pl.BlockSpec((r, dm.in_w), lambda b, s: (blk(b, s), 0))] + pspecs,
        out_shape=[jax.ShapeDtypeStruct((t, dm.in_w), BF16)] + [jax.ShapeDtypeStruct(s, F32) for s in pshapes],
        scratch_shapes=[pltpu.VMEM((r, dm.kvw), F32), pltpu.VMEM((r, dm.kvw), F32), pltpu.VMEM((SUBLANES, dm.wa), F32),
                        pltpu.VMEM((SUBLANES, dm.wa), F32), pltpu.VMEM((SUBLANES, dm.wc), F32),
                        pltpu.VMEM((SUBLANES, dm.wc), F32)],
        compiler_params=_params(("arbitrary", "arbitrary")),
    )(h, h, h, h, h, h, h, sst, dmix, bias, *prm)
    assert len(outs) == 1 + n_small
    return outs[0], outs[1:]


def _token_spec():
    return pl.BlockSpec((SUBLANES, LANES), lambda *_: (0, 0))


def _no_token():
    return jnp.zeros((SUBLANES, LANES), F32)


def _matmul(a, b, *, dims, tm, tn, tk, out_dtype, name, addend=None, alpha=None, token=None):
    if dims == TN:
        (k_dim, m), n_dim = a.shape, b.shape[1]
        a_spec = pl.BlockSpec((tk, tm), lambda i, j, k: (k, i))
    else:
        (m, k_dim), n_dim = a.shape, (b.shape[0] if dims == NT else b.shape[1])
        a_spec = pl.BlockSpec((tm, tk), lambda i, j, k: (i, k))
    b_spec = pl.BlockSpec((tn, tk), lambda i, j, k: (j, k)) if dims == NT else pl.BlockSpec((tk, tn), lambda i, j, k: (k, j))
    assert m % tm == 0 and n_dim % tn == 0 and k_dim % tk == 0, (a.shape, b.shape, tm, tn, tk)
    nk = k_dim // tk
    o_spec = pl.BlockSpec((tm, tn), lambda i, j, k: (i, j))

    def body(*refs):
        a_ref, b_ref = refs[0], refs[1]
        add_ref = refs[2] if addend is not None else None
        o_ref, acc_ref = refs[-2], refs[-1]
        k = pl.program_id(2)
        part = lax.dot_general(a_ref[...], b_ref[...], (dims, ((), ())), preferred_element_type=F32)

        def finish(acc):
            if add_ref is not None:
                acc = acc + alpha * add_ref[...]
            o_ref[...] = acc.astype(out_dtype)

        if nk == 1:
            finish(part)
        else:
            @pl.when(k == 0)
            def _():
                acc_ref[...] = part

            @pl.when((k > 0) & (k < nk - 1))
            def _():
                acc_ref[...] += part

            @pl.when(k == nk - 1)
            def _():
                finish(acc_ref[...] + part)

    ins = [a, b] + ([addend] if addend is not None else []) + ([token] if token is not None else [])
    in_specs = [a_spec, b_spec] + ([o_spec] if addend is not None else []) + ([_token_spec()] if token is not None else [])
    return _pcall(
        body, name=name, grid=(m // tm, n_dim // tn, nk), in_specs=in_specs, out_specs=o_spec,
        out_shape=jax.ShapeDtypeStruct((m, n_dim), out_dtype),
        scratch_shapes=[pltpu.VMEM((tm, tn) if nk > 1 else (SUBLANES, LANES), F32)],
        compiler_params=_params(("parallel", "parallel", "arbitrary")),
    )(*ins)


def _tile(n, want, quantum=LANES):
    if n <= want:
        return n
    for cand in range(want - want % quantum, 0, -quantum):
        if n % cand == 0:
            return cand
    return n


def _row_tile(t, d):
    return _tile(t, max(2 * SUBLANES, (1 << 19) // d), 2 * SUBLANES)


def _ln_fwd(z, g, b, token):
    t, d = z.shape
    tr = _row_tile(t, d)

    def body(z_ref, g_ref, b_ref, _, y_ref, yb_ref):
        zz = z_ref[...]
        mu = jnp.mean(zz, axis=1, keepdims=True)
        zc = zz - mu
        var = jnp.mean(zc * zc, axis=1, keepdims=True)
        y = zc * lax.rsqrt(var + LN_EPS) * g_ref[...] + b_ref[...]
        y_ref[...] = y
        yb_ref[...] = y.astype(BF16)

    row = pl.BlockSpec((tr, d), lambda i: (i, 0))
    vec = pl.BlockSpec((1, d), lambda i: (0, 0))
    return _pcall(body, name="ln_fwd", grid=(t // tr,), in_specs=[row, vec, vec, _token_spec()], out_specs=[row, row],
                  out_shape=[jax.ShapeDtypeStruct((t, d), F32), jax.ShapeDtypeStruct((t, d), BF16)],
                  compiler_params=_params(("parallel",)))(z, g, b, token)


def _ln_bwd(dy, z, g, token):
    t, d = z.shape
    tr = _row_tile(t, d)

    def body(dy_ref, z_ref, g_ref, _, dz_ref, dzb_ref, dg_ref, db_ref):
        @pl.when(pl.program_id(0) == 0)
        def _():
            dg_ref[...] = jnp.zeros(dg_ref.shape, F32)
            db_ref[...] = jnp.zeros(db_ref.shape, F32)

        zz = z_ref[...]
        dyy = dy_ref[...]
        mu = jnp.mean(zz, axis=1, keepdims=True)
        zc = zz - mu
        rstd = lax.rsqrt(jnp.mean(zc * zc, axis=1, keepdims=True) + LN_EPS)
        xhat = zc * rstd
        dyg = dyy * g_ref[...]
        dz = rstd * (dyg - jnp.mean(dyg, axis=1, keepdims=True) - xhat * jnp.mean(dyg * xhat, axis=1, keepdims=True))
        dz_ref[...] = dz
        dzb_ref[...] = dz.astype(BF16)
        dg_ref[...] += jnp.sum(dyy * xhat, axis=0, keepdims=True)
        db_ref[...] += jnp.sum(dyy, axis=0, keepdims=True)

    row = pl.BlockSpec((tr, d), lambda i: (i, 0))
    vec = pl.BlockSpec((1, d), lambda i: (0, 0))
    return _pcall(body, name="ln_bwd", grid=(t // tr,), in_specs=[row, row, vec, _token_spec()], out_specs=[row, row, vec, vec],
                  out_shape=[jax.ShapeDtypeStruct((t, d), F32), jax.ShapeDtypeStruct((t, d), BF16),
                             jax.ShapeDtypeStruct((1, d), F32), jax.ShapeDtypeStruct((1, d), F32)],
                  compiler_params=_params(("arbitrary",)))(dy, z, g, token)


def _loss_head(y, target):
    t, d = y.shape
    tr = _row_tile(t, d)

    def body(y_ref, t_ref, dy_ref, loss_ref):
        @pl.when(pl.program_id(0) == 0)
        def _():
            loss_ref[...] = jnp.zeros(loss_ref.shape, F32)

        err = y_ref[...] - t_ref[...]
        dy_ref[...] = err * (1.0 / d)
        per_token = jnp.sum(err * err, axis=1, keepdims=True) * (1.0 / d)
        loss_ref[...] += 0.5 * jnp.sum(per_token, axis=0, keepdims=True)

    row = pl.BlockSpec((tr, d), lambda i: (i, 0))
    one = pl.BlockSpec((1, 1), lambda i: (0, 0))
    return _pcall(body, name="loss_head", grid=(t // tr,), in_specs=[row, row], out_specs=[row, one],
                  out_shape=[jax.ShapeDtypeStruct((t, d), F32), jax.ShapeDtypeStruct((1, 1), F32)],
                  compiler_params=_params(("arbitrary",)))(y, target)


def _adamw_scattered(w, m, v, layer, sums, far, prev, name):
    n_layers, r, c = w.shape
    tr = _row_tile(r, c)

    def body(*refs):
        w_ref, m_ref, v_ref, s_ref, f_ref = refs[:5]
        go_ref, d_ref, mo_ref, vo_ref = refs[-4:]
        gg = s_ref[...].astype(F32)
        for i in range(3):
            gg = gg + f_ref[i:i + 1].astype(F32)
        m_new = ADAM_B1 * m_ref[...] + (1.0 - ADAM_B1) * gg
        v_new = ADAM_B2 * v_ref[...] + (1.0 - ADAM_B2) * (gg * gg)
        m_hat = m_new / (1.0 - ADAM_B1 ** ADAM_STEP)
        v_hat = v_new / (1.0 - ADAM_B2 ** ADAM_STEP)
        go_ref[...] = gg
        d_ref[...] = -ADAM_LR * (m_hat / (jnp.sqrt(v_hat) + ADAM_EPS) + ADAM_WD * w_ref[...])
        mo_ref[...] = m_new
        vo_ref[...] = v_new

    own = pl.BlockSpec((1, tr, c), lambda i: (layer, i, 0))
    in_specs = [own, own, own, pl.BlockSpec((1, tr, c), lambda i: (2 * lax.axis_index("x") + lax.axis_index("y"), i, 0)),
                pl.BlockSpec((3, tr, c), lambda i: (0, i, 0))] + ([ANY] * 4 if prev is not None else [])
    return _pcall(body, name=name, grid=(r // tr,), in_specs=in_specs, out_specs=[own] * 4,
                  out_shape=[jax.ShapeDtypeStruct((n_layers, r, c), F32)] * 4,
                  input_output_aliases={5 + i: i for i in range(4)} if prev is not None else {},
                  compiler_params=_params(("parallel",)))(w, m, v, sums, far, *(prev if prev is not None else []))


def _pair_add(a, b, name):
    p, r, c = b.shape
    tr = _row_tile(r, c)

    def body(a_ref, b_ref, o_ref):
        o_ref[...] = (a_ref[...].astype(F32) + b_ref[...].astype(F32)).astype(BF16)

    spec = pl.BlockSpec((1, tr, c), lambda q, i: (q, i, 0))
    return _pcall(body, name=name, grid=(p, r // tr),
                  in_specs=[pl.BlockSpec((1, tr, c), lambda q, i: (2 * q + lax.axis_index("c"), i, 0)), spec], out_specs=spec,
                  out_shape=jax.ShapeDtypeStruct((p, r, c), BF16), compiler_params=_params(("parallel", "parallel")))(a, b)


def _adamw(w, g_parts, m, v, name):
    r, c = w.shape
    n_parts = g_parts.shape[0]
    tr = _row_tile(r, c) if r % SUBLANES == 0 else r

    def body(w_ref, g_ref, m_ref, v_ref, go_ref, d_ref, mo_ref, vo_ref):
        g = g_ref[0].astype(F32)
        for i in range(1, n_parts):
            g = g + g_ref[i].astype(F32)
        m_new = ADAM_B1 * m_ref[...] + (1.0 - ADAM_B1) * g
        v_new = ADAM_B2 * v_ref[...] + (1.0 - ADAM_B2) * (g * g)
        m_hat = m_new / (1.0 - ADAM_B1 ** ADAM_STEP)
        v_hat = v_new / (1.0 - ADAM_B2 ** ADAM_STEP)
        go_ref[...] = g
        d_ref[...] = -ADAM_LR * (m_hat / (jnp.sqrt(v_hat) + ADAM_EPS) + ADAM_WD * w_ref[...])
        mo_ref[...] = m_new
        vo_ref[...] = v_new

    spec = pl.BlockSpec((tr, c), lambda i: (i, 0))
    shape = jax.ShapeDtypeStruct((r, c), F32)
    return _pcall(body, name=name, grid=(r // tr,),
                  in_specs=[spec, pl.BlockSpec((n_parts, tr, c), lambda i: (0, i, 0)), spec, spec],
                  out_specs=[spec] * 4, out_shape=[shape] * 4, compiler_params=_params(("parallel",)))(w, g_parts, m, v)


def _me():
    return lax.axis_index("x"), lax.axis_index("y"), lax.axis_index("c")


def _dev(px, py, pc):
    return 4 * px + 2 * py + pc


def _remote(src, dst, send_sems, recv_sems, k, to):
    return pltpu.make_async_remote_copy(src_ref=src, dst_ref=dst, send_sem=send_sems.at[k], recv_sem=recv_sems.at[k],
                                        device_id=to, device_id_type=MESH)


def _all_gather(arrs, name):
    n = len(arrs)

    def body(*refs):
        ins, outs = refs[:n], refs[n:2 * n]
        send_sems, recv_sems, local_sems = refs[2 * n:]
        x, y, c = _me()
        me, sibling = (x, y, c), (x, y, 1 - c)
        chips = [(1 - x, y), (x, 1 - y), (1 - x, 1 - y)]
        pending = []
        for a in range(n):
            mine = pltpu.make_async_copy(ins[a], outs[a].at[_dev(*me)], local_sems.at[a])
            mine.start()
            pending.append(mine)
        sends = []
        for a in range(n):
            dst = outs[a].at[_dev(*me)]
            sends.append(_remote(ins[a], dst, send_sems, recv_sems, 7 * a, sibling))
            sends += [_remote(ins[a], dst, send_sems, recv_sems, 7 * a + 1 + j, (*chip, c)) for j, chip in enumerate(chips)]
        for cp in sends:
            cp.start()
        for a in range(n):
            for j, chip in enumerate(chips):
                blk = outs[a].at[_dev(*chip, c)]
                _remote(blk, blk, send_sems, recv_sems, 7 * a + 1 + j, me).wait_recv()
                fwd = _remote(blk, blk, send_sems, recv_sems, 7 * a + 4 + j, sibling)
                fwd.start()
                sends.append(fwd)
        for a in range(n):
            blk = outs[a].at[_dev(*sibling)]
            _remote(blk, blk, send_sems, recv_sems, 7 * a, me).wait_recv()
            for j, chip in enumerate(chips):
                blk = outs[a].at[_dev(*chip, 1 - c)]
                _remote(blk, blk, send_sems, recv_sems, 7 * a + 4 + j, me).wait_recv()
        for cp in sends:
            cp.wait_send()
        for cp in pending:
            cp.wait()

    return _pcall(
        body, name=name, in_specs=[ANY] * n, out_specs=[ANY] * n,
        out_shape=[jax.ShapeDtypeStruct((N_DEV,) + a.shape, a.dtype) for a in arrs],
        scratch_shapes=[pltpu.SemaphoreType.DMA((7 * n,)), pltpu.SemaphoreType.DMA((7 * n,)), pltpu.SemaphoreType.DMA((n,))],
    )(*arrs)


def _relations(x, y):
    return [(x, y), (1 - x, y), (x, 1 - y), (1 - x, 1 - y)]


HBM_SPEC = pl.BlockSpec(memory_space=pltpu.HBM)
SEM_SPEC = pl.BlockSpec(memory_space=pltpu.SEMAPHORE)
DATAFLOW = pltpu.SideEffectType.DATAFLOW_SIDE_EFFECTING


def _exchange_start(name, bufs, plan, n_remote, n_local, dep):
    nb = len(bufs)
    sem_shapes = [pltpu.SemaphoreType.DMA((n_remote,)), pltpu.SemaphoreType.DMA((n_remote,))]
    if n_local:
        sem_shapes.append(pltpu.SemaphoreType.DMA((n_local,)))
    ns = len(sem_shapes)

    def body(*refs):
        ins, sems, token = refs[:nb], refs[nb + 1:nb + 1 + ns], refs[-1]
        starts, _, local = plan(ins, sems[0], sems[1], sems[2] if n_local else None)
        for cp in local + starts:
            cp.start()
        token[...] = jnp.zeros(token.shape, F32)

    outs = _pcall(
        body, name=name, in_specs=[HBM_SPEC] * nb + [ANY],
        out_specs=[SEM_SPEC] * ns + [HBM_SPEC] * nb + [pl.BlockSpec(memory_space=pltpu.VMEM)],
        out_shape=sem_shapes + [pltpu.HBM(b.shape, b.dtype) for b in bufs] + [jax.ShapeDtypeStruct((SUBLANES, LANES), F32)],
        input_output_aliases={i: ns + i for i in range(nb)}, compiler_params=pltpu.CompilerParams(has_side_effects=DATAFLOW),
    )(*[pltpu.with_memory_space_constraint(b, pltpu.HBM) for b in bufs], dep)
    return dict(sems=outs[:ns], thru=outs[ns:ns + nb], plan=plan, n_local=n_local), outs[-1]


def _exchange_wait(name, handle, *after):
    thru, sems, plan, n_local = handle["thru"], handle["sems"], handle["plan"], handle["n_local"]
    nb, ns = len(thru), len(sems)

    def body(*refs):
        ins, sem_refs = refs[:nb], refs[nb:nb + ns]
        starts, arrivals, local = plan(ins, sem_refs[0], sem_refs[1], sem_refs[2] if n_local else None)
        for cp in starts:
            cp.wait_send()
        for cp in arrivals:
            cp.wait_recv()
        for cp in local:
            cp.wait()

    return _pcall(
        body, name=name, in_specs=[HBM_SPEC] * nb + [SEM_SPEC] * ns + [ANY] * len(after), out_specs=[HBM_SPEC] * nb,
        out_shape=[pltpu.HBM(b.shape, b.dtype) for b in thru], input_output_aliases={i: i for i in range(nb)},
        compiler_params=pltpu.CompilerParams(has_side_effects=DATAFLOW),
    )(*thru, *sems, *after)


def _landing(shape, dtype):
    return lax.empty(shape, dtype)


def _plan_gather_ici(n):
    def plan(refs, send_sems, recv_sems, local_sems):
        x, y, c = _me()
        me, sibling = (x, y, c), (x, y, 1 - c)
        chips = _relations(x, y)[1:]
        starts, arrivals, local = [], [], []
        for a in range(n):
            shard, land = refs[a], refs[n + a]
            own = land.at[_dev(*me)]
            local.append(pltpu.make_async_copy(shard, own, local_sems.at[a]))
            starts.append(_remote(shard, own, send_sems, recv_sems, 4 * a, sibling))
            blk = land.at[_dev(*sibling)]
            arrivals.append(_remote(blk, blk, send_sems, recv_sems, 4 * a, me))
            for j, chip in enumerate(chips):
                starts.append(_remote(shard, own, send_sems, recv_sems, 4 * a + 1 + j, (*chip, c)))
                blk = land.at[_dev(*chip, c)]
                arrivals.append(_remote(blk, blk, send_sems, recv_sems, 4 * a + 1 + j, me))
        return starts, arrivals, local
    return plan


def _plan_gather_d2d(n):
    def plan(refs, send_sems, recv_sems, local_sems):
        x, y, c = _me()
        me, sibling = (x, y, c), (x, y, 1 - c)
        starts, arrivals = [], []
        for a in range(n):
            for j, chip in enumerate(_relations(x, y)[1:]):
                blk = refs[a].at[_dev(*chip, c)]
                starts.append(_remote(blk, blk, send_sems, recv_sems, 3 * a + j, sibling))
                blk = refs[a].at[_dev(*chip, 1 - c)]
                arrivals.append(_remote(blk, blk, send_sems, recv_sems, 3 * a + j, me))
        return starts, arrivals, []
    return plan


def _plan_scatter_d2d(n):
    def plan(refs, send_sems, recv_sems, local_sems):
        x, y, c = _me()
        me, sibling = (x, y, c), (x, y, 1 - c)
        starts, arrivals = [], []
        for a in range(n):
            for k in range(4):
                starts.append(_remote(refs[a].at[2 * k + 1 - c], refs[n + a].at[k], send_sems, recv_sems, 4 * a + k, sibling))
                blk = refs[n + a].at[k]
                arrivals.append(_remote(blk, blk, send_sems, recv_sems, 4 * a + k, me))
        return starts, arrivals, []
    return plan


def _plan_scatter_ici(n):
    def plan(refs, send_sems, recv_sems, local_sems):
        x, y, c = _me()
        me = (x, y, c)
        starts, arrivals = [], []
        for a in range(n):
            for j, (cx, cy) in enumerate(_relations(x, y)[1:]):
                starts.append(_remote(refs[a].at[2 * cx + cy], refs[n + a].at[j], send_sems, recv_sems, 3 * a + j, (cx, cy, c)))
                blk = refs[n + a].at[j]
                arrivals.append(_remote(blk, blk, send_sems, recv_sems, 3 * a + j, me))
        return starts, arrivals, []
    return plan


SMALL = ("conv_a_w", "conv_c_w", "conv_c_b", "gate_r_w", "gate_r_b", "gate_i_w", "gate_i_b", "rg_lambda",
         "norm_a", "norm_b", "norm_c", "sinks", "ln_g", "ln_b")
PACK_COLS = 1024


def _pack(arrs):
    flat = jnp.concatenate([a.reshape(-1) for a in arrs])
    pad = (-flat.shape[0]) % (SUBLANES * PACK_COLS)
    return jnp.pad(flat, (0, pad)).reshape(-1, PACK_COLS)


def _unpack(packed, shapes):
    flat = packed.reshape(-1)
    out, off = [], 0
    for s in shapes:
        size = 1
        for dim in s:
            size *= dim
        out.append(flat[off:off + size].reshape(s))
        off += size
    return out


def kernel(x, w_in, conv_a_w, sinks, conv_c_w, conv_c_b, gate_r_w, gate_r_b, gate_i_w, gate_i_b, rg_lambda, norm_a, norm_b, norm_c, w_out, ln_g, ln_b, loss_target, m_w_in, m_conv_a_w, m_sinks, m_conv_c_w, m_conv_c_b, m_gate_r_w, m_gate_r_b, m_gate_i_w, m_gate_i_b, m_rg_lambda, m_norm_a, m_norm_b, m_norm_c, m_w_out, m_ln_g, m_ln_b, v_w_in, v_conv_a_w, v_sinks, v_conv_c_w, v_conv_c_b, v_gate_r_w, v_gate_r_b, v_gate_i_w, v_gate_i_b, v_rg_lambda, v_norm_a, v_norm_b, v_norm_c, v_w_out, v_ln_g, v_ln_b):
    b_loc, seq, d = x.shape
    depth = w_in.shape[0]
    dm = _Dims(d, gate_r_w.shape[1])
    t = b_loc * seq
    nblk = seq // ATT_BLOCK
    alpha = (2.0 * depth) ** 0.25
    ch = dm.wa // N_DEV
    dev = _dev(*_me())

    wt_shard = jnp.swapaxes(w_in, 1, 2).astype(BF16)
    wo_shard = w_out.astype(BF16)
    conv_shard = jnp.concatenate([conv_a_w.reshape(depth * CONV_A, ch), conv_c_w.reshape(depth * CONV_C, ch)], axis=0)
    conv_shard = jnp.pad(conv_shard, ((0, (-conv_shard.shape[0]) % SUBLANES), (0, 0)))
    wt, conv_all = _all_gather([wt_shard[0], conv_shard], "ag_weights")
    conv_all = jnp.swapaxes(conv_all, 0, 1).reshape(conv_all.shape[1], dm.wa)
    conv_a_full = conv_all[:depth * CONV_A].reshape(depth, CONV_A, dm.wa)
    conv_c_full = conv_all[depth * CONV_A:depth * (CONV_A + CONV_C)].reshape(depth, CONV_C, dm.wc)
    sinks_wide = jnp.broadcast_to(sinks[:, :, None], (depth, dm.nq, LANES))

    def layer_params(l):
        return (conv_a_full[l], conv_c_full[l], conv_c_b[l][None], gate_r_w[l], gate_r_b[l][None], gate_i_w[l],
                gate_i_b[l][None], rg_lambda[l][None], norm_a[l][None], norm_b[l][None], norm_c[l][None], sinks_wide[l])

    tm = _tile(t, 1024)
    xs = x.reshape(t, d)
    xb = xs.astype(BF16)
    saved = []
    bias = _attention_bias()
    rows_t, rows_o = dm.in_w // N_DEV, d // N_DEV
    wo0_ici, token = _exchange_start("ag_wo0_ici_start", [wo_shard[0], _landing((N_DEV, rows_o, d), BF16)],
                                     _plan_gather_ici(1), 4, 1, wt)
    wo = None
    for l in range(depth):
        if l + 1 < depth:
            lands = [_landing((N_DEV, rows_t, d), BF16), _landing((N_DEV, rows_o, d), BF16)]
            ici, token = _exchange_start("ag_ici_start_%d" % l, [wt_shard[l + 1], wo_shard[l + 1]] + lands, _plan_gather_ici(2),
                                         8, 2, token if l == 0 else wt)
        elif l > 0:
            token = _no_token()
        h = _matmul(xb, wt.reshape(dm.in_w, d), dims=NT, tm=tm, tn=_tile(dm.in_w, 512), tk=d, out_dtype=F32, name="mm_in",
                    token=token)
        mix, sst = _mixer_fwd(dm, h, bias, layer_params(l), b_loc, nblk)
        if l == 0:
            land = _exchange_wait("ag_wo0_ici_wait", wo0_ici, mix)[1:]
            wo0_d2d, _ = _exchange_start("ag_wo0_d2d_start", land, _plan_gather_d2d(1), 3, 0, mix)
            wo = _exchange_wait("ag_wo0_d2d_wait", wo0_d2d, mix)[0]
        z = _matmul(mix, wo.reshape(d, d), dims=NN, tm=tm, tn=_tile(d, 512), tk=d, out_dtype=F32, name="mm_out", addend=xs,
                    alpha=alpha)
        saved.append((xb, h, sst, mix, z, wt, wo))
        token = _no_token()
        if l + 1 < depth:
            lands = _exchange_wait("ag_ici_wait_%d" % l, ici, z)[2:]
            d2d, token = _exchange_start("ag_d2d_start_%d" % l, lands, _plan_gather_d2d(2), 6, 0, z)
        xs, xb = _ln_fwd(z, ln_g[l][None], ln_b[l][None], token)
        if l + 1 < depth:
            wt, wo = _exchange_wait("ag_d2d_wait_%d" % l, d2d, xb)

    dy, loss_part = _loss_head(xs, loss_target.reshape(t, d))
    loss = lax.psum(loss_part[0, 0], ("x", "y", "c"))

    scattered, small = [None] * depth, [None] * depth
    ici, token_ici = None, _no_token()

    def finish_scatter(l, ici, *after):
        scattered[l] = _exchange_wait("rs_ici_wait_%d" % l, ici, *after)

    def scatter_add_start(l, d2d, *after):
        done = _exchange_wait("rs_d2d_wait_%d" % l, d2d, *after)
        sums = [_pair_add(p, g, "rs_add_%d" % i) for i, (p, g) in enumerate(zip(done[:2], done[2:]))]
        lands = [_landing((3, rows_t, d), BF16), _landing((3, rows_o, d), BF16)]
        return _exchange_start("rs_ici_start_%d" % l, sums + lands, _plan_scatter_ici(2), 6, 0, after[0])

    for l in reversed(range(depth)):
        xb, h, sst, mix, z, wt, wo = saved[l]
        dz, dzb, d_lng, d_lnb = _ln_bwd(dy, z, ln_g[l][None], token_ici)
        dmix = _matmul(dzb, wo.reshape(d, d), dims=NT, tm=tm, tn=_tile(d, 512), tk=d, out_dtype=F32, name="mm_dmix")
        dwo = _matmul(mix, dzb, dims=TN, tm=_tile(d, 1024), tn=_tile(d, 512), tk=t, out_dtype=BF16, name="mm_dwo")
        dh, sm = _mixer_bwd(dm, h, sst, dmix, bias, layer_params(l), b_loc, nblk)
        (d_caw, d_ccw, d_ccb, d_grw, d_grb, d_giw, d_gib, d_lam, d_na, d_nb, d_nc, d_snk) = sm
        small[l] = dict(conv_a_w=d_caw, conv_c_w=d_ccw, conv_c_b=d_ccb[0], gate_r_w=d_grw, gate_r_b=d_grb[0], gate_i_w=d_giw,
                        gate_i_b=d_gib[0], rg_lambda=d_lam[0], norm_a=d_na[0], norm_b=d_nb[0], norm_c=d_nc[0],
                        sinks=d_snk[:, 0], ln_g=d_lng[0], ln_b=d_lnb[0])
        token = _no_token()
        if l == 0:
            g_local = _pack([jnp.stack([small[i][n] for i in range(depth)]) for n in SMALL])
            small_ici, token = _exchange_start("ag_small_ici_start", [g_local, _landing((N_DEV,) + g_local.shape, F32)],
                                               _plan_gather_ici(1), 4, 1, dh)
        dwt = _matmul(dh, xb, dims=TN, tm=_tile(dm.in_w, 1536), tn=_tile(d, 512), tk=t, out_dtype=BF16, name="mm_dwt",
                      token=token)
        if ici is not None:
            finish_scatter(l + 1, ici, dh)
        parts = [dwt.reshape(N_DEV, rows_t, d), dwo.reshape(N_DEV, rows_o, d)]
        lands = [_landing((4, rows_t, d), BF16), _landing((4, rows_o, d), BF16)]
        d2d, token = _exchange_start("rs_d2d_start_%d" % l, parts + lands, _plan_scatter_d2d(2), 8, 0, dh)
        if l > 0:
            dy = _matmul(dh, wt.reshape(dm.in_w, d), dims=NN, tm=tm, tn=_tile(d, 1024), tk=_tile(dm.in_w, 1792), out_dtype=F32,
                         name="mm_dx", addend=dz, alpha=alpha, token=token)
            ici, token_ici = scatter_add_start(l, d2d, dy)
        else:
            small_land = _exchange_wait("ag_small_ici_wait", small_ici, token)[1:]
            small_d2d, token = _exchange_start("ag_small_d2d_start", small_land, _plan_gather_d2d(1), 3, 0, token)
            ici, token_ici = scatter_add_start(l, d2d, token)
            dy = _matmul(dh, wt.reshape(dm.in_w, d), dims=NN, tm=tm, tn=_tile(d, 1024), tk=_tile(dm.in_w, 1792), out_dtype=F32,
                         name="mm_dx", addend=dz, alpha=alpha, token=token_ici)
            g_all = _exchange_wait("ag_small_d2d_wait", small_d2d, dy)[0]
    grad_x = dy.reshape(b_loc, seq, d)

    w_t, m_t, v_t = [jnp.swapaxes(a, 1, 2) for a in (w_in, m_w_in, v_w_in)]
    res_in, res_out = None, None
    for l in reversed(range(depth)):
        if l == 0:
            finish_scatter(0, ici, dy, *([res_in[0], res_out[0]] if depth > 1 else []))
        sums_t, sums_o, far_t, far_o = scattered[l]
        res_in = _adamw_scattered(w_t, m_t, v_t, l, sums_t, far_t, res_in, "adamw_in_%d" % l)
        res_out = _adamw_scattered(w_out, m_w_out, v_w_out, l, sums_o, far_o, res_out, "adamw_out_%d" % l)
    gw_in, dl_in, nm_in, nv_in = [jnp.swapaxes(a, 1, 2) for a in res_in]
    gw_out, dl_out, nm_out, nv_out = res_out

    given = dict(conv_a_w=(conv_a_w, m_conv_a_w, v_conv_a_w), conv_c_w=(conv_c_w, m_conv_c_w, v_conv_c_w),
                 conv_c_b=(conv_c_b, m_conv_c_b, v_conv_c_b), gate_r_w=(gate_r_w, m_gate_r_w, v_gate_r_w),
                 gate_r_b=(gate_r_b, m_gate_r_b, v_gate_r_b), gate_i_w=(gate_i_w, m_gate_i_w, v_gate_i_w),
                 gate_i_b=(gate_i_b, m_gate_i_b, v_gate_i_b), rg_lambda=(rg_lambda, m_rg_lambda, v_rg_lambda),
                 norm_a=(norm_a, m_norm_a, v_norm_a), norm_b=(norm_b, m_norm_b, v_norm_b), norm_c=(norm_c, m_norm_c, v_norm_c),
                 sinks=(sinks, m_sinks, v_sinks), ln_g=(ln_g, m_ln_g, v_ln_g), ln_b=(ln_b, m_ln_b, v_ln_b))
    full_shapes = [jnp.stack([small[l][n] for l in range(depth)]).shape for n in SMALL]

    def mine_of(n, a):
        if n in ("conv_a_w", "conv_c_w"):
            return lax.dynamic_update_slice(jnp.zeros(a.shape[:2] + (dm.wa,), F32), a, (0, 0, dev * ch))
        return a

    packs = [_pack([mine_of(n, given[n][i]) for n in SMALL]) for i in range(3)]
    outs = _adamw(packs[0], g_all, packs[1], packs[2], "adamw_small")
    res = {}
    for kind, packed in zip(("grad", "delta", "new_m", "new_v"), outs):
        for n, a in zip(SMALL, _unpack(packed, full_shapes)):
            if n in ("conv_a_w", "conv_c_w"):
                a = lax.dynamic_slice(a, (0, 0, dev * ch), a.shape[:2] + (ch,))
            res[kind, n] = a
    res.update({("grad", "w_in"): gw_in, ("delta", "w_in"): dl_in, ("new_m", "w_in"): nm_in, ("new_v", "w_in"): nv_in,
                ("grad", "w_out"): gw_out, ("delta", "w_out"): dl_out, ("new_m", "w_out"): nm_out, ("new_v", "w_out"): nv_out})
    order = ("w_in", "conv_a_w", "sinks", "conv_c_w", "conv_c_b", "gate_r_w", "gate_r_b", "gate_i_w", "gate_i_b", "rg_lambda",
             "norm_a", "norm_b", "norm_c", "w_out", "ln_g", "ln_b")
    return (loss, grad_x, *[res[kind, n] for kind in ("grad", "delta", "new_m", "new_v") for n in order])
```

```python
import functools

import jax
import jax.numpy as jnp
from jax import lax
from jax.experimental import pallas as pl
from jax.experimental.pallas import tpu as pltpu

F32 = jnp.float32
BF16 = jnp.bfloat16
MESH = pl.DeviceIdType.MESH
ANY = pl.BlockSpec(memory_space=pl.ANY)

N_DEV = 8
LANES = 128
SUBLANES = 8
HEAD_DIM = 64
KV_GROUP = 8
ATT_BLOCK = 128
ATT_STACK = 8
CONV_A = 3
CONV_C = 4
RG_C = 8.0
LN_EPS = 1e-5
RMS_EPS = 1e-6
NEG_INF = -1e30
ADAM_LR = 0.001
ADAM_B1 = 0.9
ADAM_B2 = 0.999
ADAM_EPS = 1e-08
ADAM_WD = 0.01
ADAM_STEP = 10
VMEM_LIMIT = 56 * 1024 * 1024

NN = ((1,), (0,))
NT = ((1,), (1,))
TN = ((0,), (0,))


def _pcall(body, **kw):
    return pl.pallas_call(body, **kw)


def _roll(x, shift, axis):
    return pltpu.roll(x, shift, axis)


def _params(sem=None, vmem=VMEM_LIMIT):
    return pltpu.CompilerParams(dimension_semantics=sem, vmem_limit_bytes=vmem)


def _dot(a, b, dims):
    return lax.dot_general(a.astype(BF16), b.astype(BF16), (dims, ((), ())), preferred_element_type=F32)


@jax.custom_vjp
def _mm(a, b):
    return _dot(a, b, NN)


def _mm_fwd(a, b):
    return _dot(a, b, NN), (a.astype(BF16), b.astype(BF16))


def _mm_bwd(res, g):
    a, b = res
    return _dot(g, b, NT), _dot(a, g, TN)


_mm.defvjp(_mm_fwd, _mm_bwd)


@jax.custom_vjp
def _mm_nt(a, b):
    return _dot(a, b, NT)


def _mm_nt_fwd(a, b):
    return _dot(a, b, NT), (a.astype(BF16), b.astype(BF16))


def _mm_nt_bwd(res, g):
    a, b = res
    return _dot(g, b, NN), _dot(g, a, TN)


_mm_nt.defvjp(_mm_nt_fwd, _mm_nt_bwd)


def _rows(shape):
    return lax.broadcasted_iota(jnp.int32, shape, 0)


@functools.partial(jax.custom_vjp, nondiff_argnums=(2,))
def _shift_halo(u, prev, k):
    r, c = u.shape
    fill = jnp.concatenate([_roll(prev, k, 0), jnp.zeros((r - SUBLANES, c), u.dtype)], axis=0)
    return jnp.where(_rows((r, c)) < k, fill, _roll(u, k, 0))


def _shift_halo_fwd(u, prev, k):
    return _shift_halo(u, prev, k), None


def _shift_halo_bwd(k, _, g):
    r, c = g.shape
    du = jnp.where(_rows((r, c)) < r - k, _roll(g, r - k, 0), 0.0)
    dprev = jnp.where(_rows((SUBLANES, c)) >= SUBLANES - k, _roll(g[0:SUBLANES], SUBLANES - k, 0), 0.0)
    return du, dprev


_shift_halo.defvjp(_shift_halo_fwd, _shift_halo_bwd)


@functools.partial(jax.custom_vjp, nondiff_argnums=(1, 2))
def _shift_fill(u, k, fill):
    return jnp.where(_rows(u.shape) < k, fill, _roll(u, k, 0))


def _shift_fill_fwd(u, k, fill):
    return _shift_fill(u, k, fill), None


def _shift_fill_bwd(k, fill, _, g):
    r = g.shape[0]
    return (jnp.where(_rows(g.shape) < r - k, _roll(g, r - k, 0), 0.0),)


_shift_fill.defvjp(_shift_fill_fwd, _shift_fill_bwd)


@jax.custom_vjp
def _swap_halves(x):
    return _roll(x, HEAD_DIM, 1)


_swap_halves.defvjp(lambda x: (_roll(x, HEAD_DIM, 1), None), lambda _, g: (_roll(g, HEAD_DIM, 1),))


@functools.partial(jax.custom_vjp, nondiff_argnums=(1,))
def _split_rows(x, n):
    r = x.shape[0] // n
    return tuple(x[i * r:(i + 1) * r] for i in range(n))


def _split_rows_fwd(x, n):
    return _split_rows(x, n), None


def _split_rows_bwd(n, _, gs):
    return (jnp.concatenate(list(gs), axis=0),)


_split_rows.defvjp(_split_rows_fwd, _split_rows_bwd)


def _logistic(x):
    return 1.0 / (1.0 + jnp.exp(-x))


@jax.custom_vjp
def _sigmoid(x):
    return _logistic(x)


def _sigmoid_fwd(x):
    s = _logistic(x)
    return s, s


_sigmoid.defvjp(_sigmoid_fwd, lambda s, g: (g * s * (1.0 - s),))


@jax.custom_vjp
def _silu(x):
    return x * _logistic(x)


def _silu_fwd(x):
    s = _logistic(x)
    return x * s, (x, s)


_silu.defvjp(_silu_fwd, lambda res, g: (g * res[1] * (1.0 + res[0] * (1.0 - res[1])),))


def _log_sigmoid(x):
    return -(jnp.maximum(-x, 0.0) + jnp.log1p(jnp.exp(-jnp.abs(x))))


@jax.custom_vjp
def _neg_expm1(x):
    series = x * (1 + x * (1 / 2) * (1 + x * (1 / 3) * (1 + x * (1 / 4) * (1 + x * (1 / 5) * (1 + x * (1 / 6) * (1 + x * (1 / 7)))))))
    return -jnp.where(jnp.abs(x) < 0.25, series, jnp.exp(x) - 1.0)


_neg_expm1.defvjp(lambda x: (_neg_expm1(x), x), lambda x, g: (-g * jnp.exp(x),))


def _shift_up(x, k, fill):
    r = x.shape[0]
    return jnp.where(_rows(x.shape) < r - k, _roll(x, r - k, 0), fill)


@jax.custom_vjp
def _scan_block(a, u, s_prev):
    acc_a, acc_b = a, u
    d = 1
    while d < a.shape[0]:
        acc_b = acc_a * _shift_fill(acc_b, d, 0.0) + acc_b
        acc_a = acc_a * _shift_fill(acc_a, d, 1.0)
        d *= 2
    return acc_a * s_prev + acc_b


def _scan_block_fwd(a, u, s_prev):
    h = _scan_block(a, u, s_prev)
    return h, (a, h, s_prev)


def _scan_block_bwd(res, dh):
    a, h, s_prev = res
    acc_a, acc_g = _shift_up(a, 1, 0.0), dh
    d = 1
    while d < a.shape[0]:
        acc_g = acc_a * _shift_up(acc_g, d, 0.0) + acc_g
        acc_a = acc_a * _shift_up(acc_a, d, 1.0)
        d *= 2
    h_prev = jnp.where(_rows(h.shape) < 1, s_prev, _roll(h, 1, 0))
    first = jnp.sum(jnp.where(_rows(h.shape) < 1, a * acc_g, 0.0), axis=0, keepdims=True)
    return acc_g * h_prev, acc_g, first


_scan_block.defvjp(_scan_block_fwd, _scan_block_bwd)


def _last_row(h):
    return jnp.sum(jnp.where(_rows(h.shape) == h.shape[0] - 1, h, 0.0), axis=0, keepdims=True)


def _branch_a(ab, ac, ax, ag, acp, axp, w0, w1, w2, na):
    u = ac * ax
    up = acp * axp
    ya = ab * (w2 * u + w1 * _shift_halo(u, up, 1) + w0 * _shift_halo(u, up, 2))
    ms = jnp.sum(ya * ya, axis=1, keepdims=True) * (1.0 / ya.shape[1])
    return ya * lax.rsqrt(ms + RMS_EPS) * na * _silu(ag)


def _branch_c(cx, cg, cxp, sp, wc, bc, wr, br, wi, bi, lam, nc):
    hs, lasts = [], []
    for j in range(len(cx)):
        xc = (wc[3][j] * cx[j] + wc[2][j] * _shift_halo(cx[j], cxp[j], 1) + wc[1][j] * _shift_halo(cx[j], cxp[j], 2)
              + wc[0][j] * _shift_halo(cx[j], cxp[j], 3) + bc[j])
        r = _sigmoid(_mm(xc, wr[j]) + br[j])
        i = _sigmoid(_mm(xc, wi[j]) + bi[j])
        log_a = RG_C * r * _log_sigmoid(lam[j])
        a = jnp.exp(log_a)
        u = jnp.sqrt(_neg_expm1(2.0 * log_a)) * (i * xc)
        h = _scan_block(a, u, sp[j])
        hs.append(h)
        lasts.append(_last_row(h))
    width = LANES * len(cx)
    ms = sum(jnp.sum(h * h, axis=1, keepdims=True) for h in hs) * (1.0 / width)
    inv = lax.rsqrt(ms + RMS_EPS)
    return [hs[j] * inv * nc[j] * _silu(cg[j]) for j in range(len(cx))], lasts


def _attention_bias():
    qi = (jnp.arange(ATT_STACK * ATT_BLOCK) % ATT_BLOCK)[:, None]
    kj = jnp.arange(2 * ATT_BLOCK)[None, :]
    dist = qi + ATT_BLOCK - kj
    band = (dist >= 0) & (dist < ATT_BLOCK)
    return jnp.where(jnp.stack([band & (kj >= ATT_BLOCK), band]), 0.0, NEG_INF).astype(F32)


def _branch_b(q, k, v, kp, vp, bg, snk, nb, bias):
    rows = ATT_BLOCK
    n_kv = 2 * len(k)
    lane0 = lax.broadcasted_iota(jnp.int32, (ATT_STACK * rows, LANES), 1) == 0
    upper = lax.broadcasted_iota(jnp.int32, (2 * rows, LANES), 1) >= HEAD_DIM
    heads = [None] * (n_kv * KV_GROUP)
    for g in range(n_kv):
        half = g % 2
        keep = upper if half else jnp.logical_not(upper)
        kc = jnp.where(keep, jnp.concatenate([kp[g // 2], k[g // 2]], axis=0), 0.0)
        vc = jnp.where(keep, jnp.concatenate([vp[g // 2], v[g // 2]], axis=0), 0.0)
        for first in range(g * KV_GROUP, (g + 1) * KV_GROUP, ATT_STACK):
            hs = range(first, first + ATT_STACK)
            qg = jnp.concatenate([q[h // 2] if h % 2 == half else _swap_halves(q[h // 2]) for h in hs], axis=0)
            s = _mm_nt(qg * (HEAD_DIM ** -0.5), kc) + bias
            sink = jnp.concatenate([jnp.broadcast_to(snk[h], (rows, LANES)) for h in hs], axis=0)
            sink = jnp.sum(jnp.where(lane0, sink, 0.0), axis=1, keepdims=True)
            m = lax.stop_gradient(jnp.maximum(jnp.max(s, axis=1, keepdims=True), sink))
            p = jnp.exp(s - m)
            inv = 1.0 / (jnp.sum(p, axis=1, keepdims=True) + jnp.exp(sink - m))
            o = _split_rows(_mm(p * inv, vc), ATT_STACK)
            for i, h in enumerate(hs):
                heads[h] = o[i] if h % 2 == half else _swap_halves(o[i])
    yb = [heads[2 * j] + heads[2 * j + 1] for j in range(len(q))]
    width = LANES * len(q)
    ms = sum(jnp.sum(y * y, axis=1, keepdims=True) for y in yb) * (1.0 / width)
    inv = lax.rsqrt(ms + RMS_EPS)
    return [yb[j] * inv * nb[j] * _silu(bg[j]) for j in range(len(q))]


class _Dims:
    def __init__(self, d_model, n_rg_heads):
        self.d = d_model
        self.wa = d_model // 4
        self.wb = d_model // 2
        self.wc = d_model // 4
        self.kvw = self.wb // KV_GROUP
        self.nq = self.wb // HEAD_DIM
        self.in_w = 4 * self.wa + 2 * self.wb + 2 * self.kvw + 2 * self.wc
        self.o_q = 4 * self.wa
        self.o_k = self.o_q + self.wb
        self.o_v = self.o_k + self.kvw
        self.o_bg = self.o_v + self.kvw
        self.o_cx = self.o_bg + self.wb
        self.o_cg = self.o_cx + self.wc
        self.nh = n_rg_heads
        assert self.wc // n_rg_heads == LANES and self.kvw % LANES == 0
        assert self.o_k % self.kvw == 0 and self.o_cx % (self.wc // 2) == 0


def _chunks(ref, rows, off, width):
    return [ref[rows, off + LANES * j: off + LANES * (j + 1)] for j in range(width // LANES)]


def _read_params(dm, caw, ccw, ccb, grw, grb, giw, gib, lam, na, nb, nc, snk):
    row = slice(0, 1)
    return dict(
        wa=[caw[k:k + 1, :] for k in range(CONV_A)], na=na[...],
        wc=[_chunks(ccw, slice(k, k + 1), 0, dm.wc) for k in range(CONV_C)], bc=_chunks(ccb, row, 0, dm.wc),
        wr=[grw[j] for j in range(dm.nh)], br=_chunks(grb, row, 0, dm.wc),
        wi=[giw[j] for j in range(dm.nh)], bi=_chunks(gib, row, 0, dm.wc),
        lam=_chunks(lam, row, 0, dm.wc), nc=_chunks(nc, row, 0, dm.wc),
        nb=_chunks(nb, row, 0, dm.wb), snk=[snk[h:h + 1, :] for h in range(dm.nq)])


def _param_specs(dm):
    shapes = [(CONV_A, dm.wa), (CONV_C, dm.wc), (1, dm.wc), (dm.nh, LANES, LANES), (1, dm.wc), (dm.nh, LANES, LANES),
              (1, dm.wc), (1, dm.wc), (1, dm.wa), (1, dm.wb), (1, dm.wc), (dm.nq, LANES)]
    specs = [pl.BlockSpec(s, (lambda b, n, _r=len(s): (0,) * _r)) for s in shapes]
    return shapes, specs


def _bias_spec():
    return pl.BlockSpec((2, ATT_STACK * ATT_BLOCK, 2 * ATT_BLOCK), lambda *_: (0, 0, 0))


def _mixer_fwd(dm, h, bias, prm, b_loc, nblk):
    t = h.shape[0]
    r = ATT_BLOCK
    tail = slice(r - SUBLANES, r)

    def body(h_ref, bias_ref, caw, ccw, ccb, grw, grb, giw, gib, lam, na, nb, nc, snk, mix_ref, sst_ref, kp, vp, acp, axp, cxp, sp):
        n = pl.program_id(1)

        @pl.when(n == 0)
        def _():
            for ref in (kp, vp, acp, axp, cxp, sp):
                ref[...] = jnp.zeros(ref.shape, ref.dtype)

        p = _read_params(dm, caw, ccw, ccb, grw, grb, giw, gib, lam, na, nb, nc, snk)
        full = slice(None)
        mix_a = _branch_a(h_ref[:, 0:dm.wa], h_ref[:, dm.wa:2 * dm.wa], h_ref[:, 2 * dm.wa:3 * dm.wa],
                          h_ref[:, 3 * dm.wa:4 * dm.wa], acp[...], axp[...], p["wa"][0], p["wa"][1], p["wa"][2], p["na"])
        mix_ref[:, 0:dm.wa] = mix_a.astype(BF16)
        bias = bias_ref[jnp.where(n == 0, 0, 1)]
        mix_b = _branch_b(_chunks(h_ref, full, dm.o_q, dm.wb), _chunks(h_ref, full, dm.o_k, dm.kvw),
                          _chunks(h_ref, full, dm.o_v, dm.kvw), _chunks(kp, full, 0, dm.kvw), _chunks(vp, full, 0, dm.kvw),
                          _chunks(h_ref, full, dm.o_bg, dm.wb), p["snk"], p["nb"], bias)
        for j, mb in enumerate(mix_b):
            mix_ref[:, dm.wa + LANES * j: dm.wa + LANES * (j + 1)] = mb.astype(BF16)
        sst_ref[0] = sp[...]
        mix_c, lasts = _branch_c(_chunks(h_ref, full, dm.o_cx, dm.wc), _chunks(h_ref, full, dm.o_cg, dm.wc),
                                 _chunks(cxp, full, 0, dm.wc), _chunks(sp, slice(0, 1), 0, dm.wc), p["wc"], p["bc"],
                                 p["wr"], p["br"], p["wi"], p["bi"], p["lam"], p["nc"])
        o_c = dm.wa + dm.wb
        for j, mc in enumerate(mix_c):
            mix_ref[:, o_c + LANES * j: o_c + LANES * (j + 1)] = mc.astype(BF16)
            sp[:, LANES * j: LANES * (j + 1)] = jnp.broadcast_to(lasts[j], (SUBLANES, LANES))
        kp[...] = h_ref[:, dm.o_k:dm.o_k + dm.kvw]
        vp[...] = h_ref[:, dm.o_v:dm.o_v + dm.kvw]
        acp[...] = h_ref[tail, dm.wa:2 * dm.wa]
        axp[...] = h_ref[tail, 2 * dm.wa:3 * dm.wa]
        cxp[...] = h_ref[tail, dm.o_cx:dm.o_cx + dm.wc]

    _, pspecs = _param_specs(dm)
    return _pcall(
        body, name="mixer_fwd", grid=(b_loc, nblk),
        in_specs=[pl.BlockSpec((r, dm.in_w), lambda b, n: (b * nblk + n, 0)), _bias_spec()] + pspecs,
        out_specs=[pl.BlockSpec((r, dm.d), lambda b, n: (b * nblk + n, 0)),
                   pl.BlockSpec((1, SUBLANES, dm.wc), lambda b, n: (b * nblk + n, 0, 0))],
        out_shape=[jax.ShapeDtypeStruct((t, dm.d), BF16), jax.ShapeDtypeStruct((b_loc * nblk, SUBLANES, dm.wc), F32)],
        scratch_shapes=[pltpu.VMEM((r, dm.kvw), F32), pltpu.VMEM((r, dm.kvw), F32), pltpu.VMEM((SUBLANES, dm.wa), F32),
                        pltpu.VMEM((SUBLANES, dm.wa), F32), pltpu.VMEM((SUBLANES, dm.wc), F32),
                        pltpu.VMEM((SUBLANES, dm.wc), F32)],
        compiler_params=_params(("arbitrary", "arbitrary")),
    )(h, bias, *prm)


def _mixer_bwd(dm, h, sst, dmix, bias, prm, b_loc, nblk):
    t = h.shape[0]
    r = ATT_BLOCK
    rb8 = r // SUBLANES
    n_small = 12

    def body(h_ref, kp_ref, vp_ref, acp_ref, axp_ref, cxp0_ref, cxp1_ref, sst_ref, dmix_ref, bias_ref,
             caw, ccw, ccb, grw, grb, giw, gib, lam, na, nb, nc, snk,
             dh_ref, d_caw, d_ccw, d_ccb, d_grw, d_grb, d_giw, d_gib, d_lam, d_na, d_nb, d_nc, d_snk,
             dkp, dvp, dacp, daxp, dcxp, dsp):
        step = pl.program_id(1)
        n = nblk - 1 - step

        @pl.when(step == 0)
        def _():
            for ref in (dkp, dvp, dacp, daxp, dcxp, dsp):
                ref[...] = jnp.zeros(ref.shape, ref.dtype)

        @pl.when((step == 0) & (pl.program_id(0) == 0))
        def _():
            for ref in (d_caw, d_ccw, d_ccb, d_grw, d_grb, d_giw, d_gib, d_lam, d_na, d_nb, d_nc, d_snk):
                ref[...] = jnp.zeros(ref.shape, ref.dtype)

        p = _read_params(dm, caw, ccw, ccb, grw, grb, giw, gib, lam, na, nb, nc, snk)
        has_prev = jnp.where(n > 0, 1.0, 0.0)
        full = slice(None)
        pad = jnp.zeros((r - SUBLANES, LANES), F32)

        def with_tail(own, carry):
            z = jnp.zeros((r - SUBLANES, own.shape[1]), F32)
            return own + jnp.concatenate([z, carry], axis=0)

        a_in = (h_ref[:, 0:dm.wa], h_ref[:, dm.wa:2 * dm.wa], h_ref[:, 2 * dm.wa:3 * dm.wa], h_ref[:, 3 * dm.wa:4 * dm.wa],
                acp_ref[...] * has_prev, axp_ref[...] * has_prev, p["wa"][0], p["wa"][1], p["wa"][2], p["na"])
        _, vjp_a = jax.vjp(_branch_a, *a_in)
        g_ab, g_ac, g_ax, g_ag, g_acp, g_axp, g_w0, g_w1, g_w2, g_na = vjp_a(dmix_ref[:, 0:dm.wa])
        dh_ref[:, 0:dm.wa] = g_ab.astype(BF16)
        dh_ref[:, dm.wa:2 * dm.wa] = with_tail(g_ac, dacp[...]).astype(BF16)
        dh_ref[:, 2 * dm.wa:3 * dm.wa] = with_tail(g_ax, daxp[...]).astype(BF16)
        dh_ref[:, 3 * dm.wa:4 * dm.wa] = g_ag.astype(BF16)
        dacp[...] = g_acp
        daxp[...] = g_axp
        for k, gw in enumerate((g_w0, g_w1, g_w2)):
            d_caw[k:k + 1, :] += gw
        d_na[...] += g_na

        bias = bias_ref[jnp.where(n == 0, 0, 1)]
        kp_in = [c * has_prev for c in _chunks(kp_ref, full, 0, dm.kvw)]
        vp_in = [c * has_prev for c in _chunks(vp_ref, full, 0, dm.kvw)]
        b_in = (_chunks(h_ref, full, dm.o_q, dm.wb), _chunks(h_ref, full, dm.o_k, dm.kvw), _chunks(h_ref, full, dm.o_v, dm.kvw),
                kp_in, vp_in, _chunks(h_ref, full, dm.o_bg, dm.wb), p["snk"], p["nb"])
        _, vjp_b = jax.vjp(lambda *a: _branch_b(*a, bias), *b_in)
        g_q, g_k, g_v, g_kp, g_vp, g_bg, g_snk, g_nb = vjp_b(_chunks(dmix_ref, full, dm.wa, dm.wb))
        for j in range(len(g_q)):
            dh_ref[:, dm.o_q + LANES * j: dm.o_q + LANES * (j + 1)] = g_q[j].astype(BF16)
            dh_ref[:, dm.o_bg + LANES * j: dm.o_bg + LANES * (j + 1)] = g_bg[j].astype(BF16)
            d_nb[:, LANES * j: LANES * (j + 1)] += g_nb[j]
        for j in range(len(g_k)):
            cols = slice(LANES * j, LANES * (j + 1))
            dh_ref[:, dm.o_k + LANES * j: dm.o_k + LANES * (j + 1)] = (g_k[j] + dkp[:, cols]).astype(BF16)
            dh_ref[:, dm.o_v + LANES * j: dm.o_v + LANES * (j + 1)] = (g_v[j] + dvp[:, cols]).astype(BF16)
            dkp[:, cols] = g_kp[j]
            dvp[:, cols] = g_vp[j]
        for hd in range(dm.nq):
            d_snk[hd:hd + 1, :] += g_snk[hd]

        half_c = dm.wc // 2
        cxp_in = ([c * has_prev for c in _chunks(cxp0_ref, full, 0, half_c)]
                  + [c * has_prev for c in _chunks(cxp1_ref, full, 0, half_c)])
        c_in = (_chunks(h_ref, full, dm.o_cx, dm.wc), _chunks(h_ref, full, dm.o_cg, dm.wc), cxp_in,
                [sst_ref[0, 0:1, LANES * j: LANES * (j + 1)] for j in range(dm.nh)], p["wc"], p["bc"], p["wr"], p["br"], p["wi"], p["bi"],
                p["lam"], p["nc"])
        _, vjp_c = jax.vjp(_branch_c, *c_in)
        ct_last = [dsp[0:1, LANES * j: LANES * (j + 1)] for j in range(dm.nh)]
        g_cx, g_cg, g_cxp, g_sp, g_wc, g_bc, g_wr, g_br, g_wi, g_bi, g_lam, g_nc = vjp_c(
            (_chunks(dmix_ref, full, dm.wa + dm.wb, dm.wc), ct_last))
        for j in range(dm.nh):
            cols = slice(LANES * j, LANES * (j + 1))
            tot = g_cx[j] + jnp.concatenate([pad, dcxp[:, cols]], axis=0)
            dh_ref[:, dm.o_cx + LANES * j: dm.o_cx + LANES * (j + 1)] = tot.astype(BF16)
            dh_ref[:, dm.o_cg + LANES * j: dm.o_cg + LANES * (j + 1)] = g_cg[j].astype(BF16)
            dcxp[:, cols] = g_cxp[j]
            dsp[:, cols] = jnp.broadcast_to(g_sp[j], (SUBLANES, LANES))
            for k in range(CONV_C):
                d_ccw[k:k + 1, cols] += g_wc[k][j]
            d_ccb[:, cols] += g_bc[j]
            d_grw[j] += g_wr[j]
            d_grb[:, cols] += g_br[j]
            d_giw[j] += g_wi[j]
            d_gib[:, cols] += g_bi[j]
            d_lam[:, cols] += g_lam[j]
            d_nc[:, cols] += g_nc[j]

    def blk(b, s):
        return b * nblk + (nblk - 1 - s)

    def prev_rows8(b, s):
        return jnp.maximum(blk(b, s) * rb8 - 1, 0)

    pshapes, pspecs = _param_specs(dm)
    half_c = dm.wc // 2
    in_specs = [
        pl.BlockSpec((r, dm.in_w), lambda b, s: (blk(b, s), 0)),
        pl.BlockSpec((r, dm.kvw), lambda b, s: (jnp.maximum(blk(b, s) - 1, 0), dm.o_k // dm.kvw)),
        pl.BlockSpec((r, dm.kvw), lambda b, s: (jnp.maximum(blk(b, s) - 1, 0), dm.o_v // dm.kvw)),
        pl.BlockSpec((SUBLANES, dm.wa), lambda b, s: (prev_rows8(b, s), 1)),
        pl.BlockSpec((SUBLANES, dm.wa), lambda b, s: (prev_rows8(b, s), 2)),
        pl.BlockSpec((SUBLANES, half_c), lambda b, s: (prev_rows8(b, s), dm.o_cx // half_c)),
        pl.BlockSpec((SUBLANES, half_c), lambda b, s: (prev_rows8(b, s), dm.o_cx // half_c + 1)),
        pl.BlockSpec((1, SUBLANES, dm.wc), lambda b, s: (blk(b, s), 0, 0)),
        pl.BlockSpec((r, dm.d), lambda b, s: (blk(b, s), 0)),
        _bias_spec(),
    ] + pspecs
    outs = _pcall(
        body, name="mixer_bwd", grid=(b_loc, nblk), in_specs=in_specs,
        out_specs=[pl.BlockSpec((r, dm.in_w), lambda b, s: (blk(b, s), 0))] + pspecs,
        out_shape=[jax.ShapeDtypeStruct((t, dm.in_w), BF16)] + [jax.ShapeDtypeStruct(s, F32) for s in pshapes],
        scratch_shapes=[pltpu.VMEM((r, dm.kvw), F32), pltpu.VMEM((r, dm.kvw), F32), pltpu.VMEM((SUBLANES, dm.wa), F32),
                        pltpu.VMEM((SUBLANES, dm.wa), F32), pltpu.VMEM((SUBLANES, dm.wc), F32),
                        pltpu.VMEM((SUBLANES, dm.wc), F32)],
        compiler_params=_params(("arbitrary", "arbitrary")),
    )(h, h, h, h, h, h, h, sst, dmix, bias, *prm)
    assert len(outs) == 1 + n_small
    return outs[0], outs[1:]


def _token_spec():
    return pl.BlockSpec((SUBLANES, LANES), lambda *_: (0, 0))


def _no_token():
    return jnp.zeros((SUBLANES, LANES), F32)


def _matmul(a, b, *, dims, tm, tn, tk, out_dtype, name, addend=None, alpha=None, token=None):
    if dims == TN:
        (k_dim, m), n_dim = a.shape, b.shape[1]
        a_spec = pl.BlockSpec((tk, tm), lambda i, j, k: (k, i))
    else:
        (m, k_dim), n_dim = a.shape, (b.shape[0] if dims == NT else b.shape[1])
        a_spec = pl.BlockSpec((tm, tk), lambda i, j, k: (i, k))
    b_spec = pl.BlockSpec((tn, tk), lambda i, j, k: (j, k)) if dims == NT else pl.BlockSpec((tk, tn), lambda i, j, k: (k, j))
    assert m % tm == 0 and n_dim % tn == 0 and k_dim % tk == 0, (a.shape, b.shape, tm, tn, tk)
    nk = k_dim // tk
    o_spec = pl.BlockSpec((tm, tn), lambda i, j, k: (i, j))

    def body(*refs):
        a_ref, b_ref = refs[0], refs[1]
        add_ref = refs[2] if addend is not None else None
        o_ref, acc_ref = refs[-2], refs[-1]
        k = pl.program_id(2)
        part = lax.dot_general(a_ref[...], b_ref[...], (dims, ((), ())), preferred_element_type=F32)

        def finish(acc):
            if add_ref is not None:
                acc = acc + alpha * add_ref[...]
            o_ref[...] = acc.astype(out_dtype)

        if nk == 1:
            finish(part)
        else:
            @pl.when(k == 0)
            def _():
                acc_ref[...] = part

            @pl.when((k > 0) & (k < nk - 1))
            def _():
                acc_ref[...] += part

            @pl.when(k == nk - 1)
            def _():
                finish(acc_ref[...] + part)

    ins = [a, b] + ([addend] if addend is not None else []) + ([token] if token is not None else [])
    in_specs = [a_spec, b_spec] + ([o_spec] if addend is not None else []) + ([_token_spec()] if token is not None else [])
    return _pcall(
        body, name=name, grid=(m // tm, n_dim // tn, nk), in_specs=in_specs, out_specs=o_spec,
        out_shape=jax.ShapeDtypeStruct((m, n_dim), out_dtype),
        scratch_shapes=[pltpu.VMEM((tm, tn) if nk > 1 else (SUBLANES, LANES), F32)],
        compiler_params=_params(("parallel", "parallel", "arbitrary")),
    )(*ins)


def _tile(n, want, quantum=LANES):
    if n <= want:
        return n
    for cand in range(want - want % quantum, 0, -quantum):
        if n % cand == 0:
            return cand
    return n


def _row_tile(t, d, elems=1 << 19):
    return _tile(t, max(2 * SUBLANES, elems // d), 2 * SUBLANES)


STREAM_ELEMS = 1 << 20


def _ln_fwd(z, g, b, token):
    t, d = z.shape
    tr = _row_tile(t, d, STREAM_ELEMS)

    def body(z_ref, g_ref, b_ref, _, y_ref, yb_ref):
        zz = z_ref[...]
        mu = jnp.mean(zz, axis=1, keepdims=True)
        zc = zz - mu
        var = jnp.mean(zc * zc, axis=1, keepdims=True)
        y = zc * lax.rsqrt(var + LN_EPS) * g_ref[...] + b_ref[...]
        y_ref[...] = y
        yb_ref[...] = y.astype(BF16)

    row = pl.BlockSpec((tr, d), lambda i: (i, 0))
    vec = pl.BlockSpec((1, d), lambda i: (0, 0))
    return _pcall(body, name="ln_fwd", grid=(t // tr,), in_specs=[row, vec, vec, _token_spec()], out_specs=[row, row],
                  out_shape=[jax.ShapeDtypeStruct((t, d), F32), jax.ShapeDtypeStruct((t, d), BF16)],
                  compiler_params=_params(("parallel",)))(z, g, b, token)


def _ln_bwd(dy, z, g, token):
    t, d = z.shape
    tr = _row_tile(t, d, STREAM_ELEMS)

    def body(dy_ref, z_ref, g_ref, _, dz_ref, dzb_ref, dg_ref, db_ref):
        @pl.when(pl.program_id(0) == 0)
        def _():
            dg_ref[...] = jnp.zeros(dg_ref.shape, F32)
            db_ref[...] = jnp.zeros(db_ref.shape, F32)

        zz = z_ref[...]
        dyy = dy_ref[...]
        mu = jnp.mean(zz, axis=1, keepdims=True)
        zc = zz - mu
        rstd = lax.rsqrt(jnp.mean(zc * zc, axis=1, keepdims=True) + LN_EPS)
        xhat = zc * rstd
        dyg = dyy * g_ref[...]
        dz = rstd * (dyg - jnp.mean(dyg, axis=1, keepdims=True) - xhat * jnp.mean(dyg * xhat, axis=1, keepdims=True))
        dz_ref[...] = dz
        dzb_ref[...] = dz.astype(BF16)
        dg_ref[...] += jnp.sum(dyy * xhat, axis=0, keepdims=True)
        db_ref[...] += jnp.sum(dyy, axis=0, keepdims=True)

    row = pl.BlockSpec((tr, d), lambda i: (i, 0))
    vec = pl.BlockSpec((1, d), lambda i: (0, 0))
    return _pcall(body, name="ln_bwd", grid=(t // tr,), in_specs=[row, row, vec, _token_spec()], out_specs=[row, row, vec, vec],
                  out_shape=[jax.ShapeDtypeStruct((t, d), F32), jax.ShapeDtypeStruct((t, d), BF16),
                             jax.ShapeDtypeStruct((1, d), F32), jax.ShapeDtypeStruct((1, d), F32)],
                  compiler_params=_params(("arbitrary",)))(dy, z, g, token)


def _loss_head(y, target):
    t, d = y.shape
    tr = _row_tile(t, d, STREAM_ELEMS)

    def body(y_ref, t_ref, dy_ref, loss_ref):
        @pl.when(pl.program_id(0) == 0)
        def _():
            loss_ref[...] = jnp.zeros(loss_ref.shape, F32)

        err = y_ref[...] - t_ref[...]
        dy_ref[...] = err * (1.0 / d)
        per_token = jnp.sum(err * err, axis=1, keepdims=True) * (1.0 / d)
        loss_ref[...] += 0.5 * jnp.sum(per_token, axis=0, keepdims=True)

    row = pl.BlockSpec((tr, d), lambda i: (i, 0))
    one = pl.BlockSpec((1, 1), lambda i: (0, 0))
    return _pcall(body, name="loss_head", grid=(t // tr,), in_specs=[row, row], out_specs=[row, one],
                  out_shape=[jax.ShapeDtypeStruct((t, d), F32), jax.ShapeDtypeStruct((1, 1), F32)],
                  compiler_params=_params(("arbitrary",)))(y, target)


def _adamw_scattered(w, m, v, layer, sums, far, prev, name):
    n_layers, r, c = w.shape
    tr = _row_tile(r, c)

    def body(*refs):
        w_ref, m_ref, v_ref, s_ref, f_ref = refs[:5]
        go_ref, d_ref, mo_ref, vo_ref = refs[-4:]
        gg = s_ref[...].astype(F32)
        for i in range(3):
            gg = gg + f_ref[i:i + 1].astype(F32)
        m_new = ADAM_B1 * m_ref[...] + (1.0 - ADAM_B1) * gg
        v_new = ADAM_B2 * v_ref[...] + (1.0 - ADAM_B2) * (gg * gg)
        m_hat = m_new / (1.0 - ADAM_B1 ** ADAM_STEP)
        v_hat = v_new / (1.0 - ADAM_B2 ** ADAM_STEP)
        go_ref[...] = gg
        d_ref[...] = -ADAM_LR * (m_hat / (jnp.sqrt(v_hat) + ADAM_EPS) + ADAM_WD * w_ref[...])
        mo_ref[...] = m_new
        vo_ref[...] = v_new

    own = pl.BlockSpec((1, tr, c), lambda i: (layer, i, 0))
    in_specs = [own, own, own, pl.BlockSpec((1, tr, c), lambda i: (2 * lax.axis_index("x") + lax.axis_index("y"), i, 0)),
                pl.BlockSpec((3, tr, c), lambda i: (0, i, 0))] + ([ANY] * 4 if prev is not None else [])
    return _pcall(body, name=name, grid=(r // tr,), in_specs=in_specs, out_specs=[own] * 4,
                  out_shape=[jax.ShapeDtypeStruct((n_layers, r, c), F32)] * 4,
                  input_output_aliases={5 + i: i for i in range(4)} if prev is not None else {},
                  compiler_params=_params(("parallel",)))(w, m, v, sums, far, *(prev if prev is not None else []))


def _pair_add(a, b, name):
    p, r, c = b.shape
    tr = _row_tile(r, c, STREAM_ELEMS)

    def body(a_ref, b_ref, o_ref):
        o_ref[...] = (a_ref[...].astype(F32) + b_ref[...].astype(F32)).astype(BF16)

    spec = pl.BlockSpec((1, tr, c), lambda q, i: (q, i, 0))
    return _pcall(body, name=name, grid=(p, r // tr),
                  in_specs=[pl.BlockSpec((1, tr, c), lambda q, i: (2 * q + lax.axis_index("c"), i, 0)), spec], out_specs=spec,
                  out_shape=jax.ShapeDtypeStruct((p, r, c), BF16), compiler_params=_params(("parallel", "parallel")))(a, b)


def _adamw(w, g_parts, m, v, name):
    r, c = w.shape
    n_parts = g_parts.shape[0]
    tr = _row_tile(r, c) if r % SUBLANES == 0 else r

    def body(w_ref, g_ref, m_ref, v_ref, go_ref, d_ref, mo_ref, vo_ref):
        g = g_ref[0].astype(F32)
        for i in range(1, n_parts):
            g = g + g_ref[i].astype(F32)
        m_new = ADAM_B1 * m_ref[...] + (1.0 - ADAM_B1) * g
        v_new = ADAM_B2 * v_ref[...] + (1.0 - ADAM_B2) * (g * g)
        m_hat = m_new / (1.0 - ADAM_B1 ** ADAM_STEP)
        v_hat = v_new / (1.0 - ADAM_B2 ** ADAM_STEP)
        go_ref[...] = g
        d_ref[...] = -ADAM_LR * (m_hat / (jnp.sqrt(v_hat) + ADAM_EPS) + ADAM_WD * w_ref[...])
        mo_ref[...] = m_new
        vo_ref[...] = v_new

    spec = pl.BlockSpec((tr, c), lambda i: (i, 0))
    shape = jax.ShapeDtypeStruct((r, c), F32)
    return _pcall(body, name=name, grid=(r // tr,),
                  in_specs=[spec, pl.BlockSpec((n_parts, tr, c), lambda i: (0, i, 0)), spec, spec],
                  out_specs=[spec] * 4, out_shape=[shape] * 4, compiler_params=_params(("parallel",)))(w, g_parts, m, v)


def _me():
    return lax.axis_index("x"), lax.axis_index("y"), lax.axis_index("c")


def _dev(px, py, pc):
    return 4 * px + 2 * py + pc


def _remote(src, dst, send_sems, recv_sems, k, to):
    return pltpu.make_async_remote_copy(src_ref=src, dst_ref=dst, send_sem=send_sems.at[k], recv_sem=recv_sems.at[k],
                                        device_id=to, device_id_type=MESH)


def _all_gather(arrs, name):
    n = len(arrs)

    def body(*refs):
        ins, outs = refs[:n], refs[n:2 * n]
        send_sems, recv_sems, local_sems = refs[2 * n:]
        x, y, c = _me()
        me, sibling = (x, y, c), (x, y, 1 - c)
        chips = [(1 - x, y), (x, 1 - y), (1 - x, 1 - y)]
        pending = []
        for a in range(n):
            mine = pltpu.make_async_copy(ins[a], outs[a].at[_dev(*me)], local_sems.at[a])
            mine.start()
            pending.append(mine)
        sends = []
        for a in range(n):
            dst = outs[a].at[_dev(*me)]
            sends.append(_remote(ins[a], dst, send_sems, recv_sems, 7 * a, sibling))
            sends += [_remote(ins[a], dst, send_sems, recv_sems, 7 * a + 1 + j, (*chip, c)) for j, chip in enumerate(chips)]
        for cp in sends:
            cp.start()
        for a in range(n):
            for j, chip in enumerate(chips):
                blk = outs[a].at[_dev(*chip, c)]
                _remote(blk, blk, send_sems, recv_sems, 7 * a + 1 + j, me).wait_recv()
                fwd = _remote(blk, blk, send_sems, recv_sems, 7 * a + 4 + j, sibling)
                fwd.start()
                sends.append(fwd)
        for a in range(n):
            blk = outs[a].at[_dev(*sibling)]
            _remote(blk, blk, send_sems, recv_sems, 7 * a, me).wait_recv()
            for j, chip in enumerate(chips):
                blk = outs[a].at[_dev(*chip, 1 - c)]
                _remote(blk, blk, send_sems, recv_sems, 7 * a + 4 + j, me).wait_recv()
        for cp in sends:
            cp.wait_send()
        for cp in pending:
            cp.wait()

    return _pcall(
        body, name=name, in_specs=[ANY] * n, out_specs=[ANY] * n,
        out_shape=[jax.ShapeDtypeStruct((N_DEV,) + a.shape, a.dtype) for a in arrs],
        scratch_shapes=[pltpu.SemaphoreType.DMA((7 * n,)), pltpu.SemaphoreType.DMA((7 * n,)), pltpu.SemaphoreType.DMA((n,))],
    )(*arrs)


def _relations(x, y):
    return [(x, y), (1 - x, y), (x, 1 - y), (1 - x, 1 - y)]


HBM_SPEC = pl.BlockSpec(memory_space=pltpu.HBM)
SEM_SPEC = pl.BlockSpec(memory_space=pltpu.SEMAPHORE)
DATAFLOW = pltpu.SideEffectType.DATAFLOW_SIDE_EFFECTING


def _exchange_start(name, bufs, plan, n_remote, n_local, dep):
    nb = len(bufs)
    sem_shapes = [pltpu.SemaphoreType.DMA((n_remote,)), pltpu.SemaphoreType.DMA((n_remote,))]
    if n_local:
        sem_shapes.append(pltpu.SemaphoreType.DMA((n_local,)))
    ns = len(sem_shapes)

    def body(*refs):
        ins, sems, token = refs[:nb], refs[nb + 1:nb + 1 + ns], refs[-1]
        starts, _, local = plan(ins, sems[0], sems[1], sems[2] if n_local else None)
        for cp in local + starts:
            cp.start()
        token[...] = jnp.zeros(token.shape, F32)

    outs = _pcall(
        body, name=name, in_specs=[HBM_SPEC] * nb + [ANY],
        out_specs=[SEM_SPEC] * ns + [HBM_SPEC] * nb + [pl.BlockSpec(memory_space=pltpu.VMEM)],
        out_shape=sem_shapes + [pltpu.HBM(b.shape, b.dtype) for b in bufs] + [jax.ShapeDtypeStruct((SUBLANES, LANES), F32)],
        input_output_aliases={i: ns + i for i in range(nb)}, compiler_params=pltpu.CompilerParams(has_side_effects=DATAFLOW),
    )(*[pltpu.with_memory_space_constraint(b, pltpu.HBM) for b in bufs], dep)
    return dict(sems=outs[:ns], thru=outs[ns:ns + nb], plan=plan, n_local=n_local), outs[-1]


def _exchange_wait(name, handle, *after):
    thru, sems, plan, n_local = handle["thru"], handle["sems"], handle["plan"], handle["n_local"]
    nb, ns = len(thru), len(sems)

    def body(*refs):
        ins, sem_refs = refs[:nb], refs[nb:nb + ns]
        starts, arrivals, local = plan(ins, sem_refs[0], sem_refs[1], sem_refs[2] if n_local else None)
        for cp in starts:
            cp.wait_send()
        for cp in arrivals:
            cp.wait_recv()
        for cp in local:
            cp.wait()

    return _pcall(
        body, name=name, in_specs=[HBM_SPEC] * nb + [SEM_SPEC] * ns + [ANY] * len(after), out_specs=[HBM_SPEC] * nb,
        out_shape=[pltpu.HBM(b.shape, b.dtype) for b in thru], input_output_aliases={i: i for i in range(nb)},
        compiler_params=pltpu.CompilerParams(has_side_effects=DATAFLOW),
    )(*thru, *sems, *after)


def _landing(shape, dtype):
    return lax.empty(shape, dtype)


def _plan_gather_ici(n):
    def plan(refs, send_sems, recv_sems, local_sems):
        x, y, c = _me()
        me, sibling = (x, y, c), (x, y, 1 - c)
        chips = _relations(x, y)[1:]
        starts, arrivals, local = [], [], []
        for a in range(n):
            shard, land = refs[a], refs[n + a]
            own = land.at[_dev(*me)]
            local.append(pltpu.make_async_copy(shard, own, local_sems.at[a]))
            starts.append(_remote(shard, own, send_sems, recv_sems, 4 * a, sibling))
            blk = land.at[_dev(*sibling)]
            arrivals.append(_remote(blk, blk, send_sems, recv_sems, 4 * a, me))
            for j, chip in enumerate(chips):
                starts.append(_remote(shard, own, send_sems, recv_sems, 4 * a + 1 + j, (*chip, c)))
                blk = land.at[_dev(*chip, c)]
                arrivals.append(_remote(blk, blk, send_sems, recv_sems, 4 * a + 1 + j, me))
        return starts, arrivals, local
    return plan


def _plan_gather_d2d(n):
    def plan(refs, send_sems, recv_sems, local_sems):
        x, y, c = _me()
        me, sibling = (x, y, c), (x, y, 1 - c)
        starts, arrivals = [], []
        for a in range(n):
            for j, chip in enumerate(_relations(x, y)[1:]):
                blk = refs[a].at[_dev(*chip, c)]
                starts.append(_remote(blk, blk, send_sems, recv_sems, 3 * a + j, sibling))
                blk = refs[a].at[_dev(*chip, 1 - c)]
                arrivals.append(_remote(blk, blk, send_sems, recv_sems, 3 * a + j, me))
        return starts, arrivals, []
    return plan


def _plan_scatter_d2d(n):
    def plan(refs, send_sems, recv_sems, local_sems):
        x, y, c = _me()
        me, sibling = (x, y, c), (x, y, 1 - c)
        starts, arrivals = [], []
        for a in range(n):
            for k in range(4):
                starts.append(_remote(refs[a].at[2 * k + 1 - c], refs[n + a].at[k], send_sems, recv_sems, 4 * a + k, sibling))
                blk = refs[n + a].at[k]
                arrivals.append(_remote(blk, blk, send_sems, recv_sems, 4 * a + k, me))
        return starts, arrivals, []
    return plan


def _plan_scatter_ici(n):
    def plan(refs, send_sems, recv_sems, local_sems):
        x, y, c = _me()
        me = (x, y, c)
        starts, arrivals = [], []
        for a in range(n):
            for j, (cx, cy) in enumerate(_relations(x, y)[1:]):
                starts.append(_remote(refs[a].at[2 * cx + cy], refs[n + a].at[j], send_sems, recv_sems, 3 * a + j, (cx, cy, c)))
                blk = refs[n + a].at[j]
                arrivals.append(_remote(blk, blk, send_sems, recv_sems, 3 * a + j, me))
        return starts, arrivals, []
    return plan


SMALL = ("conv_a_w", "conv_c_w", "conv_c_b", "gate_r_w", "gate_r_b", "gate_i_w", "gate_i_b", "rg_lambda",
         "norm_a", "norm_b", "norm_c", "sinks", "ln_g", "ln_b")
PACK_COLS = 1024


def _pack(arrs):
    flat = jnp.concatenate([a.reshape(-1) for a in arrs])
    pad = (-flat.shape[0]) % (SUBLANES * PACK_COLS)
    return jnp.pad(flat, (0, pad)).reshape(-1, PACK_COLS)


def _unpack(packed, shapes):
    flat = packed.reshape(-1)
    out, off = [], 0
    for s in shapes:
        size = 1
        for dim in s:
            size *= dim
        out.append(flat[off:off + size].reshape(s))
        off += size
    return out


def kernel(x, w_in, conv_a_w, sinks, conv_c_w, conv_c_b, gate_r_w, gate_r_b, gate_i_w, gate_i_b, rg_lambda, norm_a, norm_b, norm_c, w_out, ln_g, ln_b, loss_target, m_w_in, m_conv_a_w, m_sinks, m_conv_c_w, m_conv_c_b, m_gate_r_w, m_gate_r_b, m_gate_i_w, m_gate_i_b, m_rg_lambda, m_norm_a, m_norm_b, m_norm_c, m_w_out, m_ln_g, m_ln_b, v_w_in, v_conv_a_w, v_sinks, v_conv_c_w, v_conv_c_b, v_gate_r_w, v_gate_r_b, v_gate_i_w, v_gate_i_b, v_rg_lambda, v_norm_a, v_norm_b, v_norm_c, v_w_out, v_ln_g, v_ln_b):
    b_loc, seq, d = x.shape
    depth = w_in.shape[0]
    dm = _Dims(d, gate_r_w.shape[1])
    t = b_loc * seq
    nblk = seq // ATT_BLOCK
    alpha = (2.0 * depth) ** 0.25
    ch = dm.wa // N_DEV
    dev = _dev(*_me())

    wt_shard = jnp.swapaxes(w_in, 1, 2).astype(BF16)
    wo_shard = w_out.astype(BF16)
    conv_shard = jnp.concatenate([conv_a_w.reshape(depth * CONV_A, ch), conv_c_w.reshape(depth * CONV_C, ch)], axis=0)
    conv_shard = jnp.pad(conv_shard, ((0, (-conv_shard.shape[0]) % SUBLANES), (0, 0)))
    wt, conv_all = _all_gather([wt_shard[0], conv_shard], "ag_weights")
    conv_all = jnp.swapaxes(conv_all, 0, 1).reshape(conv_all.shape[1], dm.wa)
    conv_a_full = conv_all[:depth * CONV_A].reshape(depth, CONV_A, dm.wa)
    conv_c_full = conv_all[depth * CONV_A:depth * (CONV_A + CONV_C)].reshape(depth, CONV_C, dm.wc)
    sinks_wide = jnp.broadcast_to(sinks[:, :, None], (depth, dm.nq, LANES))

    def layer_params(l):
        return (conv_a_full[l], conv_c_full[l], conv_c_b[l][None], gate_r_w[l], gate_r_b[l][None], gate_i_w[l],
                gate_i_b[l][None], rg_lambda[l][None], norm_a[l][None], norm_b[l][None], norm_c[l][None], sinks_wide[l])

    tm = _tile(t, 1024)
    xs = x.reshape(t, d)
    xb = xs.astype(BF16)
    saved = []
    bias = _attention_bias()
    rows_t, rows_o = dm.in_w // N_DEV, d // N_DEV
    wo0_ici, token = _exchange_start("ag_wo0_ici_start", [wo_shard[0], _landing((N_DEV, rows_o, d), BF16)],
                                     _plan_gather_ici(1), 4, 1, wt)
    wo = None
    for l in range(depth):
        if l + 1 < depth:
            lands = [_landing((N_DEV, rows_t, d), BF16), _landing((N_DEV, rows_o, d), BF16)]
            ici, token = _exchange_start("ag_ici_start_%d" % l, [wt_shard[l + 1], wo_shard[l + 1]] + lands, _plan_gather_ici(2),
                                         8, 2, token if l == 0 else wt)
        elif l > 0:
            token = _no_token()
        h = _matmul(xb, wt.reshape(dm.in_w, d), dims=NT, tm=tm, tn=_tile(dm.in_w, 512), tk=d, out_dtype=F32, name="mm_in",
                    token=token)
        mix, sst = _mixer_fwd(dm, h, bias, layer_params(l), b_loc, nblk)
        if l == 0:
            land = _exchange_wait("ag_wo0_ici_wait", wo0_ici, mix)[1:]
            wo0_d2d, _ = _exchange_start("ag_wo0_d2d_start", land, _plan_gather_d2d(1), 3, 0, mix)
            wo = _exchange_wait("ag_wo0_d2d_wait", wo0_d2d, mix)[0]
        z = _matmul(mix, wo.reshape(d, d), dims=NN, tm=tm, tn=_tile(d, 512), tk=d, out_dtype=F32, name="mm_out", addend=xs,
                    alpha=alpha)
        saved.append((xb, h, sst, mix, z, wt, wo))
        token = _no_token()
        if l + 1 < depth:
            lands = _exchange_wait("ag_ici_wait_%d" % l, ici, z)[2:]
            d2d, token = _exchange_start("ag_d2d_start_%d" % l, lands, _plan_gather_d2d(2), 6, 0, z)
        xs, xb = _ln_fwd(z, ln_g[l][None], ln_b[l][None], token)
        if l + 1 < depth:
            wt, wo = _exchange_wait("ag_d2d_wait_%d" % l, d2d, xb)

    dy, loss_part = _loss_head(xs, loss_target.reshape(t, d))
    loss = lax.psum(loss_part[0, 0], ("x", "y", "c"))

    scattered, small = [None] * depth, [None] * depth
    ici, token_ici = None, _no_token()

    def finish_scatter(l, ici, *after):
        scattered[l] = _exchange_wait("rs_ici_wait_%d" % l, ici, *after)

    def scatter_add_start(l, d2d, *after):
        done = _exchange_wait("rs_d2d_wait_%d" % l, d2d, *after)
        sums = [_pair_add(p, g, "rs_add_%d" % i) for i, (p, g) in enumerate(zip(done[:2], done[2:]))]
        lands = [_landing((3, rows_t, d), BF16), _landing((3, rows_o, d), BF16)]
        return _exchange_start("rs_ici_start_%d" % l, sums + lands, _plan_scatter_ici(2), 6, 0, after[0])

    for l in reversed(range(depth)):
        xb, h, sst, mix, z, wt, wo = saved[l]
        dz, dzb, d_lng, d_lnb = _ln_bwd(dy, z, ln_g[l][None], token_ici)
        dmix = _matmul(dzb, wo.reshape(d, d), dims=NT, tm=tm, tn=_tile(d, 512), tk=d, out_dtype=F32, name="mm_dmix")
        dwo = _matmul(mix, dzb, dims=TN, tm=_tile(d, 1024), tn=_tile(d, 512), tk=t, out_dtype=BF16, name="mm_dwo")
        dh, sm = _mixer_bwd(dm, h, sst, dmix, bias, layer_params(l), b_loc, nblk)
        (d_caw, d_ccw, d_ccb, d_grw, d_grb, d_giw, d_gib, d_lam, d_na, d_nb, d_nc, d_snk) = sm
        small[l] = dict(conv_a_w=d_caw, conv_c_w=d_ccw, conv_c_b=d_ccb[0], gate_r_w=d_grw, gate_r_b=d_grb[0], gate_i_w=d_giw,
                        gate_i_b=d_gib[0], rg_lambda=d_lam[0], norm_a=d_na[0], norm_b=d_nb[0], norm_c=d_nc[0],
                        sinks=d_snk[:, 0], ln_g=d_lng[0], ln_b=d_lnb[0])
        token = _no_token()
        if l == 0:
            g_local = _pack([jnp.stack([small[i][n] for i in range(depth)]) for n in SMALL])
            small_ici, token = _exchange_start("ag_small_ici_start", [g_local, _landing((N_DEV,) + g_local.shape, F32)],
                                               _plan_gather_ici(1), 4, 1, dh)
        dwt = _matmul(dh, xb, dims=TN, tm=_tile(dm.in_w, 1536), tn=_tile(d, 512), tk=t, out_dtype=BF16, name="mm_dwt",
                      token=token)
        if ici is not None:
            finish_scatter(l + 1, ici, dh)
        parts = [dwt.reshape(N_DEV, rows_t, d), dwo.reshape(N_DEV, rows_o, d)]
        lands = [_landing((4, rows_t, d), BF16), _landing((4, rows_o, d), BF16)]
        d2d, token = _exchange_start("rs_d2d_start_%d" % l, parts + lands, _plan_scatter_d2d(2), 8, 0, dh)
        if l > 0:
            dy = _matmul(dh, wt.reshape(dm.in_w, d), dims=NN, tm=tm, tn=_tile(d, 1024), tk=_tile(dm.in_w, 3584), out_dtype=F32,
                         name="mm_dx", addend=dz, alpha=alpha, token=token)
            ici, token_ici = scatter_add_start(l, d2d, dy)
        else:
            small_land = _exchange_wait("ag_small_ici_wait", small_ici, token)[1:]
            small_d2d, token = _exchange_start("ag_small_d2d_start", small_land, _plan_gather_d2d(1), 3, 0, token)
            ici, token_ici = scatter_add_start(l, d2d, token)
            dy = _matmul(dh, wt.reshape(dm.in_w, d), dims=NN, tm=tm, tn=_tile(d, 1024), tk=_tile(dm.in_w, 3584), out_dtype=F32,
                         name="mm_dx", addend=dz, alpha=alpha, token=token_ici)
            g_all = _exchange_wait("ag_small_d2d_wait", small_d2d, dy)[0]
    grad_x = dy.reshape(b_loc, seq, d)

    w_t, m_t, v_t = [jnp.swapaxes(a, 1, 2) for a in (w_in, m_w_in, v_w_in)]
    res_in, res_out = None, None
    for l in reversed(range(depth)):
        if l == 0:
            finish_scatter(0, ici, dy, *([res_in[0], res_out[0]] if depth > 1 else []))
        sums_t, sums_o, far_t, far_o = scattered[l]
        res_in = _adamw_scattered(w_t, m_t, v_t, l, sums_t, far_t, res_in, "adamw_in_%d" % l)
        res_out = _adamw_scattered(w_out, m_w_out, v_w_out, l, sums_o, far_o, res_out, "adamw_out_%d" % l)
    gw_in, dl_in, nm_in, nv_in = [jnp.swapaxes(a, 1, 2) for a in res_in]
    gw_out, dl_out, nm_out, nv_out = res_out

    given = dict(conv_a_w=(conv_a_w, m_conv_a_w, v_conv_a_w), conv_c_w=(conv_c_w, m_conv_c_w, v_conv_c_w),
                 conv_c_b=(conv_c_b, m_conv_c_b, v_conv_c_b), gate_r_w=(gate_r_w, m_gate_r_w, v_gate_r_w),
                 gate_r_b=(gate_r_b, m_gate_r_b, v_gate_r_b), gate_i_w=(gate_i_w, m_gate_i_w, v_gate_i_w),
                 gate_i_b=(gate_i_b, m_gate_i_b, v_gate_i_b), rg_lambda=(rg_lambda, m_rg_lambda, v_rg_lambda),
                 norm_a=(norm_a, m_norm_a, v_norm_a), norm_b=(norm_b, m_norm_b, v_norm_b), norm_c=(norm_c, m_norm_c, v_norm_c),
                 sinks=(sinks, m_sinks, v_sinks), ln_g=(ln_g, m_ln_g, v_ln_g), ln_b=(ln_b, m_ln_b, v_ln_b))
    full_shapes = [jnp.stack([small[l][n] for l in range(depth)]).shape for n in SMALL]

    def mine_of(n, a):
        if n in ("conv_a_w", "conv_c_w"):
            return lax.dynamic_update_slice(jnp.zeros(a.shape[:2] + (dm.wa,), F32), a, (0, 0, dev * ch))
        return a

    packs = [_pack([mine_of(n, given[n][i]) for n in SMALL]) for i in range(3)]
    outs = _adamw(packs[0], g_all, packs[1], packs[2], "adamw_small")
    res = {}
    for kind, packed in zip(("grad", "delta", "new_m", "new_v"), outs):
        for n, a in zip(SMALL, _unpack(packed, full_shapes)):
            if n in ("conv_a_w", "conv_c_w"):
                a = lax.dynamic_slice(a, (0, 0, dev * ch), a.shape[:2] + (ch,))
            res[kind, n] = a
    res.update({("grad", "w_in"): gw_in, ("delta", "w_in"): dl_in, ("new_m", "w_in"): nm_in, ("new_v", "w_in"): nv_in,
                ("grad", "w_out"): gw_out, ("delta", "w_out"): dl_out, ("new_m", "w_out"): nm_out, ("new_v", "w_out"): nv_out})
    order = ("w_in", "conv_a_w", "sinks", "conv_c_w", "conv_c_b", "gate_r_w", "gate_r_b", "gate_i_w", "gate_i_b", "rg_lambda",
             "norm_a", "norm_b", "norm_c", "w_out", "ln_g", "ln_b")
    return (loss, grad_x, *[res[kind, n] for kind in ("grad", "delta", "new_m", "new_v") for n in order])
```

```python
import functools

import jax
import jax.numpy as jnp
from jax import lax
from jax.experimental import pallas as pl
from jax.experimental.pallas import tpu as pltpu

F32 = jnp.float32
BF16 = jnp.bfloat16
MESH = pl.DeviceIdType.MESH
ANY = pl.BlockSpec(memory_space=pl.ANY)

N_DEV = 8
LANES = 128
SUBLANES = 8
HEAD_DIM = 64
KV_GROUP = 8
ATT_BLOCK = 128
ATT_STACK = 8
CONV_A = 3
CONV_C = 4
RG_C = 8.0
LN_EPS = 1e-5
RMS_EPS = 1e-6
NEG_INF = -1e30
ADAM_LR = 0.001
ADAM_B1 = 0.9
ADAM_B2 = 0.999
ADAM_EPS = 1e-08
ADAM_WD = 0.01
ADAM_STEP = 10
VMEM_LIMIT = 56 * 1024 * 1024

NN = ((1,), (0,))
NT = ((1,), (1,))
TN = ((0,), (0,))


def _pcall(body, **kw):
    return pl.pallas_call(body, **kw)


def _roll(x, shift, axis):
    return pltpu.roll(x, shift, axis)


def _params(sem=None, vmem=VMEM_LIMIT):
    return pltpu.CompilerParams(dimension_semantics=sem, vmem_limit_bytes=vmem)


def _dot(a, b, dims):
    return lax.dot_general(a.astype(BF16), b.astype(BF16), (dims, ((), ())), preferred_element_type=F32)


@jax.custom_vjp
def _mm(a, b):
    return _dot(a, b, NN)


def _mm_fwd(a, b):
    return _dot(a, b, NN), (a.astype(BF16), b.astype(BF16))


def _mm_bwd(res, g):
    a, b = res
    return _dot(g, b, NT), _dot(a, g, TN)


_mm.defvjp(_mm_fwd, _mm_bwd)


@jax.custom_vjp
def _mm_nt(a, b):
    return _dot(a, b, NT)


def _mm_nt_fwd(a, b):
    return _dot(a, b, NT), (a.astype(BF16), b.astype(BF16))


def _mm_nt_bwd(res, g):
    a, b = res
    return _dot(g, b, NN), _dot(g, a, TN)


_mm_nt.defvjp(_mm_nt_fwd, _mm_nt_bwd)


def _rows(shape):
    return lax.broadcasted_iota(jnp.int32, shape, 0)


@functools.partial(jax.custom_vjp, nondiff_argnums=(2,))
def _shift_halo(u, prev, k):
    r, c = u.shape
    fill = jnp.concatenate([_roll(prev, k, 0), jnp.zeros((r - SUBLANES, c), u.dtype)], axis=0)
    return jnp.where(_rows((r, c)) < k, fill, _roll(u, k, 0))


def _shift_halo_fwd(u, prev, k):
    return _shift_halo(u, prev, k), None


def _shift_halo_bwd(k, _, g):
    r, c = g.shape
    du = jnp.where(_rows((r, c)) < r - k, _roll(g, r - k, 0), 0.0)
    dprev = jnp.where(_rows((SUBLANES, c)) >= SUBLANES - k, _roll(g[0:SUBLANES], SUBLANES - k, 0), 0.0)
    return du, dprev


_shift_halo.defvjp(_shift_halo_fwd, _shift_halo_bwd)


@functools.partial(jax.custom_vjp, nondiff_argnums=(1, 2))
def _shift_fill(u, k, fill):
    return jnp.where(_rows(u.shape) < k, fill, _roll(u, k, 0))


def _shift_fill_fwd(u, k, fill):
    return _shift_fill(u, k, fill), None


def _shift_fill_bwd(k, fill, _, g):
    r = g.shape[0]
    return (jnp.where(_rows(g.shape) < r - k, _roll(g, r - k, 0), 0.0),)


_shift_fill.defvjp(_shift_fill_fwd, _shift_fill_bwd)


@jax.custom_vjp
def _swap_halves(x):
    return _roll(x, HEAD_DIM, 1)


_swap_halves.defvjp(lambda x: (_roll(x, HEAD_DIM, 1), None), lambda _, g: (_roll(g, HEAD_DIM, 1),))


@functools.partial(jax.custom_vjp, nondiff_argnums=(1,))
def _split_rows(x, n):
    r = x.shape[0] // n
    return tuple(x[i * r:(i + 1) * r] for i in range(n))


def _split_rows_fwd(x, n):
    return _split_rows(x, n), None


def _split_rows_bwd(n, _, gs):
    return (jnp.concatenate(list(gs), axis=0),)


_split_rows.defvjp(_split_rows_fwd, _split_rows_bwd)


def _logistic(x):
    return 1.0 / (1.0 + jnp.exp(-x))


@jax.custom_vjp
def _sigmoid(x):
    return _logistic(x)


def _sigmoid_fwd(x):
    s = _logistic(x)
    return s, s


_sigmoid.defvjp(_sigmoid_fwd, lambda s, g: (g * s * (1.0 - s),))


@jax.custom_vjp
def _silu(x):
    return x * _logistic(x)


def _silu_fwd(x):
    s = _logistic(x)
    return x * s, (x, s)


_silu.defvjp(_silu_fwd, lambda res, g: (g * res[1] * (1.0 + res[0] * (1.0 - res[1])),))


def _log_sigmoid(x):
    return -(jnp.maximum(-x, 0.0) + jnp.log1p(jnp.exp(-jnp.abs(x))))


@jax.custom_vjp
def _neg_expm1(x):
    series = x * (1 + x * (1 / 2) * (1 + x * (1 / 3) * (1 + x * (1 / 4) * (1 + x * (1 / 5) * (1 + x * (1 / 6) * (1 + x * (1 / 7)))))))
    return -jnp.where(jnp.abs(x) < 0.25, series, jnp.exp(x) - 1.0)


_neg_expm1.defvjp(lambda x: (_neg_expm1(x), x), lambda x, g: (-g * jnp.exp(x),))


def _shift_up(x, k, fill):
    r = x.shape[0]
    return jnp.where(_rows(x.shape) < r - k, _roll(x, r - k, 0), fill)


@jax.custom_vjp
def _scan_block(a, u, s_prev):
    acc_a, acc_b = a, u
    d = 1
    while d < a.shape[0]:
        acc_b = acc_a * _shift_fill(acc_b, d, 0.0) + acc_b
        acc_a = acc_a * _shift_fill(acc_a, d, 1.0)
        d *= 2
    return acc_a * s_prev + acc_b


def _scan_block_fwd(a, u, s_prev):
    h = _scan_block(a, u, s_prev)
    return h, (a, h, s_prev)


def _scan_block_bwd(res, dh):
    a, h, s_prev = res
    acc_a, acc_g = _shift_up(a, 1, 0.0), dh
    d = 1
    while d < a.shape[0]:
        acc_g = acc_a * _shift_up(acc_g, d, 0.0) + acc_g
        acc_a = acc_a * _shift_up(acc_a, d, 1.0)
        d *= 2
    h_prev = jnp.where(_rows(h.shape) < 1, s_prev, _roll(h, 1, 0))
    first = jnp.sum(jnp.where(_rows(h.shape) < 1, a * acc_g, 0.0), axis=0, keepdims=True)
    return acc_g * h_prev, acc_g, first


_scan_block.defvjp(_scan_block_fwd, _scan_block_bwd)


def _last_row(h):
    return jnp.sum(jnp.where(_rows(h.shape) == h.shape[0] - 1, h, 0.0), axis=0, keepdims=True)


def _branch_a(ab, ac, ax, ag, acp, axp, w0, w1, w2, na):
    u = ac * ax
    up = acp * axp
    ya = ab * (w2 * u + w1 * _shift_halo(u, up, 1) + w0 * _shift_halo(u, up, 2))
    ms = jnp.sum(ya * ya, axis=1, keepdims=True) * (1.0 / ya.shape[1])
    return ya * lax.rsqrt(ms + RMS_EPS) * na * _silu(ag)


def _branch_c(cx, cg, cxp, sp, wc, bc, wr, br, wi, bi, lam, nc):
    hs, lasts = [], []
    for j in range(len(cx)):
        xc = (wc[3][j] * cx[j] + wc[2][j] * _shift_halo(cx[j], cxp[j], 1) + wc[1][j] * _shift_halo(cx[j], cxp[j], 2)
              + wc[0][j] * _shift_halo(cx[j], cxp[j], 3) + bc[j])
        r = _sigmoid(_mm(xc, wr[j]) + br[j])
        i = _sigmoid(_mm(xc, wi[j]) + bi[j])
        log_a = RG_C * r * _log_sigmoid(lam[j])
        a = jnp.exp(log_a)
        u = jnp.sqrt(_neg_expm1(2.0 * log_a)) * (i * xc)
        h = _scan_block(a, u, sp[j])
        hs.append(h)
        lasts.append(_last_row(h))
    width = LANES * len(cx)
    ms = sum(jnp.sum(h * h, axis=1, keepdims=True) for h in hs) * (1.0 / width)
    inv = lax.rsqrt(ms + RMS_EPS)
    return [hs[j] * inv * nc[j] * _silu(cg[j]) for j in range(len(cx))], lasts


def _attention_bias():
    qi = (jnp.arange(ATT_STACK * ATT_BLOCK) % ATT_BLOCK)[:, None]
    kj = jnp.arange(2 * ATT_BLOCK)[None, :]
    dist = qi + ATT_BLOCK - kj
    band = (dist >= 0) & (dist < ATT_BLOCK)
    return jnp.where(jnp.stack([band & (kj >= ATT_BLOCK), band]), 0.0, NEG_INF).astype(F32)


def _branch_b(q, k, v, kp, vp, bg, snk, nb, bias):
    rows = ATT_BLOCK
    n_kv = 2 * len(k)
    lane0 = lax.broadcasted_iota(jnp.int32, (ATT_STACK * rows, LANES), 1) == 0
    upper = lax.broadcasted_iota(jnp.int32, (2 * rows, LANES), 1) >= HEAD_DIM
    heads = [None] * (n_kv * KV_GROUP)
    for g in range(n_kv):
        half = g % 2
        keep = upper if half else jnp.logical_not(upper)
        kc = jnp.where(keep, jnp.concatenate([kp[g // 2], k[g // 2]], axis=0), 0.0)
        vc = jnp.where(keep, jnp.concatenate([vp[g // 2], v[g // 2]], axis=0), 0.0)
        for first in range(g * KV_GROUP, (g + 1) * KV_GROUP, ATT_STACK):
            hs = range(first, first + ATT_STACK)
            qg = jnp.concatenate([q[h // 2] if h % 2 == half else _swap_halves(q[h // 2]) for h in hs], axis=0)
            s = _mm_nt(qg * (HEAD_DIM ** -0.5), kc) + bias
            sink = jnp.concatenate([jnp.broadcast_to(snk[h], (rows, LANES)) for h in hs], axis=0)
            sink = jnp.sum(jnp.where(lane0, sink, 0.0), axis=1, keepdims=True)
            m = lax.stop_gradient(jnp.maximum(jnp.max(s, axis=1, keepdims=True), sink))
            p = jnp.exp(s - m)
            inv = 1.0 / (jnp.sum(p, axis=1, keepdims=True) + jnp.exp(sink - m))
            o = _split_rows(_mm(p * inv, vc), ATT_STACK)
            for i, h in enumerate(hs):
                heads[h] = o[i] if h % 2 == half else _swap_halves(o[i])
    yb = [heads[2 * j] + heads[2 * j + 1] for j in range(len(q))]
    width = LANES * len(q)
    ms = sum(jnp.sum(y * y, axis=1, keepdims=True) for y in yb) * (1.0 / width)
    inv = lax.rsqrt(ms + RMS_EPS)
    return [yb[j] * inv * nb[j] * _silu(bg[j]) for j in range(len(q))]


class _Dims:
    def __init__(self, d_model, n_rg_heads):
        self.d = d_model
        self.wa = d_model // 4
        self.wb = d_model // 2
        self.wc = d_model // 4
        self.kvw = self.wb // KV_GROUP
        self.nq = self.wb // HEAD_DIM
        self.in_w = 4 * self.wa + 2 * self.wb + 2 * self.kvw + 2 * self.wc
        self.o_q = 4 * self.wa
        self.o_k = self.o_q + self.wb
        self.o_v = self.o_k + self.kvw
        self.o_bg = self.o_v + self.kvw
        self.o_cx = self.o_bg + self.wb
        self.o_cg = self.o_cx + self.wc
        self.nh = n_rg_heads
        assert self.wc // n_rg_heads == LANES and self.kvw % LANES == 0
        assert self.o_k % self.kvw == 0 and self.o_cx % (self.wc // 2) == 0


def _chunks(ref, rows, off, width):
    return [ref[rows, off + LANES * j: off + LANES * (j + 1)] for j in range(width // LANES)]


def _read_params(dm, caw, ccw, ccb, grw, grb, giw, gib, lam, na, nb, nc, snk):
    row = slice(0, 1)
    return dict(
        wa=[caw[k:k + 1, :] for k in range(CONV_A)], na=na[...],
        wc=[_chunks(ccw, slice(k, k + 1), 0, dm.wc) for k in range(CONV_C)], bc=_chunks(ccb, row, 0, dm.wc),
        wr=[grw[j] for j in range(dm.nh)], br=_chunks(grb, row, 0, dm.wc),
        wi=[giw[j] for j in range(dm.nh)], bi=_chunks(gib, row, 0, dm.wc),
        lam=_chunks(lam, row, 0, dm.wc), nc=_chunks(nc, row, 0, dm.wc),
        nb=_chunks(nb, row, 0, dm.wb), snk=[snk[h:h + 1, :] for h in range(dm.nq)])


def _param_specs(dm):
    shapes = [(CONV_A, dm.wa), (CONV_C, dm.wc), (1, dm.wc), (dm.nh, LANES, LANES), (1, dm.wc), (dm.nh, LANES, LANES),
              (1, dm.wc), (1, dm.wc), (1, dm.wa), (1, dm.wb), (1, dm.wc), (dm.nq, LANES)]
    specs = [pl.BlockSpec(s, (lambda b, n, _r=len(s): (0,) * _r)) for s in shapes]
    return shapes, specs


def _bias_spec():
    return pl.BlockSpec((2, ATT_STACK * ATT_BLOCK, 2 * ATT_BLOCK), lambda *_: (0, 0, 0))


def _mixer_fwd(dm, h, bias, prm, b_loc, nblk):
    t = h.shape[0]
    r = ATT_BLOCK
    tail = slice(r - SUBLANES, r)

    def body(h_ref, bias_ref, caw, ccw, ccb, grw, grb, giw, gib, lam, na, nb, nc, snk, mix_ref, sst_ref, kp, vp, acp, axp, cxp, sp):
        n = pl.program_id(1)

        @pl.when(n == 0)
        def _():
            for ref in (kp, vp, acp, axp, cxp, sp):
                ref[...] = jnp.zeros(ref.shape, ref.dtype)

        p = _read_params(dm, caw, ccw, ccb, grw, grb, giw, gib, lam, na, nb, nc, snk)
        full = slice(None)
        mix_a = _branch_a(h_ref[:, 0:dm.wa], h_ref[:, dm.wa:2 * dm.wa], h_ref[:, 2 * dm.wa:3 * dm.wa],
                          h_ref[:, 3 * dm.wa:4 * dm.wa], acp[...], axp[...], p["wa"][0], p["wa"][1], p["wa"][2], p["na"])
        mix_ref[:, 0:dm.wa] = mix_a.astype(BF16)
        bias = bias_ref[jnp.where(n == 0, 0, 1)]
        mix_b = _branch_b(_chunks(h_ref, full, dm.o_q, dm.wb), _chunks(h_ref, full, dm.o_k, dm.kvw),
                          _chunks(h_ref, full, dm.o_v, dm.kvw), _chunks(kp, full, 0, dm.kvw), _chunks(vp, full, 0, dm.kvw),
                          _chunks(h_ref, full, dm.o_bg, dm.wb), p["snk"], p["nb"], bias)
        for j, mb in enumerate(mix_b):
            mix_ref[:, dm.wa + LANES * j: dm.wa + LANES * (j + 1)] = mb.astype(BF16)
        sst_ref[0] = sp[...]
        mix_c, lasts = _branch_c(_chunks(h_ref, full, dm.o_cx, dm.wc), _chunks(h_ref, full, dm.o_cg, dm.wc),
                                 _chunks(cxp, full, 0, dm.wc), _chunks(sp, slice(0, 1), 0, dm.wc), p["wc"], p["bc"],
                                 p["wr"], p["br"], p["wi"], p["bi"], p["lam"], p["nc"])
        o_c = dm.wa + dm.wb
        for j, mc in enumerate(mix_c):
            mix_ref[:, o_c + LANES * j: o_c + LANES * (j + 1)] = mc.astype(BF16)
            sp[:, LANES * j: LANES * (j + 1)] = jnp.broadcast_to(lasts[j], (SUBLANES, LANES))
        kp[...] = h_ref[:, dm.o_k:dm.o_k + dm.kvw]
        vp[...] = h_ref[:, dm.o_v:dm.o_v + dm.kvw]
        acp[...] = h_ref[tail, dm.wa:2 * dm.wa]
        axp[...] = h_ref[tail, 2 * dm.wa:3 * dm.wa]
        cxp[...] = h_ref[tail, dm.o_cx:dm.o_cx + dm.wc]

    _, pspecs = _param_specs(dm)
    return _pcall(
        body, name="mixer_fwd", grid=(b_loc, nblk),
        in_specs=[pl.BlockSpec((r, dm.in_w), lambda b, n: (b * nblk + n, 0)), _bias_spec()] + pspecs,
        out_specs=[pl.BlockSpec((r, dm.d), lambda b, n: (b * nblk + n, 0)),
                   pl.BlockSpec((1, SUBLANES, dm.wc), lambda b, n: (b * nblk + n, 0, 0))],
        out_shape=[jax.ShapeDtypeStruct((t, dm.d), BF16), jax.ShapeDtypeStruct((b_loc * nblk, SUBLANES, dm.wc), F32)],
        scratch_shapes=[pltpu.VMEM((r, dm.kvw), F32), pltpu.VMEM((r, dm.kvw), F32), pltpu.VMEM((SUBLANES, dm.wa), F32),
                        pltpu.VMEM((SUBLANES, dm.wa), F32), pltpu.VMEM((SUBLANES, dm.wc), F32),
                        pltpu.VMEM((SUBLANES, dm.wc), F32)],
        compiler_params=_params(("arbitrary", "arbitrary")),
    )(h, bias, *prm)


def _mixer_bwd(dm, h, sst, dmix, bias, prm, b_loc, nblk):
    t = h.shape[0]
    r = ATT_BLOCK
    rb8 = r // SUBLANES
    n_small = 12

    def body(h_ref, kp_ref, vp_ref, acp_ref, axp_ref, cxp0_ref, cxp1_ref, sst_ref, dmix_ref, bias_ref,
             caw, ccw, ccb, grw, grb, giw, gib, lam, na, nb, nc, snk,
             dh_ref, d_caw, d_ccw, d_ccb, d_grw, d_grb, d_giw, d_gib, d_lam, d_na, d_nb, d_nc, d_snk,
             dkp, dvp, dacp, daxp, dcxp, dsp):
        step = pl.program_id(1)
        n = nblk - 1 - step

        @pl.when(step == 0)
        def _():
            for ref in (dkp, dvp, dacp, daxp, dcxp, dsp):
                ref[...] = jnp.zeros(ref.shape, ref.dtype)

        @pl.when((step == 0) & (pl.program_id(0) == 0))
        def _():
            for ref in (d_caw, d_ccw, d_ccb, d_grw, d_grb, d_giw, d_gib, d_lam, d_na, d_nb, d_nc, d_snk):
                ref[...] = jnp.zeros(ref.shape, ref.dtype)

        p = _read_params(dm, caw, ccw, ccb, grw, grb, giw, gib, lam, na, nb, nc, snk)
        has_prev = jnp.where(n > 0, 1.0, 0.0)
        full = slice(None)
        pad = jnp.zeros((r - SUBLANES, LANES), F32)

        def with_tail(own, carry):
            z = jnp.zeros((r - SUBLANES, own.shape[1]), F32)
            return own + jnp.concatenate([z, carry], axis=0)

        a_in = (h_ref[:, 0:dm.wa], h_ref[:, dm.wa:2 * dm.wa], h_ref[:, 2 * dm.wa:3 * dm.wa], h_ref[:, 3 * dm.wa:4 * dm.wa],
                acp_ref[...] * has_prev, axp_ref[...] * has_prev, p["wa"][0], p["wa"][1], p["wa"][2], p["na"])
        _, vjp_a = jax.vjp(_branch_a, *a_in)
        g_ab, g_ac, g_ax, g_ag, g_acp, g_axp, g_w0, g_w1, g_w2, g_na = vjp_a(dmix_ref[:, 0:dm.wa])
        dh_ref[:, 0:dm.wa] = g_ab.astype(BF16)
        dh_ref[:, dm.wa:2 * dm.wa] = with_tail(g_ac, dacp[...]).astype(BF16)
        dh_ref[:, 2 * dm.wa:3 * dm.wa] = with_tail(g_ax, daxp[...]).astype(BF16)
        dh_ref[:, 3 * dm.wa:4 * dm.wa] = g_ag.astype(BF16)
        dacp[...] = g_acp
        daxp[...] = g_axp
        for k, gw in enumerate((g_w0, g_w1, g_w2)):
            d_caw[k:k + 1, :] += gw
        d_na[...] += g_na

        bias = bias_ref[jnp.where(n == 0, 0, 1)]
        kp_in = [c * has_prev for c in _chunks(kp_ref, full, 0, dm.kvw)]
        vp_in = [c * has_prev for c in _chunks(vp_ref, full, 0, dm.kvw)]
        b_in = (_chunks(h_ref, full, dm.o_q, dm.wb), _chunks(h_ref, full, dm.o_k, dm.kvw), _chunks(h_ref, full, dm.o_v, dm.kvw),
                kp_in, vp_in, _chunks(h_ref, full, dm.o_bg, dm.wb), p["snk"], p["nb"])
        _, vjp_b = jax.vjp(lambda *a: _branch_b(*a, bias), *b_in)
        g_q, g_k, g_v, g_kp, g_vp, g_bg, g_snk, g_nb = vjp_b(_chunks(dmix_ref, full, dm.wa, dm.wb))
        for j in range(len(g_q)):
            dh_ref[:, dm.o_q + LANES * j: dm.o_q + LANES * (j + 1)] = g_q[j].astype(BF16)
            dh_ref[:, dm.o_bg + LANES * j: dm.o_bg + LANES * (j + 1)] = g_bg[j].astype(BF16)
            d_nb[:, LANES * j: LANES * (j + 1)] += g_nb[j]
        for j in range(len(g_k)):
            cols = slice(LANES * j, LANES * (j + 1))
            dh_ref[:, dm.o_k + LANES * j: dm.o_k + LANES * (j + 1)] = (g_k[j] + dkp[:, cols]).astype(BF16)
            dh_ref[:, dm.o_v + LANES * j: dm.o_v + LANES * (j + 1)] = (g_v[j] + dvp[:, cols]).astype(BF16)
            dkp[:, cols] = g_kp[j]
            dvp[:, cols] = g_vp[j]
        for hd in range(dm.nq):
            d_snk[hd:hd + 1, :] += g_snk[hd]

        half_c = dm.wc // 2
        cxp_in = ([c * has_prev for c in _chunks(cxp0_ref, full, 0, half_c)]
                  + [c * has_prev for c in _chunks(cxp1_ref, full, 0, half_c)])
        c_in = (_chunks(h_ref, full, dm.o_cx, dm.wc), _chunks(h_ref, full, dm.o_cg, dm.wc), cxp_in,
                [sst_ref[0, 0:1, LANES * j: LANES * (j + 1)] for j in range(dm.nh)], p["wc"], p["bc"], p["wr"], p["br"], p["wi"], p["bi"],
                p["lam"], p["nc"])
        _, vjp_c = jax.vjp(_branch_c, *c_in)
        ct_last = [dsp[0:1, LANES * j: LANES * (j + 1)] for j in range(dm.nh)]
        g_cx, g_cg, g_cxp, g_sp, g_wc, g_bc, g_wr, g_br, g_wi, g_bi, g_lam, g_nc = vjp_c(
            (_chunks(dmix_ref, full, dm.wa + dm.wb, dm.wc), ct_last))
        for j in range(dm.nh):
            cols = slice(LANES * j, LANES * (j + 1))
            tot = g_cx[j] + jnp.concatenate([pad, dcxp[:, cols]], axis=0)
            dh_ref[:, dm.o_cx + LANES * j: dm.o_cx + LANES * (j + 1)] = tot.astype(BF16)
            dh_ref[:, dm.o_cg + LANES * j: dm.o_cg + LANES * (j + 1)] = g_cg[j].astype(BF16)
            dcxp[:, cols] = g_cxp[j]
            dsp[:, cols] = jnp.broadcast_to(g_sp[j], (SUBLANES, LANES))
            for k in range(CONV_C):
                d_ccw[k:k + 1, cols] += g_wc[k][j]
            d_ccb[:, cols] += g_bc[j]
            d_grw[j] += g_wr[j]
            d_grb[:, cols] += g_br[j]
            d_giw[j] += g_wi[j]
            d_gib[:, cols] += g_bi[j]
            d_lam[:, cols] += g_lam[j]
            d_nc[:, cols] += g_nc[j]

    def blk(b, s):
        return b * nblk + (nblk - 1 - s)

    def prev_rows8(b, s):
        return jnp.maximum(blk(b, s) * rb8 - 1, 0)

    pshapes, pspecs = _param_specs(dm)
    half_c = dm.wc // 2
    in_specs = [
        pl.BlockSpec((r, dm.in_w), lambda b, s: (blk(b, s), 0)),
        pl.BlockSpec((r, dm.kvw), lambda b, s: (jnp.maximum(blk(b, s) - 1, 0), dm.o_k // dm.kvw)),
        pl.BlockSpec((r, dm.kvw), lambda b, s: (jnp.maximum(blk(b, s) - 1, 0), dm.o_v // dm.kvw)),
        pl.BlockSpec((SUBLANES, dm.wa), lambda b, s: (prev_rows8(b, s), 1)),
        pl.BlockSpec((SUBLANES, dm.wa), lambda b, s: (prev_rows8(b, s), 2)),
        pl.BlockSpec((SUBLANES, half_c), lambda b, s: (prev_rows8(b, s), dm.o_cx // half_c)),
        pl.BlockSpec((SUBLANES, half_c), lambda b, s: (prev_rows8(b, s), dm.o_cx // half_c + 1)),
        pl.BlockSpec((1, SUBLANES, dm.wc), lambda b, s: (blk(b, s), 0, 0)),
        pl.BlockSpec((r, dm.d), lambda b, s: (blk(b, s), 0)),
        _bias_spec(),
    ] + pspecs
    outs = _pcall(
        body, name="mixer_bwd", grid=(b_loc, nblk), in_specs=in_specs,
        out_specs=[pl.BlockSpec((r, dm.in_w), lambda b, s: (blk(b, s), 0))] + pspecs,
        out_shape=[jax.ShapeDtypeStruct((t, dm.in_w), BF16)] + [jax.ShapeDtypeStruct(s, F32) for s in pshapes],
        scratch_shapes=[pltpu.VMEM((r, dm.kvw), F32), pltpu.VMEM((r, dm.kvw), F32), pltpu.VMEM((SUBLANES, dm.wa), F32),
                        pltpu.VMEM((SUBLANES, dm.wa), F32), pltpu.VMEM((SUBLANES, dm.wc), F32),
                        pltpu.VMEM((SUBLANES, dm.wc), F32)],
        compiler_params=_params(("arbitrary", "arbitrary")),
    )(h, h, h, h, h, h, h, sst, dmix, bias, *prm)
    assert len(outs) == 1 + n_small
    return outs[0], outs[1:]


def _token_spec():
    return pl.BlockSpec((SUBLANES, LANES), lambda *_: (0, 0))


def _no_token():
    return jnp.zeros((SUBLANES, LANES), F32)


def _matmul(a, b, *, dims, tm, tn, tk, out_dtype, name, addend=None, alpha=None, token=None):
    if dims == TN:
        (k_dim, m), n_dim = a.shape, b.shape[1]
        a_spec = pl.BlockSpec((tk, tm), lambda i, j, k: (k, i))
    else:
        (m, k_dim), n_dim = a.shape, (b.shape[0] if dims == NT else b.shape[1])
        a_spec = pl.BlockSpec((tm, tk), lambda i, j, k: (i, k))
    b_spec = pl.BlockSpec((tn, tk), lambda i, j, k: (j, k)) if dims == NT else pl.BlockSpec((tk, tn), lambda i, j, k: (k, j))
    assert m % tm == 0 and n_dim % tn == 0 and k_dim % tk == 0, (a.shape, b.shape, tm, tn, tk)
    nk = k_dim // tk
    o_spec = pl.BlockSpec((tm, tn), lambda i, j, k: (i, j))

    def body(*refs):
        a_ref, b_ref = refs[0], refs[1]
        add_ref = refs[2] if addend is not None else None
        o_ref, acc_ref = refs[-2], refs[-1]
        k = pl.program_id(2)
        part = lax.dot_general(a_ref[...], b_ref[...], (dims, ((), ())), preferred_element_type=F32)

        def finish(acc):
            if add_ref is not None:
                acc = acc + alpha * add_ref[...]
            o_ref[...] = acc.astype(out_dtype)

        if nk == 1:
            finish(part)
        else:
            @pl.when(k == 0)
            def _():
                acc_ref[...] = part

            @pl.when((k > 0) & (k < nk - 1))
            def _():
                acc_ref[...] += part

            @pl.when(k == nk - 1)
            def _():
                finish(acc_ref[...] + part)

    ins = [a, b] + ([addend] if addend is not None else []) + ([token] if token is not None else [])
    in_specs = [a_spec, b_spec] + ([o_spec] if addend is not None else []) + ([_token_spec()] if token is not None else [])
    return _pcall(
        body, name=name, grid=(m // tm, n_dim // tn, nk), in_specs=in_specs, out_specs=o_spec,
        out_shape=jax.ShapeDtypeStruct((m, n_dim), out_dtype),
        scratch_shapes=[pltpu.VMEM((tm, tn) if nk > 1 else (SUBLANES, LANES), F32)],
        compiler_params=_params(("parallel", "parallel", "arbitrary")),
    )(*ins)


def _tile(n, want, quantum=LANES):
    if n <= want:
        return n
    for cand in range(want - want % quantum, 0, -quantum):
        if n % cand == 0:
            return cand
    return n


def _row_tile(t, d, elems=1 << 19):
    return _tile(t, max(2 * SUBLANES, elems // d), 2 * SUBLANES)


STREAM_ELEMS = 1 << 20


def _ln_fwd(z, g, b, token):
    t, d = z.shape
    tr = _row_tile(t, d, STREAM_ELEMS)

    def body(z_ref, g_ref, b_ref, _, y_ref, yb_ref):
        zz = z_ref[...]
        mu = jnp.mean(zz, axis=1, keepdims=True)
        zc = zz - mu
        var = jnp.mean(zc * zc, axis=1, keepdims=True)
        y = zc * lax.rsqrt(var + LN_EPS) * g_ref[...] + b_ref[...]
        y_ref[...] = y
        yb_ref[...] = y.astype(BF16)

    row = pl.BlockSpec((tr, d), lambda i: (i, 0))
    vec = pl.BlockSpec((1, d), lambda i: (0, 0))
    return _pcall(body, name="ln_fwd", grid=(t // tr,), in_specs=[row, vec, vec, _token_spec()], out_specs=[row, row],
                  out_shape=[jax.ShapeDtypeStruct((t, d), F32), jax.ShapeDtypeStruct((t, d), BF16)],
                  compiler_params=_params(("parallel",)))(z, g, b, token)


def _ln_bwd(dy, z, g, token):
    t, d = z.shape
    tr = _row_tile(t, d, STREAM_ELEMS)

    def body(dy_ref, z_ref, g_ref, _, dz_ref, dzb_ref, dg_ref, db_ref):
        @pl.when(pl.program_id(0) == 0)
        def _():
            dg_ref[...] = jnp.zeros(dg_ref.shape, F32)
            db_ref[...] = jnp.zeros(db_ref.shape, F32)

        zz = z_ref[...]
        dyy = dy_ref[...]
        mu = jnp.mean(zz, axis=1, keepdims=True)
        zc = zz - mu
        rstd = lax.rsqrt(jnp.mean(zc * zc, axis=1, keepdims=True) + LN_EPS)
        xhat = zc * rstd
        dyg = dyy * g_ref[...]
        dz = rstd * (dyg - jnp.mean(dyg, axis=1, keepdims=True) - xhat * jnp.mean(dyg * xhat, axis=1, keepdims=True))
        dz_ref[...] = dz
        dzb_ref[...] = dz.astype(BF16)
        dg_ref[...] += jnp.sum(dyy * xhat, axis=0, keepdims=True)
        db_ref[...] += jnp.sum(dyy, axis=0, keepdims=True)

    row = pl.BlockSpec((tr, d), lambda i: (i, 0))
    vec = pl.BlockSpec((1, d), lambda i: (0, 0))
    return _pcall(body, name="ln_bwd", grid=(t // tr,), in_specs=[row, row, vec, _token_spec()], out_specs=[row, row, vec, vec],
                  out_shape=[jax.ShapeDtypeStruct((t, d), F32), jax.ShapeDtypeStruct((t, d), BF16),
                             jax.ShapeDtypeStruct((1, d), F32), jax.ShapeDtypeStruct((1, d), F32)],
                  compiler_params=_params(("arbitrary",)))(dy, z, g, token)


def _loss_head(y, target):
    t, d = y.shape
    tr = _row_tile(t, d, STREAM_ELEMS)

    def body(y_ref, t_ref, dy_ref, loss_ref):
        @pl.when(pl.program_id(0) == 0)
        def _():
            loss_ref[...] = jnp.zeros(loss_ref.shape, F32)

        err = y_ref[...] - t_ref[...]
        dy_ref[...] = err * (1.0 / d)
        per_token = jnp.sum(err * err, axis=1, keepdims=True) * (1.0 / d)
        loss_ref[...] += 0.5 * jnp.sum(per_token, axis=0, keepdims=True)

    row = pl.BlockSpec((tr, d), lambda i: (i, 0))
    one = pl.BlockSpec((1, 1), lambda i: (0, 0))
    return _pcall(body, name="loss_head", grid=(t // tr,), in_specs=[row, row], out_specs=[row, one],
                  out_shape=[jax.ShapeDtypeStruct((t, d), F32), jax.ShapeDtypeStruct((1, 1), F32)],
                  compiler_params=_params(("arbitrary",)))(y, target)


def _adamw_scattered(w, m, v, layer, sums, far, prev, name):
    n_layers, r, c = w.shape
    tr = _row_tile(r, c)

    def body(*refs):
        w_ref, m_ref, v_ref, s_ref, f_ref = refs[:5]
        go_ref, d_ref, mo_ref, vo_ref = refs[-4:]
        gg = s_ref[...].astype(F32)
        for i in range(3):
            gg = gg + f_ref[i:i + 1].astype(F32)
        m_new = ADAM_B1 * m_ref[...] + (1.0 - ADAM_B1) * gg
        v_new = ADAM_B2 * v_ref[...] + (1.0 - ADAM_B2) * (gg * gg)
        m_hat = m_new / (1.0 - ADAM_B1 ** ADAM_STEP)
        v_hat = v_new / (1.0 - ADAM_B2 ** ADAM_STEP)
        go_ref[...] = gg
        d_ref[...] = -ADAM_LR * (m_hat / (jnp.sqrt(v_hat) + ADAM_EPS) + ADAM_WD * w_ref[...])
        mo_ref[...] = m_new
        vo_ref[...] = v_new

    own = pl.BlockSpec((1, tr, c), lambda i: (layer, i, 0))
    in_specs = [own, own, own, pl.BlockSpec((1, tr, c), lambda i: (2 * lax.axis_index("x") + lax.axis_index("y"), i, 0)),
                pl.BlockSpec((3, tr, c), lambda i: (0, i, 0))] + ([ANY] * 4 if prev is not None else [])
    return _pcall(body, name=name, grid=(r // tr,), in_specs=in_specs, out_specs=[own] * 4,
                  out_shape=[jax.ShapeDtypeStruct((n_layers, r, c), F32)] * 4,
                  input_output_aliases={5 + i: i for i in range(4)} if prev is not None else {},
                  compiler_params=_params(("parallel",)))(w, m, v, sums, far, *(prev if prev is not None else []))


def _pair_add(a, b, name):
    p, r, c = b.shape
    tr = _row_tile(r, c, STREAM_ELEMS)

    def body(a_ref, b_ref, o_ref):
        o_ref[...] = (a_ref[...].astype(F32) + b_ref[...].astype(F32)).astype(BF16)

    spec = pl.BlockSpec((1, tr, c), lambda q, i: (q, i, 0))
    return _pcall(body, name=name, grid=(p, r // tr),
                  in_specs=[pl.BlockSpec((1, tr, c), lambda q, i: (2 * q + lax.axis_index("c"), i, 0)), spec], out_specs=spec,
                  out_shape=jax.ShapeDtypeStruct((p, r, c), BF16), compiler_params=_params(("parallel", "parallel")))(a, b)


def _adamw(w, g_parts, m, v, name):
    r, c = w.shape
    n_parts = g_parts.shape[0]
    tr = _row_tile(r, c) if r % SUBLANES == 0 else r

    def body(w_ref, g_ref, m_ref, v_ref, go_ref, d_ref, mo_ref, vo_ref):
        g = g_ref[0].astype(F32)
        for i in range(1, n_parts):
            g = g + g_ref[i].astype(F32)
        m_new = ADAM_B1 * m_ref[...] + (1.0 - ADAM_B1) * g
        v_new = ADAM_B2 * v_ref[...] + (1.0 - ADAM_B2) * (g * g)
        m_hat = m_new / (1.0 - ADAM_B1 ** ADAM_STEP)
        v_hat = v_new / (1.0 - ADAM_B2 ** ADAM_STEP)
        go_ref[...] = g
        d_ref[...] = -ADAM_LR * (m_hat / (jnp.sqrt(v_hat) + ADAM_EPS) + ADAM_WD * w_ref[...])
        mo_ref[...] = m_new
        vo_ref[...] = v_new

    spec = pl.BlockSpec((tr, c), lambda i: (i, 0))
    shape = jax.ShapeDtypeStruct((r, c), F32)
    return _pcall(body, name=name, grid=(r // tr,),
                  in_specs=[spec, pl.BlockSpec((n_parts, tr, c), lambda i: (0, i, 0)), spec, spec],
                  out_specs=[spec] * 4, out_shape=[shape] * 4, compiler_params=_params(("parallel",)))(w, g_parts, m, v)


def _me():
    return lax.axis_index("x"), lax.axis_index("y"), lax.axis_index("c")


def _dev(px, py, pc):
    return 4 * px + 2 * py + pc


def _remote(src, dst, send_sems, recv_sems, k, to):
    return pltpu.make_async_remote_copy(src_ref=src, dst_ref=dst, send_sem=send_sems.at[k], recv_sem=recv_sems.at[k],
                                        device_id=to, device_id_type=MESH)


def _all_gather(arrs, name):
    n = len(arrs)

    def body(*refs):
        ins, outs = refs[:n], refs[n:2 * n]
        send_sems, recv_sems, local_sems = refs[2 * n:]
        x, y, c = _me()
        me, sibling = (x, y, c), (x, y, 1 - c)
        first = ((x + 1 - c) % 2, (y + c) % 2)
        second = ((x + c) % 2, (y + 1 - c) % 2)
        chips = [first, second, (1 - x, 1 - y)]
        pending = []
        for a in range(n):
            mine = pltpu.make_async_copy(ins[a], outs[a].at[_dev(*me)], local_sems.at[a])
            mine.start()
            pending.append(mine)
        sends = []
        for a in range(n):
            dst = outs[a].at[_dev(*me)]
            sends.append(_remote(ins[a], dst, send_sems, recv_sems, 7 * a, sibling))
            sends += [_remote(ins[a], dst, send_sems, recv_sems, 7 * a + 1 + j, (*chip, c)) for j, chip in enumerate(chips[:2])]
        for cp in sends:
            cp.start()
        for a in range(n):
            for j, chip in enumerate(chips):
                blk = outs[a].at[_dev(*chip, c)]
                _remote(blk, blk, send_sems, recv_sems, 7 * a + 1 + j, me).wait_recv()
                onward = [_remote(blk, blk, send_sems, recv_sems, 7 * a + 4 + j, sibling)]
                if j == 0:
                    onward.append(_remote(blk, blk, send_sems, recv_sems, 7 * a + 3, (*second, c)))
                for cp in onward:
                    cp.start()
                sends += onward
        for a in range(n):
            blk = outs[a].at[_dev(*sibling)]
            _remote(blk, blk, send_sems, recv_sems, 7 * a, me).wait_recv()
            for j, chip in enumerate([second, first, (1 - x, 1 - y)]):
                blk = outs[a].at[_dev(*chip, 1 - c)]
                _remote(blk, blk, send_sems, recv_sems, 7 * a + 4 + j, me).wait_recv()
        for cp in sends:
            cp.wait_send()
        for cp in pending:
            cp.wait()

    return _pcall(
        body, name=name, in_specs=[ANY] * n, out_specs=[ANY] * n,
        out_shape=[jax.ShapeDtypeStruct((N_DEV,) + a.shape, a.dtype) for a in arrs],
        scratch_shapes=[pltpu.SemaphoreType.DMA((7 * n,)), pltpu.SemaphoreType.DMA((7 * n,)), pltpu.SemaphoreType.DMA((n,))],
    )(*arrs)


def _relations(x, y):
    return [(x, y), (1 - x, y), (x, 1 - y), (1 - x, 1 - y)]


HBM_SPEC = pl.BlockSpec(memory_space=pltpu.HBM)
SEM_SPEC = pl.BlockSpec(memory_space=pltpu.SEMAPHORE)
DATAFLOW = pltpu.SideEffectType.DATAFLOW_SIDE_EFFECTING


def _exchange_start(name, bufs, plan, n_remote, n_local, dep):
    nb = len(bufs)
    sem_shapes = [pltpu.SemaphoreType.DMA((n_remote,)), pltpu.SemaphoreType.DMA((n_remote,))]
    if n_local:
        sem_shapes.append(pltpu.SemaphoreType.DMA((n_local,)))
    ns = len(sem_shapes)

    def body(*refs):
        ins, sems, token = refs[:nb], refs[nb + 1:nb + 1 + ns], refs[-1]
        starts, _, local = plan(ins, sems[0], sems[1], sems[2] if n_local else None)
        for cp in local + starts:
            cp.start()
        token[...] = jnp.zeros(token.shape, F32)

    outs = _pcall(
        body, name=name, in_specs=[HBM_SPEC] * nb + [ANY],
        out_specs=[SEM_SPEC] * ns + [HBM_SPEC] * nb + [pl.BlockSpec(memory_space=pltpu.VMEM)],
        out_shape=sem_shapes + [pltpu.HBM(b.shape, b.dtype) for b in bufs] + [jax.ShapeDtypeStruct((SUBLANES, LANES), F32)],
        input_output_aliases={i: ns + i for i in range(nb)}, compiler_params=pltpu.CompilerParams(has_side_effects=DATAFLOW),
    )(*[pltpu.with_memory_space_constraint(b, pltpu.HBM) for b in bufs], dep)
    return dict(sems=outs[:ns], thru=outs[ns:ns + nb], plan=plan, n_local=n_local), outs[-1]


def _exchange_wait(name, handle, *after):
    thru, sems, plan, n_local = handle["thru"], handle["sems"], handle["plan"], handle["n_local"]
    nb, ns = len(thru), len(sems)

    def body(*refs):
        ins, sem_refs = refs[:nb], refs[nb:nb + ns]
        starts, arrivals, local = plan(ins, sem_refs[0], sem_refs[1], sem_refs[2] if n_local else None)
        for cp in starts:
            cp.wait_send()
        for cp in arrivals:
            cp.wait_recv()
        for cp in local:
            cp.wait()

    return _pcall(
        body, name=name, in_specs=[HBM_SPEC] * nb + [SEM_SPEC] * ns + [ANY] * len(after), out_specs=[HBM_SPEC] * nb,
        out_shape=[pltpu.HBM(b.shape, b.dtype) for b in thru], input_output_aliases={i: i for i in range(nb)},
        compiler_params=pltpu.CompilerParams(has_side_effects=DATAFLOW),
    )(*thru, *sems, *after)


def _landing(shape, dtype):
    return lax.empty(shape, dtype)


def _plan_gather_ici(n):
    def plan(refs, send_sems, recv_sems, local_sems):
        x, y, c = _me()
        me, sibling = (x, y, c), (x, y, 1 - c)
        chips = _relations(x, y)[1:]
        starts, arrivals, local = [], [], []
        for a in range(n):
            shard, land = refs[a], refs[n + a]
            own = land.at[_dev(*me)]
            local.append(pltpu.make_async_copy(shard, own, local_sems.at[a]))
            starts.append(_remote(shard, own, send_sems, recv_sems, 4 * a, sibling))
            blk = land.at[_dev(*sibling)]
            arrivals.append(_remote(blk, blk, send_sems, recv_sems, 4 * a, me))
            for j, chip in enumerate(chips):
                starts.append(_remote(shard, own, send_sems, recv_sems, 4 * a + 1 + j, (*chip, c)))
                blk = land.at[_dev(*chip, c)]
                arrivals.append(_remote(blk, blk, send_sems, recv_sems, 4 * a + 1 + j, me))
        return starts, arrivals, local
    return plan


def _plan_gather_d2d(n):
    def plan(refs, send_sems, recv_sems, local_sems):
        x, y, c = _me()
        me, sibling = (x, y, c), (x, y, 1 - c)
        starts, arrivals = [], []
        for a in range(n):
            for j, chip in enumerate(_relations(x, y)[1:]):
                blk = refs[a].at[_dev(*chip, c)]
                starts.append(_remote(blk, blk, send_sems, recv_sems, 3 * a + j, sibling))
                blk = refs[a].at[_dev(*chip, 1 - c)]
                arrivals.append(_remote(blk, blk, send_sems, recv_sems, 3 * a + j, me))
        return starts, arrivals, []
    return plan


def _plan_scatter_d2d(n):
    def plan(refs, send_sems, recv_sems, local_sems):
        x, y, c = _me()
        me, sibling = (x, y, c), (x, y, 1 - c)
        starts, arrivals = [], []
        for a in range(n):
            for k in range(4):
                starts.append(_remote(refs[a].at[2 * k + 1 - c], refs[n + a].at[k], send_sems, recv_sems, 4 * a + k, sibling))
                blk = refs[n + a].at[k]
                arrivals.append(_remote(blk, blk, send_sems, recv_sems, 4 * a + k, me))
        return starts, arrivals, []
    return plan


def _plan_scatter_ici(n):
    def plan(refs, send_sems, recv_sems, local_sems):
        x, y, c = _me()
        me = (x, y, c)
        starts, arrivals = [], []
        for a in range(n):
            for j, (cx, cy) in enumerate(_relations(x, y)[1:]):
                starts.append(_remote(refs[a].at[2 * cx + cy], refs[n + a].at[j], send_sems, recv_sems, 3 * a + j, (cx, cy, c)))
                blk = refs[n + a].at[j]
                arrivals.append(_remote(blk, blk, send_sems, recv_sems, 3 * a + j, me))
        return starts, arrivals, []
    return plan


SMALL = ("conv_a_w", "conv_c_w", "conv_c_b", "gate_r_w", "gate_r_b", "gate_i_w", "gate_i_b", "rg_lambda",
         "norm_a", "norm_b", "norm_c", "sinks", "ln_g", "ln_b")
PACK_COLS = 1024


def _pack(arrs):
    flat = jnp.concatenate([a.reshape(-1) for a in arrs])
    pad = (-flat.shape[0]) % (SUBLANES * PACK_COLS)
    return jnp.pad(flat, (0, pad)).reshape(-1, PACK_COLS)


def _unpack(packed, shapes):
    flat = packed.reshape(-1)
    out, off = [], 0
    for s in shapes:
        size = 1
        for dim in s:
            size *= dim
        out.append(flat[off:off + size].reshape(s))
        off += size
    return out


def kernel(x, w_in, conv_a_w, sinks, conv_c_w, conv_c_b, gate_r_w, gate_r_b, gate_i_w, gate_i_b, rg_lambda, norm_a, norm_b, norm_c, w_out, ln_g, ln_b, loss_target, m_w_in, m_conv_a_w, m_sinks, m_conv_c_w, m_conv_c_b, m_gate_r_w, m_gate_r_b, m_gate_i_w, m_gate_i_b, m_rg_lambda, m_norm_a, m_norm_b, m_norm_c, m_w_out, m_ln_g, m_ln_b, v_w_in, v_conv_a_w, v_sinks, v_conv_c_w, v_conv_c_b, v_gate_r_w, v_gate_r_b, v_gate_i_w, v_gate_i_b, v_rg_lambda, v_norm_a, v_norm_b, v_norm_c, v_w_out, v_ln_g, v_ln_b):
    b_loc, seq, d = x.shape
    depth = w_in.shape[0]
    dm = _Dims(d, gate_r_w.shape[1])
    t = b_loc * seq
    nblk = seq // ATT_BLOCK
    alpha = (2.0 * depth) ** 0.25
    ch = dm.wa // N_DEV
    dev = _dev(*_me())

    wt_shard = [jnp.swapaxes(w_in[l], 0, 1).astype(BF16) for l in range(depth)]
    wo_shard = [w_out[l].astype(BF16) for l in range(depth)]
    conv_shard = jnp.concatenate([conv_a_w.reshape(depth * CONV_A, ch), conv_c_w.reshape(depth * CONV_C, ch)], axis=0)
    conv_shard = jnp.pad(conv_shard, ((0, (-conv_shard.shape[0]) % SUBLANES), (0, 0)))
    wt, conv_all = _all_gather([wt_shard[0], conv_shard], "ag_weights")
    conv_all = jnp.swapaxes(conv_all, 0, 1).reshape(conv_all.shape[1], dm.wa)
    conv_a_full = conv_all[:depth * CONV_A].reshape(depth, CONV_A, dm.wa)
    conv_c_full = conv_all[depth * CONV_A:depth * (CONV_A + CONV_C)].reshape(depth, CONV_C, dm.wc)
    sinks_wide = jnp.broadcast_to(sinks[:, :, None], (depth, dm.nq, LANES))

    def layer_params(l):
        return (conv_a_full[l], conv_c_full[l], conv_c_b[l][None], gate_r_w[l], gate_r_b[l][None], gate_i_w[l],
                gate_i_b[l][None], rg_lambda[l][None], norm_a[l][None], norm_b[l][None], norm_c[l][None], sinks_wide[l])

    tm = _tile(t, 1024)
    xs = x.reshape(t, d)
    xb = xs.astype(BF16)
    saved = []
    bias = _attention_bias()
    rows_t, rows_o = dm.in_w // N_DEV, d // N_DEV
    wo0_ici, token = _exchange_start("ag_wo0_ici_start", [wo_shard[0], _landing((N_DEV, rows_o, d), BF16)],
                                     _plan_gather_ici(1), 4, 1, wt)
    wo = None
    for l in range(depth):
        if l + 1 < depth:
            lands = [_landing((N_DEV, rows_t, d), BF16), _landing((N_DEV, rows_o, d), BF16)]
            ici, token = _exchange_start("ag_ici_start_%d" % l, [wt_shard[l + 1], wo_shard[l + 1]] + lands, _plan_gather_ici(2),
                                         8, 2, token if l == 0 else wt)
        elif l > 0:
            token = _no_token()
        h = _matmul(xb, wt.reshape(dm.in_w, d), dims=NT, tm=tm, tn=_tile(dm.in_w, 512), tk=d, out_dtype=F32, name="mm_in",
                    token=token)
        mix, sst = _mixer_fwd(dm, h, bias, layer_params(l), b_loc, nblk)
        if l == 0:
            land = _exchange_wait("ag_wo0_ici_wait", wo0_ici, mix)[1:]
            wo0_d2d, _ = _exchange_start("ag_wo0_d2d_start", land, _plan_gather_d2d(1), 3, 0, mix)
            wo = _exchange_wait("ag_wo0_d2d_wait", wo0_d2d, mix)[0]
        z = _matmul(mix, wo.reshape(d, d), dims=NN, tm=tm, tn=_tile(d, 512), tk=d, out_dtype=F32, name="mm_out", addend=xs,
                    alpha=alpha)
        saved.append((xb, h, sst, mix, z, wt, wo))
        token = _no_token()
        if l + 1 < depth:
            lands = _exchange_wait("ag_ici_wait_%d" % l, ici, z)[2:]
            d2d, token = _exchange_start("ag_d2d_start_%d" % l, lands, _plan_gather_d2d(2), 6, 0, z)
        xs, xb = _ln_fwd(z, ln_g[l][None], ln_b[l][None], token)
        if l + 1 < depth:
            wt, wo = _exchange_wait("ag_d2d_wait_%d" % l, d2d, xb)

    dy, loss_part = _loss_head(xs, loss_target.reshape(t, d))
    loss = lax.psum(loss_part[0, 0], ("x", "y", "c"))

    scattered, small = [None] * depth, [None] * depth
    ici, token_ici = None, _no_token()

    def finish_scatter(l, ici, *after):
        scattered[l] = _exchange_wait("rs_ici_wait_%d" % l, ici, *after)

    def scatter_add_start(l, d2d, *after):
        done = _exchange_wait("rs_d2d_wait_%d" % l, d2d, *after)
        sums = [_pair_add(p, g, "rs_add_%d" % i) for i, (p, g) in enumerate(zip(done[:2], done[2:]))]
        lands = [_landing((3, rows_t, d), BF16), _landing((3, rows_o, d), BF16)]
        return _exchange_start("rs_ici_start_%d" % l, sums + lands, _plan_scatter_ici(2), 6, 0, after[0])

    for l in reversed(range(depth)):
        xb, h, sst, mix, z, wt, wo = saved[l]
        dz, dzb, d_lng, d_lnb = _ln_bwd(dy, z, ln_g[l][None], token_ici)
        dmix = _matmul(dzb, wo.reshape(d, d), dims=NT, tm=tm, tn=_tile(d, 512), tk=d, out_dtype=F32, name="mm_dmix")
        dwo = _matmul(mix, dzb, dims=TN, tm=_tile(d, 1024), tn=_tile(d, 512), tk=t, out_dtype=BF16, name="mm_dwo")
        dh, sm = _mixer_bwd(dm, h, sst, dmix, bias, layer_params(l), b_loc, nblk)
        (d_caw, d_ccw, d_ccb, d_grw, d_grb, d_giw, d_gib, d_lam, d_na, d_nb, d_nc, d_snk) = sm
        small[l] = dict(conv_a_w=d_caw, conv_c_w=d_ccw, conv_c_b=d_ccb[0], gate_r_w=d_grw, gate_r_b=d_grb[0], gate_i_w=d_giw,
                        gate_i_b=d_gib[0], rg_lambda=d_lam[0], norm_a=d_na[0], norm_b=d_nb[0], norm_c=d_nc[0],
                        sinks=d_snk[:, 0], ln_g=d_lng[0], ln_b=d_lnb[0])
        token = _no_token()
        if l == 0:
            g_local = _pack([jnp.stack([small[i][n] for i in range(depth)]) for n in SMALL])
            small_ici, token = _exchange_start("ag_small_ici_start", [g_local, _landing((N_DEV,) + g_local.shape, F32)],
                                               _plan_gather_ici(1), 4, 1, dh)
        dwt = _matmul(dh, xb, dims=TN, tm=_tile(dm.in_w, 1536), tn=_tile(d, 512), tk=t, out_dtype=BF16, name="mm_dwt",
                      token=token)
        if ici is not None:
            finish_scatter(l + 1, ici, dh)
        parts = [dwt.reshape(N_DEV, rows_t, d), dwo.reshape(N_DEV, rows_o, d)]
        lands = [_landing((4, rows_t, d), BF16), _landing((4, rows_o, d), BF16)]
        d2d, token = _exchange_start("rs_d2d_start_%d" % l, parts + lands, _plan_scatter_d2d(2), 8, 0, dh)
        if l > 0:
            dy = _matmul(dh, wt.reshape(dm.in_w, d), dims=NN, tm=tm, tn=_tile(d, 1024), tk=_tile(dm.in_w, 3584), out_dtype=F32,
                         name="mm_dx", addend=dz, alpha=alpha, token=token)
            ici, token_ici = scatter_add_start(l, d2d, dy)
        else:
            small_land = _exchange_wait("ag_small_ici_wait", small_ici, token)[1:]
            small_d2d, token = _exchange_start("ag_small_d2d_start", small_land, _plan_gather_d2d(1), 3, 0, token)
            ici, token_ici = scatter_add_start(l, d2d, token)
            dy = _matmul(dh, wt.reshape(dm.in_w, d), dims=NN, tm=tm, tn=_tile(d, 1024), tk=_tile(dm.in_w, 3584), out_dtype=F32,
                         name="mm_dx", addend=dz, alpha=alpha, token=token_ici)
            g_all = _exchange_wait("ag_small_d2d_wait", small_d2d, dy)[0]
    grad_x = dy.reshape(b_loc, seq, d)

    w_t, m_t, v_t = [jnp.swapaxes(a, 1, 2) for a in (w_in, m_w_in, v_w_in)]
    res_in, res_out = None, None
    for l in reversed(range(depth)):
        if l == 0:
            finish_scatter(0, ici, dy, *([res_in[0], res_out[0]] if depth > 1 else []))
        sums_t, sums_o, far_t, far_o = scattered[l]
        res_in = _adamw_scattered(w_t, m_t, v_t, l, sums_t, far_t, res_in, "adamw_in_%d" % l)
        res_out = _adamw_scattered(w_out, m_w_out, v_w_out, l, sums_o, far_o, res_out, "adamw_out_%d" % l)
    gw_in, dl_in, nm_in, nv_in = [jnp.swapaxes(a, 1, 2) for a in res_in]
    gw_out, dl_out, nm_out, nv_out = res_out

    given = dict(conv_a_w=(conv_a_w, m_conv_a_w, v_conv_a_w), conv_c_w=(conv_c_w, m_conv_c_w, v_conv_c_w),
                 conv_c_b=(conv_c_b, m_conv_c_b, v_conv_c_b), gate_r_w=(gate_r_w, m_gate_r_w, v_gate_r_w),
                 gate_r_b=(gate_r_b, m_gate_r_b, v_gate_r_b), gate_i_w=(gate_i_w, m_gate_i_w, v_gate_i_w),
                 gate_i_b=(gate_i_b, m_gate_i_b, v_gate_i_b), rg_lambda=(rg_lambda, m_rg_lambda, v_rg_lambda),
                 norm_a=(norm_a, m_norm_a, v_norm_a), norm_b=(norm_b, m_norm_b, v_norm_b), norm_c=(norm_c, m_norm_c, v_norm_c),
                 sinks=(sinks, m_sinks, v_sinks), ln_g=(ln_g, m_ln_g, v_ln_g), ln_b=(ln_b, m_ln_b, v_ln_b))
    full_shapes = [jnp.stack([small[l][n] for l in range(depth)]).shape for n in SMALL]

    def mine_of(n, a):
        if n in ("conv_a_w", "conv_c_w"):
            return lax.dynamic_update_slice(jnp.zeros(a.shape[:2] + (dm.wa,), F32), a, (0, 0, dev * ch))
        return a

    packs = [_pack([mine_of(n, given[n][i]) for n in SMALL]) for i in range(3)]
    outs = _adamw(packs[0], g_all, packs[1], packs[2], "adamw_small")
    res = {}
    for kind, packed in zip(("grad", "delta", "new_m", "new_v"), outs):
        for n, a in zip(SMALL, _unpack(packed, full_shapes)):
            if n in ("conv_a_w", "conv_c_w"):
                a = lax.dynamic_slice(a, (0, 0, dev * ch), a.shape[:2] + (ch,))
            res[kind, n] = a
    res.update({("grad", "w_in"): gw_in, ("delta", "w_in"): dl_in, ("new_m", "w_in"): nm_in, ("new_v", "w_in"): nv_in,
                ("grad", "w_out"): gw_out, ("delta", "w_out"): dl_out, ("new_m", "w_out"): nm_out, ("new_v", "w_out"): nv_out})
    order = ("w_in", "conv_a_w", "sinks", "conv_c_w", "conv_c_b", "gate_r_w", "gate_r_b", "gate_i_w", "gate_i_b", "rg_lambda",
             "norm_a", "norm_b", "norm_c", "w_out", "ln_g", "ln_b")
    return (loss, grad_x, *[res[kind, n] for kind in ("grad", "delta", "new_m", "new_v") for n in order])
```

```python
import functools

import jax
import jax.numpy as jnp
from jax import lax
from jax.experimental import pallas as pl
from jax.experimental.pallas import tpu as pltpu

F32 = jnp.float32
BF16 = jnp.bfloat16
MESH = pl.DeviceIdType.MESH
ANY = pl.BlockSpec(memory_space=pl.ANY)

N_DEV = 8
LANES = 128
SUBLANES = 8
HEAD_DIM = 64
KV_GROUP = 8
ATT_BLOCK = 128
ATT_STACK = 8
CONV_A = 3
CONV_C = 4
RG_C = 8.0
LN_EPS = 1e-5
RMS_EPS = 1e-6
NEG_INF = -1e30
ADAM_LR = 0.001
ADAM_B1 = 0.9
ADAM_B2 = 0.999
ADAM_EPS = 1e-08
ADAM_WD = 0.01
ADAM_STEP = 10
VMEM_LIMIT = 56 * 1024 * 1024

NN = ((1,), (0,))
NT = ((1,), (1,))
TN = ((0,), (0,))


def _pcall(body, **kw):
    return pl.pallas_call(body, **kw)


def _roll(x, shift, axis):
    return pltpu.roll(x, shift, axis)


def _params(sem=None, vmem=VMEM_LIMIT):
    return pltpu.CompilerParams(dimension_semantics=sem, vmem_limit_bytes=vmem)


def _dot(a, b, dims):
    return lax.dot_general(a.astype(BF16), b.astype(BF16), (dims, ((), ())), preferred_element_type=F32)


@jax.custom_vjp
def _mm(a, b):
    return _dot(a, b, NN)


def _mm_fwd(a, b):
    return _dot(a, b, NN), (a.astype(BF16), b.astype(BF16))


def _mm_bwd(res, g):
    a, b = res
    return _dot(g, b, NT), _dot(a, g, TN)


_mm.defvjp(_mm_fwd, _mm_bwd)


@jax.custom_vjp
def _mm_nt(a, b):
    return _dot(a, b, NT)


def _mm_nt_fwd(a, b):
    return _dot(a, b, NT), (a.astype(BF16), b.astype(BF16))


def _mm_nt_bwd(res, g):
    a, b = res
    return _dot(g, b, NN), _dot(g, a, TN)


_mm_nt.defvjp(_mm_nt_fwd, _mm_nt_bwd)


def _rows(shape):
    return lax.broadcasted_iota(jnp.int32, shape, 0)


@functools.partial(jax.custom_vjp, nondiff_argnums=(2,))
def _shift_halo(u, prev, k):
    r, c = u.shape
    fill = jnp.concatenate([_roll(prev, k, 0), jnp.zeros((r - SUBLANES, c), u.dtype)], axis=0)
    return jnp.where(_rows((r, c)) < k, fill, _roll(u, k, 0))


def _shift_halo_fwd(u, prev, k):
    return _shift_halo(u, prev, k), None


def _shift_halo_bwd(k, _, g):
    r, c = g.shape
    du = jnp.where(_rows((r, c)) < r - k, _roll(g, r - k, 0), 0.0)
    dprev = jnp.where(_rows((SUBLANES, c)) >= SUBLANES - k, _roll(g[0:SUBLANES], SUBLANES - k, 0), 0.0)
    return du, dprev


_shift_halo.defvjp(_shift_halo_fwd, _shift_halo_bwd)


@functools.partial(jax.custom_vjp, nondiff_argnums=(1, 2))
def _shift_fill(u, k, fill):
    return jnp.where(_rows(u.shape) < k, fill, _roll(u, k, 0))


def _shift_fill_fwd(u, k, fill):
    return _shift_fill(u, k, fill), None


def _shift_fill_bwd(k, fill, _, g):
    r = g.shape[0]
    return (jnp.where(_rows(g.shape) < r - k, _roll(g, r - k, 0), 0.0),)


_shift_fill.defvjp(_shift_fill_fwd, _shift_fill_bwd)


@jax.custom_vjp
def _swap_halves(x):
    return _roll(x, HEAD_DIM, 1)


_swap_halves.defvjp(lambda x: (_roll(x, HEAD_DIM, 1), None), lambda _, g: (_roll(g, HEAD_DIM, 1),))


@functools.partial(jax.custom_vjp, nondiff_argnums=(1,))
def _split_rows(x, n):
    r = x.shape[0] // n
    return tuple(x[i * r:(i + 1) * r] for i in range(n))


def _split_rows_fwd(x, n):
    return _split_rows(x, n), None


def _split_rows_bwd(n, _, gs):
    return (jnp.concatenate(list(gs), axis=0),)


_split_rows.defvjp(_split_rows_fwd, _split_rows_bwd)


def _logistic(x):
    return 1.0 / (1.0 + jnp.exp(-x))


@jax.custom_vjp
def _sigmoid(x):
    return _logistic(x)


def _sigmoid_fwd(x):
    s = _logistic(x)
    return s, s


_sigmoid.defvjp(_sigmoid_fwd, lambda s, g: (g * s * (1.0 - s),))


@jax.custom_vjp
def _silu(x):
    return x * _logistic(x)


def _silu_fwd(x):
    s = _logistic(x)
    return x * s, (x, s)


_silu.defvjp(_silu_fwd, lambda res, g: (g * res[1] * (1.0 + res[0] * (1.0 - res[1])),))


def _log_sigmoid(x):
    return -(jnp.maximum(-x, 0.0) + jnp.log1p(jnp.exp(-jnp.abs(x))))


@jax.custom_vjp
def _neg_expm1(x):
    series = x * (1 + x * (1 / 2) * (1 + x * (1 / 3) * (1 + x * (1 / 4) * (1 + x * (1 / 5) * (1 + x * (1 / 6) * (1 + x * (1 / 7)))))))
    return -jnp.where(jnp.abs(x) < 0.25, series, jnp.exp(x) - 1.0)


_neg_expm1.defvjp(lambda x: (_neg_expm1(x), x), lambda x, g: (-g * jnp.exp(x),))


def _shift_up(x, k, fill):
    r = x.shape[0]
    return jnp.where(_rows(x.shape) < r - k, _roll(x, r - k, 0), fill)


@jax.custom_vjp
def _scan_block(a, u, s_prev):
    acc_a, acc_b = a, u
    d = 1
    while d < a.shape[0]:
        acc_b = acc_a * _shift_fill(acc_b, d, 0.0) + acc_b
        acc_a = acc_a * _shift_fill(acc_a, d, 1.0)
        d *= 2
    return acc_a * s_prev + acc_b


def _scan_block_fwd(a, u, s_prev):
    h = _scan_block(a, u, s_prev)
    return h, (a, h, s_prev)


def _scan_block_bwd(res, dh):
    a, h, s_prev = res
    acc_a, acc_g = _shift_up(a, 1, 0.0), dh
    d = 1
    while d < a.shape[0]:
        acc_g = acc_a * _shift_up(acc_g, d, 0.0) + acc_g
        acc_a = acc_a * _shift_up(acc_a, d, 1.0)
        d *= 2
    h_prev = jnp.where(_rows(h.shape) < 1, s_prev, _roll(h, 1, 0))
    first = jnp.sum(jnp.where(_rows(h.shape) < 1, a * acc_g, 0.0), axis=0, keepdims=True)
    return acc_g * h_prev, acc_g, first


_scan_block.defvjp(_scan_block_fwd, _scan_block_bwd)


def _last_row(h):
    return jnp.sum(jnp.where(_rows(h.shape) == h.shape[0] - 1, h, 0.0), axis=0, keepdims=True)


def _branch_a(ab, ac, ax, ag, acp, axp, w0, w1, w2, na):
    u = ac * ax
    up = acp * axp
    ya = ab * (w2 * u + w1 * _shift_halo(u, up, 1) + w0 * _shift_halo(u, up, 2))
    ms = jnp.sum(ya * ya, axis=1, keepdims=True) * (1.0 / ya.shape[1])
    return ya * lax.rsqrt(ms + RMS_EPS) * na * _silu(ag)


def _branch_c(cx, cg, cxp, sp, wc, bc, wr, br, wi, bi, lam, nc):
    hs, lasts = [], []
    for j in range(len(cx)):
        xc = (wc[3][j] * cx[j] + wc[2][j] * _shift_halo(cx[j], cxp[j], 1) + wc[1][j] * _shift_halo(cx[j], cxp[j], 2)
              + wc[0][j] * _shift_halo(cx[j], cxp[j], 3) + bc[j])
        r = _sigmoid(_mm(xc, wr[j]) + br[j])
        i = _sigmoid(_mm(xc, wi[j]) + bi[j])
        log_a = RG_C * r * _log_sigmoid(lam[j])
        a = jnp.exp(log_a)
        u = jnp.sqrt(_neg_expm1(2.0 * log_a)) * (i * xc)
        h = _scan_block(a, u, sp[j])
        hs.append(h)
        lasts.append(_last_row(h))
    width = LANES * len(cx)
    ms = sum(jnp.sum(h * h, axis=1, keepdims=True) for h in hs) * (1.0 / width)
    inv = lax.rsqrt(ms + RMS_EPS)
    return [hs[j] * inv * nc[j] * _silu(cg[j]) for j in range(len(cx))], lasts


def _attention_bias():
    qi = (jnp.arange(ATT_STACK * ATT_BLOCK) % ATT_BLOCK)[:, None]
    kj = jnp.arange(2 * ATT_BLOCK)[None, :]
    dist = qi + ATT_BLOCK - kj
    band = (dist >= 0) & (dist < ATT_BLOCK)
    return jnp.where(jnp.stack([band & (kj >= ATT_BLOCK), band]), 0.0, NEG_INF).astype(F32)


def _branch_b(q, k, v, kp, vp, bg, snk, nb, bias):
    rows = ATT_BLOCK
    n_kv = 2 * len(k)
    lane0 = lax.broadcasted_iota(jnp.int32, (ATT_STACK * rows, LANES), 1) == 0
    upper = lax.broadcasted_iota(jnp.int32, (2 * rows, LANES), 1) >= HEAD_DIM
    heads = [None] * (n_kv * KV_GROUP)
    for g in range(n_kv):
        half = g % 2
        keep = upper if half else jnp.logical_not(upper)
        kc = jnp.where(keep, jnp.concatenate([kp[g // 2], k[g // 2]], axis=0), 0.0)
        vc = jnp.where(keep, jnp.concatenate([vp[g // 2], v[g // 2]], axis=0), 0.0)
        for first in range(g * KV_GROUP, (g + 1) * KV_GROUP, ATT_STACK):
            hs = range(first, first + ATT_STACK)
            qg = jnp.concatenate([q[h // 2] if h % 2 == half else _swap_halves(q[h // 2]) for h in hs], axis=0)
            s = _mm_nt(qg * (HEAD_DIM ** -0.5), kc) + bias
            sink = jnp.concatenate([jnp.broadcast_to(snk[h], (rows, LANES)) for h in hs], axis=0)
            sink = jnp.sum(jnp.where(lane0, sink, 0.0), axis=1, keepdims=True)
            m = lax.stop_gradient(jnp.maximum(jnp.max(s, axis=1, keepdims=True), sink))
            p = jnp.exp(s - m)
            inv = 1.0 / (jnp.sum(p, axis=1, keepdims=True) + jnp.exp(sink - m))
            o = _split_rows(_mm(p * inv, vc), ATT_STACK)
            for i, h in enumerate(hs):
                heads[h] = o[i] if h % 2 == half else _swap_halves(o[i])
    yb = [heads[2 * j] + heads[2 * j + 1] for j in range(len(q))]
    width = LANES * len(q)
    ms = sum(jnp.sum(y * y, axis=1, keepdims=True) for y in yb) * (1.0 / width)
    inv = lax.rsqrt(ms + RMS_EPS)
    return [yb[j] * inv * nb[j] * _silu(bg[j]) for j in range(len(q))]


class _Dims:
    def __init__(self, d_model, n_rg_heads):
        self.d = d_model
        self.wa = d_model // 4
        self.wb = d_model // 2
        self.wc = d_model // 4
        self.kvw = self.wb // KV_GROUP
        self.nq = self.wb // HEAD_DIM
        self.in_w = 4 * self.wa + 2 * self.wb + 2 * self.kvw + 2 * self.wc
        self.o_q = 4 * self.wa
        self.o_k = self.o_q + self.wb
        self.o_v = self.o_k + self.kvw
        self.o_bg = self.o_v + self.kvw
        self.o_cx = self.o_bg + self.wb
        self.o_cg = self.o_cx + self.wc
        self.nh = n_rg_heads
        assert self.wc // n_rg_heads == LANES and self.kvw % LANES == 0
        assert self.o_k % self.kvw == 0 and self.o_cx % (self.wc // 2) == 0


def _chunks(ref, rows, off, width):
    return [ref[rows, off + LANES * j: off + LANES * (j + 1)] for j in range(width // LANES)]


def _read_params(dm, caw, ccw, ccb, grw, grb, giw, gib, lam, na, nb, nc, snk):
    row = slice(0, 1)
    return dict(
        wa=[caw[k:k + 1, :] for k in range(CONV_A)], na=na[...],
        wc=[_chunks(ccw, slice(k, k + 1), 0, dm.wc) for k in range(CONV_C)], bc=_chunks(ccb, row, 0, dm.wc),
        wr=[grw[j] for j in range(dm.nh)], br=_chunks(grb, row, 0, dm.wc),
        wi=[giw[j] for j in range(dm.nh)], bi=_chunks(gib, row, 0, dm.wc),
        lam=_chunks(lam, row, 0, dm.wc), nc=_chunks(nc, row, 0, dm.wc),
        nb=_chunks(nb, row, 0, dm.wb), snk=[snk[h:h + 1, :] for h in range(dm.nq)])


def _param_specs(dm):
    shapes = [(CONV_A, dm.wa), (CONV_C, dm.wc), (1, dm.wc), (dm.nh, LANES, LANES), (1, dm.wc), (dm.nh, LANES, LANES),
              (1, dm.wc), (1, dm.wc), (1, dm.wa), (1, dm.wb), (1, dm.wc), (dm.nq, LANES)]
    specs = [pl.BlockSpec(s, (lambda b, n, _r=len(s): (0,) * _r)) for s in shapes]
    return shapes, specs


def _bias_spec():
    return pl.BlockSpec((2, ATT_STACK * ATT_BLOCK, 2 * ATT_BLOCK), lambda *_: (0, 0, 0))


def _mixer_fwd(dm, h, bias, prm, b_loc, nblk):
    t = h.shape[0]
    r = ATT_BLOCK
    tail = slice(r - SUBLANES, r)

    def body(h_ref, bias_ref, caw, ccw, ccb, grw, grb, giw, gib, lam, na, nb, nc, snk, mix_ref, sst_ref, kp, vp, acp, axp, cxp, sp):
        n = pl.program_id(1)

        @pl.when(n == 0)
        def _():
            for ref in (kp, vp, acp, axp, cxp, sp):
                ref[...] = jnp.zeros(ref.shape, ref.dtype)

        p = _read_params(dm, caw, ccw, ccb, grw, grb, giw, gib, lam, na, nb, nc, snk)
        full = slice(None)
        mix_a = _branch_a(h_ref[:, 0:dm.wa], h_ref[:, dm.wa:2 * dm.wa], h_ref[:, 2 * dm.wa:3 * dm.wa],
                          h_ref[:, 3 * dm.wa:4 * dm.wa], acp[...], axp[...], p["wa"][0], p["wa"][1], p["wa"][2], p["na"])
        mix_ref[:, 0:dm.wa] = mix_a.astype(BF16)
        bias = bias_ref[jnp.where(n == 0, 0, 1)]
        mix_b = _branch_b(_chunks(h_ref, full, dm.o_q, dm.wb), _chunks(h_ref, full, dm.o_k, dm.kvw),
                          _chunks(h_ref, full, dm.o_v, dm.kvw), _chunks(kp, full, 0, dm.kvw), _chunks(vp, full, 0, dm.kvw),
                          _chunks(h_ref, full, dm.o_bg, dm.wb), p["snk"], p["nb"], bias)
        for j, mb in enumerate(mix_b):
            mix_ref[:, dm.wa + LANES * j: dm.wa + LANES * (j + 1)] = mb.astype(BF16)
        sst_ref[0] = sp[...]
        mix_c, lasts = _branch_c(_chunks(h_ref, full, dm.o_cx, dm.wc), _chunks(h_ref, full, dm.o_cg, dm.wc),
                                 _chunks(cxp, full, 0, dm.wc), _chunks(sp, slice(0, 1), 0, dm.wc), p["wc"], p["bc"],
                                 p["wr"], p["br"], p["wi"], p["bi"], p["lam"], p["nc"])
        o_c = dm.wa + dm.wb
        for j, mc in enumerate(mix_c):
            mix_ref[:, o_c + LANES * j: o_c + LANES * (j + 1)] = mc.astype(BF16)
            sp[:, LANES * j: LANES * (j + 1)] = jnp.broadcast_to(lasts[j], (SUBLANES, LANES))
        kp[...] = h_ref[:, dm.o_k:dm.o_k + dm.kvw]
        vp[...] = h_ref[:, dm.o_v:dm.o_v + dm.kvw]
        acp[...] = h_ref[tail, dm.wa:2 * dm.wa]
        axp[...] = h_ref[tail, 2 * dm.wa:3 * dm.wa]
        cxp[...] = h_ref[tail, dm.o_cx:dm.o_cx + dm.wc]

    _, pspecs = _param_specs(dm)
    return _pcall(
        body, name="mixer_fwd", grid=(b_loc, nblk),
        in_specs=[pl.BlockSpec((r, dm.in_w), lambda b, n: (b * nblk + n, 0)), _bias_spec()] + pspecs,
        out_specs=[pl.BlockSpec((r, dm.d), lambda b, n: (b * nblk + n, 0)),
                   pl.BlockSpec((1, SUBLANES, dm.wc), lambda b, n: (b * nblk + n, 0, 0))],
        out_shape=[jax.ShapeDtypeStruct((t, dm.d), BF16), jax.ShapeDtypeStruct((b_loc * nblk, SUBLANES, dm.wc), F32)],
        scratch_shapes=[pltpu.VMEM((r, dm.kvw), F32), pltpu.VMEM((r, dm.kvw), F32), pltpu.VMEM((SUBLANES, dm.wa), F32),
                        pltpu.VMEM((SUBLANES, dm.wa), F32), pltpu.VMEM((SUBLANES, dm.wc), F32),
                        pltpu.VMEM((SUBLANES, dm.wc), F32)],
        compiler_params=_params(("arbitrary", "arbitrary")),
    )(h, bias, *prm)


def _mixer_bwd(dm, h, sst, dmix, bias, prm, b_loc, nblk):
    t = h.shape[0]
    r = ATT_BLOCK
    rb8 = r // SUBLANES
    n_small = 12

    def body(h_ref, kp_ref, vp_ref, acp_ref, axp_ref, cxp0_ref, cxp1_ref, sst_ref, dmix_ref, bias_ref,
             caw, ccw, ccb, grw, grb, giw, gib, lam, na, nb, nc, snk,
             dh_ref, d_caw, d_ccw, d_ccb, d_grw, d_grb, d_giw, d_gib, d_lam, d_na, d_nb, d_nc, d_snk,
             dkp, dvp, dacp, daxp, dcxp, dsp):
        step = pl.program_id(1)
        n = nblk - 1 - step

        @pl.when(step == 0)
        def _():
            for ref in (dkp, dvp, dacp, daxp, dcxp, dsp):
                ref[...] = jnp.zeros(ref.shape, ref.dtype)

        @pl.when((step == 0) & (pl.program_id(0) == 0))
        def _():
            for ref in (d_caw, d_ccw, d_ccb, d_grw, d_grb, d_giw, d_gib, d_lam, d_na, d_nb, d_nc, d_snk):
                ref[...] = jnp.zeros(ref.shape, ref.dtype)

        p = _read_params(dm, caw, ccw, ccb, grw, grb, giw, gib, lam, na, nb, nc, snk)
        has_prev = jnp.where(n > 0, 1.0, 0.0)
        full = slice(None)
        pad = jnp.zeros((r - SUBLANES, LANES), F32)

        def with_tail(own, carry):
            z = jnp.zeros((r - SUBLANES, own.shape[1]), F32)
            return own + jnp.concatenate([z, carry], axis=0)

        a_in = (h_ref[:, 0:dm.wa], h_ref[:, dm.wa:2 * dm.wa], h_ref[:, 2 * dm.wa:3 * dm.wa], h_ref[:, 3 * dm.wa:4 * dm.wa],
                acp_ref[...] * has_prev, axp_ref[...] * has_prev, p["wa"][0], p["wa"][1], p["wa"][2], p["na"])
        _, vjp_a = jax.vjp(_branch_a, *a_in)
        g_ab, g_ac, g_ax, g_ag, g_acp, g_axp, g_w0, g_w1, g_w2, g_na = vjp_a(dmix_ref[:, 0:dm.wa])
        dh_ref[:, 0:dm.wa] = g_ab.astype(BF16)
        dh_ref[:, dm.wa:2 * dm.wa] = with_tail(g_ac, dacp[...]).astype(BF16)
        dh_ref[:, 2 * dm.wa:3 * dm.wa] = with_tail(g_ax, daxp[...]).astype(BF16)
        dh_ref[:, 3 * dm.wa:4 * dm.wa] = g_ag.astype(BF16)
        dacp[...] = g_acp
        daxp[...] = g_axp
        for k, gw in enumerate((g_w0, g_w1, g_w2)):
            d_caw[k:k + 1, :] += gw
        d_na[...] += g_na

        bias = bias_ref[jnp.where(n == 0, 0, 1)]
        kp_in = [c * has_prev for c in _chunks(kp_ref, full, 0, dm.kvw)]
        vp_in = [c * has_prev for c in _chunks(vp_ref, full, 0, dm.kvw)]
        b_in = (_chunks(h_ref, full, dm.o_q, dm.wb), _chunks(h_ref, full, dm.o_k, dm.kvw), _chunks(h_ref, full, dm.o_v, dm.kvw),
                kp_in, vp_in, _chunks(h_ref, full, dm.o_bg, dm.wb), p["snk"], p["nb"])
        _, vjp_b = jax.vjp(lambda *a: _branch_b(*a, bias), *b_in)
        g_q, g_k, g_v, g_kp, g_vp, g_bg, g_snk, g_nb = vjp_b(_chunks(dmix_ref, full, dm.wa, dm.wb))
        for j in range(len(g_q)):
            dh_ref[:, dm.o_q + LANES * j: dm.o_q + LANES * (j + 1)] = g_q[j].astype(BF16)
            dh_ref[:, dm.o_bg + LANES * j: dm.o_bg + LANES * (j + 1)] = g_bg[j].astype(BF16)
            d_nb[:, LANES * j: LANES * (j + 1)] += g_nb[j]
        for j in range(len(g_k)):
            cols = slice(LANES * j, LANES * (j + 1))
            dh_ref[:, dm.o_k + LANES * j: dm.o_k + LANES * (j + 1)] = (g_k[j] + dkp[:, cols]).astype(BF16)
            dh_ref[:, dm.o_v + LANES * j: dm.o_v + LANES * (j + 1)] = (g_v[j] + dvp[:, cols]).astype(BF16)
            dkp[:, cols] = g_kp[j]
            dvp[:, cols] = g_vp[j]
        for hd in range(dm.nq):
            d_snk[hd:hd + 1, :] += g_snk[hd]

        half_c = dm.wc // 2
        cxp_in = ([c * has_prev for c in _chunks(cxp0_ref, full, 0, half_c)]
                  + [c * has_prev for c in _chunks(cxp1_ref, full, 0, half_c)])
        c_in = (_chunks(h_ref, full, dm.o_cx, dm.wc), _chunks(h_ref, full, dm.o_cg, dm.wc), cxp_in,
                [sst_ref[0, 0:1, LANES * j: LANES * (j + 1)] for j in range(dm.nh)], p["wc"], p["bc"], p["wr"], p["br"], p["wi"], p["bi"],
                p["lam"], p["nc"])
        _, vjp_c = jax.vjp(_branch_c, *c_in)
        ct_last = [dsp[0:1, LANES * j: LANES * (j + 1)] for j in range(dm.nh)]
        g_cx, g_cg, g_cxp, g_sp, g_wc, g_bc, g_wr, g_br, g_wi, g_bi, g_lam, g_nc = vjp_c(
            (_chunks(dmix_ref, full, dm.wa + dm.wb, dm.wc), ct_last))
        for j in range(dm.nh):
            cols = slice(LANES * j, LANES * (j + 1))
            tot = g_cx[j] + jnp.concatenate([pad, dcxp[:, cols]], axis=0)
            dh_ref[:, dm.o_cx + LANES * j: dm.o_cx + LANES * (j + 1)] = tot.astype(BF16)
            dh_ref[:, dm.o_cg + LANES * j: dm.o_cg + LANES * (j + 1)] = g_cg[j].astype(BF16)
            dcxp[:, cols] = g_cxp[j]
            dsp[:, cols] = jnp.broadcast_to(g_sp[j], (SUBLANES, LANES))
            for k in range(CONV_C):
                d_ccw[k:k + 1, cols] += g_wc[k][j]
            d_ccb[:, cols] += g_bc[j]
            d_grw[j] += g_wr[j]
            d_grb[:, cols] += g_br[j]
            d_giw[j] += g_wi[j]
            d_gib[:, cols] += g_bi[j]
            d_lam[:, cols] += g_lam[j]
            d_nc[:, cols] += g_nc[j]

    def blk(b, s):
        return b * nblk + (nblk - 1 - s)

    def prev_rows8(b, s):
        return jnp.maximum(blk(b, s) * rb8 - 1, 0)

    pshapes, pspecs = _param_specs(dm)
    half_c = dm.wc // 2
    in_specs = [
        pl.BlockSpec((r, dm.in_w), lambda b, s: (blk(b, s), 0)),
        pl.BlockSpec((r, dm.kvw), lambda b, s: (jnp.maximum(blk(b, s) - 1, 0), dm.o_k // dm.kvw)),
        pl.BlockSpec((r, dm.kvw), lambda b, s: (jnp.maximum(blk(b, s) - 1, 0), dm.o_v // dm.kvw)),
        pl.BlockSpec((SUBLANES, dm.wa), lambda b, s: (prev_rows8(b, s), 1)),
        pl.BlockSpec((SUBLANES, dm.wa), lambda b, s: (prev_rows8(b, s), 2)),
        pl.BlockSpec((SUBLANES, half_c), lambda b, s: (prev_rows8(b, s), dm.o_cx // half_c)),
        pl.BlockSpec((SUBLANES, half_c), lambda b, s: (prev_rows8(b, s), dm.o_cx // half_c + 1)),
        pl.BlockSpec((1, SUBLANES, dm.wc), lambda b, s: (blk(b, s), 0, 0)),
        pl.BlockSpec((r, dm.d), lambda b, s: (blk(b, s), 0)),
        _bias_spec(),
    ] + pspecs
    outs = _pcall(
        body, name="mixer_bwd", grid=(b_loc, nblk), in_specs=in_specs,
        out_specs=[pl.BlockSpec((r, dm.in_w), lambda b, s: (blk(b, s), 0))] + pspecs,
        out_shape=[jax.ShapeDtypeStruct((t, dm.in_w), BF16)] + [jax.ShapeDtypeStruct(s, F32) for s in pshapes],
        scratch_shapes=[pltpu.VMEM((r, dm.kvw), F32), pltpu.VMEM((r, dm.kvw), F32), pltpu.VMEM((SUBLANES, dm.wa), F32),
                        pltpu.VMEM((SUBLANES, dm.wa), F32), pltpu.VMEM((SUBLANES, dm.wc), F32),
                        pltpu.VMEM((SUBLANES, dm.wc), F32)],
        compiler_params=_params(("arbitrary", "arbitrary")),
    )(h, h, h, h, h, h, h, sst, dmix, bias, *prm)
    assert len(outs) == 1 + n_small
    return outs[0], outs[1:]


def _token_spec():
    return pl.BlockSpec((SUBLANES, LANES), lambda *_: (0, 0))


def _no_token():
    return jnp.zeros((SUBLANES, LANES), F32)


def _matmul(a, b, *, dims, tm, tn, tk, out_dtype, name, addend=None, alpha=None, token=None):
    if dims == TN:
        (k_dim, m), n_dim = a.shape, b.shape[1]
        a_spec = pl.BlockSpec((tk, tm), lambda i, j, k: (k, i))
    else:
        (m, k_dim), n_dim = a.shape, (b.shape[0] if dims == NT else b.shape[1])
        a_spec = pl.BlockSpec((tm, tk), lambda i, j, k: (i, k))
    b_spec = pl.BlockSpec((tn, tk), lambda i, j, k: (j, k)) if dims == NT else pl.BlockSpec((tk, tn), lambda i, j, k: (k, j))
    assert m % tm == 0 and n_dim % tn == 0 and k_dim % tk == 0, (a.shape, b.shape, tm, tn, tk)
    nk = k_dim // tk
    o_spec = pl.BlockSpec((tm, tn), lambda i, j, k: (i, j))

    def body(*refs):
        a_ref, b_ref = refs[0], refs[1]
        add_ref = refs[2] if addend is not None else None
        o_ref, acc_ref = refs[-2], refs[-1]
        k = pl.program_id(2)
        part = lax.dot_general(a_ref[...], b_ref[...], (dims, ((), ())), preferred_element_type=F32)

        def finish(acc):
            if add_ref is not None:
                acc = acc + alpha * add_ref[...]
            o_ref[...] = acc.astype(out_dtype)

        if nk == 1:
            finish(part)
        else:
            @pl.when(k == 0)
            def _():
                acc_ref[...] = part

            @pl.when((k > 0) & (k < nk - 1))
            def _():
                acc_ref[...] += part

            @pl.when(k == nk - 1)
            def _():
                finish(acc_ref[...] + part)

    ins = [a, b] + ([addend] if addend is not None else []) + ([token] if token is not None else [])
    in_specs = [a_spec, b_spec] + ([o_spec] if addend is not None else []) + ([_token_spec()] if token is not None else [])
    return _pcall(
        body, name=name, grid=(m // tm, n_dim // tn, nk), in_specs=in_specs, out_specs=o_spec,
        out_shape=jax.ShapeDtypeStruct((m, n_dim), out_dtype),
        scratch_shapes=[pltpu.VMEM((tm, tn) if nk > 1 else (SUBLANES, LANES), F32)],
        compiler_params=_params(("parallel", "parallel", "arbitrary")),
    )(*ins)


def _tile(n, want, quantum=LANES):
    if n <= want:
        return n
    for cand in range(want - want % quantum, 0, -quantum):
        if n % cand == 0:
            return cand
    return n


def _row_tile(t, d, elems=1 << 19):
    return _tile(t, max(2 * SUBLANES, elems // d), 2 * SUBLANES)


STREAM_ELEMS = 1 << 20


def _ln_fwd(z, g, b, token):
    t, d = z.shape
    tr = _row_tile(t, d, STREAM_ELEMS)

    def body(z_ref, g_ref, b_ref, _, y_ref, yb_ref):
        zz = z_ref[...]
        mu = jnp.mean(zz, axis=1, keepdims=True)
        zc = zz - mu
        var = jnp.mean(zc * zc, axis=1, keepdims=True)
        y = zc * lax.rsqrt(var + LN_EPS) * g_ref[...] + b_ref[...]
        y_ref[...] = y
        yb_ref[...] = y.astype(BF16)

    row = pl.BlockSpec((tr, d), lambda i: (i, 0))
    vec = pl.BlockSpec((1, d), lambda i: (0, 0))
    return _pcall(body, name="ln_fwd", grid=(t // tr,), in_specs=[row, vec, vec, _token_spec()], out_specs=[row, row],
                  out_shape=[jax.ShapeDtypeStruct((t, d), F32), jax.ShapeDtypeStruct((t, d), BF16)],
                  compiler_params=_params(("parallel",)))(z, g, b, token)


def _ln_bwd(dy, z, g, token):
    t, d = z.shape
    tr = _row_tile(t, d, STREAM_ELEMS)

    def body(dy_ref, z_ref, g_ref, _, dz_ref, dzb_ref, dg_ref, db_ref):
        @pl.when(pl.program_id(0) == 0)
        def _():
            dg_ref[...] = jnp.zeros(dg_ref.shape, F32)
            db_ref[...] = jnp.zeros(db_ref.shape, F32)

        zz = z_ref[...]
        dyy = dy_ref[...]
        mu = jnp.mean(zz, axis=1, keepdims=True)
        zc = zz - mu
        rstd = lax.rsqrt(jnp.mean(zc * zc, axis=1, keepdims=True) + LN_EPS)
        xhat = zc * rstd
        dyg = dyy * g_ref[...]
        dz = rstd * (dyg - jnp.mean(dyg, axis=1, keepdims=True) - xhat * jnp.mean(dyg * xhat, axis=1, keepdims=True))
        dz_ref[...] = dz
        dzb_ref[...] = dz.astype(BF16)
        dg_ref[...] += jnp.sum(dyy * xhat, axis=0, keepdims=True)
        db_ref[...] += jnp.sum(dyy, axis=0, keepdims=True)

    row = pl.BlockSpec((tr, d), lambda i: (i, 0))
    vec = pl.BlockSpec((1, d), lambda i: (0, 0))
    return _pcall(body, name="ln_bwd", grid=(t // tr,), in_specs=[row, row, vec, _token_spec()], out_specs=[row, row, vec, vec],
                  out_shape=[jax.ShapeDtypeStruct((t, d), F32), jax.ShapeDtypeStruct((t, d), BF16),
                             jax.ShapeDtypeStruct((1, d), F32), jax.ShapeDtypeStruct((1, d), F32)],
                  compiler_params=_params(("arbitrary",)))(dy, z, g, token)


def _ln_loss_head(z, g, b, target):
    t, d = z.shape
    tr = _row_tile(t, d, STREAM_ELEMS)

    def body(z_ref, g_ref, b_ref, t_ref, dy_ref, loss_ref):
        @pl.when(pl.program_id(0) == 0)
        def _():
            loss_ref[...] = jnp.zeros(loss_ref.shape, F32)

        zz = z_ref[...]
        mu = jnp.mean(zz, axis=1, keepdims=True)
        zc = zz - mu
        var = jnp.mean(zc * zc, axis=1, keepdims=True)
        err = zc * lax.rsqrt(var + LN_EPS) * g_ref[...] + b_ref[...] - t_ref[...]
        dy_ref[...] = err * (1.0 / d)
        per_token = jnp.sum(err * err, axis=1, keepdims=True) * (1.0 / d)
        loss_ref[...] += 0.5 * jnp.sum(per_token, axis=0, keepdims=True)

    row = pl.BlockSpec((tr, d), lambda i: (i, 0))
    vec = pl.BlockSpec((1, d), lambda i: (0, 0))
    one = pl.BlockSpec((1, 1), lambda i: (0, 0))
    return _pcall(body, name="ln_loss_head", grid=(t // tr,), in_specs=[row, vec, vec, row], out_specs=[row, one],
                  out_shape=[jax.ShapeDtypeStruct((t, d), F32), jax.ShapeDtypeStruct((1, 1), F32)],
                  compiler_params=_params(("arbitrary",)))(z, g, b, target)


def _adamw_scattered(w, m, v, layer, sums, far, prev, name):
    n_layers, r, c = w.shape
    tr = _row_tile(r, c)

    def body(*refs):
        w_ref, m_ref, v_ref, s_ref, f_ref = refs[:5]
        go_ref, d_ref, mo_ref, vo_ref = refs[-4:]
        gg = s_ref[...].astype(F32)
        for i in range(3):
            gg = gg + f_ref[i:i + 1].astype(F32)
        m_new = ADAM_B1 * m_ref[...] + (1.0 - ADAM_B1) * gg
        v_new = ADAM_B2 * v_ref[...] + (1.0 - ADAM_B2) * (gg * gg)
        m_hat = m_new / (1.0 - ADAM_B1 ** ADAM_STEP)
        v_hat = v_new / (1.0 - ADAM_B2 ** ADAM_STEP)
        go_ref[...] = gg
        d_ref[...] = -ADAM_LR * (m_hat / (jnp.sqrt(v_hat) + ADAM_EPS) + ADAM_WD * w_ref[...])
        mo_ref[...] = m_new
        vo_ref[...] = v_new

    own = pl.BlockSpec((1, tr, c), lambda i: (layer, i, 0))
    in_specs = [own, own, own, pl.BlockSpec((1, tr, c), lambda i: (2 * lax.axis_index("x") + lax.axis_index("y"), i, 0)),
                pl.BlockSpec((3, tr, c), lambda i: (0, i, 0))] + ([ANY] * 4 if prev is not None else [])
    return _pcall(body, name=name, grid=(r // tr,), in_specs=in_specs, out_specs=[own] * 4,
                  out_shape=[jax.ShapeDtypeStruct((n_layers, r, c), F32)] * 4,
                  input_output_aliases={5 + i: i for i in range(4)} if prev is not None else {},
                  compiler_params=_params(("parallel",)))(w, m, v, sums, far, *(prev if prev is not None else []))


def _pair_add(a, b, name):
    p, r, c = b.shape
    tr = _row_tile(r, c, STREAM_ELEMS)

    def body(a_ref, b_ref, o_ref):
        o_ref[...] = (a_ref[...].astype(F32) + b_ref[...].astype(F32)).astype(BF16)

    spec = pl.BlockSpec((1, tr, c), lambda q, i: (q, i, 0))
    return _pcall(body, name=name, grid=(p, r // tr),
                  in_specs=[pl.BlockSpec((1, tr, c), lambda q, i: (2 * q + lax.axis_index("c"), i, 0)), spec], out_specs=spec,
                  out_shape=jax.ShapeDtypeStruct((p, r, c), BF16), compiler_params=_params(("parallel", "parallel")))(a, b)


def _adamw(w, g_parts, m, v, name):
    r, c = w.shape
    n_parts = g_parts.shape[0]
    tr = _row_tile(r, c) if r % SUBLANES == 0 else r

    def body(w_ref, g_ref, m_ref, v_ref, go_ref, d_ref, mo_ref, vo_ref):
        g = g_ref[0].astype(F32)
        for i in range(1, n_parts):
            g = g + g_ref[i].astype(F32)
        m_new = ADAM_B1 * m_ref[...] + (1.0 - ADAM_B1) * g
        v_new = ADAM_B2 * v_ref[...] + (1.0 - ADAM_B2) * (g * g)
        m_hat = m_new / (1.0 - ADAM_B1 ** ADAM_STEP)
        v_hat = v_new / (1.0 - ADAM_B2 ** ADAM_STEP)
        go_ref[...] = g
        d_ref[...] = -ADAM_LR * (m_hat / (jnp.sqrt(v_hat) + ADAM_EPS) + ADAM_WD * w_ref[...])
        mo_ref[...] = m_new
        vo_ref[...] = v_new

    spec = pl.BlockSpec((tr, c), lambda i: (i, 0))
    shape = jax.ShapeDtypeStruct((r, c), F32)
    return _pcall(body, name=name, grid=(r // tr,),
                  in_specs=[spec, pl.BlockSpec((n_parts, tr, c), lambda i: (0, i, 0)), spec, spec],
                  out_specs=[spec] * 4, out_shape=[shape] * 4, compiler_params=_params(("parallel",)))(w, g_parts, m, v)


def _me():
    return lax.axis_index("x"), lax.axis_index("y"), lax.axis_index("c")


def _dev(px, py, pc):
    return 4 * px + 2 * py + pc


def _remote(src, dst, send_sems, recv_sems, k, to):
    return pltpu.make_async_remote_copy(src_ref=src, dst_ref=dst, send_sem=send_sems.at[k], recv_sem=recv_sems.at[k],
                                        device_id=to, device_id_type=MESH)


def _all_gather(arrs, name):
    n = len(arrs)

    def body(*refs):
        ins, outs = refs[:n], refs[n:2 * n]
        send_sems, recv_sems, local_sems = refs[2 * n:]
        x, y, c = _me()
        me, sibling = (x, y, c), (x, y, 1 - c)
        first = ((x + 1 - c) % 2, (y + c) % 2)
        second = ((x + c) % 2, (y + 1 - c) % 2)
        chips = [first, second, (1 - x, 1 - y)]
        pending = []
        for a in range(n):
            mine = pltpu.make_async_copy(ins[a], outs[a].at[_dev(*me)], local_sems.at[a])
            mine.start()
            pending.append(mine)
        sends = []
        for a in range(n):
            dst = outs[a].at[_dev(*me)]
            sends.append(_remote(ins[a], dst, send_sems, recv_sems, 7 * a, sibling))
            sends += [_remote(ins[a], dst, send_sems, recv_sems, 7 * a + 1 + j, (*chip, c)) for j, chip in enumerate(chips[:2])]
        for cp in sends:
            cp.start()
        for a in range(n):
            for j, chip in enumerate(chips):
                blk = outs[a].at[_dev(*chip, c)]
                _remote(blk, blk, send_sems, recv_sems, 7 * a + 1 + j, me).wait_recv()
                onward = [_remote(blk, blk, send_sems, recv_sems, 7 * a + 4 + j, sibling)]
                if j == 0:
                    onward.append(_remote(blk, blk, send_sems, recv_sems, 7 * a + 3, (*second, c)))
                for cp in onward:
                    cp.start()
                sends += onward
        for a in range(n):
            blk = outs[a].at[_dev(*sibling)]
            _remote(blk, blk, send_sems, recv_sems, 7 * a, me).wait_recv()
            for j, chip in enumerate([second, first, (1 - x, 1 - y)]):
                blk = outs[a].at[_dev(*chip, 1 - c)]
                _remote(blk, blk, send_sems, recv_sems, 7 * a + 4 + j, me).wait_recv()
        for cp in sends:
            cp.wait_send()
        for cp in pending:
            cp.wait()

    return _pcall(
        body, name=name, in_specs=[ANY] * n, out_specs=[ANY] * n,
        out_shape=[jax.ShapeDtypeStruct((N_DEV,) + a.shape, a.dtype) for a in arrs],
        scratch_shapes=[pltpu.SemaphoreType.DMA((7 * n,)), pltpu.SemaphoreType.DMA((7 * n,)), pltpu.SemaphoreType.DMA((n,))],
    )(*arrs)


def _relations(x, y):
    return [(x, y), (1 - x, y), (x, 1 - y), (1 - x, 1 - y)]


HBM_SPEC = pl.BlockSpec(memory_space=pltpu.HBM)
SEM_SPEC = pl.BlockSpec(memory_space=pltpu.SEMAPHORE)
DATAFLOW = pltpu.SideEffectType.DATAFLOW_SIDE_EFFECTING


def _exchange_start(name, bufs, plan, n_remote, n_local, dep):
    nb = len(bufs)
    sem_shapes = [pltpu.SemaphoreType.DMA((n_remote,)), pltpu.SemaphoreType.DMA((n_remote,))]
    if n_local:
        sem_shapes.append(pltpu.SemaphoreType.DMA((n_local,)))
    ns = len(sem_shapes)

    def body(*refs):
        ins, sems, token = refs[:nb], refs[nb + 1:nb + 1 + ns], refs[-1]
        starts, _, local = plan(ins, sems[0], sems[1], sems[2] if n_local else None)
        for cp in local + starts:
            cp.start()
        token[...] = jnp.zeros(token.shape, F32)

    outs = _pcall(
        body, name=name, in_specs=[HBM_SPEC] * nb + [ANY],
        out_specs=[SEM_SPEC] * ns + [HBM_SPEC] * nb + [pl.BlockSpec(memory_space=pltpu.VMEM)],
        out_shape=sem_shapes + [pltpu.HBM(b.shape, b.dtype) for b in bufs] + [jax.ShapeDtypeStruct((SUBLANES, LANES), F32)],
        input_output_aliases={i: ns + i for i in range(nb)}, compiler_params=pltpu.CompilerParams(has_side_effects=DATAFLOW),
    )(*[pltpu.with_memory_space_constraint(b, pltpu.HBM) for b in bufs], dep)
    return dict(sems=outs[:ns], thru=outs[ns:ns + nb], plan=plan, n_local=n_local), outs[-1]


def _exchange_wait(name, handle, *after):
    thru, sems, plan, n_local = handle["thru"], handle["sems"], handle["plan"], handle["n_local"]
    nb, ns = len(thru), len(sems)

    def body(*refs):
        ins, sem_refs = refs[:nb], refs[nb:nb + ns]
        starts, arrivals, local = plan(ins, sem_refs[0], sem_refs[1], sem_refs[2] if n_local else None)
        for cp in starts:
            cp.wait_send()
        for cp in arrivals:
            cp.wait_recv()
        for cp in local:
            cp.wait()

    return _pcall(
        body, name=name, in_specs=[HBM_SPEC] * nb + [SEM_SPEC] * ns + [ANY] * len(after), out_specs=[HBM_SPEC] * nb,
        out_shape=[pltpu.HBM(b.shape, b.dtype) for b in thru], input_output_aliases={i: i for i in range(nb)},
        compiler_params=pltpu.CompilerParams(has_side_effects=DATAFLOW),
    )(*thru, *sems, *after)


def _landing(shape, dtype):
    return lax.empty(shape, dtype)


def _plan_gather_ici(n):
    def plan(refs, send_sems, recv_sems, local_sems):
        x, y, c = _me()
        me, sibling = (x, y, c), (x, y, 1 - c)
        chips = _relations(x, y)[1:]
        starts, arrivals, local = [], [], []
        for a in range(n):
            shard, land = refs[a], refs[n + a]
            own = land.at[_dev(*me)]
            local.append(pltpu.make_async_copy(shard, own, local_sems.at[a]))
            starts.append(_remote(shard, own, send_sems, recv_sems, 4 * a, sibling))
            blk = land.at[_dev(*sibling)]
            arrivals.append(_remote(blk, blk, send_sems, recv_sems, 4 * a, me))
            for j, chip in enumerate(chips):
                starts.append(_remote(shard, own, send_sems, recv_sems, 4 * a + 1 + j, (*chip, c)))
                blk = land.at[_dev(*chip, c)]
                arrivals.append(_remote(blk, blk, send_sems, recv_sems, 4 * a + 1 + j, me))
        return starts, arrivals, local
    return plan


def _plan_gather_d2d(n):
    def plan(refs, send_sems, recv_sems, local_sems):
        x, y, c = _me()
        me, sibling = (x, y, c), (x, y, 1 - c)
        starts, arrivals = [], []
        for a in range(n):
            for j, chip in enumerate(_relations(x, y)[1:]):
                blk = refs[a].at[_dev(*chip, c)]
                starts.append(_remote(blk, blk, send_sems, recv_sems, 3 * a + j, sibling))
                blk = refs[a].at[_dev(*chip, 1 - c)]
                arrivals.append(_remote(blk, blk, send_sems, recv_sems, 3 * a + j, me))
        return starts, arrivals, []
    return plan


def _plan_scatter_d2d(n):
    def plan(refs, send_sems, recv_sems, local_sems):
        x, y, c = _me()
        me, sibling = (x, y, c), (x, y, 1 - c)
        starts, arrivals = [], []
        for a in range(n):
            for k in range(4):
                starts.append(_remote(refs[a].at[2 * k + 1 - c], refs[n + a].at[k], send_sems, recv_sems, 4 * a + k, sibling))
                blk = refs[n + a].at[k]
                arrivals.append(_remote(blk, blk, send_sems, recv_sems, 4 * a + k, me))
        return starts, arrivals, []
    return plan


def _plan_scatter_ici(n):
    def plan(refs, send_sems, recv_sems, local_sems):
        x, y, c = _me()
        me = (x, y, c)
        starts, arrivals = [], []
        for a in range(n):
            for j, (cx, cy) in enumerate(_relations(x, y)[1:]):
                starts.append(_remote(refs[a].at[2 * cx + cy], refs[n + a].at[j], send_sems, recv_sems, 3 * a + j, (cx, cy, c)))
                blk = refs[n + a].at[j]
                arrivals.append(_remote(blk, blk, send_sems, recv_sems, 3 * a + j, me))
        return starts, arrivals, []
    return plan


SMALL = ("conv_a_w", "conv_c_w", "conv_c_b", "gate_r_w", "gate_r_b", "gate_i_w", "gate_i_b", "rg_lambda",
         "norm_a", "norm_b", "norm_c", "sinks", "ln_g", "ln_b")
PACK_COLS = 1024


def _pack(arrs):
    flat = jnp.concatenate([a.reshape(-1) for a in arrs])
    pad = (-flat.shape[0]) % (SUBLANES * PACK_COLS)
    return jnp.pad(flat, (0, pad)).reshape(-1, PACK_COLS)


def _unpack(packed, shapes):
    flat = packed.reshape(-1)
    out, off = [], 0
    for s in shapes:
        size = 1
        for dim in s:
            size *= dim
        out.append(flat[off:off + size].reshape(s))
        off += size
    return out


def kernel(x, w_in, conv_a_w, sinks, conv_c_w, conv_c_b, gate_r_w, gate_r_b, gate_i_w, gate_i_b, rg_lambda, norm_a, norm_b, norm_c, w_out, ln_g, ln_b, loss_target, m_w_in, m_conv_a_w, m_sinks, m_conv_c_w, m_conv_c_b, m_gate_r_w, m_gate_r_b, m_gate_i_w, m_gate_i_b, m_rg_lambda, m_norm_a, m_norm_b, m_norm_c, m_w_out, m_ln_g, m_ln_b, v_w_in, v_conv_a_w, v_sinks, v_conv_c_w, v_conv_c_b, v_gate_r_w, v_gate_r_b, v_gate_i_w, v_gate_i_b, v_rg_lambda, v_norm_a, v_norm_b, v_norm_c, v_w_out, v_ln_g, v_ln_b):
    b_loc, seq, d = x.shape
    depth = w_in.shape[0]
    dm = _Dims(d, gate_r_w.shape[1])
    t = b_loc * seq
    nblk = seq // ATT_BLOCK
    alpha = (2.0 * depth) ** 0.25
    ch = dm.wa // N_DEV
    dev = _dev(*_me())

    wt_shard = [jnp.swapaxes(w_in[l], 0, 1).astype(BF16) for l in range(depth)]
    wo_shard = [w_out[l].astype(BF16) for l in range(depth)]
    conv_shard = jnp.concatenate([conv_a_w.reshape(depth * CONV_A, ch), conv_c_w.reshape(depth * CONV_C, ch)], axis=0)
    conv_shard = jnp.pad(conv_shard, ((0, (-conv_shard.shape[0]) % SUBLANES), (0, 0)))
    wt, conv_all = _all_gather([wt_shard[0], conv_shard], "ag_weights")
    conv_all = jnp.swapaxes(conv_all, 0, 1).reshape(conv_all.shape[1], dm.wa)
    conv_a_full = conv_all[:depth * CONV_A].reshape(depth, CONV_A, dm.wa)
    conv_c_full = conv_all[depth * CONV_A:depth * (CONV_A + CONV_C)].reshape(depth, CONV_C, dm.wc)
    sinks_wide = jnp.broadcast_to(sinks[:, :, None], (depth, dm.nq, LANES))

    def layer_params(l):
        return (conv_a_full[l], conv_c_full[l], conv_c_b[l][None], gate_r_w[l], gate_r_b[l][None], gate_i_w[l],
                gate_i_b[l][None], rg_lambda[l][None], norm_a[l][None], norm_b[l][None], norm_c[l][None], sinks_wide[l])

    tm = _tile(t, 1024)
    xs = x.reshape(t, d)
    xb = xs.astype(BF16)
    saved = []
    bias = _attention_bias()
    rows_t, rows_o = dm.in_w // N_DEV, d // N_DEV
    wo0_ici, token = _exchange_start("ag_wo0_ici_start", [wo_shard[0], _landing((N_DEV, rows_o, d), BF16)],
                                     _plan_gather_ici(1), 4, 1, wt)
    wo = None
    for l in range(depth):
        if l + 1 < depth:
            lands = [_landing((N_DEV, rows_t, d), BF16), _landing((N_DEV, rows_o, d), BF16)]
            ici, token = _exchange_start("ag_ici_start_%d" % l, [wt_shard[l + 1], wo_shard[l + 1]] + lands, _plan_gather_ici(2),
                                         8, 2, token if l == 0 else wt)
        elif l > 0:
            token = _no_token()
        h = _matmul(xb, wt.reshape(dm.in_w, d), dims=NT, tm=tm, tn=_tile(dm.in_w, 512), tk=d, out_dtype=F32, name="mm_in",
                    token=token)
        mix, sst = _mixer_fwd(dm, h, bias, layer_params(l), b_loc, nblk)
        if l == 0:
            land = _exchange_wait("ag_wo0_ici_wait", wo0_ici, mix)[1:]
            wo0_d2d, _ = _exchange_start("ag_wo0_d2d_start", land, _plan_gather_d2d(1), 3, 0, mix)
            wo = _exchange_wait("ag_wo0_d2d_wait", wo0_d2d, mix)[0]
        z = _matmul(mix, wo.reshape(d, d), dims=NN, tm=tm, tn=_tile(d, 1024), tk=d, out_dtype=F32, name="mm_out", addend=xs,
                    alpha=alpha)
        saved.append((xb, h, sst, mix, z, wt, wo))
        if l + 1 < depth:
            lands = _exchange_wait("ag_ici_wait_%d" % l, ici, z)[2:]
            d2d, token = _exchange_start("ag_d2d_start_%d" % l, lands, _plan_gather_d2d(2), 6, 0, z)
            xs, xb = _ln_fwd(z, ln_g[l][None], ln_b[l][None], token)
            wt, wo = _exchange_wait("ag_d2d_wait_%d" % l, d2d, xb)

    dy, loss_part = _ln_loss_head(z, ln_g[depth - 1][None], ln_b[depth - 1][None], loss_target.reshape(t, d))
    loss = lax.psum(loss_part[0, 0], ("x", "y", "c"))

    scattered, small = [None] * depth, [None] * depth
    ici, token_ici = None, _no_token()

    def finish_scatter(l, ici, *after):
        scattered[l] = _exchange_wait("rs_ici_wait_%d" % l, ici, *after)

    def scatter_add_start(l, d2d, *after):
        done = _exchange_wait("rs_d2d_wait_%d" % l, d2d, *after)
        sums = [_pair_add(p, g, "rs_add_%d" % i) for i, (p, g) in enumerate(zip(done[:2], done[2:]))]
        lands = [_landing((3, rows_t, d), BF16), _landing((3, rows_o, d), BF16)]
        return _exchange_start("rs_ici_start_%d" % l, sums + lands, _plan_scatter_ici(2), 6, 0, after[0])

    for l in reversed(range(depth)):
        xb, h, sst, mix, z, wt, wo = saved[l]
        dz, dzb, d_lng, d_lnb = _ln_bwd(dy, z, ln_g[l][None], token_ici)
        dmix = _matmul(dzb, wo.reshape(d, d), dims=NT, tm=tm, tn=_tile(d, 1024), tk=d, out_dtype=F32, name="mm_dmix")
        dwo = _matmul(mix, dzb, dims=TN, tm=_tile(d, 1024), tn=_tile(d, 512), tk=t, out_dtype=BF16, name="mm_dwo")
        dh, sm = _mixer_bwd(dm, h, sst, dmix, bias, layer_params(l), b_loc, nblk)
        (d_caw, d_ccw, d_ccb, d_grw, d_grb, d_giw, d_gib, d_lam, d_na, d_nb, d_nc, d_snk) = sm
        small[l] = dict(conv_a_w=d_caw, conv_c_w=d_ccw, conv_c_b=d_ccb[0], gate_r_w=d_grw, gate_r_b=d_grb[0], gate_i_w=d_giw,
                        gate_i_b=d_gib[0], rg_lambda=d_lam[0], norm_a=d_na[0], norm_b=d_nb[0], norm_c=d_nc[0],
                        sinks=d_snk[:, 0], ln_g=d_lng[0], ln_b=d_lnb[0])
        token = _no_token()
        if l == 0:
            g_local = _pack([jnp.stack([small[i][n] for i in range(depth)]) for n in SMALL])
            small_ici, token = _exchange_start("ag_small_ici_start", [g_local, _landing((N_DEV,) + g_local.shape, F32)],
                                               _plan_gather_ici(1), 4, 1, dh)
        dwt = _matmul(dh, xb, dims=TN, tm=_tile(dm.in_w, 1536), tn=_tile(d, 512), tk=t, out_dtype=BF16, name="mm_dwt",
                      token=token)
        if ici is not None:
            finish_scatter(l + 1, ici, dh)
        parts = [dwt.reshape(N_DEV, rows_t, d), dwo.reshape(N_DEV, rows_o, d)]
        lands = [_landing((4, rows_t, d), BF16), _landing((4, rows_o, d), BF16)]
        d2d, token = _exchange_start("rs_d2d_start_%d" % l, parts + lands, _plan_scatter_d2d(2), 8, 0, dh)
        if l > 0:
            dy = _matmul(dh, wt.reshape(dm.in_w, d), dims=NN, tm=tm, tn=_tile(d, 1024), tk=_tile(dm.in_w, 3584), out_dtype=F32,
                         name="mm_dx", addend=dz, alpha=alpha, token=token)
            ici, token_ici = scatter_add_start(l, d2d, dy)
        else:
            small_land = _exchange_wait("ag_small_ici_wait", small_ici, token)[1:]
            small_d2d, token = _exchange_start("ag_small_d2d_start", small_land, _plan_gather_d2d(1), 3, 0, token)
            ici, token_ici = scatter_add_start(l, d2d, token)
            dy = _matmul(dh, wt.reshape(dm.in_w, d), dims=NN, tm=tm, tn=_tile(d, 1024), tk=_tile(dm.in_w, 3584), out_dtype=F32,
                         name="mm_dx", addend=dz, alpha=alpha, token=token_ici)
            g_all = _exchange_wait("ag_small_d2d_wait", small_d2d, dy)[0]
    grad_x = dy.reshape(b_loc, seq, d)

    w_t, m_t, v_t = [jnp.swapaxes(a, 1, 2) for a in (w_in, m_w_in, v_w_in)]
    res_in, res_out = None, None
    for l in reversed(range(depth)):
        if l == 0:
            finish_scatter(0, ici, dy, *([res_in[0], res_out[0]] if depth > 1 else []))
        sums_t, sums_o, far_t, far_o = scattered[l]
        res_in = _adamw_scattered(w_t, m_t, v_t, l, sums_t, far_t, res_in, "adamw_in_%d" % l)
        res_out = _adamw_scattered(w_out, m_w_out, v_w_out, l, sums_o, far_o, res_out, "adamw_out_%d" % l)
    gw_in, dl_in, nm_in, nv_in = [jnp.swapaxes(a, 1, 2) for a in res_in]
    gw_out, dl_out, nm_out, nv_out = res_out

    given = dict(conv_a_w=(conv_a_w, m_conv_a_w, v_conv_a_w), conv_c_w=(conv_c_w, m_conv_c_w, v_conv_c_w),
                 conv_c_b=(conv_c_b, m_conv_c_b, v_conv_c_b), gate_r_w=(gate_r_w, m_gate_r_w, v_gate_r_w),
                 gate_r_b=(gate_r_b, m_gate_r_b, v_gate_r_b), gate_i_w=(gate_i_w, m_gate_i_w, v_gate_i_w),
                 gate_i_b=(gate_i_b, m_gate_i_b, v_gate_i_b), rg_lambda=(rg_lambda, m_rg_lambda, v_rg_lambda),
                 norm_a=(norm_a, m_norm_a, v_norm_a), norm_b=(norm_b, m_norm_b, v_norm_b), norm_c=(norm_c, m_norm_c, v_norm_c),
                 sinks=(sinks, m_sinks, v_sinks), ln_g=(ln_g, m_ln_g, v_ln_g), ln_b=(ln_b, m_ln_b, v_ln_b))
    full_shapes = [jnp.stack([small[l][n] for l in range(depth)]).shape for n in SMALL]

    def mine_of(n, a):
        if n in ("conv_a_w", "conv_c_w"):
            return lax.dynamic_update_slice(jnp.zeros(a.shape[:2] + (dm.wa,), F32), a, (0, 0, dev * ch))
        return a

    packs = [_pack([mine_of(n, given[n][i]) for n in SMALL]) for i in range(3)]
    outs = _adamw(packs[0], g_all, packs[1], packs[2], "adamw_small")
    res = {}
    for kind, packed in zip(("grad", "delta", "new_m", "new_v"), outs):
        for n, a in zip(SMALL, _unpack(packed, full_shapes)):
            if n in ("conv_a_w", "conv_c_w"):
                a = lax.dynamic_slice(a, (0, 0, dev * ch), a.shape[:2] + (ch,))
            res[kind, n] = a
    res.update({("grad", "w_in"): gw_in, ("delta", "w_in"): dl_in, ("new_m", "w_in"): nm_in, ("new_v", "w_in"): nv_in,
                ("grad", "w_out"): gw_out, ("delta", "w_out"): dl_out, ("new_m", "w_out"): nm_out, ("new_v", "w_out"): nv_out})
    order = ("w_in", "conv_a_w", "sinks", "conv_c_w", "conv_c_b", "gate_r_w", "gate_r_b", "gate_i_w", "gate_i_b", "rg_lambda",
             "norm_a", "norm_b", "norm_c", "w_out", "ln_g", "ln_b")
    return (loss, grad_x, *[res[kind, n] for kind in ("grad", "delta", "new_m", "new_v") for n in order])
```

```python
import functools

import jax
import jax.numpy as jnp
from jax import lax
from jax.experimental import pallas as pl
from jax.experimental.pallas import tpu as pltpu

F32 = jnp.float32
BF16 = jnp.bfloat16
MESH = pl.DeviceIdType.MESH
ANY = pl.BlockSpec(memory_space=pl.ANY)

N_DEV = 8
LANES = 128
SUBLANES = 8
HEAD_DIM = 64
KV_GROUP = 8
ATT_BLOCK = 128
ATT_STACK = 8
CONV_A = 3
CONV_C = 4
RG_C = 8.0
LN_EPS = 1e-5
RMS_EPS = 1e-6
NEG_INF = -1e30
ADAM_LR = 0.001
ADAM_B1 = 0.9
ADAM_B2 = 0.999
ADAM_EPS = 1e-08
ADAM_WD = 0.01
ADAM_STEP = 10
VMEM_LIMIT = 56 * 1024 * 1024

NN = ((1,), (0,))
NT = ((1,), (1,))
TN = ((0,), (0,))


def _pcall(body, **kw):
    return pl.pallas_call(body, **kw)


def _roll(x, shift, axis):
    return pltpu.roll(x, shift, axis)


def _params(sem=None, vmem=VMEM_LIMIT):
    return pltpu.CompilerParams(dimension_semantics=sem, vmem_limit_bytes=vmem)


def _dot(a, b, dims):
    return lax.dot_general(a.astype(BF16), b.astype(BF16), (dims, ((), ())), preferred_element_type=F32)


@jax.custom_vjp
def _mm(a, b):
    return _dot(a, b, NN)


def _mm_fwd(a, b):
    return _dot(a, b, NN), (a.astype(BF16), b.astype(BF16))


def _mm_bwd(res, g):
    a, b = res
    return _dot(g, b, NT), _dot(a, g, TN)


_mm.defvjp(_mm_fwd, _mm_bwd)


@jax.custom_vjp
def _mm_nt(a, b):
    return _dot(a, b, NT)


def _mm_nt_fwd(a, b):
    return _dot(a, b, NT), (a.astype(BF16), b.astype(BF16))


def _mm_nt_bwd(res, g):
    a, b = res
    return _dot(g, b, NN), _dot(g, a, TN)


_mm_nt.defvjp(_mm_nt_fwd, _mm_nt_bwd)


def _rows(shape):
    return lax.broadcasted_iota(jnp.int32, shape, 0)


@functools.partial(jax.custom_vjp, nondiff_argnums=(2,))
def _shift_halo(u, prev, k):
    r, c = u.shape
    fill = jnp.concatenate([_roll(prev, k, 0), jnp.zeros((r - SUBLANES, c), u.dtype)], axis=0)
    return jnp.where(_rows((r, c)) < k, fill, _roll(u, k, 0))


def _shift_halo_fwd(u, prev, k):
    return _shift_halo(u, prev, k), None


def _shift_halo_bwd(k, _, g):
    r, c = g.shape
    du = jnp.where(_rows((r, c)) < r - k, _roll(g, r - k, 0), 0.0)
    dprev = jnp.where(_rows((SUBLANES, c)) >= SUBLANES - k, _roll(g[0:SUBLANES], SUBLANES - k, 0), 0.0)
    return du, dprev


_shift_halo.defvjp(_shift_halo_fwd, _shift_halo_bwd)


@functools.partial(jax.custom_vjp, nondiff_argnums=(1, 2))
def _shift_fill(u, k, fill):
    return jnp.where(_rows(u.shape) < k, fill, _roll(u, k, 0))


def _shift_fill_fwd(u, k, fill):
    return _shift_fill(u, k, fill), None


def _shift_fill_bwd(k, fill, _, g):
    r = g.shape[0]
    return (jnp.where(_rows(g.shape) < r - k, _roll(g, r - k, 0), 0.0),)


_shift_fill.defvjp(_shift_fill_fwd, _shift_fill_bwd)


@jax.custom_vjp
def _swap_halves(x):
    return _roll(x, HEAD_DIM, 1)


_swap_halves.defvjp(lambda x: (_roll(x, HEAD_DIM, 1), None), lambda _, g: (_roll(g, HEAD_DIM, 1),))


@functools.partial(jax.custom_vjp, nondiff_argnums=(1,))
def _split_rows(x, n):
    r = x.shape[0] // n
    return tuple(x[i * r:(i + 1) * r] for i in range(n))


def _split_rows_fwd(x, n):
    return _split_rows(x, n), None


def _split_rows_bwd(n, _, gs):
    return (jnp.concatenate(list(gs), axis=0),)


_split_rows.defvjp(_split_rows_fwd, _split_rows_bwd)


def _logistic(x):
    return 1.0 / (1.0 + jnp.exp(-x))


@jax.custom_vjp
def _sigmoid(x):
    return _logistic(x)


def _sigmoid_fwd(x):
    s = _logistic(x)
    return s, s


_sigmoid.defvjp(_sigmoid_fwd, lambda s, g: (g * s * (1.0 - s),))


@jax.custom_vjp
def _silu(x):
    return x * _logistic(x)


def _silu_fwd(x):
    s = _logistic(x)
    return x * s, (x, s)


_silu.defvjp(_silu_fwd, lambda res, g: (g * res[1] * (1.0 + res[0] * (1.0 - res[1])),))


def _log_sigmoid(x):
    return -(jnp.maximum(-x, 0.0) + jnp.log1p(jnp.exp(-jnp.abs(x))))


@jax.custom_vjp
def _neg_expm1(x):
    series = x * (1 + x * (1 / 2) * (1 + x * (1 / 3) * (1 + x * (1 / 4) * (1 + x * (1 / 5) * (1 + x * (1 / 6) * (1 + x * (1 / 7)))))))
    return -jnp.where(jnp.abs(x) < 0.25, series, jnp.exp(x) - 1.0)


_neg_expm1.defvjp(lambda x: (_neg_expm1(x), x), lambda x, g: (-g * jnp.exp(x),))


def _shift_up(x, k, fill):
    r = x.shape[0]
    return jnp.where(_rows(x.shape) < r - k, _roll(x, r - k, 0), fill)


@jax.custom_vjp
def _scan_block(a, u, s_prev):
    acc_a, acc_b = a, u
    d = 1
    while d < a.shape[0]:
        acc_b = acc_a * _shift_fill(acc_b, d, 0.0) + acc_b
        acc_a = acc_a * _shift_fill(acc_a, d, 1.0)
        d *= 2
    return acc_a * s_prev + acc_b


def _scan_block_fwd(a, u, s_prev):
    h = _scan_block(a, u, s_prev)
    return h, (a, h, s_prev)


def _scan_block_bwd(res, dh):
    a, h, s_prev = res
    acc_a, acc_g = _shift_up(a, 1, 0.0), dh
    d = 1
    while d < a.shape[0]:
        acc_g = acc_a * _shift_up(acc_g, d, 0.0) + acc_g
        acc_a = acc_a * _shift_up(acc_a, d, 1.0)
        d *= 2
    h_prev = jnp.where(_rows(h.shape) < 1, s_prev, _roll(h, 1, 0))
    first = jnp.sum(jnp.where(_rows(h.shape) < 1, a * acc_g, 0.0), axis=0, keepdims=True)
    return acc_g * h_prev, acc_g, first


_scan_block.defvjp(_scan_block_fwd, _scan_block_bwd)


def _last_row(h):
    return jnp.sum(jnp.where(_rows(h.shape) == h.shape[0] - 1, h, 0.0), axis=0, keepdims=True)


def _branch_a(ab, ac, ax, ag, acp, axp, w0, w1, w2, na):
    yas = []
    for j in range(len(ab)):
        u = ac[j] * ax[j]
        up = acp[j] * axp[j]
        yas.append(ab[j] * (w2[j] * u + w1[j] * _shift_halo(u, up, 1) + w0[j] * _shift_halo(u, up, 2)))
    ms = sum(jnp.sum(ya * ya, axis=1, keepdims=True) for ya in yas) * (1.0 / (LANES * len(ab)))
    inv = lax.rsqrt(ms + RMS_EPS)
    return [yas[j] * inv * na[j] * _silu(ag[j]) for j in range(len(ab))]


def _branch_c(cx, cg, cxp, sp, wc, bc, wr, br, wi, bi, lam, nc):
    hs, lasts = [], []
    for j in range(len(cx)):
        xc = (wc[3][j] * cx[j] + wc[2][j] * _shift_halo(cx[j], cxp[j], 1) + wc[1][j] * _shift_halo(cx[j], cxp[j], 2)
              + wc[0][j] * _shift_halo(cx[j], cxp[j], 3) + bc[j])
        r = _sigmoid(_mm(xc, wr[j]) + br[j])
        i = _sigmoid(_mm(xc, wi[j]) + bi[j])
        log_a = RG_C * r * _log_sigmoid(lam[j])
        a = jnp.exp(log_a)
        u = jnp.sqrt(_neg_expm1(2.0 * log_a)) * (i * xc)
        h = _scan_block(a, u, sp[j])
        hs.append(h)
        lasts.append(_last_row(h))
    width = LANES * len(cx)
    ms = sum(jnp.sum(h * h, axis=1, keepdims=True) for h in hs) * (1.0 / width)
    inv = lax.rsqrt(ms + RMS_EPS)
    return [hs[j] * inv * nc[j] * _silu(cg[j]) for j in range(len(cx))], lasts


def _attention_bias():
    qi = (jnp.arange(ATT_STACK * ATT_BLOCK) % ATT_BLOCK)[:, None]
    kj = jnp.arange(2 * ATT_BLOCK)[None, :]
    dist = qi + ATT_BLOCK - kj
    band = (dist >= 0) & (dist < ATT_BLOCK)
    return jnp.where(jnp.stack([band & (kj >= ATT_BLOCK), band]), 0.0, NEG_INF).astype(F32)


def _branch_b(q, k, v, kp, vp, bg, snk, nb, bias):
    rows = ATT_BLOCK
    n_kv = 2 * len(k)
    lane0 = lax.broadcasted_iota(jnp.int32, (ATT_STACK * rows, LANES), 1) == 0
    upper = lax.broadcasted_iota(jnp.int32, (2 * rows, LANES), 1) >= HEAD_DIM
    heads = [None] * (n_kv * KV_GROUP)
    for g in range(n_kv):
        half = g % 2
        keep = upper if half else jnp.logical_not(upper)
        kc = jnp.where(keep, jnp.concatenate([kp[g // 2], k[g // 2]], axis=0), 0.0)
        vc = jnp.where(keep, jnp.concatenate([vp[g // 2], v[g // 2]], axis=0), 0.0)
        for first in range(g * KV_GROUP, (g + 1) * KV_GROUP, ATT_STACK):
            hs = range(first, first + ATT_STACK)
            qg = jnp.concatenate([q[h // 2] if h % 2 == half else _swap_halves(q[h // 2]) for h in hs], axis=0)
            s = _mm_nt(qg * (HEAD_DIM ** -0.5), kc) + bias
            sink = jnp.concatenate([jnp.broadcast_to(snk[h], (rows, LANES)) for h in hs], axis=0)
            sink = jnp.sum(jnp.where(lane0, sink, 0.0), axis=1, keepdims=True)
            m = lax.stop_gradient(jnp.maximum(jnp.max(s, axis=1, keepdims=True), sink))
            p = jnp.exp(s - m)
            inv = 1.0 / (jnp.sum(p, axis=1, keepdims=True) + jnp.exp(sink - m))
            o = _split_rows(_mm(p * inv, vc), ATT_STACK)
            for i, h in enumerate(hs):
                heads[h] = o[i] if h % 2 == half else _swap_halves(o[i])
    yb = [heads[2 * j] + heads[2 * j + 1] for j in range(len(q))]
    width = LANES * len(q)
    ms = sum(jnp.sum(y * y, axis=1, keepdims=True) for y in yb) * (1.0 / width)
    inv = lax.rsqrt(ms + RMS_EPS)
    return [yb[j] * inv * nb[j] * _silu(bg[j]) for j in range(len(q))]


class _Dims:
    def __init__(self, d_model, n_rg_heads):
        self.d = d_model
        self.wa = d_model // 4
        self.wb = d_model // 2
        self.wc = d_model // 4
        self.kvw = self.wb // KV_GROUP
        self.nq = self.wb // HEAD_DIM
        self.in_w = 4 * self.wa + 2 * self.wb + 2 * self.kvw + 2 * self.wc
        self.o_q = 4 * self.wa
        self.o_k = self.o_q + self.wb
        self.o_v = self.o_k + self.kvw
        self.o_bg = self.o_v + self.kvw
        self.o_cx = self.o_bg + self.wb
        self.o_cg = self.o_cx + self.wc
        self.nh = n_rg_heads
        assert self.wc // n_rg_heads == LANES and self.kvw % LANES == 0
        assert self.o_k % self.kvw == 0 and self.o_cx % (self.wc // 2) == 0


def _chunks(ref, rows, off, width):
    return [ref[rows, off + LANES * j: off + LANES * (j + 1)] for j in range(width // LANES)]


def _read_params(dm, caw, ccw, ccb, grw, grb, giw, gib, lam, na, nb, nc, snk):
    row = slice(0, 1)
    return dict(
        wa=[_chunks(caw, slice(k, k + 1), 0, dm.wa) for k in range(CONV_A)], na=_chunks(na, row, 0, dm.wa),
        wc=[_chunks(ccw, slice(k, k + 1), 0, dm.wc) for k in range(CONV_C)], bc=_chunks(ccb, row, 0, dm.wc),
        wr=[grw[j] for j in range(dm.nh)], br=_chunks(grb, row, 0, dm.wc),
        wi=[giw[j] for j in range(dm.nh)], bi=_chunks(gib, row, 0, dm.wc),
        lam=_chunks(lam, row, 0, dm.wc), nc=_chunks(nc, row, 0, dm.wc),
        nb=_chunks(nb, row, 0, dm.wb), snk=[snk[h:h + 1, :] for h in range(dm.nq)])


def _param_specs(dm):
    shapes = [(CONV_A, dm.wa), (CONV_C, dm.wc), (1, dm.wc), (dm.nh, LANES, LANES), (1, dm.wc), (dm.nh, LANES, LANES),
              (1, dm.wc), (1, dm.wc), (1, dm.wa), (1, dm.wb), (1, dm.wc), (dm.nq, LANES)]
    specs = [pl.BlockSpec(s, (lambda b, n, _r=len(s): (0,) * _r)) for s in shapes]
    return shapes, specs


def _bias_spec():
    return pl.BlockSpec((2, ATT_STACK * ATT_BLOCK, 2 * ATT_BLOCK), lambda *_: (0, 0, 0))


def _mixer_fwd(dm, h, bias, prm, b_loc, nblk):
    t = h.shape[0]
    r = ATT_BLOCK
    tail = slice(r - SUBLANES, r)

    def body(h_ref, bias_ref, caw, ccw, ccb, grw, grb, giw, gib, lam, na, nb, nc, snk, mix_ref, sst_ref, kp, vp, acp, axp, cxp, sp):
        n = pl.program_id(1)

        @pl.when(n == 0)
        def _():
            for ref in (kp, vp, acp, axp, cxp, sp):
                ref[...] = jnp.zeros(ref.shape, ref.dtype)

        p = _read_params(dm, caw, ccw, ccb, grw, grb, giw, gib, lam, na, nb, nc, snk)
        full = slice(None)
        mix_a = _branch_a(*[_chunks(h_ref, full, k * dm.wa, dm.wa) for k in range(4)], _chunks(acp, full, 0, dm.wa),
                          _chunks(axp, full, 0, dm.wa), p["wa"][0], p["wa"][1], p["wa"][2], p["na"])
        for j, ma in enumerate(mix_a):
            mix_ref[:, LANES * j: LANES * (j + 1)] = ma.astype(BF16)
        bias = bias_ref[jnp.where(n == 0, 0, 1)]
        mix_b = _branch_b(_chunks(h_ref, full, dm.o_q, dm.wb), _chunks(h_ref, full, dm.o_k, dm.kvw),
                          _chunks(h_ref, full, dm.o_v, dm.kvw), _chunks(kp, full, 0, dm.kvw), _chunks(vp, full, 0, dm.kvw),
                          _chunks(h_ref, full, dm.o_bg, dm.wb), p["snk"], p["nb"], bias)
        for j, mb in enumerate(mix_b):
            mix_ref[:, dm.wa + LANES * j: dm.wa + LANES * (j + 1)] = mb.astype(BF16)
        sst_ref[0] = sp[...]
        mix_c, lasts = _branch_c(_chunks(h_ref, full, dm.o_cx, dm.wc), _chunks(h_ref, full, dm.o_cg, dm.wc),
                                 _chunks(cxp, full, 0, dm.wc), _chunks(sp, slice(0, 1), 0, dm.wc), p["wc"], p["bc"],
                                 p["wr"], p["br"], p["wi"], p["bi"], p["lam"], p["nc"])
        o_c = dm.wa + dm.wb
        for j, mc in enumerate(mix_c):
            mix_ref[:, o_c + LANES * j: o_c + LANES * (j + 1)] = mc.astype(BF16)
            sp[:, LANES * j: LANES * (j + 1)] = jnp.broadcast_to(lasts[j], (SUBLANES, LANES))
        kp[...] = h_ref[:, dm.o_k:dm.o_k + dm.kvw]
        vp[...] = h_ref[:, dm.o_v:dm.o_v + dm.kvw]
        acp[...] = h_ref[tail, dm.wa:2 * dm.wa]
        axp[...] = h_ref[tail, 2 * dm.wa:3 * dm.wa]
        cxp[...] = h_ref[tail, dm.o_cx:dm.o_cx + dm.wc]

    _, pspecs = _param_specs(dm)
    return _pcall(
        body, name="mixer_fwd", grid=(b_loc, nblk),
        in_specs=[pl.BlockSpec((r, dm.in_w), lambda b, n: (b * nblk + n, 0)), _bias_spec()] + pspecs,
        out_specs=[pl.BlockSpec((r, dm.d), lambda b, n: (b * nblk + n, 0)),
                   pl.BlockSpec((1, SUBLANES, dm.wc), lambda b, n: (b * nblk + n, 0, 0))],
        out_shape=[jax.ShapeDtypeStruct((t, dm.d), BF16), jax.ShapeDtypeStruct((b_loc * nblk, SUBLANES, dm.wc), F32)],
        scratch_shapes=[pltpu.VMEM((r, dm.kvw), F32), pltpu.VMEM((r, dm.kvw), F32), pltpu.VMEM((SUBLANES, dm.wa), F32),
                        pltpu.VMEM((SUBLANES, dm.wa), F32), pltpu.VMEM((SUBLANES, dm.wc), F32),
                        pltpu.VMEM((SUBLANES, dm.wc), F32)],
        compiler_params=_params(("arbitrary", "arbitrary")),
    )(h, bias, *prm)


def _mixer_bwd(dm, h, sst, dmix, bias, prm, b_loc, nblk):
    t = h.shape[0]
    r = ATT_BLOCK
    rb8 = r // SUBLANES
    n_small = 12

    def body(h_ref, kp_ref, vp_ref, acp_ref, axp_ref, cxp0_ref, cxp1_ref, sst_ref, dmix_ref, bias_ref,
             caw, ccw, ccb, grw, grb, giw, gib, lam, na, nb, nc, snk,
             dh_ref, d_caw, d_ccw, d_ccb, d_grw, d_grb, d_giw, d_gib, d_lam, d_na, d_nb, d_nc, d_snk,
             dkp, dvp, dacp, daxp, dcxp, dsp):
        step = pl.program_id(1)
        n = nblk - 1 - step

        @pl.when(step == 0)
        def _():
            for ref in (dkp, dvp, dacp, daxp, dcxp, dsp):
                ref[...] = jnp.zeros(ref.shape, ref.dtype)

        @pl.when((step == 0) & (pl.program_id(0) == 0))
        def _():
            for ref in (d_caw, d_ccw, d_ccb, d_grw, d_grb, d_giw, d_gib, d_lam, d_na, d_nb, d_nc, d_snk):
                ref[...] = jnp.zeros(ref.shape, ref.dtype)

        p = _read_params(dm, caw, ccw, ccb, grw, grb, giw, gib, lam, na, nb, nc, snk)
        has_prev = jnp.where(n > 0, 1.0, 0.0)
        full = slice(None)
        pad = jnp.zeros((r - SUBLANES, LANES), F32)

        def with_tail(own, carry):
            z = jnp.zeros((r - SUBLANES, own.shape[1]), F32)
            return own + jnp.concatenate([z, carry], axis=0)

        a_in = (*[_chunks(h_ref, full, k * dm.wa, dm.wa) for k in range(4)],
                [c * has_prev for c in _chunks(acp_ref, full, 0, dm.wa)], [c * has_prev for c in _chunks(axp_ref, full, 0, dm.wa)],
                p["wa"][0], p["wa"][1], p["wa"][2], p["na"])
        _, vjp_a = jax.vjp(_branch_a, *a_in)
        g_ab, g_ac, g_ax, g_ag, g_acp, g_axp, g_w0, g_w1, g_w2, g_na = vjp_a(_chunks(dmix_ref, full, 0, dm.wa))
        for j in range(len(g_ab)):
            cols = slice(LANES * j, LANES * (j + 1))
            dh_ref[:, LANES * j: LANES * (j + 1)] = g_ab[j].astype(BF16)
            dh_ref[:, dm.wa + LANES * j: dm.wa + LANES * (j + 1)] = with_tail(g_ac[j], dacp[:, cols]).astype(BF16)
            dh_ref[:, 2 * dm.wa + LANES * j: 2 * dm.wa + LANES * (j + 1)] = with_tail(g_ax[j], daxp[:, cols]).astype(BF16)
            dh_ref[:, 3 * dm.wa + LANES * j: 3 * dm.wa + LANES * (j + 1)] = g_ag[j].astype(BF16)
            dacp[:, cols] = g_acp[j]
            daxp[:, cols] = g_axp[j]
            for k, gw in enumerate((g_w0, g_w1, g_w2)):
                d_caw[k:k + 1, cols] += gw[j]
            d_na[:, cols] += g_na[j]

        bias = bias_ref[jnp.where(n == 0, 0, 1)]
        kp_in = [c * has_prev for c in _chunks(kp_ref, full, 0, dm.kvw)]
        vp_in = [c * has_prev for c in _chunks(vp_ref, full, 0, dm.kvw)]
        b_in = (_chunks(h_ref, full, dm.o_q, dm.wb), _chunks(h_ref, full, dm.o_k, dm.kvw), _chunks(h_ref, full, dm.o_v, dm.kvw),
                kp_in, vp_in, _chunks(h_ref, full, dm.o_bg, dm.wb), p["snk"], p["nb"])
        _, vjp_b = jax.vjp(lambda *a: _branch_b(*a, bias), *b_in)
        g_q, g_k, g_v, g_kp, g_vp, g_bg, g_snk, g_nb = vjp_b(_chunks(dmix_ref, full, dm.wa, dm.wb))
        for j in range(len(g_q)):
            dh_ref[:, dm.o_q + LANES * j: dm.o_q + LANES * (j + 1)] = g_q[j].astype(BF16)
            dh_ref[:, dm.o_bg + LANES * j: dm.o_bg + LANES * (j + 1)] = g_bg[j].astype(BF16)
            d_nb[:, LANES * j: LANES * (j + 1)] += g_nb[j]
        for j in range(len(g_k)):
            cols = slice(LANES * j, LANES * (j + 1))
            dh_ref[:, dm.o_k + LANES * j: dm.o_k + LANES * (j + 1)] = (g_k[j] + dkp[:, cols]).astype(BF16)
            dh_ref[:, dm.o_v + LANES * j: dm.o_v + LANES * (j + 1)] = (g_v[j] + dvp[:, cols]).astype(BF16)
            dkp[:, cols] = g_kp[j]
            dvp[:, cols] = g_vp[j]
        for hd in range(dm.nq):
            d_snk[hd:hd + 1, :] += g_snk[hd]

        half_c = dm.wc // 2
        cxp_in = ([c * has_prev for c in _chunks(cxp0_ref, full, 0, half_c)]
                  + [c * has_prev for c in _chunks(cxp1_ref, full, 0, half_c)])
        c_in = (_chunks(h_ref, full, dm.o_cx, dm.wc), _chunks(h_ref, full, dm.o_cg, dm.wc), cxp_in,
                [sst_ref[0, 0:1, LANES * j: LANES * (j + 1)] for j in range(dm.nh)], p["wc"], p["bc"], p["wr"], p["br"], p["wi"], p["bi"],
                p["lam"], p["nc"])
        _, vjp_c = jax.vjp(_branch_c, *c_in)
        ct_last = [dsp[0:1, LANES * j: LANES * (j + 1)] for j in range(dm.nh)]
        g_cx, g_cg, g_cxp, g_sp, g_wc, g_bc, g_wr, g_br, g_wi, g_bi, g_lam, g_nc = vjp_c(
            (_chunks(dmix_ref, full, dm.wa + dm.wb, dm.wc), ct_last))
        for j in range(dm.nh):
            cols = slice(LANES * j, LANES * (j + 1))
            tot = g_cx[j] + jnp.concatenate([pad, dcxp[:, cols]], axis=0)
            dh_ref[:, dm.o_cx + LANES * j: dm.o_cx + LANES * (j + 1)] = tot.astype(BF16)
            dh_ref[:, dm.o_cg + LANES * j: dm.o_cg + LANES * (j + 1)] = g_cg[j].astype(BF16)
            dcxp[:, cols] = g_cxp[j]
            dsp[:, cols] = jnp.broadcast_to(g_sp[j], (SUBLANES, LANES))
            for k in range(CONV_C):
                d_ccw[k:k + 1, cols] += g_wc[k][j]
            d_ccb[:, cols] += g_bc[j]
            d_grw[j] += g_wr[j]
            d_grb[:, cols] += g_br[j]
            d_giw[j] += g_wi[j]
            d_gib[:, cols] += g_bi[j]
            d_lam[:, cols] += g_lam[j]
            d_nc[:, cols] += g_nc[j]

    def blk(b, s):
        return b * nblk + (nblk - 1 - s)

    def prev_rows8(b, s):
        return jnp.maximum(blk(b, s) * rb8 - 1, 0)

    pshapes, pspecs = _param_specs(dm)
    half_c = dm.wc // 2
    in_specs = [
        pl.BlockSpec((r, dm.in_w), lambda b, s: (blk(b, s), 0)),
        pl.BlockSpec((r, dm.kvw), lambda b, s: (jnp.maximum(blk(b, s) - 1, 0), dm.o_k // dm.kvw)),
        pl.BlockSpec((r, dm.kvw), lambda b, s: (jnp.maximum(blk(b, s) - 1, 0), dm.o_v // dm.kvw)),
        pl.BlockSpec((SUBLANES, dm.wa), lambda b, s: (prev_rows8(b, s), 1)),
        pl.BlockSpec((SUBLANES, dm.wa), lambda b, s: (prev_rows8(b, s), 2)),
        pl.BlockSpec((SUBLANES, half_c), lambda b, s: (prev_rows8(b, s), dm.o_cx // half_c)),
        pl.BlockSpec((SUBLANES, half_c), lambda b, s: (prev_rows8(b, s), dm.o_cx // half_c + 1)),
        pl.BlockSpec((1, SUBLANES, dm.wc), lambda b, s: (blk(b, s), 0, 0)),
        pl.BlockSpec((r, dm.d), lambda b, s: (blk(b, s), 0)),
        _bias_spec(),
    ] + pspecs
    outs = _pcall(
        body, name="mixer_bwd", grid=(b_loc, nblk), in_specs=in_specs,
        out_specs=[pl.BlockSpec((r, dm.in_w), lambda b, s: (blk(b, s), 0))] + pspecs,
        out_shape=[jax.ShapeDtypeStruct((t, dm.in_w), BF16)] + [jax.ShapeDtypeStruct(s, F32) for s in pshapes],
        scratch_shapes=[pltpu.VMEM((r, dm.kvw), F32), pltpu.VMEM((r, dm.kvw), F32), pltpu.VMEM((SUBLANES, dm.wa), F32),
                        pltpu.VMEM((SUBLANES, dm.wa), F32), pltpu.VMEM((SUBLANES, dm.wc), F32),
                        pltpu.VMEM((SUBLANES, dm.wc), F32)],
        compiler_params=_params(("arbitrary", "arbitrary")),
    )(h, h, h, h, h, h, h, sst, dmix, bias, *prm)
    assert len(outs) == 1 + n_small
    return outs[0], outs[1:]


def _token_spec():
    return pl.BlockSpec((SUBLANES, LANES), lambda *_: (0, 0))


def _no_token():
    return jnp.zeros((SUBLANES, LANES), F32)


def _matmul(a, b, *, dims, tm, tn, tk, out_dtype, name, addend=None, alpha=None, token=None):
    if dims == TN:
        (k_dim, m), n_dim = a.shape, b.shape[1]
        a_spec = pl.BlockSpec((tk, tm), lambda i, j, k: (k, i))
    else:
        (m, k_dim), n_dim = a.shape, (b.shape[0] if dims == NT else b.shape[1])
        a_spec = pl.BlockSpec((tm, tk), lambda i, j, k: (i, k))
    b_spec = pl.BlockSpec((tn, tk), lambda i, j, k: (j, k)) if dims == NT else pl.BlockSpec((tk, tn), lambda i, j, k: (k, j))
    assert m % tm == 0 and n_dim % tn == 0 and k_dim % tk == 0, (a.shape, b.shape, tm, tn, tk)
    nk = k_dim // tk
    o_spec = pl.BlockSpec((tm, tn), lambda i, j, k: (i, j))

    def body(*refs):
        a_ref, b_ref = refs[0], refs[1]
        add_ref = refs[2] if addend is not None else None
        o_ref, acc_ref = refs[-2], refs[-1]
        k = pl.program_id(2)
        part = lax.dot_general(a_ref[...], b_ref[...], (dims, ((), ())), preferred_element_type=F32)

        def finish(acc):
            if add_ref is not None:
                acc = acc + alpha * add_ref[...]
            o_ref[...] = acc.astype(out_dtype)

        if nk == 1:
            finish(part)
        else:
            @pl.when(k == 0)
            def _():
                acc_ref[...] = part

            @pl.when((k > 0) & (k < nk - 1))
            def _():
                acc_ref[...] += part

            @pl.when(k == nk - 1)
            def _():
                finish(acc_ref[...] + part)

    ins = [a, b] + ([addend] if addend is not None else []) + ([token] if token is not None else [])
    in_specs = [a_spec, b_spec] + ([o_spec] if addend is not None else []) + ([_token_spec()] if token is not None else [])
    return _pcall(
        body, name=name, grid=(m // tm, n_dim // tn, nk), in_specs=in_specs, out_specs=o_spec,
        out_shape=jax.ShapeDtypeStruct((m, n_dim), out_dtype),
        scratch_shapes=[pltpu.VMEM((tm, tn) if nk > 1 else (SUBLANES, LANES), F32)],
        compiler_params=_params(("parallel", "parallel", "arbitrary")),
    )(*ins)


def _tile(n, want, quantum=LANES):
    if n <= want:
        return n
    for cand in range(want - want % quantum, 0, -quantum):
        if n % cand == 0:
            return cand
    return n


def _row_tile(t, d, elems=1 << 19):
    return _tile(t, max(2 * SUBLANES, elems // d), 2 * SUBLANES)


STREAM_ELEMS = 1 << 20


def _ln_fwd(z, g, b, token):
    t, d = z.shape
    tr = _row_tile(t, d, STREAM_ELEMS)

    def body(z_ref, g_ref, b_ref, _, y_ref, yb_ref):
        zz = z_ref[...]
        mu = jnp.mean(zz, axis=1, keepdims=True)
        zc = zz - mu
        var = jnp.mean(zc * zc, axis=1, keepdims=True)
        y = zc * lax.rsqrt(var + LN_EPS) * g_ref[...] + b_ref[...]
        y_ref[...] = y
        yb_ref[...] = y.astype(BF16)

    row = pl.BlockSpec((tr, d), lambda i: (i, 0))
    vec = pl.BlockSpec((1, d), lambda i: (0, 0))
    return _pcall(body, name="ln_fwd", grid=(t // tr,), in_specs=[row, vec, vec, _token_spec()], out_specs=[row, row],
                  out_shape=[jax.ShapeDtypeStruct((t, d), F32), jax.ShapeDtypeStruct((t, d), BF16)],
                  compiler_params=_params(("parallel",)))(z, g, b, token)


def _ln_bwd(dy, z, g, token):
    t, d = z.shape
    tr = _row_tile(t, d, STREAM_ELEMS)

    def body(dy_ref, z_ref, g_ref, _, dz_ref, dzb_ref, dg_ref, db_ref):
        @pl.when(pl.program_id(0) == 0)
        def _():
            dg_ref[...] = jnp.zeros(dg_ref.shape, F32)
            db_ref[...] = jnp.zeros(db_ref.shape, F32)

        zz = z_ref[...]
        dyy = dy_ref[...]
        mu = jnp.mean(zz, axis=1, keepdims=True)
        zc = zz - mu
        rstd = lax.rsqrt(jnp.mean(zc * zc, axis=1, keepdims=True) + LN_EPS)
        xhat = zc * rstd
        dyg = dyy * g_ref[...]
        dz = rstd * (dyg - jnp.mean(dyg, axis=1, keepdims=True) - xhat * jnp.mean(dyg * xhat, axis=1, keepdims=True))
        dz_ref[...] = dz
        dzb_ref[...] = dz.astype(BF16)
        dg_ref[...] += jnp.sum(dyy * xhat, axis=0, keepdims=True)
        db_ref[...] += jnp.sum(dyy, axis=0, keepdims=True)

    row = pl.BlockSpec((tr, d), lambda i: (i, 0))
    vec = pl.BlockSpec((1, d), lambda i: (0, 0))
    return _pcall(body, name="ln_bwd", grid=(t // tr,), in_specs=[row, row, vec, _token_spec()], out_specs=[row, row, vec, vec],
                  out_shape=[jax.ShapeDtypeStruct((t, d), F32), jax.ShapeDtypeStruct((t, d), BF16),
                             jax.ShapeDtypeStruct((1, d), F32), jax.ShapeDtypeStruct((1, d), F32)],
                  compiler_params=_params(("arbitrary",)))(dy, z, g, token)


def _ln_loss_head(z, g, b, target):
    t, d = z.shape
    tr = _row_tile(t, d, STREAM_ELEMS)

    def body(z_ref, g_ref, b_ref, t_ref, dy_ref, loss_ref):
        @pl.when(pl.program_id(0) == 0)
        def _():
            loss_ref[...] = jnp.zeros(loss_ref.shape, F32)

        zz = z_ref[...]
        mu = jnp.mean(zz, axis=1, keepdims=True)
        zc = zz - mu
        var = jnp.mean(zc * zc, axis=1, keepdims=True)
        err = zc * lax.rsqrt(var + LN_EPS) * g_ref[...] + b_ref[...] - t_ref[...]
        dy_ref[...] = err * (1.0 / d)
        per_token = jnp.sum(err * err, axis=1, keepdims=True) * (1.0 / d)
        loss_ref[...] += 0.5 * jnp.sum(per_token, axis=0, keepdims=True)

    row = pl.BlockSpec((tr, d), lambda i: (i, 0))
    vec = pl.BlockSpec((1, d), lambda i: (0, 0))
    one = pl.BlockSpec((1, 1), lambda i: (0, 0))
    return _pcall(body, name="ln_loss_head", grid=(t // tr,), in_specs=[row, vec, vec, row], out_specs=[row, one],
                  out_shape=[jax.ShapeDtypeStruct((t, d), F32), jax.ShapeDtypeStruct((1, 1), F32)],
                  compiler_params=_params(("arbitrary",)))(z, g, b, target)


def _adamw_scattered(w, m, v, layer, sums, far, prev, name):
    n_layers, r, c = w.shape
    tr = _row_tile(r, c)

    def body(*refs):
        w_ref, m_ref, v_ref, s_ref, f_ref = refs[:5]
        go_ref, d_ref, mo_ref, vo_ref = refs[-4:]
        gg = s_ref[...].astype(F32)
        for i in range(3):
            gg = gg + f_ref[i:i + 1].astype(F32)
        m_new = ADAM_B1 * m_ref[...] + (1.0 - ADAM_B1) * gg
        v_new = ADAM_B2 * v_ref[...] + (1.0 - ADAM_B2) * (gg * gg)
        m_hat = m_new / (1.0 - ADAM_B1 ** ADAM_STEP)
        v_hat = v_new / (1.0 - ADAM_B2 ** ADAM_STEP)
        go_ref[...] = gg
        d_ref[...] = -ADAM_LR * (m_hat / (jnp.sqrt(v_hat) + ADAM_EPS) + ADAM_WD * w_ref[...])
        mo_ref[...] = m_new
        vo_ref[...] = v_new

    own = pl.BlockSpec((1, tr, c), lambda i: (layer, i, 0))
    in_specs = [own, own, own, pl.BlockSpec((1, tr, c), lambda i: (2 * lax.axis_index("x") + lax.axis_index("y"), i, 0)),
                pl.BlockSpec((3, tr, c), lambda i: (0, i, 0))] + ([ANY] * 4 if prev is not None else [])
    return _pcall(body, name=name, grid=(r // tr,), in_specs=in_specs, out_specs=[own] * 4,
                  out_shape=[jax.ShapeDtypeStruct((n_layers, r, c), F32)] * 4,
                  input_output_aliases={5 + i: i for i in range(4)} if prev is not None else {},
                  compiler_params=_params(("parallel",)))(w, m, v, sums, far, *(prev if prev is not None else []))


def _pair_add(a, b, name):
    p, r, c = b.shape
    tr = _row_tile(r, c, STREAM_ELEMS)

    def body(a_ref, b_ref, o_ref):
        o_ref[...] = (a_ref[...].astype(F32) + b_ref[...].astype(F32)).astype(BF16)

    spec = pl.BlockSpec((1, tr, c), lambda q, i: (q, i, 0))
    return _pcall(body, name=name, grid=(p, r // tr),
                  in_specs=[pl.BlockSpec((1, tr, c), lambda q, i: (2 * q + lax.axis_index("c"), i, 0)), spec], out_specs=spec,
                  out_shape=jax.ShapeDtypeStruct((p, r, c), BF16), compiler_params=_params(("parallel", "parallel")))(a, b)


def _adamw(w, g_parts, m, v, name):
    r, c = w.shape
    n_parts = g_parts.shape[0]
    tr = _row_tile(r, c) if r % SUBLANES == 0 else r

    def body(w_ref, g_ref, m_ref, v_ref, go_ref, d_ref, mo_ref, vo_ref):
        g = g_ref[0].astype(F32)
        for i in range(1, n_parts):
            g = g + g_ref[i].astype(F32)
        m_new = ADAM_B1 * m_ref[...] + (1.0 - ADAM_B1) * g
        v_new = ADAM_B2 * v_ref[...] + (1.0 - ADAM_B2) * (g * g)
        m_hat = m_new / (1.0 - ADAM_B1 ** ADAM_STEP)
        v_hat = v_new / (1.0 - ADAM_B2 ** ADAM_STEP)
        go_ref[...] = g
        d_ref[...] = -ADAM_LR * (m_hat / (jnp.sqrt(v_hat) + ADAM_EPS) + ADAM_WD * w_ref[...])
        mo_ref[...] = m_new
        vo_ref[...] = v_new

    spec = pl.BlockSpec((tr, c), lambda i: (i, 0))
    shape = jax.ShapeDtypeStruct((r, c), F32)
    return _pcall(body, name=name, grid=(r // tr,),
                  in_specs=[spec, pl.BlockSpec((n_parts, tr, c), lambda i: (0, i, 0)), spec, spec],
                  out_specs=[spec] * 4, out_shape=[shape] * 4, compiler_params=_params(("parallel",)))(w, g_parts, m, v)


def _me():
    return lax.axis_index("x"), lax.axis_index("y"), lax.axis_index("c")


def _dev(px, py, pc):
    return 4 * px + 2 * py + pc


def _remote(src, dst, send_sems, recv_sems, k, to):
    return pltpu.make_async_remote_copy(src_ref=src, dst_ref=dst, send_sem=send_sems.at[k], recv_sem=recv_sems.at[k],
                                        device_id=to, device_id_type=MESH)


def _all_gather(arrs, name):
    n = len(arrs)

    def body(*refs):
        ins, outs = refs[:n], refs[n:2 * n]
        send_sems, recv_sems, local_sems = refs[2 * n:]
        x, y, c = _me()
        me, sibling = (x, y, c), (x, y, 1 - c)
        first = ((x + 1 - c) % 2, (y + c) % 2)
        second = ((x + c) % 2, (y + 1 - c) % 2)
        chips = [first, second, (1 - x, 1 - y)]
        pending = []
        for a in range(n):
            mine = pltpu.make_async_copy(ins[a], outs[a].at[_dev(*me)], local_sems.at[a])
            mine.start()
            pending.append(mine)
        sends = []
        for a in range(n):
            dst = outs[a].at[_dev(*me)]
            sends.append(_remote(ins[a], dst, send_sems, recv_sems, 7 * a, sibling))
            sends += [_remote(ins[a], dst, send_sems, recv_sems, 7 * a + 1 + j, (*chip, c)) for j, chip in enumerate(chips[:2])]
        for cp in sends:
            cp.start()
        for a in range(n):
            for j, chip in enumerate(chips):
                blk = outs[a].at[_dev(*chip, c)]
                _remote(blk, blk, send_sems, recv_sems, 7 * a + 1 + j, me).wait_recv()
                onward = [_remote(blk, blk, send_sems, recv_sems, 7 * a + 4 + j, sibling)]
                if j == 0:
                    onward.append(_remote(blk, blk, send_sems, recv_sems, 7 * a + 3, (*second, c)))
                for cp in onward:
                    cp.start()
                sends += onward
        for a in range(n):
            blk = outs[a].at[_dev(*sibling)]
            _remote(blk, blk, send_sems, recv_sems, 7 * a, me).wait_recv()
            for j, chip in enumerate([second, first, (1 - x, 1 - y)]):
                blk = outs[a].at[_dev(*chip, 1 - c)]
                _remote(blk, blk, send_sems, recv_sems, 7 * a + 4 + j, me).wait_recv()
        for cp in sends:
            cp.wait_send()
        for cp in pending:
            cp.wait()

    return _pcall(
        body, name=name, in_specs=[ANY] * n, out_specs=[ANY] * n,
        out_shape=[jax.ShapeDtypeStruct((N_DEV,) + a.shape, a.dtype) for a in arrs],
        scratch_shapes=[pltpu.SemaphoreType.DMA((7 * n,)), pltpu.SemaphoreType.DMA((7 * n,)), pltpu.SemaphoreType.DMA((n,))],
    )(*arrs)


def _relations(x, y):
    return [(x, y), (1 - x, y), (x, 1 - y), (1 - x, 1 - y)]


HBM_SPEC = pl.BlockSpec(memory_space=pltpu.HBM)
SEM_SPEC = pl.BlockSpec(memory_space=pltpu.SEMAPHORE)
DATAFLOW = pltpu.SideEffectType.DATAFLOW_SIDE_EFFECTING


def _exchange_start(name, bufs, plan, n_remote, n_local, dep):
    nb = len(bufs)
    sem_shapes = [pltpu.SemaphoreType.DMA((n_remote,)), pltpu.SemaphoreType.DMA((n_remote,))]
    if n_local:
        sem_shapes.append(pltpu.SemaphoreType.DMA((n_local,)))
    ns = len(sem_shapes)

    def body(*refs):
        ins, sems, token = refs[:nb], refs[nb + 1:nb + 1 + ns], refs[-1]
        starts, _, local = plan(ins, sems[0], sems[1], sems[2] if n_local else None)
        for cp in local + starts:
            cp.start()
        token[...] = jnp.zeros(token.shape, F32)

    outs = _pcall(
        body, name=name, in_specs=[HBM_SPEC] * nb + [ANY],
        out_specs=[SEM_SPEC] * ns + [HBM_SPEC] * nb + [pl.BlockSpec(memory_space=pltpu.VMEM)],
        out_shape=sem_shapes + [pltpu.HBM(b.shape, b.dtype) for b in bufs] + [jax.ShapeDtypeStruct((SUBLANES, LANES), F32)],
        input_output_aliases={i: ns + i for i in range(nb)}, compiler_params=pltpu.CompilerParams(has_side_effects=DATAFLOW),
    )(*[pltpu.with_memory_space_constraint(b, pltpu.HBM) for b in bufs], dep)
    return dict(sems=outs[:ns], thru=outs[ns:ns + nb], plan=plan, n_local=n_local), outs[-1]


def _exchange_wait(name, handle, *after):
    thru, sems, plan, n_local = handle["thru"], handle["sems"], handle["plan"], handle["n_local"]
    nb, ns = len(thru), len(sems)

    def body(*refs):
        ins, sem_refs = refs[:nb], refs[nb:nb + ns]
        starts, arrivals, local = plan(ins, sem_refs[0], sem_refs[1], sem_refs[2] if n_local else None)
        for cp in starts:
            cp.wait_send()
        for cp in arrivals:
            cp.wait_recv()
        for cp in local:
            cp.wait()

    return _pcall(
        body, name=name, in_specs=[HBM_SPEC] * nb + [SEM_SPEC] * ns + [ANY] * len(after), out_specs=[HBM_SPEC] * nb,
        out_shape=[pltpu.HBM(b.shape, b.dtype) for b in thru], input_output_aliases={i: i for i in range(nb)},
        compiler_params=pltpu.CompilerParams(has_side_effects=DATAFLOW),
    )(*thru, *sems, *after)


def _landing(shape, dtype):
    return lax.empty(shape, dtype)


def _plan_gather_ici(n):
    def plan(refs, send_sems, recv_sems, local_sems):
        x, y, c = _me()
        me, sibling = (x, y, c), (x, y, 1 - c)
        chips = _relations(x, y)[1:]
        starts, arrivals, local = [], [], []
        for a in range(n):
            shard, land = refs[a], refs[n + a]
            own = land.at[_dev(*me)]
            local.append(pltpu.make_async_copy(shard, own, local_sems.at[a]))
            starts.append(_remote(shard, own, send_sems, recv_sems, 4 * a, sibling))
            blk = land.at[_dev(*sibling)]
            arrivals.append(_remote(blk, blk, send_sems, recv_sems, 4 * a, me))
            for j, chip in enumerate(chips):
                starts.append(_remote(shard, own, send_sems, recv_sems, 4 * a + 1 + j, (*chip, c)))
                blk = land.at[_dev(*chip, c)]
                arrivals.append(_remote(blk, blk, send_sems, recv_sems, 4 * a + 1 + j, me))
        return starts, arrivals, local
    return plan


def _plan_gather_d2d(n):
    def plan(refs, send_sems, recv_sems, local_sems):
        x, y, c = _me()
        me, sibling = (x, y, c), (x, y, 1 - c)
        starts, arrivals = [], []
        for a in range(n):
            for j, chip in enumerate(_relations(x, y)[1:]):
                blk = refs[a].at[_dev(*chip, c)]
                starts.append(_remote(blk, blk, send_sems, recv_sems, 3 * a + j, sibling))
                blk = refs[a].at[_dev(*chip, 1 - c)]
                arrivals.append(_remote(blk, blk, send_sems, recv_sems, 3 * a + j, me))
        return starts, arrivals, []
    return plan


def _plan_scatter_d2d(n):
    def plan(refs, send_sems, recv_sems, local_sems):
        x, y, c = _me()
        me, sibling = (x, y, c), (x, y, 1 - c)
        starts, arrivals = [], []
        for a in range(n):
            for k in range(4):
                starts.append(_remote(refs[a].at[2 * k + 1 - c], refs[n + a].at[k], send_sems, recv_sems, 4 * a + k, sibling))
                blk = refs[n + a].at[k]
                arrivals.append(_remote(blk, blk, send_sems, recv_sems, 4 * a + k, me))
        return starts, arrivals, []
    return plan


def _plan_scatter_ici(n):
    def plan(refs, send_sems, recv_sems, local_sems):
        x, y, c = _me()
        me = (x, y, c)
        starts, arrivals = [], []
        for a in range(n):
            for j, (cx, cy) in enumerate(_relations(x, y)[1:]):
                starts.append(_remote(refs[a].at[2 * cx + cy], refs[n + a].at[j], send_sems, recv_sems, 3 * a + j, (cx, cy, c)))
                blk = refs[n + a].at[j]
                arrivals.append(_remote(blk, blk, send_sems, recv_sems, 3 * a + j, me))
        return starts, arrivals, []
    return plan


SMALL = ("conv_a_w", "conv_c_w", "conv_c_b", "gate_r_w", "gate_r_b", "gate_i_w", "gate_i_b", "rg_lambda",
         "norm_a", "norm_b", "norm_c", "sinks", "ln_g", "ln_b")
PACK_COLS = 1024


def _pack(arrs):
    flat = jnp.concatenate([a.reshape(-1) for a in arrs])
    pad = (-flat.shape[0]) % (SUBLANES * PACK_COLS)
    return jnp.pad(flat, (0, pad)).reshape(-1, PACK_COLS)


def _unpack(packed, shapes):
    flat = packed.reshape(-1)
    out, off = [], 0
    for s in shapes:
        size = 1
        for dim in s:
            size *= dim
        out.append(flat[off:off + size].reshape(s))
        off += size
    return out


def kernel(x, w_in, conv_a_w, sinks, conv_c_w, conv_c_b, gate_r_w, gate_r_b, gate_i_w, gate_i_b, rg_lambda, norm_a, norm_b, norm_c, w_out, ln_g, ln_b, loss_target, m_w_in, m_conv_a_w, m_sinks, m_conv_c_w, m_conv_c_b, m_gate_r_w, m_gate_r_b, m_gate_i_w, m_gate_i_b, m_rg_lambda, m_norm_a, m_norm_b, m_norm_c, m_w_out, m_ln_g, m_ln_b, v_w_in, v_conv_a_w, v_sinks, v_conv_c_w, v_conv_c_b, v_gate_r_w, v_gate_r_b, v_gate_i_w, v_gate_i_b, v_rg_lambda, v_norm_a, v_norm_b, v_norm_c, v_w_out, v_ln_g, v_ln_b):
    b_loc, seq, d = x.shape
    depth = w_in.shape[0]
    dm = _Dims(d, gate_r_w.shape[1])
    t = b_loc * seq
    nblk = seq // ATT_BLOCK
    alpha = (2.0 * depth) ** 0.25
    ch = dm.wa // N_DEV
    dev = _dev(*_me())

    wt_shard = [jnp.swapaxes(w_in[l], 0, 1).astype(BF16) for l in range(depth)]
    wo_shard = [w_out[l].astype(BF16) for l in range(depth)]
    conv_shard = jnp.concatenate([conv_a_w.reshape(depth * CONV_A, ch), conv_c_w.reshape(depth * CONV_C, ch)], axis=0)
    conv_shard = jnp.pad(conv_shard, ((0, (-conv_shard.shape[0]) % SUBLANES), (0, 0)))
    wt, conv_all = _all_gather([wt_shard[0], conv_shard], "ag_weights")
    conv_all = jnp.swapaxes(conv_all, 0, 1).reshape(conv_all.shape[1], dm.wa)
    conv_a_full = conv_all[:depth * CONV_A].reshape(depth, CONV_A, dm.wa)
    conv_c_full = conv_all[depth * CONV_A:depth * (CONV_A + CONV_C)].reshape(depth, CONV_C, dm.wc)
    sinks_wide = jnp.broadcast_to(sinks[:, :, None], (depth, dm.nq, LANES))

    def layer_params(l):
        return (conv_a_full[l], conv_c_full[l], conv_c_b[l][None], gate_r_w[l], gate_r_b[l][None], gate_i_w[l],
                gate_i_b[l][None], rg_lambda[l][None], norm_a[l][None], norm_b[l][None], norm_c[l][None], sinks_wide[l])

    tm = _tile(t, 1024)
    xs = x.reshape(t, d)
    xb = xs.astype(BF16)
    saved = []
    bias = _attention_bias()
    rows_t, rows_o = dm.in_w // N_DEV, d // N_DEV
    wo0_ici, token = _exchange_start("ag_wo0_ici_start", [wo_shard[0], _landing((N_DEV, rows_o, d), BF16)],
                                     _plan_gather_ici(1), 4, 1, wt)
    wo = None
    for l in range(depth):
        if l + 1 < depth:
            lands = [_landing((N_DEV, rows_t, d), BF16), _landing((N_DEV, rows_o, d), BF16)]
            ici, token = _exchange_start("ag_ici_start_%d" % l, [wt_shard[l + 1], wo_shard[l + 1]] + lands, _plan_gather_ici(2),
                                         8, 2, token if l == 0 else wt)
        elif l > 0:
            token = _no_token()
        h = _matmul(xb, wt.reshape(dm.in_w, d), dims=NT, tm=tm, tn=_tile(dm.in_w, 512), tk=d, out_dtype=F32, name="mm_in",
                    token=token)
        mix, sst = _mixer_fwd(dm, h, bias, layer_params(l), b_loc, nblk)
        if l == 0:
            land = _exchange_wait("ag_wo0_ici_wait", wo0_ici, mix)[1:]
            wo0_d2d, _ = _exchange_start("ag_wo0_d2d_start", land, _plan_gather_d2d(1), 3, 0, mix)
            wo = _exchange_wait("ag_wo0_d2d_wait", wo0_d2d, mix)[0]
        z = _matmul(mix, wo.reshape(d, d), dims=NN, tm=tm, tn=_tile(d, 1024), tk=d, out_dtype=F32, name="mm_out", addend=xs,
                    alpha=alpha)
        saved.append((xb, h, sst, mix, z, wt, wo))
        if l + 1 < depth:
            lands = _exchange_wait("ag_ici_wait_%d" % l, ici, z)[2:]
            d2d, token = _exchange_start("ag_d2d_start_%d" % l, lands, _plan_gather_d2d(2), 6, 0, z)
            xs, xb = _ln_fwd(z, ln_g[l][None], ln_b[l][None], token)
            wt, wo = _exchange_wait("ag_d2d_wait_%d" % l, d2d, xb)

    dy, loss_part = _ln_loss_head(z, ln_g[depth - 1][None], ln_b[depth - 1][None], loss_target.reshape(t, d))
    loss = lax.psum(loss_part[0, 0], ("x", "y", "c"))

    scattered, small = [None] * depth, [None] * depth
    ici, token_ici = None, _no_token()

    def finish_scatter(l, ici, *after):
        scattered[l] = _exchange_wait("rs_ici_wait_%d" % l, ici, *after)

    def scatter_add_start(l, d2d, *after):
        done = _exchange_wait("rs_d2d_wait_%d" % l, d2d, *after)
        sums = [_pair_add(p, g, "rs_add_%d" % i) for i, (p, g) in enumerate(zip(done[:2], done[2:]))]
        lands = [_landing((3, rows_t, d), BF16), _landing((3, rows_o, d), BF16)]
        return _exchange_start("rs_ici_start_%d" % l, sums + lands, _plan_scatter_ici(2), 6, 0, after[0])

    for l in reversed(range(depth)):
        xb, h, sst, mix, z, wt, wo = saved[l]
        dz, dzb, d_lng, d_lnb = _ln_bwd(dy, z, ln_g[l][None], token_ici)
        dmix = _matmul(dzb, wo.reshape(d, d), dims=NT, tm=tm, tn=_tile(d, 1024), tk=d, out_dtype=F32, name="mm_dmix")
        dwo = _matmul(mix, dzb, dims=TN, tm=_tile(d, 1024), tn=_tile(d, 512), tk=t, out_dtype=BF16, name="mm_dwo")
        dh, sm = _mixer_bwd(dm, h, sst, dmix, bias, layer_params(l), b_loc, nblk)
        (d_caw, d_ccw, d_ccb, d_grw, d_grb, d_giw, d_gib, d_lam, d_na, d_nb, d_nc, d_snk) = sm
        small[l] = dict(conv_a_w=d_caw, conv_c_w=d_ccw, conv_c_b=d_ccb[0], gate_r_w=d_grw, gate_r_b=d_grb[0], gate_i_w=d_giw,
                        gate_i_b=d_gib[0], rg_lambda=d_lam[0], norm_a=d_na[0], norm_b=d_nb[0], norm_c=d_nc[0],
                        sinks=d_snk[:, 0], ln_g=d_lng[0], ln_b=d_lnb[0])
        token = _no_token()
        if l == 0:
            g_local = _pack([jnp.stack([small[i][n] for i in range(depth)]) for n in SMALL])
            small_ici, token = _exchange_start("ag_small_ici_start", [g_local, _landing((N_DEV,) + g_local.shape, F32)],
                                               _plan_gather_ici(1), 4, 1, dh)
        dwt = _matmul(dh, xb, dims=TN, tm=_tile(dm.in_w, 1536), tn=_tile(d, 512), tk=t, out_dtype=BF16, name="mm_dwt",
                      token=token)
        if ici is not None:
            finish_scatter(l + 1, ici, dh)
        parts = [dwt.reshape(N_DEV, rows_t, d), dwo.reshape(N_DEV, rows_o, d)]
        lands = [_landing((4, rows_t, d), BF16), _landing((4, rows_o, d), BF16)]
        d2d, token = _exchange_start("rs_d2d_start_%d" % l, parts + lands, _plan_scatter_d2d(2), 8, 0, dh)
        if l > 0:
            dy = _matmul(dh, wt.reshape(dm.in_w, d), dims=NN, tm=tm, tn=_tile(d, 1024), tk=_tile(dm.in_w, 3584), out_dtype=F32,
                         name="mm_dx", addend=dz, alpha=alpha, token=token)
            ici, token_ici = scatter_add_start(l, d2d, dy)
        else:
            small_land = _exchange_wait("ag_small_ici_wait", small_ici, token)[1:]
            small_d2d, token = _exchange_start("ag_small_d2d_start", small_land, _plan_gather_d2d(1), 3, 0, token)
            ici, token_ici = scatter_add_start(l, d2d, token)
            dy = _matmul(dh, wt.reshape(dm.in_w, d), dims=NN, tm=tm, tn=_tile(d, 1024), tk=_tile(dm.in_w, 3584), out_dtype=F32,
                         name="mm_dx", addend=dz, alpha=alpha, token=token_ici)
            g_all = _exchange_wait("ag_small_d2d_wait", small_d2d, dy)[0]
    grad_x = dy.reshape(b_loc, seq, d)

    w_t, m_t, v_t = [jnp.swapaxes(a, 1, 2) for a in (w_in, m_w_in, v_w_in)]
    res_in, res_out = None, None
    for l in reversed(range(depth)):
        if l == 0:
            finish_scatter(0, ici, dy, *([res_in[0], res_out[0]] if depth > 1 else []))
        sums_t, sums_o, far_t, far_o = scattered[l]
        res_in = _adamw_scattered(w_t, m_t, v_t, l, sums_t, far_t, res_in, "adamw_in_%d" % l)
        res_out = _adamw_scattered(w_out, m_w_out, v_w_out, l, sums_o, far_o, res_out, "adamw_out_%d" % l)
    gw_in, dl_in, nm_in, nv_in = [jnp.swapaxes(a, 1, 2) for a in res_in]
    gw_out, dl_out, nm_out, nv_out = res_out

    given = dict(conv_a_w=(conv_a_w, m_conv_a_w, v_conv_a_w), conv_c_w=(conv_c_w, m_conv_c_w, v_conv_c_w),
                 conv_c_b=(conv_c_b, m_conv_c_b, v_conv_c_b), gate_r_w=(gate_r_w, m_gate_r_w, v_gate_r_w),
                 gate_r_b=(gate_r_b, m_gate_r_b, v_gate_r_b), gate_i_w=(gate_i_w, m_gate_i_w, v_gate_i_w),
                 gate_i_b=(gate_i_b, m_gate_i_b, v_gate_i_b), rg_lambda=(rg_lambda, m_rg_lambda, v_rg_lambda),
                 norm_a=(norm_a, m_norm_a, v_norm_a), norm_b=(norm_b, m_norm_b, v_norm_b), norm_c=(norm_c, m_norm_c, v_norm_c),
                 sinks=(sinks, m_sinks, v_sinks), ln_g=(ln_g, m_ln_g, v_ln_g), ln_b=(ln_b, m_ln_b, v_ln_b))
    full_shapes = [jnp.stack([small[l][n] for l in range(depth)]).shape for n in SMALL]

    def mine_of(n, a):
        if n in ("conv_a_w", "conv_c_w"):
            return lax.dynamic_update_slice(jnp.zeros(a.shape[:2] + (dm.wa,), F32), a, (0, 0, dev * ch))
        return a

    packs = [_pack([mine_of(n, given[n][i]) for n in SMALL]) for i in range(3)]
    outs = _adamw(packs[0], g_all, packs[1], packs[2], "adamw_small")
    res = {}
    for kind, packed in zip(("grad", "delta", "new_m", "new_v"), outs):
        for n, a in zip(SMALL, _unpack(packed, full_shapes)):
            if n in ("conv_a_w", "conv_c_w"):
                a = lax.dynamic_slice(a, (0, 0, dev * ch), a.shape[:2] + (ch,))
            res[kind, n] = a
    res.update({("grad", "w_in"): gw_in, ("delta", "w_in"): dl_in, ("new_m", "w_in"): nm_in, ("new_v", "w_in"): nv_in,
                ("grad", "w_out"): gw_out, ("delta", "w_out"): dl_out, ("new_m", "w_out"): nm_out, ("new_v", "w_out"): nv_out})
    order = ("w_in", "conv_a_w", "sinks", "conv_c_w", "conv_c_b", "gate_r_w", "gate_r_b", "gate_i_w", "gate_i_b", "rg_lambda",
             "norm_a", "norm_b", "norm_c", "w_out", "ln_g", "ln_b")
    return (loss, grad_x, *[res[kind, n] for kind in ("grad", "delta", "new_m", "new_v") for n in order])
```

```python
import functools

import jax
import jax.numpy as jnp
from jax import lax
from jax.experimental import pallas as pl
from jax.experimental.pallas import tpu as pltpu

F32 = jnp.float32
BF16 = jnp.bfloat16
MESH = pl.DeviceIdType.MESH
ANY = pl.BlockSpec(memory_space=pl.ANY)

N_DEV = 8
LANES = 128
SUBLANES = 8
HEAD_DIM = 64
KV_GROUP = 8
ATT_BLOCK = 128
ATT_STACK = 8
CONV_A = 3
CONV_C = 4
RG_C = 8.0
LN_EPS = 1e-5
RMS_EPS = 1e-6
NEG_INF = -1e30
ADAM_LR = 0.001
ADAM_B1 = 0.9
ADAM_B2 = 0.999
ADAM_EPS = 1e-08
ADAM_WD = 0.01
ADAM_STEP = 10
VMEM_LIMIT = 56 * 1024 * 1024

NN = ((1,), (0,))
NT = ((1,), (1,))
TN = ((0,), (0,))


def _pcall(body, **kw):
    return pl.pallas_call(body, **kw)


def _roll(x, shift, axis):
    return pltpu.roll(x, shift, axis)


def _params(sem=None, vmem=VMEM_LIMIT):
    return pltpu.CompilerParams(dimension_semantics=sem, vmem_limit_bytes=vmem)


def _dot(a, b, dims):
    return lax.dot_general(a.astype(BF16), b.astype(BF16), (dims, ((), ())), preferred_element_type=F32)


@jax.custom_vjp
def _mm(a, b):
    return _dot(a, b, NN)


def _mm_fwd(a, b):
    return _dot(a, b, NN), (a.astype(BF16), b.astype(BF16))


def _mm_bwd(res, g):
    a, b = res
    return _dot(g, b, NT), _dot(a, g, TN)


_mm.defvjp(_mm_fwd, _mm_bwd)


@jax.custom_vjp
def _mm_nt(a, b):
    return _dot(a, b, NT)


def _mm_nt_fwd(a, b):
    return _dot(a, b, NT), (a.astype(BF16), b.astype(BF16))


def _mm_nt_bwd(res, g):
    a, b = res
    return _dot(g, b, NN), _dot(g, a, TN)


_mm_nt.defvjp(_mm_nt_fwd, _mm_nt_bwd)


def _rows(shape):
    return lax.broadcasted_iota(jnp.int32, shape, 0)


@functools.partial(jax.custom_vjp, nondiff_argnums=(2,))
def _shift_halo(u, prev, k):
    r, c = u.shape
    fill = jnp.concatenate([_roll(prev, k, 0), jnp.zeros((r - SUBLANES, c), u.dtype)], axis=0)
    return jnp.where(_rows((r, c)) < k, fill, _roll(u, k, 0))


def _shift_halo_fwd(u, prev, k):
    return _shift_halo(u, prev, k), None


def _shift_halo_bwd(k, _, g):
    r, c = g.shape
    du = jnp.where(_rows((r, c)) < r - k, _roll(g, r - k, 0), 0.0)
    dprev = jnp.where(_rows((SUBLANES, c)) >= SUBLANES - k, _roll(g[0:SUBLANES], SUBLANES - k, 0), 0.0)
    return du, dprev


_shift_halo.defvjp(_shift_halo_fwd, _shift_halo_bwd)


@functools.partial(jax.custom_vjp, nondiff_argnums=(1, 2))
def _shift_fill(u, k, fill):
    return jnp.where(_rows(u.shape) < k, fill, _roll(u, k, 0))


def _shift_fill_fwd(u, k, fill):
    return _shift_fill(u, k, fill), None


def _shift_fill_bwd(k, fill, _, g):
    r = g.shape[0]
    return (jnp.where(_rows(g.shape) < r - k, _roll(g, r - k, 0), 0.0),)


_shift_fill.defvjp(_shift_fill_fwd, _shift_fill_bwd)


@jax.custom_vjp
def _swap_halves(x):
    return _roll(x, HEAD_DIM, 1)


_swap_halves.defvjp(lambda x: (_roll(x, HEAD_DIM, 1), None), lambda _, g: (_roll(g, HEAD_DIM, 1),))


@functools.partial(jax.custom_vjp, nondiff_argnums=(1,))
def _split_rows(x, n):
    r = x.shape[0] // n
    return tuple(x[i * r:(i + 1) * r] for i in range(n))


def _split_rows_fwd(x, n):
    return _split_rows(x, n), None


def _split_rows_bwd(n, _, gs):
    return (jnp.concatenate(list(gs), axis=0),)


_split_rows.defvjp(_split_rows_fwd, _split_rows_bwd)


def _logistic(x):
    return 1.0 / (1.0 + jnp.exp(-x))


@jax.custom_vjp
def _sigmoid(x):
    return _logistic(x)


def _sigmoid_fwd(x):
    s = _logistic(x)
    return s, s


_sigmoid.defvjp(_sigmoid_fwd, lambda s, g: (g * s * (1.0 - s),))


@jax.custom_vjp
def _silu(x):
    return x * _logistic(x)


def _silu_fwd(x):
    s = _logistic(x)
    return x * s, (x, s)


_silu.defvjp(_silu_fwd, lambda res, g: (g * res[1] * (1.0 + res[0] * (1.0 - res[1])),))


def _log_sigmoid(x):
    return -(jnp.maximum(-x, 0.0) + jnp.log1p(jnp.exp(-jnp.abs(x))))


@jax.custom_vjp
def _neg_expm1(x):
    series = x * (1 + x * (1 / 2) * (1 + x * (1 / 3) * (1 + x * (1 / 4) * (1 + x * (1 / 5) * (1 + x * (1 / 6) * (1 + x * (1 / 7)))))))
    return -jnp.where(jnp.abs(x) < 0.25, series, jnp.exp(x) - 1.0)


_neg_expm1.defvjp(lambda x: (_neg_expm1(x), x), lambda x, g: (-g * jnp.exp(x),))


def _shift_up(x, k, fill):
    r = x.shape[0]
    return jnp.where(_rows(x.shape) < r - k, _roll(x, r - k, 0), fill)


@jax.custom_vjp
def _scan_block(a, u, s_prev):
    acc_a, acc_b = a, u
    d = 1
    while d < a.shape[0]:
        acc_b = acc_a * _shift_fill(acc_b, d, 0.0) + acc_b
        acc_a = acc_a * _shift_fill(acc_a, d, 1.0)
        d *= 2
    return acc_a * s_prev + acc_b


def _scan_block_fwd(a, u, s_prev):
    h = _scan_block(a, u, s_prev)
    return h, (a, h, s_prev)


def _scan_block_bwd(res, dh):
    a, h, s_prev = res
    acc_a, acc_g = _shift_up(a, 1, 0.0), dh
    d = 1
    while d < a.shape[0]:
        acc_g = acc_a * _shift_up(acc_g, d, 0.0) + acc_g
        acc_a = acc_a * _shift_up(acc_a, d, 1.0)
        d *= 2
    h_prev = jnp.where(_rows(h.shape) < 1, s_prev, _roll(h, 1, 0))
    first = jnp.sum(jnp.where(_rows(h.shape) < 1, a * acc_g, 0.0), axis=0, keepdims=True)
    return acc_g * h_prev, acc_g, first


_scan_block.defvjp(_scan_block_fwd, _scan_block_bwd)


def _last_row(h):
    return jnp.sum(jnp.where(_rows(h.shape) == h.shape[0] - 1, h, 0.0), axis=0, keepdims=True)


def _branch_a(ab, ac, ax, ag, acp, axp, w0, w1, w2, na):
    yas = []
    for j in range(len(ab)):
        u = ac[j] * ax[j]
        up = acp[j] * axp[j]
        yas.append(ab[j] * (w2[j] * u + w1[j] * _shift_halo(u, up, 1) + w0[j] * _shift_halo(u, up, 2)))
    ms = sum(jnp.sum(ya * ya, axis=1, keepdims=True) for ya in yas) * (1.0 / (LANES * len(ab)))
    inv = lax.rsqrt(ms + RMS_EPS)
    return [yas[j] * inv * na[j] * _silu(ag[j]) for j in range(len(ab))]


def _branch_c(cx, cg, cxp, sp, wc, bc, wr, br, wi, bi, lam, nc):
    hs, lasts = [], []
    for j in range(len(cx)):
        xc = (wc[3][j] * cx[j] + wc[2][j] * _shift_halo(cx[j], cxp[j], 1) + wc[1][j] * _shift_halo(cx[j], cxp[j], 2)
              + wc[0][j] * _shift_halo(cx[j], cxp[j], 3) + bc[j])
        r = _sigmoid(_mm(xc, wr[j]) + br[j])
        i = _sigmoid(_mm(xc, wi[j]) + bi[j])
        log_a = RG_C * r * _log_sigmoid(lam[j])
        a = jnp.exp(log_a)
        u = jnp.sqrt(_neg_expm1(2.0 * log_a)) * (i * xc)
        h = _scan_block(a, u, sp[j])
        hs.append(h)
        lasts.append(_last_row(h))
    width = LANES * len(cx)
    ms = sum(jnp.sum(h * h, axis=1, keepdims=True) for h in hs) * (1.0 / width)
    inv = lax.rsqrt(ms + RMS_EPS)
    return [hs[j] * inv * nc[j] * _silu(cg[j]) for j in range(len(cx))], lasts


def _attention_bias():
    qi = (jnp.arange(ATT_STACK * ATT_BLOCK) % ATT_BLOCK)[:, None]
    kj = jnp.arange(2 * ATT_BLOCK)[None, :]
    dist = qi + ATT_BLOCK - kj
    band = (dist >= 0) & (dist < ATT_BLOCK)
    return jnp.where(jnp.stack([band & (kj >= ATT_BLOCK), band]), 0.0, NEG_INF).astype(F32)


def _branch_b(q, k, v, kp, vp, bg, snk, nb, bias):
    rows = ATT_BLOCK
    n_kv = 2 * len(k)
    lane0 = lax.broadcasted_iota(jnp.int32, (ATT_STACK * rows, LANES), 1) == 0
    upper = lax.broadcasted_iota(jnp.int32, (2 * rows, LANES), 1) >= HEAD_DIM
    heads = [None] * (n_kv * KV_GROUP)
    for g in range(n_kv):
        half = g % 2
        keep = upper if half else jnp.logical_not(upper)
        kc = jnp.where(keep, jnp.concatenate([kp[g // 2], k[g // 2]], axis=0), 0.0)
        vc = jnp.where(keep, jnp.concatenate([vp[g // 2], v[g // 2]], axis=0), 0.0)
        for first in range(g * KV_GROUP, (g + 1) * KV_GROUP, ATT_STACK):
            hs = range(first, first + ATT_STACK)
            qg = jnp.concatenate([q[h // 2] if h % 2 == half else _swap_halves(q[h // 2]) for h in hs], axis=0)
            s = _mm_nt(qg * (HEAD_DIM ** -0.5), kc) + bias
            sink = jnp.concatenate([jnp.broadcast_to(snk[h], (rows, LANES)) for h in hs], axis=0)
            sink = jnp.sum(jnp.where(lane0, sink, 0.0), axis=1, keepdims=True)
            m = lax.stop_gradient(jnp.maximum(jnp.max(s, axis=1, keepdims=True), sink))
            p = jnp.exp(s - m)
            inv = 1.0 / (jnp.sum(p, axis=1, keepdims=True) + jnp.exp(sink - m))
            o = _split_rows(_mm(p * inv, vc), ATT_STACK)
            for i, h in enumerate(hs):
                heads[h] = o[i] if h % 2 == half else _swap_halves(o[i])
    yb = [heads[2 * j] + heads[2 * j + 1] for j in range(len(q))]
    width = LANES * len(q)
    ms = sum(jnp.sum(y * y, axis=1, keepdims=True) for y in yb) * (1.0 / width)
    inv = lax.rsqrt(ms + RMS_EPS)
    return [yb[j] * inv * nb[j] * _silu(bg[j]) for j in range(len(q))]


class _Dims:
    def __init__(self, d_model, n_rg_heads):
        self.d = d_model
        self.wa = d_model // 4
        self.wb = d_model // 2
        self.wc = d_model // 4
        self.kvw = self.wb // KV_GROUP
        self.nq = self.wb // HEAD_DIM
        self.in_w = 4 * self.wa + 2 * self.wb + 2 * self.kvw + 2 * self.wc
        self.o_q = 4 * self.wa
        self.o_k = self.o_q + self.wb
        self.o_v = self.o_k + self.kvw
        self.o_bg = self.o_v + self.kvw
        self.o_cx = self.o_bg + self.wb
        self.o_cg = self.o_cx + self.wc
        self.nh = n_rg_heads
        assert self.wc // n_rg_heads == LANES and self.kvw % LANES == 0
        assert self.o_k % self.kvw == 0 and self.o_cx % (self.wc // 2) == 0


def _chunks(ref, rows, off, width):
    return [ref[rows, off + LANES * j: off + LANES * (j + 1)] for j in range(width // LANES)]


def _read_params(dm, caw, ccw, ccb, grw, grb, giw, gib, lam, na, nb, nc, snk):
    row = slice(0, 1)
    return dict(
        wa=[_chunks(caw, slice(k, k + 1), 0, dm.wa) for k in range(CONV_A)], na=_chunks(na, row, 0, dm.wa),
        wc=[_chunks(ccw, slice(k, k + 1), 0, dm.wc) for k in range(CONV_C)], bc=_chunks(ccb, row, 0, dm.wc),
        wr=[grw[j] for j in range(dm.nh)], br=_chunks(grb, row, 0, dm.wc),
        wi=[giw[j] for j in range(dm.nh)], bi=_chunks(gib, row, 0, dm.wc),
        lam=_chunks(lam, row, 0, dm.wc), nc=_chunks(nc, row, 0, dm.wc),
        nb=_chunks(nb, row, 0, dm.wb), snk=[snk[h:h + 1, :] for h in range(dm.nq)])


def _param_specs(dm):
    shapes = [(CONV_A, dm.wa), (CONV_C, dm.wc), (1, dm.wc), (dm.nh, LANES, LANES), (1, dm.wc), (dm.nh, LANES, LANES),
              (1, dm.wc), (1, dm.wc), (1, dm.wa), (1, dm.wb), (1, dm.wc), (dm.nq, LANES)]
    specs = [pl.BlockSpec(s, (lambda b, n, _r=len(s): (0,) * _r)) for s in shapes]
    return shapes, specs


def _bias_spec():
    return pl.BlockSpec((2, ATT_STACK * ATT_BLOCK, 2 * ATT_BLOCK), lambda *_: (0, 0, 0))


def _mixer_fwd(dm, h, bias, prm, b_loc, nblk):
    t = h.shape[0]
    r = ATT_BLOCK
    tail = slice(r - SUBLANES, r)

    def body(h_ref, bias_ref, caw, ccw, ccb, grw, grb, giw, gib, lam, na, nb, nc, snk, mix_ref, sst_ref, kp, vp, acp, axp, cxp, sp):
        n = pl.program_id(1)

        @pl.when(n == 0)
        def _():
            for ref in (kp, vp, acp, axp, cxp, sp):
                ref[...] = jnp.zeros(ref.shape, ref.dtype)

        p = _read_params(dm, caw, ccw, ccb, grw, grb, giw, gib, lam, na, nb, nc, snk)
        full = slice(None)
        mix_a = _branch_a(*[_chunks(h_ref, full, k * dm.wa, dm.wa) for k in range(4)], _chunks(acp, full, 0, dm.wa),
                          _chunks(axp, full, 0, dm.wa), p["wa"][0], p["wa"][1], p["wa"][2], p["na"])
        for j, ma in enumerate(mix_a):
            mix_ref[:, LANES * j: LANES * (j + 1)] = ma.astype(BF16)
        bias = bias_ref[jnp.where(n == 0, 0, 1)]
        mix_b = _branch_b(_chunks(h_ref, full, dm.o_q, dm.wb), _chunks(h_ref, full, dm.o_k, dm.kvw),
                          _chunks(h_ref, full, dm.o_v, dm.kvw), _chunks(kp, full, 0, dm.kvw), _chunks(vp, full, 0, dm.kvw),
                          _chunks(h_ref, full, dm.o_bg, dm.wb), p["snk"], p["nb"], bias)
        for j, mb in enumerate(mix_b):
            mix_ref[:, dm.wa + LANES * j: dm.wa + LANES * (j + 1)] = mb.astype(BF16)
        sst_ref[0] = sp[...]
        mix_c, lasts = _branch_c(_chunks(h_ref, full, dm.o_cx, dm.wc), _chunks(h_ref, full, dm.o_cg, dm.wc),
                                 _chunks(cxp, full, 0, dm.wc), _chunks(sp, slice(0, 1), 0, dm.wc), p["wc"], p["bc"],
                                 p["wr"], p["br"], p["wi"], p["bi"], p["lam"], p["nc"])
        o_c = dm.wa + dm.wb
        for j, mc in enumerate(mix_c):
            mix_ref[:, o_c + LANES * j: o_c + LANES * (j + 1)] = mc.astype(BF16)
            sp[:, LANES * j: LANES * (j + 1)] = jnp.broadcast_to(lasts[j], (SUBLANES, LANES))
        kp[...] = h_ref[:, dm.o_k:dm.o_k + dm.kvw]
        vp[...] = h_ref[:, dm.o_v:dm.o_v + dm.kvw]
        acp[...] = h_ref[tail, dm.wa:2 * dm.wa]
        axp[...] = h_ref[tail, 2 * dm.wa:3 * dm.wa]
        cxp[...] = h_ref[tail, dm.o_cx:dm.o_cx + dm.wc]

    _, pspecs = _param_specs(dm)
    return _pcall(
        body, name="mixer_fwd", grid=(b_loc, nblk),
        in_specs=[pl.BlockSpec((r, dm.in_w), lambda b, n: (b * nblk + n, 0)), _bias_spec()] + pspecs,
        out_specs=[pl.BlockSpec((r, dm.d), lambda b, n: (b * nblk + n, 0)),
                   pl.BlockSpec((1, SUBLANES, dm.wc), lambda b, n: (b * nblk + n, 0, 0))],
        out_shape=[jax.ShapeDtypeStruct((t, dm.d), BF16), jax.ShapeDtypeStruct((b_loc * nblk, SUBLANES, dm.wc), F32)],
        scratch_shapes=[pltpu.VMEM((r, dm.kvw), F32), pltpu.VMEM((r, dm.kvw), F32), pltpu.VMEM((SUBLANES, dm.wa), F32),
                        pltpu.VMEM((SUBLANES, dm.wa), F32), pltpu.VMEM((SUBLANES, dm.wc), F32),
                        pltpu.VMEM((SUBLANES, dm.wc), F32)],
        compiler_params=_params(("arbitrary", "arbitrary")),
    )(h, bias, *prm)


def _mixer_bwd(dm, h, sst, dmix, bias, prm, b_loc, nblk):
    t = h.shape[0]
    r = ATT_BLOCK
    rb8 = r // SUBLANES
    n_small = 12

    def body(h_ref, kp_ref, vp_ref, acp_ref, axp_ref, cxp0_ref, cxp1_ref, sst_ref, dmix_ref, bias_ref,
             caw, ccw, ccb, grw, grb, giw, gib, lam, na, nb, nc, snk,
             dh_ref, d_caw, d_ccw, d_ccb, d_grw, d_grb, d_giw, d_gib, d_lam, d_na, d_nb, d_nc, d_snk,
             dkp, dvp, dacp, daxp, dcxp, dsp):
        step = pl.program_id(1)
        n = nblk - 1 - step

        @pl.when(step == 0)
        def _():
            for ref in (dkp, dvp, dacp, daxp, dcxp, dsp):
                ref[...] = jnp.zeros(ref.shape, ref.dtype)

        @pl.when((step == 0) & (pl.program_id(0) == 0))
        def _():
            for ref in (d_caw, d_ccw, d_ccb, d_grw, d_grb, d_giw, d_gib, d_lam, d_na, d_nb, d_nc, d_snk):
                ref[...] = jnp.zeros(ref.shape, ref.dtype)

        p = _read_params(dm, caw, ccw, ccb, grw, grb, giw, gib, lam, na, nb, nc, snk)
        has_prev = jnp.where(n > 0, 1.0, 0.0)
        full = slice(None)
        pad = jnp.zeros((r - SUBLANES, LANES), F32)

        def with_tail(own, carry):
            z = jnp.zeros((r - SUBLANES, own.shape[1]), F32)
            return own + jnp.concatenate([z, carry], axis=0)

        a_in = (*[_chunks(h_ref, full, k * dm.wa, dm.wa) for k in range(4)],
                [c * has_prev for c in _chunks(acp_ref, full, 0, dm.wa)], [c * has_prev for c in _chunks(axp_ref, full, 0, dm.wa)],
                p["wa"][0], p["wa"][1], p["wa"][2], p["na"])
        _, vjp_a = jax.vjp(_branch_a, *a_in)
        g_ab, g_ac, g_ax, g_ag, g_acp, g_axp, g_w0, g_w1, g_w2, g_na = vjp_a(_chunks(dmix_ref, full, 0, dm.wa))
        for j in range(len(g_ab)):
            cols = slice(LANES * j, LANES * (j + 1))
            dh_ref[:, LANES * j: LANES * (j + 1)] = g_ab[j].astype(BF16)
            dh_ref[:, dm.wa + LANES * j: dm.wa + LANES * (j + 1)] = with_tail(g_ac[j], dacp[:, cols]).astype(BF16)
            dh_ref[:, 2 * dm.wa + LANES * j: 2 * dm.wa + LANES * (j + 1)] = with_tail(g_ax[j], daxp[:, cols]).astype(BF16)
            dh_ref[:, 3 * dm.wa + LANES * j: 3 * dm.wa + LANES * (j + 1)] = g_ag[j].astype(BF16)
            dacp[:, cols] = g_acp[j]
            daxp[:, cols] = g_axp[j]
            for k, gw in enumerate((g_w0, g_w1, g_w2)):
                d_caw[k:k + 1, cols] += gw[j]
            d_na[:, cols] += g_na[j]

        bias = bias_ref[jnp.where(n == 0, 0, 1)]
        kp_in = [c * has_prev for c in _chunks(kp_ref, full, 0, dm.kvw)]
        vp_in = [c * has_prev for c in _chunks(vp_ref, full, 0, dm.kvw)]
        b_in = (_chunks(h_ref, full, dm.o_q, dm.wb), _chunks(h_ref, full, dm.o_k, dm.kvw), _chunks(h_ref, full, dm.o_v, dm.kvw),
                kp_in, vp_in, _chunks(h_ref, full, dm.o_bg, dm.wb), p["snk"], p["nb"])
        _, vjp_b = jax.vjp(lambda *a: _branch_b(*a, bias), *b_in)
        g_q, g_k, g_v, g_kp, g_vp, g_bg, g_snk, g_nb = vjp_b(_chunks(dmix_ref, full, dm.wa, dm.wb))
        for j in range(len(g_q)):
            dh_ref[:, dm.o_q + LANES * j: dm.o_q + LANES * (j + 1)] = g_q[j].astype(BF16)
            dh_ref[:, dm.o_bg + LANES * j: dm.o_bg + LANES * (j + 1)] = g_bg[j].astype(BF16)
            d_nb[:, LANES * j: LANES * (j + 1)] += g_nb[j]
        for j in range(len(g_k)):
            cols = slice(LANES * j, LANES * (j + 1))
            dh_ref[:, dm.o_k + LANES * j: dm.o_k + LANES * (j + 1)] = (g_k[j] + dkp[:, cols]).astype(BF16)
            dh_ref[:, dm.o_v + LANES * j: dm.o_v + LANES * (j + 1)] = (g_v[j] + dvp[:, cols]).astype(BF16)
            dkp[:, cols] = g_kp[j]
            dvp[:, cols] = g_vp[j]
        for hd in range(dm.nq):
            d_snk[hd:hd + 1, :] += g_snk[hd]

        half_c = dm.wc // 2
        cxp_in = ([c * has_prev for c in _chunks(cxp0_ref, full, 0, half_c)]
                  + [c * has_prev for c in _chunks(cxp1_ref, full, 0, half_c)])
        c_in = (_chunks(h_ref, full, dm.o_cx, dm.wc), _chunks(h_ref, full, dm.o_cg, dm.wc), cxp_in,
                [sst_ref[0, 0:1, LANES * j: LANES * (j + 1)] for j in range(dm.nh)], p["wc"], p["bc"], p["wr"], p["br"], p["wi"], p["bi"],
                p["lam"], p["nc"])
        _, vjp_c = jax.vjp(_branch_c, *c_in)
        ct_last = [dsp[0:1, LANES * j: LANES * (j + 1)] for j in range(dm.nh)]
        g_cx, g_cg, g_cxp, g_sp, g_wc, g_bc, g_wr, g_br, g_wi, g_bi, g_lam, g_nc = vjp_c(
            (_chunks(dmix_ref, full, dm.wa + dm.wb, dm.wc), ct_last))
        for j in range(dm.nh):
            cols = slice(LANES * j, LANES * (j + 1))
            tot = g_cx[j] + jnp.concatenate([pad, dcxp[:, cols]], axis=0)
            dh_ref[:, dm.o_cx + LANES * j: dm.o_cx + LANES * (j + 1)] = tot.astype(BF16)
            dh_ref[:, dm.o_cg + LANES * j: dm.o_cg + LANES * (j + 1)] = g_cg[j].astype(BF16)
            dcxp[:, cols] = g_cxp[j]
            dsp[:, cols] = jnp.broadcast_to(g_sp[j], (SUBLANES, LANES))
            for k in range(CONV_C):
                d_ccw[k:k + 1, cols] += g_wc[k][j]
            d_ccb[:, cols] += g_bc[j]
            d_grw[j] += g_wr[j]
            d_grb[:, cols] += g_br[j]
            d_giw[j] += g_wi[j]
            d_gib[:, cols] += g_bi[j]
            d_lam[:, cols] += g_lam[j]
            d_nc[:, cols] += g_nc[j]

    def blk(b, s):
        return b * nblk + (nblk - 1 - s)

    def prev_rows8(b, s):
        return jnp.maximum(blk(b, s) * rb8 - 1, 0)

    pshapes, pspecs = _param_specs(dm)
    half_c = dm.wc // 2
    in_specs = [
        pl.BlockSpec((r, dm.in_w), lambda b, s: (blk(b, s), 0)),
        pl.BlockSpec((r, dm.kvw), lambda b, s: (jnp.maximum(blk(b, s) - 1, 0), dm.o_k // dm.kvw)),
        pl.BlockSpec((r, dm.kvw), lambda b, s: (jnp.maximum(blk(b, s) - 1, 0), dm.o_v // dm.kvw)),
        pl.BlockSpec((SUBLANES, dm.wa), lambda b, s: (prev_rows8(b, s), 1)),
        pl.BlockSpec((SUBLANES, dm.wa), lambda b, s: (prev_rows8(b, s), 2)),
        pl.BlockSpec((SUBLANES, half_c), lambda b, s: (prev_rows8(b, s), dm.o_cx // half_c)),
        pl.BlockSpec((SUBLANES, half_c), lambda b, s: (prev_rows8(b, s), dm.o_cx // half_c + 1)),
        pl.BlockSpec((1, SUBLANES, dm.wc), lambda b, s: (blk(b, s), 0, 0)),
        pl.BlockSpec((r, dm.d), lambda b, s: (blk(b, s), 0)),
        _bias_spec(),
    ] + pspecs
    outs = _pcall(
        body, name="mixer_bwd", grid=(b_loc, nblk), in_specs=in_specs,
        out_specs=[pl.BlockSpec((r, dm.in_w), lambda b, s: (blk(b, s), 0))] + pspecs,
        out_shape=[jax.ShapeDtypeStruct((t, dm.in_w), BF16)] + [jax.ShapeDtypeStruct(s, F32) for s in pshapes],
        scratch_shapes=[pltpu.VMEM((r, dm.kvw), F32), pltpu.VMEM((r, dm.kvw), F32), pltpu.VMEM((SUBLANES, dm.wa), F32),
                        pltpu.VMEM((SUBLANES, dm.wa), F32), pltpu.VMEM((SUBLANES, dm.wc), F32),
                        pltpu.VMEM((SUBLANES, dm.wc), F32)],
        compiler_params=_params(("arbitrary", "arbitrary")),
    )(h, h, h, h, h, h, h, sst, dmix, bias, *prm)
    assert len(outs) == 1 + n_small
    return outs[0], outs[1:]


def _token_spec():
    return pl.BlockSpec((SUBLANES, LANES), lambda *_: (0, 0))


def _no_token():
    return jnp.zeros((SUBLANES, LANES), F32)


def _matmul(a, b, *, dims, tm, tn, tk, out_dtype, name, addend=None, alpha=None, token=None):
    if dims == TN:
        (k_dim, m), n_dim = a.shape, b.shape[1]
        a_spec = pl.BlockSpec((tk, tm), lambda i, j, k: (k, i))
    else:
        (m, k_dim), n_dim = a.shape, (b.shape[0] if dims == NT else b.shape[1])
        a_spec = pl.BlockSpec((tm, tk), lambda i, j, k: (i, k))
    b_spec = pl.BlockSpec((tn, tk), lambda i, j, k: (j, k)) if dims == NT else pl.BlockSpec((tk, tn), lambda i, j, k: (k, j))
    assert m % tm == 0 and n_dim % tn == 0 and k_dim % tk == 0, (a.shape, b.shape, tm, tn, tk)
    nk = k_dim // tk
    o_spec = pl.BlockSpec((tm, tn), lambda i, j, k: (i, j))

    def body(*refs):
        a_ref, b_ref = refs[0], refs[1]
        add_ref = refs[2] if addend is not None else None
        o_ref, acc_ref = refs[-2], refs[-1]
        k = pl.program_id(2)
        part = lax.dot_general(a_ref[...], b_ref[...], (dims, ((), ())), preferred_element_type=F32)

        def finish(acc):
            if add_ref is not None:
                acc = acc + alpha * add_ref[...]
            o_ref[...] = acc.astype(out_dtype)

        if nk == 1:
            finish(part)
        else:
            @pl.when(k == 0)
            def _():
                acc_ref[...] = part

            @pl.when((k > 0) & (k < nk - 1))
            def _():
                acc_ref[...] += part

            @pl.when(k == nk - 1)
            def _():
                finish(acc_ref[...] + part)

    ins = [a, b] + ([addend] if addend is not None else []) + ([token] if token is not None else [])
    in_specs = [a_spec, b_spec] + ([o_spec] if addend is not None else []) + ([_token_spec()] if token is not None else [])
    return _pcall(
        body, name=name, grid=(m // tm, n_dim // tn, nk), in_specs=in_specs, out_specs=o_spec,
        out_shape=jax.ShapeDtypeStruct((m, n_dim), out_dtype),
        scratch_shapes=[pltpu.VMEM((tm, tn) if nk > 1 else (SUBLANES, LANES), F32)],
        compiler_params=_params(("parallel", "parallel", "arbitrary")),
    )(*ins)


def _tile(n, want, quantum=LANES):
    if n <= want:
        return n
    for cand in range(want - want % quantum, 0, -quantum):
        if n % cand == 0:
            return cand
    return n


def _row_tile(t, d, elems=1 << 19):
    return _tile(t, max(2 * SUBLANES, elems // d), 2 * SUBLANES)


STREAM_ELEMS = 1 << 20


def _ln_fwd(z, g, b, token):
    t, d = z.shape
    tr = _row_tile(t, d, STREAM_ELEMS)

    def body(z_ref, g_ref, b_ref, _, y_ref, yb_ref):
        zz = z_ref[...]
        mu = jnp.mean(zz, axis=1, keepdims=True)
        zc = zz - mu
        var = jnp.mean(zc * zc, axis=1, keepdims=True)
        y = zc * lax.rsqrt(var + LN_EPS) * g_ref[...] + b_ref[...]
        y_ref[...] = y
        yb_ref[...] = y.astype(BF16)

    row = pl.BlockSpec((tr, d), lambda i: (i, 0))
    vec = pl.BlockSpec((1, d), lambda i: (0, 0))
    return _pcall(body, name="ln_fwd", grid=(t // tr,), in_specs=[row, vec, vec, _token_spec()], out_specs=[row, row],
                  out_shape=[jax.ShapeDtypeStruct((t, d), F32), jax.ShapeDtypeStruct((t, d), BF16)],
                  compiler_params=_params(("parallel",)))(z, g, b, token)


def _ln_bwd(dy, z, g, token):
    t, d = z.shape
    tr = _row_tile(t, d, STREAM_ELEMS)

    def body(dy_ref, z_ref, g_ref, _, dz_ref, dzb_ref, dg_ref, db_ref):
        @pl.when(pl.program_id(0) == 0)
        def _():
            dg_ref[...] = jnp.zeros(dg_ref.shape, F32)
            db_ref[...] = jnp.zeros(db_ref.shape, F32)

        zz = z_ref[...]
        dyy = dy_ref[...]
        mu = jnp.mean(zz, axis=1, keepdims=True)
        zc = zz - mu
        rstd = lax.rsqrt(jnp.mean(zc * zc, axis=1, keepdims=True) + LN_EPS)
        xhat = zc * rstd
        dyg = dyy * g_ref[...]
        dz = rstd * (dyg - jnp.mean(dyg, axis=1, keepdims=True) - xhat * jnp.mean(dyg * xhat, axis=1, keepdims=True))
        dz_ref[...] = dz
        dzb_ref[...] = dz.astype(BF16)
        dg_ref[...] += jnp.sum(dyy * xhat, axis=0, keepdims=True)
        db_ref[...] += jnp.sum(dyy, axis=0, keepdims=True)

    row = pl.BlockSpec((tr, d), lambda i: (i, 0))
    vec = pl.BlockSpec((1, d), lambda i: (0, 0))
    return _pcall(body, name="ln_bwd", grid=(t // tr,), in_specs=[row, row, vec, _token_spec()], out_specs=[row, row, vec, vec],
                  out_shape=[jax.ShapeDtypeStruct((t, d), F32), jax.ShapeDtypeStruct((t, d), BF16),
                             jax.ShapeDtypeStruct((1, d), F32), jax.ShapeDtypeStruct((1, d), F32)],
                  compiler_params=_params(("arbitrary",)))(dy, z, g, token)


def _ln_loss_head(z, g, b, target):
    t, d = z.shape
    tr = _row_tile(t, d, STREAM_ELEMS)

    def body(z_ref, g_ref, b_ref, t_ref, dy_ref, loss_ref):
        @pl.when(pl.program_id(0) == 0)
        def _():
            loss_ref[...] = jnp.zeros(loss_ref.shape, F32)

        zz = z_ref[...]
        mu = jnp.mean(zz, axis=1, keepdims=True)
        zc = zz - mu
        var = jnp.mean(zc * zc, axis=1, keepdims=True)
        err = zc * lax.rsqrt(var + LN_EPS) * g_ref[...] + b_ref[...] - t_ref[...]
        dy_ref[...] = err * (1.0 / d)
        per_token = jnp.sum(err * err, axis=1, keepdims=True) * (1.0 / d)
        loss_ref[...] += 0.5 * jnp.sum(per_token, axis=0, keepdims=True)

    row = pl.BlockSpec((tr, d), lambda i: (i, 0))
    vec = pl.BlockSpec((1, d), lambda i: (0, 0))
    one = pl.BlockSpec((1, 1), lambda i: (0, 0))
    return _pcall(body, name="ln_loss_head", grid=(t // tr,), in_specs=[row, vec, vec, row], out_specs=[row, one],
                  out_shape=[jax.ShapeDtypeStruct((t, d), F32), jax.ShapeDtypeStruct((1, 1), F32)],
                  compiler_params=_params(("arbitrary",)))(z, g, b, target)


def _adamw_scattered(w, m, v, layer, sums, far, prev, name):
    n_layers, r, c = w.shape
    tr = _row_tile(r, c)

    def body(*refs):
        w_ref, m_ref, v_ref, s_ref, f_ref = refs[:5]
        go_ref, d_ref, mo_ref, vo_ref = refs[-4:]
        gg = s_ref[...].astype(F32)
        for i in range(3):
            gg = gg + f_ref[i:i + 1].astype(F32)
        m_new = ADAM_B1 * m_ref[...] + (1.0 - ADAM_B1) * gg
        v_new = ADAM_B2 * v_ref[...] + (1.0 - ADAM_B2) * (gg * gg)
        m_hat = m_new / (1.0 - ADAM_B1 ** ADAM_STEP)
        v_hat = v_new / (1.0 - ADAM_B2 ** ADAM_STEP)
        go_ref[...] = gg
        d_ref[...] = -ADAM_LR * (m_hat / (jnp.sqrt(v_hat) + ADAM_EPS) + ADAM_WD * w_ref[...])
        mo_ref[...] = m_new
        vo_ref[...] = v_new

    own = pl.BlockSpec((1, tr, c), lambda i: (layer, i, 0))
    in_specs = [own, own, own, pl.BlockSpec((1, tr, c), lambda i: (2 * lax.axis_index("x") + lax.axis_index("y"), i, 0)),
                pl.BlockSpec((3, tr, c), lambda i: (0, i, 0))] + ([ANY] * 4 if prev is not None else [])
    return _pcall(body, name=name, grid=(r // tr,), in_specs=in_specs, out_specs=[own] * 4,
                  out_shape=[jax.ShapeDtypeStruct((n_layers, r, c), F32)] * 4,
                  input_output_aliases={5 + i: i for i in range(4)} if prev is not None else {},
                  compiler_params=_params(("parallel",)))(w, m, v, sums, far, *(prev if prev is not None else []))


def _pair_add(a, b, name):
    p, r, c = b.shape
    tr = _row_tile(r, c, STREAM_ELEMS)

    def body(a_ref, b_ref, o_ref):
        o_ref[...] = (a_ref[...].astype(F32) + b_ref[...].astype(F32)).astype(BF16)

    spec = pl.BlockSpec((1, tr, c), lambda q, i: (q, i, 0))
    return _pcall(body, name=name, grid=(p, r // tr),
                  in_specs=[pl.BlockSpec((1, tr, c), lambda q, i: (2 * q + lax.axis_index("c"), i, 0)), spec], out_specs=spec,
                  out_shape=jax.ShapeDtypeStruct((p, r, c), BF16), compiler_params=_params(("parallel", "parallel")))(a, b)


def _adamw(w, g_parts, m, v, name):
    r, c = w.shape
    n_parts = g_parts.shape[0]
    tr = _row_tile(r, c) if r % SUBLANES == 0 else r

    def body(w_ref, g_ref, m_ref, v_ref, go_ref, d_ref, mo_ref, vo_ref):
        g = g_ref[0].astype(F32)
        for i in range(1, n_parts):
            g = g + g_ref[i].astype(F32)
        m_new = ADAM_B1 * m_ref[...] + (1.0 - ADAM_B1) * g
        v_new = ADAM_B2 * v_ref[...] + (1.0 - ADAM_B2) * (g * g)
        m_hat = m_new / (1.0 - ADAM_B1 ** ADAM_STEP)
        v_hat = v_new / (1.0 - ADAM_B2 ** ADAM_STEP)
        go_ref[...] = g
        d_ref[...] = -ADAM_LR * (m_hat / (jnp.sqrt(v_hat) + ADAM_EPS) + ADAM_WD * w_ref[...])
        mo_ref[...] = m_new
        vo_ref[...] = v_new

    spec = pl.BlockSpec((tr, c), lambda i: (i, 0))
    shape = jax.ShapeDtypeStruct((r, c), F32)
    return _pcall(body, name=name, grid=(r // tr,),
                  in_specs=[spec, pl.BlockSpec((n_parts, tr, c), lambda i: (0, i, 0)), spec, spec],
                  out_specs=[spec] * 4, out_shape=[shape] * 4, compiler_params=_params(("parallel",)))(w, g_parts, m, v)


def _me():
    return lax.axis_index("x"), lax.axis_index("y"), lax.axis_index("c")


def _dev(px, py, pc):
    return 4 * px + 2 * py + pc


def _remote(src, dst, send_sems, recv_sems, k, to):
    return pltpu.make_async_remote_copy(src_ref=src, dst_ref=dst, send_sem=send_sems.at[k], recv_sem=recv_sems.at[k],
                                        device_id=to, device_id_type=MESH)


def _all_gather(arrs, name):
    n = len(arrs)

    def body(*refs):
        ins, outs = refs[:n], refs[n:2 * n]
        send_sems, recv_sems, local_sems = refs[2 * n:]
        x, y, c = _me()
        me, sibling = (x, y, c), (x, y, 1 - c)
        first = ((x + 1 - c) % 2, (y + c) % 2)
        second = ((x + c) % 2, (y + 1 - c) % 2)
        chips = [first, second, (1 - x, 1 - y)]
        pending = []
        for a in range(n):
            mine = pltpu.make_async_copy(ins[a], outs[a].at[_dev(*me)], local_sems.at[a])
            mine.start()
            pending.append(mine)
        sends = []
        for a in range(n):
            dst = outs[a].at[_dev(*me)]
            sends.append(_remote(ins[a], dst, send_sems, recv_sems, 7 * a, sibling))
            sends += [_remote(ins[a], dst, send_sems, recv_sems, 7 * a + 1 + j, (*chip, c)) for j, chip in enumerate(chips[:2])]
        for cp in sends:
            cp.start()
        for a in range(n):
            for j, chip in enumerate(chips):
                blk = outs[a].at[_dev(*chip, c)]
                _remote(blk, blk, send_sems, recv_sems, 7 * a + 1 + j, me).wait_recv()
                onward = [_remote(blk, blk, send_sems, recv_sems, 7 * a + 4 + j, sibling)]
                if j == 0:
                    onward.append(_remote(blk, blk, send_sems, recv_sems, 7 * a + 3, (*second, c)))
                for cp in onward:
                    cp.start()
                sends += onward
        for a in range(n):
            blk = outs[a].at[_dev(*sibling)]
            _remote(blk, blk, send_sems, recv_sems, 7 * a, me).wait_recv()
            for j, chip in enumerate([second, first, (1 - x, 1 - y)]):
                blk = outs[a].at[_dev(*chip, 1 - c)]
                _remote(blk, blk, send_sems, recv_sems, 7 * a + 4 + j, me).wait_recv()
        for cp in sends:
            cp.wait_send()
        for cp in pending:
            cp.wait()

    return _pcall(
        body, name=name, in_specs=[ANY] * n, out_specs=[ANY] * n,
        out_shape=[jax.ShapeDtypeStruct((N_DEV,) + a.shape, a.dtype) for a in arrs],
        scratch_shapes=[pltpu.SemaphoreType.DMA((7 * n,)), pltpu.SemaphoreType.DMA((7 * n,)), pltpu.SemaphoreType.DMA((n,))],
    )(*arrs)


def _relations(x, y):
    return [(x, y), (1 - x, y), (x, 1 - y), (1 - x, 1 - y)]


HBM_SPEC = pl.BlockSpec(memory_space=pltpu.HBM)
SEM_SPEC = pl.BlockSpec(memory_space=pltpu.SEMAPHORE)
DATAFLOW = pltpu.SideEffectType.DATAFLOW_SIDE_EFFECTING


def _exchange_start(name, bufs, plan, n_remote, n_local, dep):
    nb = len(bufs)
    sem_shapes = [pltpu.SemaphoreType.DMA((n_remote,)), pltpu.SemaphoreType.DMA((n_remote,))]
    if n_local:
        sem_shapes.append(pltpu.SemaphoreType.DMA((n_local,)))
    ns = len(sem_shapes)

    def body(*refs):
        ins, sems, token = refs[:nb], refs[nb + 1:nb + 1 + ns], refs[-1]
        starts, _, local = plan(ins, sems[0], sems[1], sems[2] if n_local else None)
        for cp in local + starts:
            cp.start()
        token[...] = jnp.zeros(token.shape, F32)

    outs = _pcall(
        body, name=name, in_specs=[HBM_SPEC] * nb + [ANY],
        out_specs=[SEM_SPEC] * ns + [HBM_SPEC] * nb + [pl.BlockSpec(memory_space=pltpu.VMEM)],
        out_shape=sem_shapes + [pltpu.HBM(b.shape, b.dtype) for b in bufs] + [jax.ShapeDtypeStruct((SUBLANES, LANES), F32)],
        input_output_aliases={i: ns + i for i in range(nb)}, compiler_params=pltpu.CompilerParams(has_side_effects=DATAFLOW),
    )(*[pltpu.with_memory_space_constraint(b, pltpu.HBM) for b in bufs], dep)
    return dict(sems=outs[:ns], thru=outs[ns:ns + nb], plan=plan, n_local=n_local), outs[-1]


def _exchange_wait(name, handle, *after):
    thru, sems, plan, n_local = handle["thru"], handle["sems"], handle["plan"], handle["n_local"]
    nb, ns = len(thru), len(sems)

    def body(*refs):
        ins, sem_refs = refs[:nb], refs[nb:nb + ns]
        starts, arrivals, local = plan(ins, sem_refs[0], sem_refs[1], sem_refs[2] if n_local else None)
        for cp in starts:
            cp.wait_send()
        for cp in arrivals:
            cp.wait_recv()
        for cp in local:
            cp.wait()

    return _pcall(
        body, name=name, in_specs=[HBM_SPEC] * nb + [SEM_SPEC] * ns + [ANY] * len(after), out_specs=[HBM_SPEC] * nb,
        out_shape=[pltpu.HBM(b.shape, b.dtype) for b in thru], input_output_aliases={i: i for i in range(nb)},
        compiler_params=pltpu.CompilerParams(has_side_effects=DATAFLOW),
    )(*thru, *sems, *after)


def _landing(shape, dtype):
    return lax.empty(shape, dtype)


def _plan_gather_ici(n):
    def plan(refs, send_sems, recv_sems, local_sems):
        x, y, c = _me()
        me, sibling = (x, y, c), (x, y, 1 - c)
        chips = _relations(x, y)[1:]
        starts, arrivals, local = [], [], []
        for a in range(n):
            shard, land = refs[a], refs[n + a]
            own = land.at[_dev(*me)]
            local.append(pltpu.make_async_copy(shard, own, local_sems.at[a]))
            starts.append(_remote(shard, own, send_sems, recv_sems, 4 * a, sibling))
            blk = land.at[_dev(*sibling)]
            arrivals.append(_remote(blk, blk, send_sems, recv_sems, 4 * a, me))
            for j, chip in enumerate(chips):
                starts.append(_remote(shard, own, send_sems, recv_sems, 4 * a + 1 + j, (*chip, c)))
                blk = land.at[_dev(*chip, c)]
                arrivals.append(_remote(blk, blk, send_sems, recv_sems, 4 * a + 1 + j, me))
        return starts, arrivals, local
    return plan


def _plan_gather_d2d(n):
    def plan(refs, send_sems, recv_sems, local_sems):
        x, y, c = _me()
        me, sibling = (x, y, c), (x, y, 1 - c)
        starts, arrivals = [], []
        for a in range(n):
            for j, chip in enumerate(_relations(x, y)[1:]):
                blk = refs[a].at[_dev(*chip, c)]
                starts.append(_remote(blk, blk, send_sems, recv_sems, 3 * a + j, sibling))
                blk = refs[a].at[_dev(*chip, 1 - c)]
                arrivals.append(_remote(blk, blk, send_sems, recv_sems, 3 * a + j, me))
        return starts, arrivals, []
    return plan


def _plan_scatter_d2d(n):
    def plan(refs, send_sems, recv_sems, local_sems):
        x, y, c = _me()
        me, sibling = (x, y, c), (x, y, 1 - c)
        starts, arrivals = [], []
        for a in range(n):
            for k in range(4):
                starts.append(_remote(refs[a].at[2 * k + 1 - c], refs[n + a].at[k], send_sems, recv_sems, 4 * a + k, sibling))
                blk = refs[n + a].at[k]
                arrivals.append(_remote(blk, blk, send_sems, recv_sems, 4 * a + k, me))
        return starts, arrivals, []
    return plan


def _plan_scatter_ici(n):
    def plan(refs, send_sems, recv_sems, local_sems):
        x, y, c = _me()
        me = (x, y, c)
        starts, arrivals = [], []
        for a in range(n):
            for j, (cx, cy) in enumerate(_relations(x, y)[1:]):
                starts.append(_remote(refs[a].at[2 * cx + cy], refs[n + a].at[j], send_sems, recv_sems, 3 * a + j, (cx, cy, c)))
                blk = refs[n + a].at[j]
                arrivals.append(_remote(blk, blk, send_sems, recv_sems, 3 * a + j, me))
        return starts, arrivals, []
    return plan


SMALL = ("conv_a_w", "conv_c_w", "conv_c_b", "gate_r_w", "gate_r_b", "gate_i_w", "gate_i_b", "rg_lambda",
         "norm_a", "norm_b", "norm_c", "sinks", "ln_g", "ln_b")
PACK_COLS = 1024


def _pack(arrs):
    flat = jnp.concatenate([a.reshape(-1) for a in arrs])
    pad = (-flat.shape[0]) % (SUBLANES * PACK_COLS)
    return jnp.pad(flat, (0, pad)).reshape(-1, PACK_COLS)


def _unpack(packed, shapes):
    flat = packed.reshape(-1)
    out, off = [], 0
    for s in shapes:
        size = 1
        for dim in s:
            size *= dim
        out.append(flat[off:off + size].reshape(s))
        off += size
    return out


def kernel(x, w_in, conv_a_w, sinks, conv_c_w, conv_c_b, gate_r_w, gate_r_b, gate_i_w, gate_i_b, rg_lambda, norm_a, norm_b, norm_c, w_out, ln_g, ln_b, loss_target, m_w_in, m_conv_a_w, m_sinks, m_conv_c_w, m_conv_c_b, m_gate_r_w, m_gate_r_b, m_gate_i_w, m_gate_i_b, m_rg_lambda, m_norm_a, m_norm_b, m_norm_c, m_w_out, m_ln_g, m_ln_b, v_w_in, v_conv_a_w, v_sinks, v_conv_c_w, v_conv_c_b, v_gate_r_w, v_gate_r_b, v_gate_i_w, v_gate_i_b, v_rg_lambda, v_norm_a, v_norm_b, v_norm_c, v_w_out, v_ln_g, v_ln_b):
    b_loc, seq, d = x.shape
    depth = w_in.shape[0]
    dm = _Dims(d, gate_r_w.shape[1])
    t = b_loc * seq
    nblk = seq // ATT_BLOCK
    alpha = (2.0 * depth) ** 0.25
    ch = dm.wa // N_DEV
    dev = _dev(*_me())

    wt_shard = [jnp.swapaxes(w_in[l], 0, 1).astype(BF16) for l in range(depth)]
    wo_shard = [w_out[l].astype(BF16) for l in range(depth)]
    conv_shard = jnp.concatenate([conv_a_w.reshape(depth * CONV_A, ch), conv_c_w.reshape(depth * CONV_C, ch)], axis=0)
    conv_shard = jnp.pad(conv_shard, ((0, (-conv_shard.shape[0]) % SUBLANES), (0, 0)))
    wt, conv_all = _all_gather([wt_shard[0], conv_shard], "ag_weights")
    conv_all = jnp.swapaxes(conv_all, 0, 1).reshape(conv_all.shape[1], dm.wa)
    conv_a_full = conv_all[:depth * CONV_A].reshape(depth, CONV_A, dm.wa)
    conv_c_full = conv_all[depth * CONV_A:depth * (CONV_A + CONV_C)].reshape(depth, CONV_C, dm.wc)
    sinks_wide = jnp.broadcast_to(sinks[:, :, None], (depth, dm.nq, LANES))

    def layer_params(l):
        return (conv_a_full[l], conv_c_full[l], conv_c_b[l][None], gate_r_w[l], gate_r_b[l][None], gate_i_w[l],
                gate_i_b[l][None], rg_lambda[l][None], norm_a[l][None], norm_b[l][None], norm_c[l][None], sinks_wide[l])

    tm = _tile(t, 1024)
    xs = x.reshape(t, d)
    xb = xs.astype(BF16)
    saved = []
    bias = _attention_bias()
    rows_t, rows_o = dm.in_w // N_DEV, d // N_DEV
    wo0_ici, token = _exchange_start("ag_wo0_ici_start", [wo_shard[0], _landing((N_DEV, rows_o, d), BF16)],
                                     _plan_gather_ici(1), 4, 1, wt)
    wo = None
    for l in range(depth):
        if l + 1 < depth:
            lands = [_landing((N_DEV, rows_t, d), BF16), _landing((N_DEV, rows_o, d), BF16)]
            ici, token = _exchange_start("ag_ici_start_%d" % l, [wt_shard[l + 1], wo_shard[l + 1]] + lands, _plan_gather_ici(2),
                                         8, 2, token if l == 0 else wt)
        elif l > 0:
            token = _no_token()
        h = _matmul(xb, wt.reshape(dm.in_w, d), dims=NT, tm=tm, tn=_tile(dm.in_w, 512), tk=d, out_dtype=F32, name="mm_in",
                    token=token)
        mix, sst = _mixer_fwd(dm, h, bias, layer_params(l), b_loc, nblk)
        if l == 0:
            land = _exchange_wait("ag_wo0_ici_wait", wo0_ici, mix)[1:]
            wo0_d2d, _ = _exchange_start("ag_wo0_d2d_start", land, _plan_gather_d2d(1), 3, 0, mix)
            wo = _exchange_wait("ag_wo0_d2d_wait", wo0_d2d, mix)[0]
        z = _matmul(mix, wo.reshape(d, d), dims=NN, tm=tm, tn=_tile(d, 1024), tk=d, out_dtype=F32, name="mm_out", addend=xs,
                    alpha=alpha)
        saved.append((xb, h, sst, mix, z, wt, wo))
        if l + 1 < depth:
            lands = _exchange_wait("ag_ici_wait_%d" % l, ici, z)[2:]
            d2d, token = _exchange_start("ag_d2d_start_%d" % l, lands, _plan_gather_d2d(2), 6, 0, z)
            xs, xb = _ln_fwd(z, ln_g[l][None], ln_b[l][None], token)
            wt, wo = _exchange_wait("ag_d2d_wait_%d" % l, d2d, xb)

    dy, loss_part = _ln_loss_head(z, ln_g[depth - 1][None], ln_b[depth - 1][None], loss_target.reshape(t, d))
    loss = lax.psum(loss_part[0, 0], ("x", "y", "c"))

    scattered, small = [None] * depth, [None] * depth
    ici, token_ici = None, _no_token()

    def finish_scatter(l, ici, *after):
        scattered[l] = _exchange_wait("rs_ici_wait_%d" % l, ici, *after)

    def scatter_add_start(l, d2d, *after):
        done = _exchange_wait("rs_d2d_wait_%d" % l, d2d, *after)
        sums = [_pair_add(p, g, "rs_add_%d" % i) for i, (p, g) in enumerate(zip(done[:2], done[2:]))]
        lands = [_landing((3, rows_t, d), BF16), _landing((3, rows_o, d), BF16)]
        return _exchange_start("rs_ici_start_%d" % l, sums + lands, _plan_scatter_ici(2), 6, 0, after[0])

    for l in reversed(range(depth)):
        xb, h, sst, mix, z, wt, wo = saved[l]
        dz, dzb, d_lng, d_lnb = _ln_bwd(dy, z, ln_g[l][None], token_ici)
        dmix = _matmul(dzb, wo.reshape(d, d), dims=NT, tm=tm, tn=_tile(d, 1024), tk=d, out_dtype=F32, name="mm_dmix")
        dwo = _matmul(mix, dzb, dims=TN, tm=_tile(d, 1024), tn=_tile(d, 1024), tk=t, out_dtype=BF16, name="mm_dwo")
        dh, sm = _mixer_bwd(dm, h, sst, dmix, bias, layer_params(l), b_loc, nblk)
        (d_caw, d_ccw, d_ccb, d_grw, d_grb, d_giw, d_gib, d_lam, d_na, d_nb, d_nc, d_snk) = sm
        small[l] = dict(conv_a_w=d_caw, conv_c_w=d_ccw, conv_c_b=d_ccb[0], gate_r_w=d_grw, gate_r_b=d_grb[0], gate_i_w=d_giw,
                        gate_i_b=d_gib[0], rg_lambda=d_lam[0], norm_a=d_na[0], norm_b=d_nb[0], norm_c=d_nc[0],
                        sinks=d_snk[:, 0], ln_g=d_lng[0], ln_b=d_lnb[0])
        token = _no_token()
        if l == 0:
            g_local = _pack([jnp.stack([small[i][n] for i in range(depth)]) for n in SMALL])
            small_ici, token = _exchange_start("ag_small_ici_start", [g_local, _landing((N_DEV,) + g_local.shape, F32)],
                                               _plan_gather_ici(1), 4, 1, dh)
        dwt = _matmul(dh, xb, dims=TN, tm=_tile(dm.in_w, 1536), tn=_tile(d, 512), tk=t, out_dtype=BF16, name="mm_dwt",
                      token=token)
        if ici is not None:
            finish_scatter(l + 1, ici, dh)
        parts = [dwt.reshape(N_DEV, rows_t, d), dwo.reshape(N_DEV, rows_o, d)]
        lands = [_landing((4, rows_t, d), BF16), _landing((4, rows_o, d), BF16)]
        d2d, token = _exchange_start("rs_d2d_start_%d" % l, parts + lands, _plan_scatter_d2d(2), 8, 0, dh)
        if l > 0:
            dy = _matmul(dh, wt.reshape(dm.in_w, d), dims=NN, tm=tm, tn=_tile(d, 1024), tk=_tile(dm.in_w, 3584), out_dtype=F32,
                         name="mm_dx", addend=dz, alpha=alpha, token=token)
            ici, token_ici = scatter_add_start(l, d2d, dy)
        else:
            small_land = _exchange_wait("ag_small_ici_wait", small_ici, token)[1:]
            small_d2d, token = _exchange_start("ag_small_d2d_start", small_land, _plan_gather_d2d(1), 3, 0, token)
            ici, token_ici = scatter_add_start(l, d2d, token)
            dy = _matmul(dh, wt.reshape(dm.in_w, d), dims=NN, tm=tm, tn=_tile(d, 1024), tk=_tile(dm.in_w, 3584), out_dtype=F32,
                         name="mm_dx", addend=dz, alpha=alpha, token=token_ici)
            g_all = _exchange_wait("ag_small_d2d_wait", small_d2d, dy)[0]
    grad_x = dy.reshape(b_loc, seq, d)

    w_t, m_t, v_t = [jnp.swapaxes(a, 1, 2) for a in (w_in, m_w_in, v_w_in)]
    res_in, res_out = None, None
    for l in reversed(range(depth)):
        if l == 0:
            finish_scatter(0, ici, dy, *([res_in[0], res_out[0]] if depth > 1 else []))
        sums_t, sums_o, far_t, far_o = scattered[l]
        res_in = _adamw_scattered(w_t, m_t, v_t, l, sums_t, far_t, res_in, "adamw_in_%d" % l)
        res_out = _adamw_scattered(w_out, m_w_out, v_w_out, l, sums_o, far_o, res_out, "adamw_out_%d" % l)
    gw_in, dl_in, nm_in, nv_in = [jnp.swapaxes(a, 1, 2) for a in res_in]
    gw_out, dl_out, nm_out, nv_out = res_out

    given = dict(conv_a_w=(conv_a_w, m_conv_a_w, v_conv_a_w), conv_c_w=(conv_c_w, m_conv_c_w, v_conv_c_w),
                 conv_c_b=(conv_c_b, m_conv_c_b, v_conv_c_b), gate_r_w=(gate_r_w, m_gate_r_w, v_gate_r_w),
                 gate_r_b=(gate_r_b, m_gate_r_b, v_gate_r_b), gate_i_w=(gate_i_w, m_gate_i_w, v_gate_i_w),
                 gate_i_b=(gate_i_b, m_gate_i_b, v_gate_i_b), rg_lambda=(rg_lambda, m_rg_lambda, v_rg_lambda),
                 norm_a=(norm_a, m_norm_a, v_norm_a), norm_b=(norm_b, m_norm_b, v_norm_b), norm_c=(norm_c, m_norm_c, v_norm_c),
                 sinks=(sinks, m_sinks, v_sinks), ln_g=(ln_g, m_ln_g, v_ln_g), ln_b=(ln_b, m_ln_b, v_ln_b))
    full_shapes = [jnp.stack([small[l][n] for l in range(depth)]).shape for n in SMALL]

    def mine_of(n, a):
        if n in ("conv_a_w", "conv_c_w"):
            return lax.dynamic_update_slice(jnp.zeros(a.shape[:2] + (dm.wa,), F32), a, (0, 0, dev * ch))
        return a

    packs = [_pack([mine_of(n, given[n][i]) for n in SMALL]) for i in range(3)]
    outs = _adamw(packs[0], g_all, packs[1], packs[2], "adamw_small")
    res = {}
    for kind, packed in zip(("grad", "delta", "new_m", "new_v"), outs):
        for n, a in zip(SMALL, _unpack(packed, full_shapes)):
            if n in ("conv_a_w", "conv_c_w"):
                a = lax.dynamic_slice(a, (0, 0, dev * ch), a.shape[:2] + (ch,))
            res[kind, n] = a
    res.update({("grad", "w_in"): gw_in, ("delta", "w_in"): dl_in, ("new_m", "w_in"): nm_in, ("new_v", "w_in"): nv_in,
                ("grad", "w_out"): gw_out, ("delta", "w_out"): dl_out, ("new_m", "w_out"): nm_out, ("new_v", "w_out"): nv_out})
    order = ("w_in", "conv_a_w", "sinks", "conv_c_w", "conv_c_b", "gate_r_w", "gate_r_b", "gate_i_w", "gate_i_b", "rg_lambda",
             "norm_a", "norm_b", "norm_c", "w_out", "ln_g", "ln_b")
    return (loss, grad_x, *[res[kind, n] for kind in ("grad", "delta", "new_m", "new_v") for n in order])
```
